```python
import math
import jax, jax.numpy as jnp
from jax import lax
import numpy as np

D_MODEL = 1024
BATCH = 32
SEQ = 256
DEPTH = 2
DEC_BATCH = 2
DEC_SEQ = 1024
PAST_LEN = 512

GRID_W = 64
N_HEADS_A = 4
HEAD_DIM_A = 64
A_QK = N_HEADS_A * HEAD_DIM_A
A_V = N_HEADS_A * 2 * HEAD_DIM_A
D_INNER_B = D_MODEL // 2
HEAD_DIM_B = 64
N_HEADS_B = D_INNER_B // HEAD_DIM_B
N_GROUPS_B = 2
D_STATE = 128
D_CONV = 5
CONV_DIM_B = D_INNER_B + 2 * N_GROUPS_B * D_STATE
CHUNK = 128
W_IN0 = 4 * A_QK + A_V + D_INNER_B + CONV_DIM_B + 2 * N_HEADS_B
POOL_WINDOWS = (2, 4, 8, 16)
N_POOL_GROUPS = 4
POOL_GROUP_DIM = D_MODEL // N_POOL_GROUPS
D_FF = 4 * D_MODEL
N_MOD = 6
ROPE_BASE = 10000.0
LN_EPS = 1e-5
ALPHA = (2 * DEPTH) ** 0.25
BETA = (8 * DEPTH) ** -0.25

kernel_name = "diffattn_ssd_pool_prefix_dit_step"


def layer_norm(x, g, b):
    xf = x.astype(jnp.float32)
    mu = jnp.mean(xf, -1, keepdims=True)
    var = jnp.mean(jnp.square(xf - mu), -1, keepdims=True)
    return ((xf - mu) * lax.rsqrt(var + LN_EPS) * g + b).astype(x.dtype)


def rms_norm(x, w, eps=1e-5):
    xf = x.astype(jnp.float32)
    return (xf * lax.rsqrt(jnp.mean(xf * xf, -1, keepdims=True) + eps) * w).astype(x.dtype)


def modulation(cond, w_mod, b_mod):
    m = jax.nn.silu(cond) @ w_mod + b_mod
    return jnp.split(m[..., None, :], N_MOD, axis=-1)


def modulate(x, shift, scale):
    return x * (1 + scale) + shift


def axial_rope_tables(rows):
    r, col = jnp.meshgrid(jnp.arange(rows), jnp.arange(GRID_W), indexing="ij")
    r = r.reshape(-1).astype(jnp.float32)
    col = col.reshape(-1).astype(jnp.float32)
    n_freq = HEAD_DIM_A // 4
    inv = ROPE_BASE ** (-jnp.arange(n_freq, dtype=jnp.float32) / n_freq)
    ang = jnp.concatenate([r[:, None] * inv, col[:, None] * inv], -1)
    return jnp.cos(ang), jnp.sin(ang)


def apply_rope(x, cos, sin):
    half = x.shape[-1] // 2
    x1 = x[..., :half].astype(jnp.float32)
    x2 = x[..., half:].astype(jnp.float32)
    c = cos[None, :, None, :]
    s = sin[None, :, None, :]
    return jnp.concatenate([x1 * c - x2 * s, x2 * c + x1 * s], -1).astype(x.dtype)


def diff_attention(q1, q2, k1, k2, v, lam):
    nb, lq, h, d = q1.shape
    qb = math.gcd(lq, 128)
    nblk = lq // qb
    scale = d ** -0.5
    f32 = jnp.float32
    k1f, k2f, vf = k1.astype(f32), k2.astype(f32), v.astype(f32)

    def blocks(q):
        return jnp.moveaxis(q.reshape(nb, nblk, qb, h, d), 1, 0)

    def one_block(qs):
        qa, qc = qs
        s1 = jnp.einsum('bqhd,bkhd->bhqk', qa.astype(f32) * scale, k1f)
        s2 = jnp.einsum('bqhd,bkhd->bhqk', qc.astype(f32) * scale, k2f)
        p = jax.nn.softmax(s1, axis=-1) - lam * jax.nn.softmax(s2, axis=-1)
        return jnp.einsum('bhqk,bkhe->bqhe', p, vf)

    out = lax.map(one_block, (blocks(q1), blocks(q2)))
    return jnp.moveaxis(out, 0, 1).reshape(nb, lq, h, 2 * d).astype(v.dtype)


def centred_dwconv(x, w, bias):
    pad = D_CONV // 2
    y = lax.conv_general_dilated(x, w[:, None, :].astype(x.dtype), window_strides=(1,),
                                 padding=[(pad, pad)], dimension_numbers=('NWC', 'WIO', 'NWC'),
                                 feature_group_count=x.shape[-1])
    return y + bias


def ssd_scan(x, dt, a, bmat, cmat, h0):
    f32 = jnp.float32
    nb, L, H, P = x.shape
    q = math.gcd(L, CHUNK)
    nc = L // q
    rep = H // N_GROUPS_B
    bh = jnp.repeat(bmat.astype(f32), rep, axis=2).reshape(nb, nc, q, H, D_STATE)
    ch = jnp.repeat(cmat.astype(f32), rep, axis=2).reshape(nb, nc, q, H, D_STATE)
    xdt = (x.astype(f32) * dt[..., None]).reshape(nb, nc, q, H, P)
    acs = jnp.cumsum((dt * a).reshape(nb, nc, q, H), axis=2)
    seg = acs[:, :, :, None, :] - acs[:, :, None, :, :]
    causal = jnp.tril(jnp.ones((q, q), bool))[None, None, :, :, None]
    decay = jnp.where(causal, jnp.exp(jnp.where(causal, seg, 0.0)), 0.0)
    cb = jnp.einsum('bclhn,bcshn->bclsh', ch, bh) * decay
    y_diag = jnp.einsum('bclsh,bcshp->bclhp', cb, xdt)
    decay_to_end = jnp.exp(acs[:, :, -1:, :] - acs)
    chunk_states = jnp.einsum('bclhn,bclh,bclhp->bchpn', bh, decay_to_end, xdt)
    chunk_decay = jnp.exp(acs[:, :, -1, :])

    def step(hs, inp):
        st, dec = inp
        return hs * dec[:, :, None, None] + st, hs

    h_final, h_enter = lax.scan(step, h0.astype(f32),
                                (jnp.moveaxis(chunk_states, 1, 0), jnp.moveaxis(chunk_decay, 1, 0)))
    h_enter = jnp.moveaxis(h_enter, 0, 1)
    y_off = jnp.einsum('bclhn,bchpn,bclh->bclhp', ch, h_enter, jnp.exp(acs))
    return (y_diag + y_off).reshape(nb, L, H, P), h_final


def bidir_ssd(x, dt_f, dt_b, a_f, a_b, bm, cm, h0_f, h0_b):
    y_f, hf = ssd_scan(x, dt_f, a_f, bm, cm, h0_f)
    flip = lambda t: jnp.flip(t, axis=1)
    y_b, hb = ssd_scan(flip(x), flip(dt_b), a_b, flip(bm), flip(cm), h0_b)
    return y_f + flip(y_b), hf, hb


def ab_mixer(h, p, lam, lam_init, ctx, rope):
    f32 = jnp.float32
    nb, L, _ = h.shape
    u = h @ p["w_in"]
    sizes = (A_QK, A_QK, A_QK, A_QK, A_V, D_INNER_B, CONV_DIM_B)
    cuts = [int(s) for s in np.cumsum(sizes)]
    q1, q2, k1, k2, v, z, xbc, dt = jnp.split(u, cuts, axis=-1)
    q1, q2, k1, k2 = (t.reshape(nb, L, N_HEADS_A, HEAD_DIM_A) for t in (q1, q2, k1, k2))
    v = v.reshape(nb, L, N_HEADS_A, 2 * HEAD_DIM_A)
    if rope is not None:
        cos, sin = rope
        q1, q2, k1, k2 = (apply_rope(t, cos, sin) for t in (q1, q2, k1, k2))
    k_new = jnp.stack([k1, k2], axis=2)
    if ctx is None:
        k_all, v_all = k_new, v
        h0 = jnp.zeros((nb, 2, N_HEADS_B, HEAD_DIM_B, D_STATE), f32)
    else:
        cache_k, cache_v, state = ctx
        k_all = jnp.concatenate([cache_k.astype(k_new.dtype), k_new], axis=1)
        v_all = jnp.concatenate([cache_v.astype(v.dtype), v], axis=1)
        h0 = state
    o_a = diff_attention(q1, q2, k_all[:, :, 0], k_all[:, :, 1], v_all, lam)
    o_a = rms_norm(o_a, p["subln_w"]) * (1.0 - lam_init)
    xbc = jax.nn.silu(centred_dwconv(xbc, p["conv_w"], p["conv_b"]))
    xs, bm, cm = jnp.split(xbc, [D_INNER_B, D_INNER_B + N_GROUPS_B * D_STATE], axis=-1)
    xs = xs.reshape(nb, L, N_HEADS_B, HEAD_DIM_B)
    bm = bm.reshape(nb, L, N_GROUPS_B, D_STATE)
    cm = cm.reshape(nb, L, N_GROUPS_B, D_STATE)
    dt = jax.nn.softplus(dt.astype(f32) + p["dt_bias"].reshape(-1).astype(f32))
    a = -jnp.exp(p["a_log"].astype(f32))
    y, hf, hb = bidir_ssd(xs, dt[..., :N_HEADS_B], dt[..., N_HEADS_B:], a[0], a[1],
                          bm, cm, h0[:, 0], h0[:, 1])
    y = y + p["d_skip"][:, None] * xs
    y = rms_norm(y.reshape(nb, L, D_INNER_B) * jax.nn.silu(z), p["ssm_norm_w"])
    out = jnp.concatenate([o_a.reshape(nb, L, A_V), y.astype(h.dtype)], axis=-1) @ p["w_out"]
    return out, (k_new, v, jnp.stack([hf, hb], axis=1))


def multiscale_pool(u):
    f32 = jnp.float32
    nb, L, _ = u.shape
    uf = u.astype(f32)
    csum = jnp.pad(jnp.cumsum(uf, axis=1), ((0, 0), (1, 0), (0, 0)))
    t = jnp.arange(L)
    outs = []
    for g, w in enumerate(POOL_WINDOWS):
        lo = jnp.clip(t - w // 2, 0, L)
        hi = jnp.clip(t + w - w // 2, 0, L)
        sl = slice(g * POOL_GROUP_DIM, (g + 1) * POOL_GROUP_DIM)
        cs = csum[..., sl]
        mean = (cs[:, hi] - cs[:, lo]) / (hi - lo).astype(f32)[None, :, None]
        outs.append(mean - uf[..., sl])
    return jnp.concatenate(outs, axis=-1)


def pool_mixer(h, p):
    nb, L, _ = h.shape
    u = h @ p["w_in"]
    pooled = multiscale_pool(u).reshape(nb, L, N_POOL_GROUPS, POOL_GROUP_DIM)
    mixed = jnp.einsum('blgc,gce->blge', pooled, p["pool_w"].astype(jnp.float32))
    mixed = mixed.reshape(nb, L, D_MODEL).astype(h.dtype) * p["pool_scale"]
    return mixed @ p["w_out"]


def sq_relu_mlp(h, w_up, w_down):
    return jnp.square(jax.nn.relu(h @ w_up)) @ w_down


def setup_inputs(seed: int = 0) -> dict:
    key = jax.random.key(seed)
    keys = iter(jax.random.split(key, 64))
    f32 = jnp.float32

    def nrm(shape, s=1.0):
        return jax.random.normal(next(keys), shape, f32) * s

    def gain(shape):
        return 1.0 + nrm(shape, 0.05)

    def dt_bias(shape):
        u = jax.random.uniform(next(keys), shape, f32)
        dt = jnp.exp(u * (math.log(1e-1) - math.log(1e-3)) + math.log(1e-3))
        return dt + jnp.log(-jnp.expm1(-dt))

    inp = {}
    inp["x_prompt"] = nrm((BATCH, SEQ, D_MODEL))
    inp["x_sample"] = nrm((DEC_BATCH, DEC_SEQ, D_MODEL))
    inp["cache_k_l0"] = nrm((DEC_BATCH, PAST_LEN, 2, N_HEADS_A, HEAD_DIM_A))
    inp["cache_v_l0"] = nrm((DEC_BATCH, PAST_LEN, N_HEADS_A, 2 * HEAD_DIM_A))
    inp["state_ssm_l0"] = nrm((DEC_BATCH, 2, N_HEADS_B, HEAD_DIM_B, D_STATE))
    inp["c"] = nrm((DEC_BATCH, D_MODEL))
    inp["c_ctx"] = nrm((D_MODEL,))
    s_d = D_MODEL ** -0.5
    inp["w_mod_l0"] = nrm((D_MODEL, N_MOD * D_MODEL), 0.5 * s_d)
    inp["b_mod_l0"] = nrm((N_MOD * D_MODEL,), 0.01)
    inp["w_in_l0"] = nrm((D_MODEL, W_IN0), s_d)
    inp["conv_w_l0"] = nrm((D_CONV, CONV_DIM_B), D_CONV ** -0.5)
    inp["conv_b_l0"] = nrm((CONV_DIM_B,), 0.01)
    inp["a_log_l0"] = jnp.log(jax.random.uniform(next(keys), (2, N_HEADS_B), f32, 1.0, 16.0))
    inp["dt_bias_l0"] = dt_bias((2, N_HEADS_B))
    inp["d_skip_l0"] = gain((N_HEADS_B,))
    inp["ssm_norm_w_l0"] = gain((D_INNER_B,))
    inp["lam_q1_l0"] = nrm((HEAD_DIM_A,), 0.1)
    inp["lam_k1_l0"] = nrm((HEAD_DIM_A,), 0.1)
    inp["lam_q2_l0"] = nrm((HEAD_DIM_A,), 0.1)
    inp["lam_k2_l0"] = nrm((HEAD_DIM_A,), 0.1)
    inp["subln_w_l0"] = gain((2 * HEAD_DIM_A,))
    inp["w_out_l0"] = nrm((A_V + D_INNER_B, D_MODEL), BETA * (A_V + D_INNER_B) ** -0.5)
    inp["ln1_g_l0"] = gain((D_MODEL,))
    inp["ln1_b_l0"] = nrm((D_MODEL,), 0.01)
    inp["w_up_l0"] = nrm((D_MODEL, D_FF), s_d)
    inp["w_down_l0"] = nrm((D_FF, D_MODEL), BETA * D_FF ** -0.5)
    inp["ln2_g_l0"] = gain((D_MODEL,))
    inp["ln2_b_l0"] = nrm((D_MODEL,), 0.01)
    inp["w_mod_l1"] = nrm((D_MODEL, N_MOD * D_MODEL), 0.5 * s_d)
    inp["b_mod_l1"] = nrm((N_MOD * D_MODEL,), 0.01)
    inp["w_in_l1"] = nrm((D_MODEL, D_MODEL), s_d)
    inp["pool_w_l1"] = nrm((N_POOL_GROUPS, POOL_GROUP_DIM, POOL_GROUP_DIM), POOL_GROUP_DIM ** -0.5)
    inp["pool_scale_l1"] = gain((D_MODEL,))
    inp["w_out_l1"] = nrm((D_MODEL, D_MODEL), BETA * s_d)
    inp["ln1_g_l1"] = gain((D_MODEL,))
    inp["ln1_b_l1"] = nrm((D_MODEL,), 0.01)
    inp["w_up_l1"] = nrm((D_MODEL, D_FF), s_d)
    inp["w_down_l1"] = nrm((D_FF, D_MODEL), BETA * D_FF ** -0.5)
    inp["ln2_g_l1"] = gain((D_MODEL,))
    inp["ln2_b_l1"] = nrm((D_MODEL,), 0.01)
    return inp


def reference(x_prompt, x_sample, cache_k_l0, cache_v_l0, state_ssm_l0, c, c_ctx,
              w_mod_l0, b_mod_l0, w_in_l0, conv_w_l0, conv_b_l0, a_log_l0, dt_bias_l0,
              d_skip_l0, ssm_norm_w_l0, lam_q1_l0, lam_k1_l0, lam_q2_l0, lam_k2_l0,
              subln_w_l0, w_out_l0, ln1_g_l0, ln1_b_l0, w_up_l0, w_down_l0, ln2_g_l0, ln2_b_l0,
              w_mod_l1, b_mod_l1, w_in_l1, pool_w_l1, pool_scale_l1, w_out_l1,
              ln1_g_l1, ln1_b_l1, w_up_l1, w_down_l1, ln2_g_l1, ln2_b_l1):
    layers = (
        dict(w_mod=w_mod_l0, b_mod=b_mod_l0, w_in=w_in_l0, conv_w=conv_w_l0, conv_b=conv_b_l0,
             a_log=a_log_l0, dt_bias=dt_bias_l0, d_skip=d_skip_l0, ssm_norm_w=ssm_norm_w_l0,
             lam_q1=lam_q1_l0, lam_k1=lam_k1_l0, lam_q2=lam_q2_l0, lam_k2=lam_k2_l0,
             subln_w=subln_w_l0, w_out=w_out_l0, ln1_g=ln1_g_l0, ln1_b=ln1_b_l0,
             w_up=w_up_l0, w_down=w_down_l0, ln2_g=ln2_g_l0, ln2_b=ln2_b_l0,
             cache_k=cache_k_l0, cache_v=cache_v_l0, state_ssm=state_ssm_l0),
        dict(w_mod=w_mod_l1, b_mod=b_mod_l1, w_in=w_in_l1, pool_w=pool_w_l1,
             pool_scale=pool_scale_l1, w_out=w_out_l1, ln1_g=ln1_g_l1, ln1_b=ln1_b_l1,
             w_up=w_up_l1, w_down=w_down_l1, ln2_g=ln2_g_l1, ln2_b=ln2_b_l1),
    )
    rows = x_sample.shape[1] // GRID_W
    rope = axial_rope_tables(rows)
    xc, xl = x_prompt, x_sample
    for l in range(DEPTH):
        p = layers[l]
        mc = modulation(c_ctx, p["w_mod"], p["b_mod"])
        ml = modulation(c, p["w_mod"], p["b_mod"])
        hc = modulate(xc, mc[0], mc[1])
        hl = modulate(xl, ml[0], ml[1])
        if l % 2 == 0:
            lam_init = 0.8 - 0.6 * math.exp(-0.3 * l)
            lam = (jnp.exp(jnp.sum(p["lam_q1"] * p["lam_k1"]))
                   - jnp.exp(jnp.sum(p["lam_q2"] * p["lam_k2"]))).astype(jnp.float32) + lam_init
            dc, ctx_tensors = ab_mixer(hc, p, lam, lam_init, None, None)
            dl, _ = ab_mixer(hl, p, lam, lam_init, (p["cache_k"], p["cache_v"], p["state_ssm"]), rope)
            new_k_l0, new_v_l0, new_ssm_l0 = ctx_tensors
        else:
            dc = pool_mixer(hc, p)
            dl = pool_mixer(hl, p)
        xc = layer_norm(ALPHA * xc + mc[2] * dc, p["ln1_g"], p["ln1_b"])
        xl = layer_norm(ALPHA * xl + ml[2] * dl, p["ln1_g"], p["ln1_b"])
        fc = sq_relu_mlp(modulate(xc, mc[3], mc[4]), p["w_up"], p["w_down"])
        fl = sq_relu_mlp(modulate(xl, ml[3], ml[4]), p["w_up"], p["w_down"])
        xc = layer_norm(ALPHA * xc + mc[5] * fc, p["ln2_g"], p["ln2_b"])
        xl = layer_norm(ALPHA * xl + ml[5] * fl, p["ln2_g"], p["ln2_b"])
    return (xc, xl, new_k_l0, new_v_l0, new_ssm_l0)
```

```python
import functools
import math

import jax
import jax.numpy as jnp
from jax import lax
from jax.experimental import pallas as pl
from jax.experimental.pallas import tpu as pltpu

F32 = jnp.float32
BF16 = jnp.bfloat16

D_MODEL = 1024
N_MOD = 6
N_HEADS_A = 4
HEAD_DIM_A = 64
A_QK = N_HEADS_A * HEAD_DIM_A
A_V = 2 * A_QK
D_INNER_B = 512
HEAD_DIM_B = 64
N_HEADS_B = 8
D_STATE = 128
D_CONV = 5
CONV_DIM_B = 1024
CHUNK = 128
GRID_W = 64
POOL_WINDOWS = (2, 4, 8, 16)
POOL_GROUP_DIM = 256
D_FF = 4096
ROPE_BASE = 10000.0
LN_EPS = 1e-5
DEPTH = 2
ALPHA = (2 * DEPTH) ** 0.25
NEG_BIG = -1e30

LANES = 128
TOKEN_TILE = 512
FF_CHUNK = 1024
VMEM_LIMIT = 56 * 1024 * 1024


def _dot(a, b):
    return jnp.dot(a, b, preferred_element_type=F32)


def _dot_nt(a, b):
    return lax.dot_general(a, b, (((1,), (1,)), ((), ())), preferred_element_type=F32)


def _dot_exact_rhs(t01, a):
    a1 = a.astype(BF16)
    r1 = a - a1.astype(F32)
    a2 = r1.astype(BF16)
    a3 = (r1 - a2.astype(F32)).astype(BF16)
    return _dot(t01, a1) + _dot(t01, a2) + _dot(t01, a3)


def _sigmoid(x):
    return 1.0 / (1.0 + jnp.exp(-x))


def _silu(x):
    return x * _sigmoid(x)


def _layer_norm(x, g, b):
    mu = jnp.mean(x, axis=-1, keepdims=True)
    xc = x - mu
    var = jnp.mean(xc * xc, axis=-1, keepdims=True)
    return xc * lax.rsqrt(var + LN_EPS) * g + b


def _const_spec(shape):
    nd = len(shape)
    return pl.BlockSpec(shape, lambda *_: (0,) * nd, pipeline_mode=pl.Buffered(1))


def _params(n_axes=1):
    return pltpu.CompilerParams(dimension_semantics=("arbitrary",) * n_axes,
                                vmem_limit_bytes=VMEM_LIMIT)


def _mod_kernel(cond_ref, w_ref, b_ref, o_ref):
    c = cond_ref[...]
    s = _silu(c).astype(BF16)
    o_ref[...] = _dot(s, w_ref[...].astype(BF16)) + b_ref[...]


def _modulation(cond8, w_mod, b_mod):
    tn = 512
    n = w_mod.shape[1]
    out = pl.pallas_call(
        _mod_kernel,
        grid=(n // tn,),
        in_specs=[pl.BlockSpec((8, D_MODEL), lambda j: (0, 0)),
                  pl.BlockSpec((D_MODEL, tn), lambda j: (0, j)),
                  pl.BlockSpec((1, tn), lambda j: (0, j))],
        out_specs=pl.BlockSpec((8, tn), lambda j: (0, j)),
        out_shape=jax.ShapeDtypeStruct((8, n), F32),
        compiler_params=_params(),
        name="modulation",
    )(cond8, w_mod, b_mod.reshape(1, n))
    return out.reshape(8, N_MOD, D_MODEL)


def _mod_spec(row_of_tile):
    return pl.BlockSpec((1, N_MOD, D_MODEL), lambda i: (row_of_tile(i), 0, 0))


def _rope(t, cosf, sinf):
    lane = lax.broadcasted_iota(jnp.int32, t.shape, 1)
    first = (lane & (HEAD_DIM_A - 1)) < HEAD_DIM_A // 2
    width = t.shape[1]
    swapped = jnp.where(first, pltpu.roll(t, width - HEAD_DIM_A // 2, 1),
                        pltpu.roll(t, HEAD_DIM_A // 2, 1))
    return t * cosf + swapped * sinf


def _inproj0_kernel(*refs, rope):
    if rope:
        (x_ref, mod_ref, w_ref, wdt_ref, cos_ref, sin_ref,
         q_ref, k_ref, v_ref, z_ref, xbc_ref, dt_ref) = refs
    else:
        (x_ref, mod_ref, w_ref, wdt_ref,
         q_ref, k_ref, v_ref, z_ref, xbc_ref, dt_ref) = refs
    m = mod_ref[0]
    h = (x_ref[...] * (1.0 + m[1:2]) + m[0:1]).astype(BF16)
    q = _dot(h, w_ref[:, 0:512])
    k = _dot(h, w_ref[:, 512:1024])
    if rope:
        q = _rope(q, cos_ref[...], sin_ref[...])
        k = _rope(k, cos_ref[...], sin_ref[...])
    q_ref[...] = q
    k_ref[...] = k
    v_ref[...] = _dot(h, w_ref[:, 1024:1536])
    z_ref[...] = _dot(h, w_ref[:, 1536:2048])
    xbc_ref[...] = _dot(h, w_ref[:, 2048:3072])
    dt_ref[...] = _dot(h, wdt_ref[...])


def _inproj0(x, mod, row_of_tile, w_main, w_dt, rope_tables=None, tiles_per_seq=1):
    t = x.shape[0]
    tm = TOKEN_TILE
    row = lambda n: pl.BlockSpec((tm, n), lambda i: (i, 0))
    in_specs = [row(D_MODEL), _mod_spec(row_of_tile), _const_spec(w_main.shape), _const_spec(w_dt.shape)]
    args = [x, mod, w_main, w_dt]
    if rope_tables is not None:
        tab = pl.BlockSpec((tm, 512), lambda i: (i % tiles_per_seq, 0))
        in_specs += [tab, tab]
        args += list(rope_tables)
    widths = (512, 512, 512, 512, 1024, LANES)
    return pl.pallas_call(
        functools.partial(_inproj0_kernel, rope=rope_tables is not None),
        grid=(t // tm,),
        in_specs=in_specs,
        out_specs=[row(n) for n in widths],
        out_shape=[jax.ShapeDtypeStruct((t, n), F32) for n in widths],
        compiler_params=_params(),
        name="inproj0",
    )(*args)


def _attn_kernel(*refs, n_seg, lam_init):
    q_ref = refs[0]
    k_refs = refs[1:1 + n_seg]
    v_refs = refs[1 + n_seg:1 + 2 * n_seg]
    lam_ref, sub_ref, o_ref = refs[1 + 2 * n_seg:]
    lv = lam_ref[...]
    lam = (jnp.exp(jnp.sum(lv[0:1] * lv[1:2], axis=1, keepdims=True))
           - jnp.exp(jnp.sum(lv[2:3] * lv[3:4], axis=1, keepdims=True)) + lam_init)
    q = q_ref[...] * (HEAD_DIM_A ** -0.5)
    q1, q2 = q[:, 0:A_QK], q[:, A_QK:2 * A_QK]
    k1 = [r[:, 0:A_QK].astype(BF16) for r in k_refs]
    k2 = [r[:, A_QK:2 * A_QK].astype(BF16) for r in k_refs]
    lane = lax.broadcasted_iota(jnp.int32, (1, A_QK), 1)

    def softmax_parts(qm, ks):
        s = [_dot_nt(qm, kk) for kk in ks]
        mx = functools.reduce(jnp.maximum, [jnp.max(x, axis=1, keepdims=True) for x in s])
        e = [jnp.exp(x - mx) for x in s]
        den = functools.reduce(jnp.add, [jnp.sum(x, axis=1, keepdims=True) for x in e])
        return e, 1.0 / den

    for h in range(N_HEADS_A):
        mask = (lane >= h * HEAD_DIM_A) & (lane < (h + 1) * HEAD_DIM_A)
        e1, r1 = softmax_parts(jnp.where(mask, q1, 0.0).astype(BF16), k1)
        e2, r2 = softmax_parts(jnp.where(mask, q2, 0.0).astype(BF16), k2)
        o = None
        for s in range(n_seg):
            p = (e1[s] * r1 - lam * (e2[s] * r2)).astype(BF16)
            pv = _dot(p, v_refs[s][:, h * LANES:(h + 1) * LANES].astype(BF16))
            o = pv if o is None else o + pv
        ms = jnp.mean(o * o, axis=1, keepdims=True)
        o_ref[:, h * LANES:(h + 1) * LANES] = (o * lax.rsqrt(ms + 1e-5) * sub_ref[...]) * (1.0 - lam_init)


def _attention(q, ks, vs, kv_rows, lam_vecs, subln_w, n_batch, q_len, lam_init):
    tq = 256
    tiles = q_len // tq
    n_seg = len(ks)
    qspec = pl.BlockSpec((tq, 512), lambda b, i: (b * tiles + i, 0))
    kvspecs = [pl.BlockSpec((r, 512), lambda b, i: (b, 0)) for r in kv_rows]
    return pl.pallas_call(
        functools.partial(_attn_kernel, n_seg=n_seg, lam_init=lam_init),
        grid=(n_batch, tiles),
        in_specs=[qspec] + kvspecs + kvspecs + [pl.BlockSpec((4, HEAD_DIM_A), lambda b, i: (0, 0)),
                                               pl.BlockSpec((1, LANES), lambda b, i: (0, 0))],
        out_specs=qspec,
        out_shape=jax.ShapeDtypeStruct(q.shape, F32),
        compiler_params=_params(2),
        name="diff_attention",
    )(q, *ks, *vs, lam_vecs, subln_w.reshape(1, LANES))


def _ssd_kernel(*refs, seq_len, has_h0):
    if has_h0:
        (xbc_ref, dt_ref, z_ref, h0_ref, cw_ref, cb_ref, alog_ref, dtb_ref, dsk_ref, nw_ref,
         y_ref, xpad, xact, dtv, yf, yb, htf, htb) = refs
        hfin_ref = None
    else:
        (xbc_ref, dt_ref, z_ref, cw_ref, cb_ref, alog_ref, dtb_ref, dsk_ref, nw_ref,
         y_ref, hfin_ref, xpad, xact, dtv, yf, yb, htf, htb) = refs
        h0_ref = None
    L = seq_len
    nc = L // CHUNK
    pad = 8

    xpad[0:pad, :] = jnp.zeros((pad, CONV_DIM_B), F32)
    xpad[pad + L:pad + L + pad, :] = jnp.zeros((pad, CONV_DIM_B), F32)
    xpad[pad:pad + L, :] = xbc_ref[...]
    for c in range(nc):
        r0 = c * CHUNK
        acc = cb_ref[...] + cw_ref[0:1, :] * xpad[r0 + pad - 2:r0 + pad - 2 + CHUNK, :]
        for j in range(1, D_CONV):
            acc = acc + cw_ref[j:j + 1, :] * xpad[r0 + pad - 2 + j:r0 + pad - 2 + j + CHUNK, :]
        xact[r0:r0 + CHUNK, :] = _silu(acc)

    t = dt_ref[...] + dtb_ref[...]
    dtv[...] = jnp.maximum(t, 0.0) + jnp.log(1.0 + jnp.exp(-jnp.abs(t)))
    a_row = -jnp.exp(alog_ref[...])

    for d, ht in ((0, htf), (1, htb)):
        for qd in range(4):
            if has_h0:
                ht[:, qd * LANES:(qd + 1) * LANES] = h0_ref[0, d, qd * LANES:(qd + 1) * LANES, :].T
            else:
                ht[:, qd * LANES:(qd + 1) * LANES] = jnp.zeros((D_STATE, LANES), F32)

    row_i = lax.broadcasted_iota(jnp.int32, (CHUNK, CHUNK), 0)
    col_i = lax.broadcasted_iota(jnp.int32, (CHUNK, CHUNK), 1)
    lower = col_i <= row_i
    upper = col_i >= row_i
    tri = (jnp.where(lower, 1.0, 0.0).astype(BF16), jnp.where(upper, 1.0, 0.0).astype(BF16))
    lo_half = col_i < HEAD_DIM_B

    def lane_column(mat, idx):
        return jnp.sum(jnp.where(col_i == idx, mat, 0.0), axis=1, keepdims=True)

    def process(c, d, ht, yout):
        r0 = pl.multiple_of(c * CHUNK, CHUNK)
        rows = pl.ds(r0, CHUNK)
        causal = lower if d == 0 else upper
        dtc = dtv[rows, :]
        col = _dot_exact_rhs(tri[d], dtc * a_row)
        rowm = col.T
        dtrow = dtc.T
        tot = lane_column(rowm, CHUNK - 1 if d == 0 else 0)
        dte = jnp.exp(tot - rowm) * dtrow
        cdec = jnp.exp(tot)
        for g in range(2):
            bg = xact[rows, 512 + g * LANES:512 + (g + 1) * LANES]
            cg = xact[rows, 768 + g * LANES:768 + (g + 1) * LANES]
            cbm = _dot_nt(cg.astype(BF16), bg.astype(BF16))
            bgt = bg.T
            for pq in range(2):
                pi = g * 2 + pq
                lanes = slice(pi * LANES, (pi + 1) * LANES)
                a_parts, ce_parts, bw_parts, cd_parts = [], [], [], []
                for hh in (2 * pi, 2 * pi + 1):
                    hd = 8 * d + hh
                    colb = lane_column(col, hd)
                    rowb = rowm[hd:hd + 1, :]
                    dec = jnp.exp(jnp.where(causal, colb - rowb, NEG_BIG))
                    a_parts.append(cbm * dec * dtrow[hd:hd + 1, :])
                    ce_parts.append(cg * jnp.exp(colb))
                    bw_parts.append(bgt * dte[hd:hd + 1, :])
                    cd_parts.append(cdec[hd:hd + 1, :])
                xs = xact[rows, lanes]
                hprev = ht[:, lanes]
                zero = jnp.zeros_like(xs)
                rhs = jnp.concatenate([jnp.where(lo_half, xs, zero), jnp.where(lo_half, zero, xs),
                                       jnp.where(lo_half, hprev, zero), jnp.where(lo_half, zero, hprev)],
                                      axis=0).astype(BF16)
                lhs = jnp.concatenate(a_parts + ce_parts, axis=1).astype(BF16)
                yout[rows, lanes] = _dot(lhs, rhs)
                st = _dot(jnp.concatenate(bw_parts, axis=1).astype(BF16), rhs[0:2 * CHUNK])
                cd = jnp.where(lo_half[0:1, :], cd_parts[0], cd_parts[1])
                ht[:, lanes] = hprev * cd + st

    def body(j, carry):
        process(j, 0, htf, yf)
        process(nc - 1 - j, 1, htb, yb)
        return carry

    lax.fori_loop(0, nc, body, 0)

    for c in range(nc):
        rows = slice(c * CHUNK, (c + 1) * CHUNK)
        y = yf[rows, :] + yb[rows, :] + dsk_ref[...] * xact[rows, 0:D_INNER_B]
        y = y * _silu(z_ref[rows, :])
        ms = jnp.mean(y * y, axis=1, keepdims=True)
        y_ref[rows, :] = y * lax.rsqrt(ms + 1e-5) * nw_ref[...]

    if hfin_ref is not None:
        for d, ht in ((0, htf), (1, htb)):
            for qd in range(4):
                hfin_ref[0, d, qd * LANES:(qd + 1) * LANES, :] = ht[:, qd * LANES:(qd + 1) * LANES].T


def _ssd(xbc, dt, z, h0, consts, n_batch, seq_len):
    L = seq_len
    row = lambda n: pl.BlockSpec((L, n), lambda b: (b, 0))
    state_spec = pl.BlockSpec((1, 2, D_INNER_B, D_STATE), lambda b: (b, 0, 0, 0))
    has_h0 = h0 is not None
    in_specs = [row(CONV_DIM_B), row(LANES), row(D_INNER_B)]
    args = [xbc, dt, z]
    if has_h0:
        in_specs.append(state_spec)
        args.append(h0)
    for cst in consts:
        in_specs.append(pl.BlockSpec(cst.shape, lambda b: (0, 0)))
        args.append(cst)
    out_specs = [row(D_INNER_B)]
    out_shape = [jax.ShapeDtypeStruct((n_batch * L, D_INNER_B), F32)]
    if not has_h0:
        out_specs.append(state_spec)
        out_shape.append(jax.ShapeDtypeStruct((n_batch, 2, D_INNER_B, D_STATE), F32))
    scratch = [pltpu.VMEM((L + 16, CONV_DIM_B), F32), pltpu.VMEM((L, CONV_DIM_B), F32),
               pltpu.VMEM((L, LANES), F32), pltpu.VMEM((L, D_INNER_B), F32),
               pltpu.VMEM((L, D_INNER_B), F32), pltpu.VMEM((D_STATE, D_INNER_B), F32),
               pltpu.VMEM((D_STATE, D_INNER_B), F32)]
    return pl.pallas_call(
        functools.partial(_ssd_kernel, seq_len=L, has_h0=has_h0),
        grid=(n_batch,),
        in_specs=in_specs,
        out_specs=out_specs,
        out_shape=out_shape,
        scratch_shapes=scratch,
        compiler_params=_params(),
        name="bidir_ssd",
    )(*args)


def _tail_kernel(*refs, n_in):
    a_refs = refs[:n_in]
    (x_ref, mod_ref, wout_ref, g1_ref, b1_ref, wup_ref, wdown_ref, g2_ref, b2_ref, o_ref) = refs[n_in:]
    m = mod_ref[0]
    x = x_ref[...]
    d = None
    off = 0
    for a_ref in a_refs:
        n = a_ref.shape[1]
        part = _dot(a_ref[...].astype(BF16), wout_ref[off:off + n, :])
        d = part if d is None else d + part
        off += n
    x1 = _layer_norm(ALPHA * x + m[2:3] * d, g1_ref[...], b1_ref[...])
    h = (x1 * (1.0 + m[4:5]) + m[3:4]).astype(BF16)
    acc = None
    for f in range(0, D_FF, FF_CHUNK):
        u = jnp.maximum(_dot(h, wup_ref[:, f:f + FF_CHUNK]), 0.0)
        part = _dot((u * u).astype(BF16), wdown_ref[f:f + FF_CHUNK, :])
        acc = part if acc is None else acc + part
    o_ref[...] = _layer_norm(ALPHA * x1 + m[5:6] * acc, g2_ref[...], b2_ref[...])


def _tail(a_list, x, mod, row_of_tile, w_out, ln1_g, ln1_b, w_up, w_down, ln2_g, ln2_b):
    t = x.shape[0]
    tm = TOKEN_TILE
    row = lambda n: pl.BlockSpec((tm, n), lambda i: (i, 0))
    vec = lambda v: v.reshape(1, D_MODEL)
    consts = [w_out, vec(ln1_g), vec(ln1_b), w_up, w_down, vec(ln2_g), vec(ln2_b)]
    return pl.pallas_call(
        functools.partial(_tail_kernel, n_in=len(a_list)),
        grid=(t // tm,),
        in_specs=[row(a.shape[1]) for a in a_list] + [row(D_MODEL), _mod_spec(row_of_tile)]
                 + [_const_spec(cst.shape) for cst in consts],
        out_specs=row(D_MODEL),
        out_shape=jax.ShapeDtypeStruct((t, D_MODEL), F32),
        compiler_params=_params(),
        name="outproj_mlp",
    )(*a_list, x, mod, *consts)


def _pool_kernel(x_ref, mod_ref, win_ref, pw_ref, ps_ref, o_ref, *, seq_len):
    L = seq_len
    m = mod_ref[0]
    h = (x_ref[...] * (1.0 + m[1:2]) + m[0:1]).astype(BF16)
    u = _dot(h, win_ref[...])
    t_i = lax.broadcasted_iota(jnp.int32, (L, L), 0)
    s_i = lax.broadcasted_iota(jnp.int32, (L, L), 1)
    t_col = lax.broadcasted_iota(jnp.int32, (L, 1), 0)
    for g, w in enumerate(POOL_WINDOWS):
        lo, hi = w // 2, w - w // 2
        band = jnp.where((s_i >= t_i - lo) & (s_i < t_i + hi), 1.0, 0.0).astype(BF16)
        cnt = (jnp.minimum(t_col + hi, L) - jnp.maximum(t_col - lo, 0)).astype(F32)
        cols = slice(g * POOL_GROUP_DIM, (g + 1) * POOL_GROUP_DIM)
        for s in range(x_ref.shape[0] // L):
            rows = slice(s * L, (s + 1) * L)
            ug = u[rows, cols]
            ug_hi = ug.astype(BF16)
            ug_lo = (ug - ug_hi.astype(F32)).astype(BF16)
            win_sum = _dot(band, ug_hi) + _dot(band, ug_lo)
            pooled = win_sum / cnt - ug
            o_ref[rows, cols] = _dot(pooled.astype(BF16), pw_ref[g]) * ps_ref[:, cols]


def _pool_mixer(x, mod, row_of_tile, w_in, pool_w, pool_scale, seq_len, tm):
    t = x.shape[0]
    row = pl.BlockSpec((tm, D_MODEL), lambda i: (i, 0))
    consts = [w_in, pool_w, pool_scale.reshape(1, D_MODEL)]
    return pl.pallas_call(
        functools.partial(_pool_kernel, seq_len=seq_len),
        grid=(t // tm,),
        in_specs=[row, _mod_spec(row_of_tile)] + [_const_spec(cst.shape) for cst in consts],
        out_specs=row,
        out_shape=jax.ShapeDtypeStruct((t, D_MODEL), F32),
        compiler_params=_params(),
        name="pool_mixer",
    )(x, mod, *consts)


def _rope_tables(rows):
    r, col = jnp.meshgrid(jnp.arange(rows), jnp.arange(GRID_W), indexing="ij")
    r = r.reshape(-1).astype(F32)
    col = col.reshape(-1).astype(F32)
    n_freq = HEAD_DIM_A // 4
    inv = ROPE_BASE ** (-jnp.arange(n_freq, dtype=F32) / n_freq)
    ang = jnp.concatenate([r[:, None] * inv, col[:, None] * inv], -1)
    cos, sin = jnp.cos(ang), jnp.sin(ang)
    reps = 512 // HEAD_DIM_A
    return (jnp.tile(jnp.concatenate([cos, cos], -1), (1, reps)),
            jnp.tile(jnp.concatenate([-sin, sin], -1), (1, reps)))


def kernel(x_prompt, x_sample, cache_k_l0, cache_v_l0, state_ssm_l0, c, c_ctx, w_mod_l0, b_mod_l0, w_in_l0, conv_w_l0, conv_b_l0, a_log_l0, dt_bias_l0, d_skip_l0, ssm_norm_w_l0, lam_q1_l0, lam_k1_l0, lam_q2_l0, lam_k2_l0, subln_w_l0, w_out_l0, ln1_g_l0, ln1_b_l0, w_up_l0, w_down_l0, ln2_g_l0, ln2_b_l0, w_mod_l1, b_mod_l1, w_in_l1, pool_w_l1, pool_scale_l1, w_out_l1, ln1_g_l1, ln1_b_l1, w_up_l1, w_down_l1, ln2_g_l1, ln2_b_l1):
    n_ctx, l_ctx, _ = x_prompt.shape
    n_lat, l_lat, _ = x_sample.shape
    past = cache_k_l0.shape[1]
    xc = x_prompt.reshape(n_ctx * l_ctx, D_MODEL)
    xl = x_sample.reshape(n_lat * l_lat, D_MODEL)

    cond8 = jnp.concatenate([c_ctx[None, :], c, jnp.zeros((8 - 1 - n_lat, D_MODEL), F32)], axis=0)
    ctx_row = lambda i: 0
    lat_tiles = l_lat // TOKEN_TILE
    lat_row = lambda i: 1 + i // lat_tiles

    mod0 = _modulation(cond8, w_mod_l0, b_mod_l0)
    n_main = 4 * A_QK + A_V + D_INNER_B + CONV_DIM_B
    w_main = w_in_l0[:, :n_main].astype(BF16)
    w_dt = jnp.pad(w_in_l0[:, n_main:], ((0, 0), (0, LANES - 2 * N_HEADS_B))).astype(BF16)
    lane_pad = lambda v: jnp.pad(v.reshape(1, -1), ((0, 0), (0, LANES - 2 * N_HEADS_B)))
    ssd_consts = [conv_w_l0, conv_b_l0.reshape(1, CONV_DIM_B), lane_pad(a_log_l0), lane_pad(dt_bias_l0),
                  jnp.repeat(d_skip_l0, HEAD_DIM_B).reshape(1, D_INNER_B),
                  ssm_norm_w_l0.reshape(1, D_INNER_B)]
    lam_vecs = jnp.stack([lam_q1_l0, lam_k1_l0, lam_q2_l0, lam_k2_l0], axis=0)
    lam_init = 0.8 - 0.6 * math.exp(-0.3 * 0)
    tail0 = (w_out_l0.astype(BF16), ln1_g_l0, ln1_b_l0, w_up_l0.astype(BF16), w_down_l0.astype(BF16),
             ln2_g_l0, ln2_b_l0)

    qc, kc, vc, zc, xbc_c, dt_c = _inproj0(xc, mod0, ctx_row, w_main, w_dt)
    oa_c = _attention(qc, [kc], [vc], [l_ctx], lam_vecs, subln_w_l0, n_ctx, l_ctx, lam_init)
    y_c, new_ssm = _ssd(xbc_c, dt_c, zc, None, ssd_consts, n_ctx, l_ctx)
    xc = _tail([oa_c, y_c], xc, mod0, ctx_row, *tail0)

    ql, kl, vl, zl, xbc_l, dt_l = _inproj0(xl, mod0, lat_row, w_main, w_dt,
                                           rope_tables=_rope_tables(l_lat // GRID_W), tiles_per_seq=lat_tiles)
    ck = cache_k_l0.reshape(n_lat * past, 2 * A_QK)
    cv = cache_v_l0.reshape(n_lat * past, A_V)
    oa_l = _attention(ql, [ck, kl], [cv, vl], [past, l_lat], lam_vecs, subln_w_l0, n_lat, l_lat, lam_init)
    h0 = state_ssm_l0.reshape(n_lat, 2, D_INNER_B, D_STATE)
    (y_l,) = _ssd(xbc_l, dt_l, zl, h0, ssd_consts, n_lat, l_lat)
    xl = _tail([oa_l, y_l], xl, mod0, lat_row, *tail0)

    mod1 = _modulation(cond8, w_mod_l1, b_mod_l1)
    w_in1 = w_in_l1.astype(BF16)
    pool_w = pool_w_l1.astype(BF16)
    tail1 = (w_out_l1.astype(BF16), ln1_g_l1, ln1_b_l1, w_up_l1.astype(BF16), w_down_l1.astype(BF16),
             ln2_g_l1, ln2_b_l1)
    mix_c = _pool_mixer(xc, mod1, ctx_row, w_in1, pool_w, pool_scale_l1, l_ctx, TOKEN_TILE)
    xc = _tail([mix_c], xc, mod1, ctx_row, *tail1)
    mix_l = _pool_mixer(xl, mod1, lambda i: 1 + i, w_in1, pool_w, pool_scale_l1, l_lat, l_lat)
    xl = _tail([mix_l], xl, mod1, lat_row, *tail1)

    return (xc.reshape(n_ctx, l_ctx, D_MODEL), xl.reshape(n_lat, l_lat, D_MODEL),
            kc.reshape(n_ctx, l_ctx, 2, N_HEADS_A, HEAD_DIM_A),
            vc.reshape(n_ctx, l_ctx, N_HEADS_A, 2 * HEAD_DIM_A),
            new_ssm.reshape(n_ctx, 2, N_HEADS_B, HEAD_DIM_B, D_STATE))
```

```python
import functools
import math

import jax
import jax.numpy as jnp
from jax import lax
from jax.experimental import pallas as pl
from jax.experimental.pallas import tpu as pltpu

F32 = jnp.float32
BF16 = jnp.bfloat16

D_MODEL = 1024
N_MOD = 6
N_HEADS_A = 4
HEAD_DIM_A = 64
A_QK = N_HEADS_A * HEAD_DIM_A
A_V = 2 * A_QK
D_INNER_B = 512
HEAD_DIM_B = 64
N_HEADS_B = 8
D_STATE = 128
D_CONV = 5
CONV_DIM_B = 1024
CHUNK = 128
GRID_W = 64
POOL_WINDOWS = (2, 4, 8, 16)
POOL_GROUP_DIM = 256
D_FF = 4096
ROPE_BASE = 10000.0
LN_EPS = 1e-5
DEPTH = 2
ALPHA = (2 * DEPTH) ** 0.25
NEG_BIG = -1e30

LANES = 128
TOKEN_TILE = 512
FF_CHUNK = 1024
VMEM_LIMIT = 56 * 1024 * 1024


def _dot(a, b):
    return jnp.dot(a, b, preferred_element_type=F32)


def _dot_nt(a, b):
    return lax.dot_general(a, b, (((1,), (1,)), ((), ())), preferred_element_type=F32)


def _bf16_pieces(a):
    a1 = a.astype(BF16)
    r1 = a - a1.astype(F32)
    a2 = r1.astype(BF16)
    a3 = (r1 - a2.astype(F32)).astype(BF16)
    return a1, a2, a3


def _dot_exact_rhs(t01, a):
    a1, a2, a3 = _bf16_pieces(a)
    return _dot(t01, a1) + _dot(t01, a2) + _dot(t01, a3)


def _sigmoid(x):
    return 1.0 / (1.0 + jnp.exp(-x))


def _silu(x):
    return x * _sigmoid(x)


def _layer_norm(x, g, b):
    mu = jnp.mean(x, axis=-1, keepdims=True)
    xc = x - mu
    var = jnp.mean(xc * xc, axis=-1, keepdims=True)
    return xc * lax.rsqrt(var + LN_EPS) * g + b


def _const_spec(shape):
    nd = len(shape)
    return pl.BlockSpec(shape, lambda *_: (0,) * nd, pipeline_mode=pl.Buffered(1))


def _params(n_axes=1):
    return pltpu.CompilerParams(dimension_semantics=("arbitrary",) * n_axes,
                                vmem_limit_bytes=VMEM_LIMIT)


def _mod_kernel(cond_ref, w_ref, b_ref, o_ref):
    c = cond_ref[...]
    s = _silu(c).astype(BF16)
    o_ref[...] = _dot(s, w_ref[...].astype(BF16)) + b_ref[...]


def _modulation(cond8, w_mod, b_mod):
    tn = 512
    n = w_mod.shape[1]
    out = pl.pallas_call(
        _mod_kernel,
        grid=(n // tn,),
        in_specs=[pl.BlockSpec((8, D_MODEL), lambda j: (0, 0)),
                  pl.BlockSpec((D_MODEL, tn), lambda j: (0, j)),
                  pl.BlockSpec((1, tn), lambda j: (0, j))],
        out_specs=pl.BlockSpec((8, tn), lambda j: (0, j)),
        out_shape=jax.ShapeDtypeStruct((8, n), F32),
        compiler_params=_params(),
        name="modulation",
    )(cond8, w_mod, b_mod.reshape(1, n))
    return out.reshape(8, N_MOD, D_MODEL)


def _mod_spec(row_of_tile):
    return pl.BlockSpec((1, N_MOD, D_MODEL), lambda i: (row_of_tile(i), 0, 0))


def _rope(t, cosf, sinf):
    lane = lax.broadcasted_iota(jnp.int32, t.shape, 1)
    first = (lane & (HEAD_DIM_A - 1)) < HEAD_DIM_A // 2
    width = t.shape[1]
    swapped = jnp.where(first, pltpu.roll(t, width - HEAD_DIM_A // 2, 1),
                        pltpu.roll(t, HEAD_DIM_A // 2, 1))
    return t * cosf + swapped * sinf


def _conv_silu(xbc, cw_ref, cb_ref, seq_len):
    rows = xbc.shape[0]
    pos = lax.rem(lax.broadcasted_iota(jnp.int32, xbc.shape, 0), seq_len)
    half = D_CONV // 2
    acc = cb_ref[...] + cw_ref[half:half + 1, :] * xbc
    for j in range(D_CONV):
        o = j - half
        if o == 0:
            continue
        shifted = pltpu.roll(xbc, (-o) % rows, 0)
        valid = (pos >= -o) if o < 0 else (pos < seq_len - o)
        acc = acc + cw_ref[j:j + 1, :] * jnp.where(valid, shifted, 0.0)
    return _silu(acc)


def _softplus(t):
    return jnp.maximum(t, 0.0) + jnp.log(1.0 + jnp.exp(-jnp.abs(t)))


def _inproj0_kernel(*refs, rope, seq_len):
    if rope:
        (x_ref, mod_ref, w_ref, wdt_ref, cw_ref, cb_ref, dtb_ref, cos_ref, sin_ref,
         q_ref, k_ref, v_ref, z_ref, xbc_ref, dt_ref) = refs
    else:
        (x_ref, mod_ref, w_ref, wdt_ref, cw_ref, cb_ref, dtb_ref,
         q_ref, k_ref, v_ref, z_ref, xbc_ref, dt_ref) = refs
    m = mod_ref[0]
    h = (x_ref[...] * (1.0 + m[1:2]) + m[0:1]).astype(BF16)
    q = _dot(h, w_ref[:, 0:512])
    k = _dot(h, w_ref[:, 512:1024])
    if rope:
        q = _rope(q, cos_ref[...], sin_ref[...])
        k = _rope(k, cos_ref[...], sin_ref[...])
    q_ref[...] = q
    k_ref[...] = k
    v_ref[...] = _dot(h, w_ref[:, 1024:1536])
    z_ref[...] = _dot(h, w_ref[:, 1536:2048])
    xbc_ref[...] = _conv_silu(_dot(h, w_ref[:, 2048:3072]), cw_ref, cb_ref, seq_len)
    dt_ref[...] = _softplus(_dot(h, wdt_ref[...]) + dtb_ref[...])


def _inproj0(x, mod, row_of_tile, w_main, w_dt, conv_w, conv_b, dt_bias, seq_len, tm, rope_tables=None):
    t = x.shape[0]
    assert tm % seq_len == 0
    row = lambda n: pl.BlockSpec((tm, n), lambda i: (i, 0))
    consts = [w_main, w_dt, conv_w, conv_b, dt_bias]
    in_specs = [row(D_MODEL), _mod_spec(row_of_tile)] + [_const_spec(cst.shape) for cst in consts]
    args = [x, mod] + consts
    if rope_tables is not None:
        assert tm == seq_len
        tab = pl.BlockSpec((tm, 512), lambda i: (0, 0))
        in_specs += [tab, tab]
        args += list(rope_tables)
    widths = (512, 512, 512, 512, 1024, LANES)
    return pl.pallas_call(
        functools.partial(_inproj0_kernel, rope=rope_tables is not None, seq_len=seq_len),
        grid=(t // tm,),
        in_specs=in_specs,
        out_specs=[row(n) for n in widths],
        out_shape=[jax.ShapeDtypeStruct((t, n), F32) for n in widths],
        compiler_params=_params(),
        name="inproj0",
    )(*args)


def _attn_kernel(*refs, n_seg, lam_init):
    q_ref = refs[0]
    k_refs = refs[1:1 + n_seg]
    v_refs = refs[1 + n_seg:1 + 2 * n_seg]
    lam_ref, sub_ref, o_ref = refs[1 + 2 * n_seg:]
    lv = lam_ref[...]
    lam = (jnp.exp(jnp.sum(lv[0:1] * lv[1:2], axis=1, keepdims=True))
           - jnp.exp(jnp.sum(lv[2:3] * lv[3:4], axis=1, keepdims=True)) + lam_init)
    q = q_ref[...] * (HEAD_DIM_A ** -0.5)
    q1, q2 = q[:, 0:A_QK], q[:, A_QK:2 * A_QK]
    k1 = [r[:, 0:A_QK].astype(BF16) for r in k_refs]
    k2 = [r[:, A_QK:2 * A_QK].astype(BF16) for r in k_refs]
    lane = lax.broadcasted_iota(jnp.int32, (1, A_QK), 1)

    def softmax_parts(qm, ks):
        s = [_dot_nt(qm, kk) for kk in ks]
        mx = functools.reduce(jnp.maximum, [jnp.max(x, axis=1, keepdims=True) for x in s])
        e = [jnp.exp(x - mx) for x in s]
        den = functools.reduce(jnp.add, [jnp.sum(x, axis=1, keepdims=True) for x in e])
        return e, 1.0 / den

    for h in range(N_HEADS_A):
        mask = (lane >= h * HEAD_DIM_A) & (lane < (h + 1) * HEAD_DIM_A)
        e1, r1 = softmax_parts(jnp.where(mask, q1, 0.0).astype(BF16), k1)
        e2, r2 = softmax_parts(jnp.where(mask, q2, 0.0).astype(BF16), k2)
        o = None
        for s in range(n_seg):
            p = (e1[s] * r1 - lam * (e2[s] * r2)).astype(BF16)
            pv = _dot(p, v_refs[s][:, h * LANES:(h + 1) * LANES].astype(BF16))
            o = pv if o is None else o + pv
        ms = jnp.mean(o * o, axis=1, keepdims=True)
        o_ref[:, h * LANES:(h + 1) * LANES] = (o * lax.rsqrt(ms + 1e-5) * sub_ref[...]) * (1.0 - lam_init)


def _attention(q, ks, vs, kv_rows, lam_vecs, subln_w, n_batch, q_len, lam_init):
    tq = 256
    tiles = q_len // tq
    n_seg = len(ks)
    qspec = pl.BlockSpec((tq, 512), lambda b, i: (b * tiles + i, 0))
    kvspecs = [pl.BlockSpec((r, 512), lambda b, i: (b, 0)) for r in kv_rows]
    return pl.pallas_call(
        functools.partial(_attn_kernel, n_seg=n_seg, lam_init=lam_init),
        grid=(n_batch, tiles),
        in_specs=[qspec] + kvspecs + kvspecs + [pl.BlockSpec((4, HEAD_DIM_A), lambda b, i: (0, 0)),
                                               pl.BlockSpec((1, LANES), lambda b, i: (0, 0))],
        out_specs=qspec,
        out_shape=jax.ShapeDtypeStruct(q.shape, F32),
        compiler_params=_params(2),
        name="diff_attention",
    )(q, *ks, *vs, lam_vecs, subln_w.reshape(1, LANES))


def _chunk_loop(n, body):
    if n <= 2:
        for c in range(n):
            body(c)
    else:
        def step(c, carry):
            body(c)
            return carry
        lax.fori_loop(0, n, step, 0)


def _ssd_kernel(*refs, seq_len, has_h0):
    if has_h0:
        (xact, dtv, z_ref, h0_ref, arow_ref, dsk_ref, nw_ref,
         y_ref, ysc, s_sc, cd_sc, ce_sc, htf, htb) = refs
        hfin_ref = None
    else:
        (xact, dtv, z_ref, arow_ref, dsk_ref, nw_ref,
         y_ref, hfin_ref, ysc, s_sc, cd_sc, ce_sc, htf, htb) = refs
        h0_ref = None
    nc = seq_len // CHUNK
    n_pairs = N_HEADS_B // 2
    a_row = -jnp.exp(arow_ref[...])

    row_i = lax.broadcasted_iota(jnp.int32, (CHUNK, CHUNK), 0)
    col_i = lax.broadcasted_iota(jnp.int32, (CHUNK, CHUNK), 1)
    lower = col_i <= row_i
    upper = col_i >= row_i
    tri = (jnp.where(lower, 1.0, 0.0).astype(BF16), jnp.where(upper, 1.0, 0.0).astype(BF16))
    lo_half = col_i < HEAD_DIM_B
    col16_i = lax.broadcasted_iota(jnp.int32, (16, CHUNK), 1)

    def lane_column(mat, idx, lane_ids):
        return jnp.sum(jnp.where(lane_ids == idx, mat, 0.0), axis=1, keepdims=True)

    def split_pair(m):
        zero = jnp.zeros_like(m)
        return jnp.concatenate([jnp.where(lo_half, m, zero), jnp.where(lo_half, zero, m)], axis=0).astype(BF16)

    def local_phase(c):
        rows = pl.ds(pl.multiple_of(c * CHUNK, CHUNK), CHUNK)
        dtc = dtv[rows, :]
        dtt = dtc.T[0:16, :]
        per_dir = []
        for d in range(2):
            col = _dot_exact_rhs(tri[d], dtc * a_row)
            rowm = col.T[0:16, :]
            tot = jnp.broadcast_to(lane_column(rowm, CHUNK - 1 if d == 0 else 0, col16_i),
                                   (16, CHUNK))
            dte = jnp.exp(tot - rowm) * dtt
            per_dir.append((col, rowm, dte, jnp.exp(tot)))
        for g in range(2):
            bg = xact[rows, 512 + g * LANES:512 + (g + 1) * LANES]
            cg = xact[rows, 768 + g * LANES:768 + (g + 1) * LANES]
            cbm = _dot_nt(cg.astype(BF16), bg.astype(BF16))
            bgt = bg.T
            for pq in range(2):
                pi = g * 2 + pq
                lanes = slice(pi * LANES, (pi + 1) * LANES)
                xrhs = split_pair(xact[rows, lanes])
                y_pair = None
                for d in range(2):
                    col, rowm, dte, cdec = per_dir[d]
                    causal = lower if d == 0 else upper
                    m_parts, ce_parts, bw_parts, cd_parts = [], [], [], []
                    for hh in (2 * pi, 2 * pi + 1):
                        hd = 8 * d + hh
                        colb = lane_column(col, hd, col_i)
                        dec = jnp.exp(jnp.where(causal, colb - rowm[hd:hd + 1, :], NEG_BIG))
                        m_parts.append(cbm * dec * dtt[hd:hd + 1, :])
                        ce_parts.append(cg * jnp.exp(colb))
                        bw_parts.append(bgt * dte[hd:hd + 1, :])
                        cd_parts.append(cdec[hd:hd + 1, :])
                    yd = _dot(jnp.concatenate(m_parts, axis=1).astype(BF16), xrhs)
                    y_pair = yd if y_pair is None else y_pair + yd
                    s_sc[c, d, :, lanes] = _dot(jnp.concatenate(bw_parts, axis=1).astype(BF16), xrhs)
                    ce_sc[c, d, :, pi * 2 * LANES:(pi + 1) * 2 * LANES] = (
                        jnp.concatenate(ce_parts, axis=1).astype(BF16))
                    cd_sc[c, d, :, lanes] = jnp.broadcast_to(
                        jnp.where(lo_half[0:1, :], cd_parts[0], cd_parts[1]), (8, LANES))
                ysc[rows, lanes] = y_pair

    _chunk_loop(nc, local_phase)

    for d, ht in ((0, htf), (1, htb)):
        for qd in range(n_pairs):
            lanes = slice(qd * LANES, (qd + 1) * LANES)
            if has_h0:
                ht[:, lanes] = h0_ref[0, d, lanes, :].T
            else:
                ht[:, lanes] = jnp.zeros((D_STATE, LANES), F32)

    def scan_phase(j):
        for d, ht in ((0, htf), (1, htb)):
            c = j if d == 0 else nc - 1 - j
            rows = pl.ds(pl.multiple_of(c * CHUNK, CHUNK), CHUNK)
            for pi in range(n_pairs):
                lanes = slice(pi * LANES, (pi + 1) * LANES)
                hprev = ht[:, lanes]
                ysc[rows, lanes] += _dot(ce_sc[c, d, :, pi * 2 * LANES:(pi + 1) * 2 * LANES], split_pair(hprev))
                ht[:, lanes] = hprev * cd_sc[c, d, 0:1, lanes] + s_sc[c, d, :, lanes]

    _chunk_loop(nc, scan_phase)

    def finish(c):
        rows = pl.ds(pl.multiple_of(c * CHUNK, CHUNK), CHUNK)
        y = ysc[rows, :] + dsk_ref[...] * xact[rows, 0:D_INNER_B]
        y = y * _silu(z_ref[rows, :])
        ms = jnp.mean(y * y, axis=1, keepdims=True)
        y_ref[rows, :] = y * lax.rsqrt(ms + 1e-5) * nw_ref[...]

    _chunk_loop(nc, finish)

    if hfin_ref is not None:
        for d, ht in ((0, htf), (1, htb)):
            for qd in range(n_pairs):
                lanes = slice(qd * LANES, (qd + 1) * LANES)
                hfin_ref[0, d, lanes, :] = ht[:, lanes].T


def _ssd(xbc, dt, z, h0, consts, n_batch, seq_len):
    L = seq_len
    nc = L // CHUNK
    row = lambda n: pl.BlockSpec((L, n), lambda b: (b, 0))
    state_spec = pl.BlockSpec((1, 2, D_INNER_B, D_STATE), lambda b: (b, 0, 0, 0))
    has_h0 = h0 is not None
    in_specs = [row(CONV_DIM_B), row(LANES), row(D_INNER_B)]
    args = [xbc, dt, z]
    if has_h0:
        in_specs.append(state_spec)
        args.append(h0)
    for cst in consts:
        in_specs.append(pl.BlockSpec(cst.shape, lambda b: (0, 0)))
        args.append(cst)
    out_specs = [row(D_INNER_B)]
    out_shape = [jax.ShapeDtypeStruct((n_batch * L, D_INNER_B), F32)]
    if not has_h0:
        out_specs.append(state_spec)
        out_shape.append(jax.ShapeDtypeStruct((n_batch, 2, D_INNER_B, D_STATE), F32))
    scratch = [pltpu.VMEM((L, D_INNER_B), F32),
               pltpu.VMEM((nc, 2, D_STATE, D_INNER_B), F32),
               pltpu.VMEM((nc, 2, 8, D_INNER_B), F32),
               pltpu.VMEM((nc, 2, CHUNK, 2 * D_INNER_B), BF16),
               pltpu.VMEM((D_STATE, D_INNER_B), F32),
               pltpu.VMEM((D_STATE, D_INNER_B), F32)]
    return pl.pallas_call(
        functools.partial(_ssd_kernel, seq_len=L, has_h0=has_h0),
        grid=(n_batch,),
        in_specs=in_specs,
        out_specs=out_specs,
        out_shape=out_shape,
        scratch_shapes=scratch,
        compiler_params=_params(),
        name="bidir_ssd",
    )(*args)


def _tail_kernel(*refs, n_in):
    a_refs = refs[:n_in]
    (x_ref, mod_ref, wout_ref, g1_ref, b1_ref, wup_ref, wdown_ref, g2_ref, b2_ref, o_ref) = refs[n_in:]
    m = mod_ref[0]
    x = x_ref[...]
    d = None
    off = 0
    for a_ref in a_refs:
        n = a_ref.shape[1]
        part = _dot(a_ref[...].astype(BF16), wout_ref[off:off + n, :])
        d = part if d is None else d + part
        off += n
    x1 = _layer_norm(ALPHA * x + m[2:3] * d, g1_ref[...], b1_ref[...])
    h = (x1 * (1.0 + m[4:5]) + m[3:4]).astype(BF16)
    acc = None
    for f in range(0, D_FF, FF_CHUNK):
        u = jnp.maximum(_dot(h, wup_ref[:, f:f + FF_CHUNK]), 0.0)
        part = _dot((u * u).astype(BF16), wdown_ref[f:f + FF_CHUNK, :])
        acc = part if acc is None else acc + part
    o_ref[...] = _layer_norm(ALPHA * x1 + m[5:6] * acc, g2_ref[...], b2_ref[...])


def _tail(a_list, x, mod, row_of_tile, w_out, ln1_g, ln1_b, w_up, w_down, ln2_g, ln2_b):
    t = x.shape[0]
    tm = TOKEN_TILE
    row = lambda n: pl.BlockSpec((tm, n), lambda i: (i, 0))
    vec = lambda v: v.reshape(1, D_MODEL)
    consts = [w_out, vec(ln1_g), vec(ln1_b), w_up, w_down, vec(ln2_g), vec(ln2_b)]
    return pl.pallas_call(
        functools.partial(_tail_kernel, n_in=len(a_list)),
        grid=(t // tm,),
        in_specs=[row(a.shape[1]) for a in a_list] + [row(D_MODEL), _mod_spec(row_of_tile)]
                 + [_const_spec(cst.shape) for cst in consts],
        out_specs=row(D_MODEL),
        out_shape=jax.ShapeDtypeStruct((t, D_MODEL), F32),
        compiler_params=_params(),
        name="outproj_mlp",
    )(*a_list, x, mod, *consts)


def _pool_kernel(x_ref, mod_ref, win_ref, pw_ref, ps_ref, o_ref, *, seq_len):
    L = seq_len
    m = mod_ref[0]
    h = (x_ref[...] * (1.0 + m[1:2]) + m[0:1]).astype(BF16)
    u = _dot(h, win_ref[...])
    t_i = lax.broadcasted_iota(jnp.int32, (L, L), 0)
    s_i = lax.broadcasted_iota(jnp.int32, (L, L), 1)
    t_col = lax.broadcasted_iota(jnp.int32, (L, 1), 0)
    for g, w in enumerate(POOL_WINDOWS):
        lo, hi = w // 2, w - w // 2
        band = jnp.where((s_i >= t_i - lo) & (s_i < t_i + hi), 1.0, 0.0).astype(BF16)
        cnt = (jnp.minimum(t_col + hi, L) - jnp.maximum(t_col - lo, 0)).astype(F32)
        cols = slice(g * POOL_GROUP_DIM, (g + 1) * POOL_GROUP_DIM)
        for s in range(x_ref.shape[0] // L):
            rows = slice(s * L, (s + 1) * L)
            ug = u[rows, cols]
            ug_hi = ug.astype(BF16)
            ug_lo = (ug - ug_hi.astype(F32)).astype(BF16)
            win_sum = _dot(band, ug_hi) + _dot(band, ug_lo)
            pooled = win_sum / cnt - ug
            o_ref[rows, cols] = _dot(pooled.astype(BF16), pw_ref[g]) * ps_ref[:, cols]


def _pool_mixer(x, mod, row_of_tile, w_in, pool_w, pool_scale, seq_len, tm):
    t = x.shape[0]
    row = pl.BlockSpec((tm, D_MODEL), lambda i: (i, 0))
    consts = [w_in, pool_w, pool_scale.reshape(1, D_MODEL)]
    return pl.pallas_call(
        functools.partial(_pool_kernel, seq_len=seq_len),
        grid=(t // tm,),
        in_specs=[row, _mod_spec(row_of_tile)] + [_const_spec(cst.shape) for cst in consts],
        out_specs=row,
        out_shape=jax.ShapeDtypeStruct((t, D_MODEL), F32),
        compiler_params=_params(),
        name="pool_mixer",
    )(x, mod, *consts)


def _rope_tables(rows):
    r, col = jnp.meshgrid(jnp.arange(rows), jnp.arange(GRID_W), indexing="ij")
    r = r.reshape(-1).astype(F32)
    col = col.reshape(-1).astype(F32)
    n_freq = HEAD_DIM_A // 4
    inv = ROPE_BASE ** (-jnp.arange(n_freq, dtype=F32) / n_freq)
    ang = jnp.concatenate([r[:, None] * inv, col[:, None] * inv], -1)
    cos, sin = jnp.cos(ang), jnp.sin(ang)
    reps = 512 // HEAD_DIM_A
    return (jnp.tile(jnp.concatenate([cos, cos], -1), (1, reps)),
            jnp.tile(jnp.concatenate([-sin, sin], -1), (1, reps)))


def kernel(x_prompt, x_sample, cache_k_l0, cache_v_l0, state_ssm_l0, c, c_ctx, w_mod_l0, b_mod_l0, w_in_l0, conv_w_l0, conv_b_l0, a_log_l0, dt_bias_l0, d_skip_l0, ssm_norm_w_l0, lam_q1_l0, lam_k1_l0, lam_q2_l0, lam_k2_l0, subln_w_l0, w_out_l0, ln1_g_l0, ln1_b_l0, w_up_l0, w_down_l0, ln2_g_l0, ln2_b_l0, w_mod_l1, b_mod_l1, w_in_l1, pool_w_l1, pool_scale_l1, w_out_l1, ln1_g_l1, ln1_b_l1, w_up_l1, w_down_l1, ln2_g_l1, ln2_b_l1):
    n_ctx, l_ctx, _ = x_prompt.shape
    n_lat, l_lat, _ = x_sample.shape
    past = cache_k_l0.shape[1]
    xc = x_prompt.reshape(n_ctx * l_ctx, D_MODEL)
    xl = x_sample.reshape(n_lat * l_lat, D_MODEL)

    cond8 = jnp.concatenate([c_ctx[None, :], c, jnp.zeros((8 - 1 - n_lat, D_MODEL), F32)], axis=0)
    ctx_row = lambda i: 0
    lat_tiles = l_lat // TOKEN_TILE
    lat_row = lambda i: 1 + i // lat_tiles

    mod0 = _modulation(cond8, w_mod_l0, b_mod_l0)
    n_main = 4 * A_QK + A_V + D_INNER_B + CONV_DIM_B
    w_main = w_in_l0[:, :n_main].astype(BF16)
    w_dt = jnp.pad(w_in_l0[:, n_main:], ((0, 0), (0, LANES - 2 * N_HEADS_B))).astype(BF16)
    lane_pad = lambda v: jnp.pad(v.reshape(1, -1), ((0, 0), (0, LANES - 2 * N_HEADS_B)))
    conv_consts = (conv_w_l0, conv_b_l0.reshape(1, CONV_DIM_B), lane_pad(dt_bias_l0))
    ssd_consts = [lane_pad(a_log_l0),
                  jnp.repeat(d_skip_l0, HEAD_DIM_B).reshape(1, D_INNER_B),
                  ssm_norm_w_l0.reshape(1, D_INNER_B)]
    lam_vecs = jnp.stack([lam_q1_l0, lam_k1_l0, lam_q2_l0, lam_k2_l0], axis=0)
    lam_init = 0.8 - 0.6 * math.exp(-0.3 * 0)
    tail0 = (w_out_l0.astype(BF16), ln1_g_l0, ln1_b_l0, w_up_l0.astype(BF16), w_down_l0.astype(BF16),
             ln2_g_l0, ln2_b_l0)

    qc, kc, vc, zc, xbc_c, dt_c = _inproj0(xc, mod0, ctx_row, w_main, w_dt, *conv_consts, l_ctx, TOKEN_TILE)
    oa_c = _attention(qc, [kc], [vc], [l_ctx], lam_vecs, subln_w_l0, n_ctx, l_ctx, lam_init)
    y_c, new_ssm = _ssd(xbc_c, dt_c, zc, None, ssd_consts, n_ctx, l_ctx)
    xc = _tail([oa_c, y_c], xc, mod0, ctx_row, *tail0)

    ql, kl, vl, zl, xbc_l, dt_l = _inproj0(xl, mod0, lambda i: 1 + i, w_main, w_dt, *conv_consts, l_lat, l_lat,
                                           rope_tables=_rope_tables(l_lat // GRID_W))
    ck = cache_k_l0.reshape(n_lat * past, 2 * A_QK)
    cv = cache_v_l0.reshape(n_lat * past, A_V)
    oa_l = _attention(ql, [ck, kl], [cv, vl], [past, l_lat], lam_vecs, subln_w_l0, n_lat, l_lat, lam_init)
    h0 = state_ssm_l0.reshape(n_lat, 2, D_INNER_B, D_STATE)
    (y_l,) = _ssd(xbc_l, dt_l, zl, h0, ssd_consts, n_lat, l_lat)
    xl = _tail([oa_l, y_l], xl, mod0, lat_row, *tail0)

    mod1 = _modulation(cond8, w_mod_l1, b_mod_l1)
    w_in1 = w_in_l1.astype(BF16)
    pool_w = pool_w_l1.astype(BF16)
    tail1 = (w_out_l1.astype(BF16), ln1_g_l1, ln1_b_l1, w_up_l1.astype(BF16), w_down_l1.astype(BF16),
             ln2_g_l1, ln2_b_l1)
    mix_c = _pool_mixer(xc, mod1, ctx_row, w_in1, pool_w, pool_scale_l1, l_ctx, TOKEN_TILE)
    xc = _tail([mix_c], xc, mod1, ctx_row, *tail1)
    mix_l = _pool_mixer(xl, mod1, lambda i: 1 + i, w_in1, pool_w, pool_scale_l1, l_lat, l_lat)
    xl = _tail([mix_l], xl, mod1, lat_row, *tail1)

    return (xc.reshape(n_ctx, l_ctx, D_MODEL), xl.reshape(n_lat, l_lat, D_MODEL),
            kc.reshape(n_ctx, l_ctx, 2, N_HEADS_A, HEAD_DIM_A),
            vc.reshape(n_ctx, l_ctx, N_HEADS_A, 2 * HEAD_DIM_A),
            new_ssm.reshape(n_ctx, 2, N_HEADS_B, HEAD_DIM_B, D_STATE))
```

```python
import functools
import math

import jax
import jax.numpy as jnp
from jax import lax
from jax.experimental import pallas as pl
from jax.experimental.pallas import tpu as pltpu

F32 = jnp.float32
BF16 = jnp.bfloat16

D_MODEL = 1024
N_MOD = 6
N_HEADS_A = 4
HEAD_DIM_A = 64
A_QK = N_HEADS_A * HEAD_DIM_A
A_V = 2 * A_QK
D_INNER_B = 512
HEAD_DIM_B = 64
N_HEADS_B = 8
D_STATE = 128
D_CONV = 5
CONV_DIM_B = 1024
CHUNK = 128
GRID_W = 64
POOL_WINDOWS = (2, 4, 8, 16)
POOL_GROUP_DIM = 256
D_FF = 4096
ROPE_BASE = 10000.0
LN_EPS = 1e-5
DEPTH = 2
ALPHA = (2 * DEPTH) ** 0.25
NEG_BIG = -1e30
LOG2E = 1.4426950408889634

LANES = 128
TOKEN_TILE = 512
ATTN_Q_TILE = 256
FF_CHUNK = 1024
VMEM_LIMIT = 56 * 1024 * 1024


def _dot(a, b):
    return jnp.dot(a, b, preferred_element_type=F32)


def _dot_nt(a, b):
    return lax.dot_general(a, b, (((1,), (1,)), ((), ())), preferred_element_type=F32)


def _bf16_pieces(a):
    a1 = a.astype(BF16)
    r1 = a - a1.astype(F32)
    a2 = r1.astype(BF16)
    a3 = (r1 - a2.astype(F32)).astype(BF16)
    return a1, a2, a3


def _dot_exact_rhs(t01, a):
    a1, a2, a3 = _bf16_pieces(a)
    return _dot(t01, a1) + _dot(t01, a2) + _dot(t01, a3)


def _sigmoid(x):
    return 1.0 / (1.0 + jnp.exp(-x))


def _silu(x):
    return x * _sigmoid(x)


def _layer_norm(x, g, b):
    mu = jnp.mean(x, axis=-1, keepdims=True)
    xc = x - mu
    var = jnp.mean(xc * xc, axis=-1, keepdims=True)
    return xc * lax.rsqrt(var + LN_EPS) * g + b


def _const_spec(shape):
    nd = len(shape)
    return pl.BlockSpec(shape, lambda *_: (0,) * nd, pipeline_mode=pl.Buffered(1))


def _params(n_axes=1):
    return pltpu.CompilerParams(dimension_semantics=("arbitrary",) * n_axes,
                                vmem_limit_bytes=VMEM_LIMIT)


def _mod_kernel(cond_ref, w_ref, b_ref, o_ref):
    c = cond_ref[...]
    s = _silu(c).astype(BF16)
    o_ref[...] = _dot(s, w_ref[...].astype(BF16)) + b_ref[...]


def _modulation(cond8, w_mod, b_mod):
    tn = 512
    n = w_mod.shape[1]
    out = pl.pallas_call(
        _mod_kernel,
        grid=(n // tn,),
        in_specs=[pl.BlockSpec((8, D_MODEL), lambda j: (0, 0)),
                  pl.BlockSpec((D_MODEL, tn), lambda j: (0, j)),
                  pl.BlockSpec((1, tn), lambda j: (0, j))],
        out_specs=pl.BlockSpec((8, tn), lambda j: (0, j)),
        out_shape=jax.ShapeDtypeStruct((8, n), F32),
        compiler_params=_params(),
        name="modulation",
    )(cond8, w_mod, b_mod.reshape(1, n))
    return out.reshape(8, N_MOD, D_MODEL)


def _mod_spec(row_of_tile):
    return pl.BlockSpec((1, N_MOD, D_MODEL), lambda i: (row_of_tile(i), 0, 0))


def _rope(t, cosf, sinf):
    lane = lax.broadcasted_iota(jnp.int32, t.shape, 1)
    first = (lane & (HEAD_DIM_A - 1)) < HEAD_DIM_A // 2
    width = t.shape[1]
    swapped = jnp.where(first, pltpu.roll(t, width - HEAD_DIM_A // 2, 1),
                        pltpu.roll(t, HEAD_DIM_A // 2, 1))
    return t * cosf + swapped * sinf


def _conv_silu(xbc, cw_ref, cb_ref, seq_len):
    rows = xbc.shape[0]
    pos = lax.rem(lax.broadcasted_iota(jnp.int32, xbc.shape, 0), seq_len)
    half = D_CONV // 2
    acc = cb_ref[...] + cw_ref[half:half + 1, :] * xbc
    for j in range(D_CONV):
        o = j - half
        if o == 0:
            continue
        shifted = pltpu.roll(xbc, (-o) % rows, 0)
        valid = (pos >= -o) if o < 0 else (pos < seq_len - o)
        acc = acc + cw_ref[j:j + 1, :] * jnp.where(valid, shifted, 0.0)
    return _silu(acc)


def _softplus(t):
    return jnp.maximum(t, 0.0) + jnp.log(1.0 + jnp.exp(-jnp.abs(t)))


def _inproj0_kernel(*refs, rope, seq_len):
    if rope:
        (x_ref, mod_ref, w_ref, wdt_ref, cw_ref, cb_ref, dtb_ref, cos_ref, sin_ref,
         q_ref, k_ref, v_ref, z_ref, xbc_ref, dt_ref) = refs
    else:
        (x_ref, mod_ref, w_ref, wdt_ref, cw_ref, cb_ref, dtb_ref,
         q_ref, k_ref, v_ref, z_ref, xbc_ref, dt_ref) = refs
    m = mod_ref[0]
    h = (x_ref[...] * (1.0 + m[1:2]) + m[0:1]).astype(BF16)
    xbc_ref[...] = _conv_silu(_dot(h, w_ref[:, 2048:3072]), cw_ref, cb_ref, seq_len)
    dt_ref[...] = _softplus(_dot(h, wdt_ref[...]) + dtb_ref[...])
    q = _dot(h, w_ref[:, 0:512])
    k = _dot(h, w_ref[:, 512:1024])
    if rope:
        q = _rope(q, cos_ref[...], sin_ref[...])
        k = _rope(k, cos_ref[...], sin_ref[...])
    q_ref[...] = q
    for b in range(x_ref.shape[0] // seq_len):
        for blk in range(2 * A_QK // LANES):
            t = k[b * seq_len:(b + 1) * seq_len, blk * LANES:(blk + 1) * LANES].T
            which, head = blk // 2, (blk % 2) * 2
            k_ref[b, which, head] = t[0:HEAD_DIM_A]
            k_ref[b, which, head + 1] = t[HEAD_DIM_A:2 * HEAD_DIM_A]
    v_ref[...] = _dot(h, w_ref[:, 1024:1536])
    z_ref[...] = _dot(h, w_ref[:, 1536:2048])


def _inproj0(x, mod, row_of_tile, w_main, w_dt, conv_w, conv_b, dt_bias, seq_len, tm, rope_tables=None):
    t = x.shape[0]
    assert tm % seq_len == 0
    row = lambda n: pl.BlockSpec((tm, n), lambda i: (i, 0))
    flat = lambda n: jax.ShapeDtypeStruct((t, n), F32)
    consts = [w_main, w_dt, conv_w, conv_b, dt_bias]
    in_specs = [row(D_MODEL), _mod_spec(row_of_tile)] + [_const_spec(cst.shape) for cst in consts]
    args = [x, mod] + consts
    if rope_tables is not None:
        assert tm == seq_len
        tab = pl.BlockSpec((tm, 512), lambda i: (0, 0))
        in_specs += [tab, tab]
        args += list(rope_tables)
    kt_dims = (2, N_HEADS_A, HEAD_DIM_A, seq_len)
    k_spec = pl.BlockSpec((tm // seq_len,) + kt_dims, lambda i: (i, 0, 0, 0, 0))
    k_shape = jax.ShapeDtypeStruct((t // seq_len,) + kt_dims, F32)
    return pl.pallas_call(
        functools.partial(_inproj0_kernel, rope=rope_tables is not None, seq_len=seq_len),
        grid=(t // tm,),
        in_specs=in_specs,
        out_specs=[row(512), k_spec, row(512), row(512), row(CONV_DIM_B), row(LANES)],
        out_shape=[flat(512), k_shape, flat(512), flat(512), flat(CONV_DIM_B), flat(LANES)],
        compiler_params=_params(),
        name="inproj0",
    )(*args)


def _lambda_full(lam_ref, lam_init):
    lv = lam_ref[...]
    return (jnp.exp(jnp.sum(lv[0:1] * lv[1:2], axis=1, keepdims=True))
            - jnp.exp(jnp.sum(lv[2:3] * lv[3:4], axis=1, keepdims=True)) + lam_init)


def _attn_kernel(*refs, n_seg, tq, lam_init):
    q_ref = refs[0]
    kt_refs = refs[1:1 + n_seg]
    v_refs = refs[1 + n_seg:1 + 2 * n_seg]
    lam_ref, sub_ref, o_ref = refs[1 + 2 * n_seg:]
    lam = _lambda_full(lam_ref, lam_init)
    lane = lax.broadcasted_iota(jnp.int32, (1, A_QK), 1)
    masks = [(lane >= h * HEAD_DIM_A) & (lane < (h + 1) * HEAD_DIM_A) for h in range(N_HEADS_A)]
    keys = [r.shape[-1] for r in kt_refs]

    def stacked(qx):
        return jnp.concatenate([jnp.where(m, qx, 0.0) for m in masks], axis=0).astype(BF16)

    def exp_scores(qx, b, which):
        qs = stacked(qx)
        s = [_dot(qs, r[b, which].reshape(A_QK, n).astype(BF16)) for r, n in zip(kt_refs, keys)]
        mx = functools.reduce(jnp.maximum, [jnp.max(x, axis=1, keepdims=True) for x in s])
        return [jnp.exp2(x - mx).astype(BF16) for x in s]

    outs = []
    for b in range(kt_refs[0].shape[0]):
        q = q_ref[b * tq:(b + 1) * tq, :] * (HEAD_DIM_A ** -0.5 * LOG2E)
        e1 = exp_scores(q[:, 0:A_QK], b, 0)
        e2 = exp_scores(q[:, A_QK:2 * A_QK], b, 1)
        heads = []
        for h in range(N_HEADS_A):
            hrows = slice(h * tq, (h + 1) * tq)
            n1 = n2 = None
            for s, (v_ref, n) in enumerate(zip(v_refs, keys)):
                v_ext = jnp.concatenate([v_ref[b * n:(b + 1) * n, h * LANES:(h + 1) * LANES].astype(BF16),
                                         jnp.ones((n, LANES), BF16)], axis=1)
                p1 = _dot(e1[s][hrows], v_ext)
                p2 = _dot(e2[s][hrows], v_ext)
                n1 = p1 if n1 is None else n1 + p1
                n2 = p2 if n2 is None else n2 + p2
            heads.append(n1[:, 0:LANES] / n1[:, LANES:] - lam * (n2[:, 0:LANES] / n2[:, LANES:]))
        o = jnp.concatenate(heads, axis=0)
        ms = jnp.mean(o * o, axis=1, keepdims=True)
        o = (o * lax.rsqrt(ms + 1e-5) * sub_ref[...]) * (1.0 - lam_init)
        outs.append(jnp.concatenate([o[h * tq:(h + 1) * tq] for h in range(N_HEADS_A)], axis=1))
    o_ref[...] = jnp.concatenate(outs, axis=0)


def _attention(q, kts, vs, lam_vecs, subln_w, q_len, tq, seqs_per_step, lam_init):
    n_seq = kts[0].shape[0]
    nb = seqs_per_step
    tiles = q_len // tq
    assert tiles == 1 or nb == 1
    qspec = pl.BlockSpec((nb * tq, 512), lambda b, i: (b * tiles + i, 0))
    ktspecs = [pl.BlockSpec((nb,) + kt.shape[1:], lambda b, i: (b, 0, 0, 0, 0)) for kt in kts]
    vspecs = [pl.BlockSpec((nb * kt.shape[-1], 512), lambda b, i: (b, 0)) for kt in kts]
    return pl.pallas_call(
        functools.partial(_attn_kernel, n_seg=len(kts), tq=tq, lam_init=lam_init),
        grid=(n_seq // nb, tiles),
        in_specs=[qspec] + ktspecs + vspecs + [pl.BlockSpec((4, HEAD_DIM_A), lambda b, i: (0, 0)),
                                             pl.BlockSpec((1, LANES), lambda b, i: (0, 0))],
        out_specs=qspec,
        out_shape=jax.ShapeDtypeStruct(q.shape, F32),
        compiler_params=_params(2),
        name="diff_attention",
    )(q, *kts, *vs, lam_vecs, subln_w.reshape(1, LANES))


def _chunk_loop(n, body):
    if n <= 2:
        for c in range(n):
            body(c)
    else:
        def step(c, carry):
            body(c)
            return carry
        lax.fori_loop(0, n, step, 0)


def _ssd_kernel(*refs, seq_len, has_h0):
    if has_h0:
        (xact, dtv, z_ref, h0_ref, arow_ref, dsk_ref, nw_ref,
         y_ref, ysc, s_sc, cd_sc, ce_sc, htf, htb) = refs
        hfin_ref = None
    else:
        (xact, dtv, z_ref, arow_ref, dsk_ref, nw_ref,
         y_ref, hfin_ref, ysc, s_sc, cd_sc, ce_sc, htf, htb) = refs
        h0_ref = None
    nc = seq_len // CHUNK
    n_pairs = N_HEADS_B // 2
    a_row = -jnp.exp(arow_ref[...])

    row_i = lax.broadcasted_iota(jnp.int32, (CHUNK, CHUNK), 0)
    col_i = lax.broadcasted_iota(jnp.int32, (CHUNK, CHUNK), 1)
    lower = col_i <= row_i
    upper = col_i >= row_i
    tri = (jnp.where(lower, 1.0, 0.0).astype(BF16), jnp.where(upper, 1.0, 0.0).astype(BF16))
    lo_half = col_i < HEAD_DIM_B
    col16_i = lax.broadcasted_iota(jnp.int32, (16, CHUNK), 1)

    def lane_column(mat, idx, lane_ids):
        return jnp.sum(jnp.where(lane_ids == idx, mat, 0.0), axis=1, keepdims=True)

    def split_pair(m):
        zero = jnp.zeros_like(m)
        return jnp.concatenate([jnp.where(lo_half, m, zero), jnp.where(lo_half, zero, m)], axis=0).astype(BF16)

    def local_phase(c):
        rows = pl.ds(pl.multiple_of(c * CHUNK, CHUNK), CHUNK)
        dtc = dtv[rows, :]
        dtt = dtc.T[0:16, :]
        per_dir = []
        for d in range(2):
            col = _dot_exact_rhs(tri[d], dtc * a_row)
            rowm = col.T[0:16, :]
            tot = jnp.broadcast_to(lane_column(rowm, CHUNK - 1 if d == 0 else 0, col16_i),
                                   (16, CHUNK))
            dte = jnp.exp(tot - rowm) * dtt
            per_dir.append((col, rowm, dte, jnp.exp(tot)))
        for g in range(2):
            bg = xact[rows, 512 + g * LANES:512 + (g + 1) * LANES]
            cg = xact[rows, 768 + g * LANES:768 + (g + 1) * LANES]
            cbm = _dot_nt(cg.astype(BF16), bg.astype(BF16))
            bgt = bg.T
            for pq in range(2):
                pi = g * 2 + pq
                lanes = slice(pi * LANES, (pi + 1) * LANES)
                xrhs = split_pair(xact[rows, lanes])
                y_pair = None
                for d in range(2):
                    col, rowm, dte, cdec = per_dir[d]
                    causal = lower if d == 0 else upper
                    m_parts, ce_parts, bw_parts, cd_parts = [], [], [], []
                    for hh in (2 * pi, 2 * pi + 1):
                        hd = 8 * d + hh
                        colb = lane_column(col, hd, col_i)
                        dec = jnp.exp(jnp.where(causal, colb - rowm[hd:hd + 1, :], NEG_BIG))
                        m_parts.append(cbm * dec * dtt[hd:hd + 1, :])
                        ce_parts.append(cg * jnp.exp(colb))
                        bw_parts.append(bgt * dte[hd:hd + 1, :])
                        cd_parts.append(cdec[hd:hd + 1, :])
                    yd = _dot(jnp.concatenate(m_parts, axis=1).astype(BF16), xrhs)
                    y_pair = yd if y_pair is None else y_pair + yd
                    s_sc[c, d, :, lanes] = _dot(jnp.concatenate(bw_parts, axis=1).astype(BF16), xrhs)
                    ce_sc[c, d, :, pi * 2 * LANES:(pi + 1) * 2 * LANES] = (
                        jnp.concatenate(ce_parts, axis=1).astype(BF16))
                    cd_sc[c, d, :, lanes] = jnp.broadcast_to(
                        jnp.where(lo_half[0:1, :], cd_parts[0], cd_parts[1]), (8, LANES))
                ysc[rows, lanes] = y_pair

    _chunk_loop(nc, local_phase)

    for d, ht in ((0, htf), (1, htb)):
        for qd in range(n_pairs):
            lanes = slice(qd * LANES, (qd + 1) * LANES)
            if has_h0:
                ht[:, lanes] = h0_ref[0, d, lanes, :].T
            else:
                ht[:, lanes] = jnp.zeros((D_STATE, LANES), F32)

    def scan_phase(j):
        for d, ht in ((0, htf), (1, htb)):
            c = j if d == 0 else nc - 1 - j
            rows = pl.ds(pl.multiple_of(c * CHUNK, CHUNK), CHUNK)
            for pi in range(n_pairs):
                lanes = slice(pi * LANES, (pi + 1) * LANES)
                hprev = ht[:, lanes]
                ysc[rows, lanes] += _dot(ce_sc[c, d, :, pi * 2 * LANES:(pi + 1) * 2 * LANES], split_pair(hprev))
                ht[:, lanes] = hprev * cd_sc[c, d, 0:1, lanes] + s_sc[c, d, :, lanes]

    _chunk_loop(nc, scan_phase)

    def finish(c):
        rows = pl.ds(pl.multiple_of(c * CHUNK, CHUNK), CHUNK)
        y = ysc[rows, :] + dsk_ref[...] * xact[rows, 0:D_INNER_B]
        y = y * _silu(z_ref[rows, :])
        ms = jnp.mean(y * y, axis=1, keepdims=True)
        y_ref[rows, :] = y * lax.rsqrt(ms + 1e-5) * nw_ref[...]

    _chunk_loop(nc, finish)

    if hfin_ref is not None:
        for d, ht in ((0, htf), (1, htb)):
            for qd in range(n_pairs):
                lanes = slice(qd * LANES, (qd + 1) * LANES)
                hfin_ref[0, d, lanes, :] = ht[:, lanes].T


def _ssd(xbc, dt, z, h0, consts, n_batch, seq_len):
    L = seq_len
    nc = L // CHUNK
    row = lambda n: pl.BlockSpec((L, n), lambda b: (b, 0))
    state_spec = pl.BlockSpec((1, 2, D_INNER_B, D_STATE), lambda b: (b, 0, 0, 0))
    has_h0 = h0 is not None
    in_specs = [row(CONV_DIM_B), row(LANES), row(D_INNER_B)]
    args = [xbc, dt, z]
    if has_h0:
        in_specs.append(state_spec)
        args.append(h0)
    for cst in consts:
        in_specs.append(pl.BlockSpec(cst.shape, lambda b: (0, 0)))
        args.append(cst)
    out_specs = [row(D_INNER_B)]
    out_shape = [jax.ShapeDtypeStruct((n_batch * L, D_INNER_B), F32)]
    if not has_h0:
        out_specs.append(state_spec)
        out_shape.append(jax.ShapeDtypeStruct((n_batch, 2, D_INNER_B, D_STATE), F32))
    scratch = [pltpu.VMEM((L, D_INNER_B), F32),
               pltpu.VMEM((nc, 2, D_STATE, D_INNER_B), F32),
               pltpu.VMEM((nc, 2, 8, D_INNER_B), F32),
               pltpu.VMEM((nc, 2, CHUNK, 2 * D_INNER_B), BF16),
               pltpu.VMEM((D_STATE, D_INNER_B), F32),
               pltpu.VMEM((D_STATE, D_INNER_B), F32)]
    return pl.pallas_call(
        functools.partial(_ssd_kernel, seq_len=L, has_h0=has_h0),
        grid=(n_batch,),
        in_specs=in_specs,
        out_specs=out_specs,
        out_shape=out_shape,
        scratch_shapes=scratch,
        compiler_params=_params(),
        name="bidir_ssd",
    )(*args)


def _tail_kernel(*refs, n_in):
    a_refs = refs[:n_in]
    (x_ref, mod_ref, wout_ref, g1_ref, b1_ref, wup_ref, wdown_ref, g2_ref, b2_ref, o_ref) = refs[n_in:]
    m = mod_ref[0]
    x = x_ref[...]
    d = None
    off = 0
    for a_ref in a_refs:
        n = a_ref.shape[1]
        part = _dot(a_ref[...].astype(BF16), wout_ref[off:off + n, :])
        d = part if d is None else d + part
        off += n
    x1 = _layer_norm(ALPHA * x + m[2:3] * d, g1_ref[...], b1_ref[...])
    h = (x1 * (1.0 + m[4:5]) + m[3:4]).astype(BF16)
    acc = None
    for f in range(0, D_FF, FF_CHUNK):
        u = jnp.maximum(_dot(h, wup_ref[:, f:f + FF_CHUNK]), 0.0)
        part = _dot((u * u).astype(BF16), wdown_ref[f:f + FF_CHUNK, :])
        acc = part if acc is None else acc + part
    o_ref[...] = _layer_norm(ALPHA * x1 + m[5:6] * acc, g2_ref[...], b2_ref[...])


def _tail(a_list, x, mod, row_of_tile, w_out, ln1_g, ln1_b, w_up, w_down, ln2_g, ln2_b):
    t = x.shape[0]
    tm = TOKEN_TILE
    row = lambda n: pl.BlockSpec((tm, n), lambda i: (i, 0))
    vec = lambda v: v.reshape(1, D_MODEL)
    consts = [w_out, vec(ln1_g), vec(ln1_b), w_up, w_down, vec(ln2_g), vec(ln2_b)]
    return pl.pallas_call(
        functools.partial(_tail_kernel, n_in=len(a_list)),
        grid=(t // tm,),
        in_specs=[row(a.shape[1]) for a in a_list] + [row(D_MODEL), _mod_spec(row_of_tile)]
                 + [_const_spec(cst.shape) for cst in consts],
        out_specs=row(D_MODEL),
        out_shape=jax.ShapeDtypeStruct((t, D_MODEL), F32),
        compiler_params=_params(),
        name="outproj_mlp",
    )(*a_list, x, mod, *consts)


def _pool_kernel(x_ref, mod_ref, win_ref, pw_ref, ps_ref, o_ref, *, seq_len):
    L = seq_len
    m = mod_ref[0]
    t_i = lax.broadcasted_iota(jnp.int32, (L, L), 0)
    s_i = lax.broadcasted_iota(jnp.int32, (L, L), 1)
    t_g = lax.broadcasted_iota(jnp.int32, (L, POOL_GROUP_DIM), 0)
    bands, scales = [], []
    for w in POOL_WINDOWS:
        lo, hi = w // 2, w - w // 2
        bands.append(jnp.where((s_i >= t_i - lo) & (s_i < t_i + hi), 1.0 / w, 0.0).astype(BF16))
        scales.append(w / (jnp.minimum(t_g + hi, L) - jnp.maximum(t_g - lo, 0)).astype(F32))
    for s in range(x_ref.shape[0] // L):
        rows = slice(s * L, (s + 1) * L)
        h = (x_ref[rows, :] * (1.0 + m[1:2]) + m[0:1]).astype(BF16)
        u = _dot(h, win_ref[...])
        mixed = []
        for g in range(len(POOL_WINDOWS)):
            cols = slice(g * POOL_GROUP_DIM, (g + 1) * POOL_GROUP_DIM)
            ug = u[:, cols]
            ug_hi = ug.astype(BF16)
            ug_lo = (ug - ug_hi.astype(F32)).astype(BF16)
            win_mean = _dot(bands[g], ug_hi) + _dot(bands[g], ug_lo)
            pooled = win_mean * scales[g] - ug
            mixed.append(_dot(pooled.astype(BF16), pw_ref[g]) * ps_ref[:, cols])
        o_ref[rows, :] = jnp.concatenate(mixed, axis=1)


def _pool_mixer(x, mod, row_of_tile, w_in, pool_w, pool_scale, seq_len, tm):
    t = x.shape[0]
    row = pl.BlockSpec((tm, D_MODEL), lambda i: (i, 0))
    consts = [w_in, pool_w, pool_scale.reshape(1, D_MODEL)]
    return pl.pallas_call(
        functools.partial(_pool_kernel, seq_len=seq_len),
        grid=(t // tm,),
        in_specs=[row, _mod_spec(row_of_tile)] + [_const_spec(cst.shape) for cst in consts],
        out_specs=row,
        out_shape=jax.ShapeDtypeStruct((t, D_MODEL), F32),
        compiler_params=_params(),
        name="pool_mixer",
    )(x, mod, *consts)


def _rope_tables(rows):
    r, col = jnp.meshgrid(jnp.arange(rows), jnp.arange(GRID_W), indexing="ij")
    r = r.reshape(-1).astype(F32)
    col = col.reshape(-1).astype(F32)
    n_freq = HEAD_DIM_A // 4
    inv = ROPE_BASE ** (-jnp.arange(n_freq, dtype=F32) / n_freq)
    ang = jnp.concatenate([r[:, None] * inv, col[:, None] * inv], -1)
    cos, sin = jnp.cos(ang), jnp.sin(ang)
    reps = 512 // HEAD_DIM_A
    return (jnp.tile(jnp.concatenate([cos, cos], -1), (1, reps)),
            jnp.tile(jnp.concatenate([-sin, sin], -1), (1, reps)))


def kernel(x_prompt, x_sample, cache_k_l0, cache_v_l0, state_ssm_l0, c, c_ctx, w_mod_l0, b_mod_l0, w_in_l0, conv_w_l0, conv_b_l0, a_log_l0, dt_bias_l0, d_skip_l0, ssm_norm_w_l0, lam_q1_l0, lam_k1_l0, lam_q2_l0, lam_k2_l0, subln_w_l0, w_out_l0, ln1_g_l0, ln1_b_l0, w_up_l0, w_down_l0, ln2_g_l0, ln2_b_l0, w_mod_l1, b_mod_l1, w_in_l1, pool_w_l1, pool_scale_l1, w_out_l1, ln1_g_l1, ln1_b_l1, w_up_l1, w_down_l1, ln2_g_l1, ln2_b_l1):
    n_ctx, l_ctx, _ = x_prompt.shape
    n_lat, l_lat, _ = x_sample.shape
    past = cache_k_l0.shape[1]
    xc = x_prompt.reshape(n_ctx * l_ctx, D_MODEL)
    xl = x_sample.reshape(n_lat * l_lat, D_MODEL)

    cond8 = jnp.concatenate([c_ctx[None, :], c, jnp.zeros((8 - 1 - n_lat, D_MODEL), F32)], axis=0)
    ctx_row = lambda i: 0
    lat_tiles = l_lat // TOKEN_TILE
    lat_row = lambda i: 1 + i // lat_tiles

    mod0 = _modulation(cond8, w_mod_l0, b_mod_l0)
    n_main = 4 * A_QK + A_V + D_INNER_B + CONV_DIM_B
    w_main = w_in_l0[:, :n_main].astype(BF16)
    w_dt = jnp.pad(w_in_l0[:, n_main:], ((0, 0), (0, LANES - 2 * N_HEADS_B))).astype(BF16)
    lane_pad = lambda v: jnp.pad(v.reshape(1, -1), ((0, 0), (0, LANES - 2 * N_HEADS_B)))
    conv_consts = (conv_w_l0, conv_b_l0.reshape(1, CONV_DIM_B), lane_pad(dt_bias_l0))
    ssd_consts = [lane_pad(a_log_l0),
                  jnp.repeat(d_skip_l0, HEAD_DIM_B).reshape(1, D_INNER_B),
                  ssm_norm_w_l0.reshape(1, D_INNER_B)]
    lam_vecs = jnp.stack([lam_q1_l0, lam_k1_l0, lam_q2_l0, lam_k2_l0], axis=0)
    lam_init = 0.8 - 0.6 * math.exp(-0.3 * 0)
    tail0 = (w_out_l0.astype(BF16), ln1_g_l0, ln1_b_l0, w_up_l0.astype(BF16), w_down_l0.astype(BF16),
             ln2_g_l0, ln2_b_l0)

    qc, kc, vc, zc, xbc_c, dt_c = _inproj0(xc, mod0, ctx_row, w_main, w_dt, *conv_consts, l_ctx, TOKEN_TILE)
    oa_c = _attention(qc, [kc], [vc], lam_vecs, subln_w_l0, l_ctx, l_ctx, 2, lam_init)
    y_c, new_ssm = _ssd(xbc_c, dt_c, zc, None, ssd_consts, n_ctx, l_ctx)
    xc = _tail([oa_c, y_c], xc, mod0, ctx_row, *tail0)

    ql, kl, vl, zl, xbc_l, dt_l = _inproj0(xl, mod0, lambda i: 1 + i, w_main, w_dt, *conv_consts, l_lat, l_lat,
                                           rope_tables=_rope_tables(l_lat // GRID_W))
    ck = jnp.transpose(cache_k_l0, (0, 2, 3, 4, 1))
    cv = cache_v_l0.reshape(n_lat * past, A_V)
    oa_l = _attention(ql, [ck, kl], [cv, vl], lam_vecs, subln_w_l0, l_lat, ATTN_Q_TILE, 1, lam_init)
    h0 = state_ssm_l0.reshape(n_lat, 2, D_INNER_B, D_STATE)
    (y_l,) = _ssd(xbc_l, dt_l, zl, h0, ssd_consts, n_lat, l_lat)
    xl = _tail([oa_l, y_l], xl, mod0, lat_row, *tail0)

    mod1 = _modulation(cond8, w_mod_l1, b_mod_l1)
    w_in1 = w_in_l1.astype(BF16)
    pool_w = pool_w_l1.astype(BF16)
    tail1 = (w_out_l1.astype(BF16), ln1_g_l1, ln1_b_l1, w_up_l1.astype(BF16), w_down_l1.astype(BF16),
             ln2_g_l1, ln2_b_l1)
    mix_c = _pool_mixer(xc, mod1, ctx_row, w_in1, pool_w, pool_scale_l1, l_ctx, TOKEN_TILE)
    xc = _tail([mix_c], xc, mod1, ctx_row, *tail1)
    mix_l = _pool_mixer(xl, mod1, lambda i: 1 + i, w_in1, pool_w, pool_scale_l1, l_lat, l_lat)
    xl = _tail([mix_l], xl, mod1, lat_row, *tail1)

    return (xc.reshape(n_ctx, l_ctx, D_MODEL), xl.reshape(n_lat, l_lat, D_MODEL),
            jnp.transpose(kc, (0, 4, 1, 2, 3)),
            vc.reshape(n_ctx, l_ctx, N_HEADS_A, 2 * HEAD_DIM_A),
            new_ssm.reshape(n_ctx, 2, N_HEADS_B, HEAD_DIM_B, D_STATE))
```

```python
import functools
import math

import jax
import jax.numpy as jnp
from jax import lax
from jax.experimental import pallas as pl
from jax.experimental.pallas import tpu as pltpu

F32 = jnp.float32
BF16 = jnp.bfloat16

D_MODEL = 1024
N_MOD = 6
N_HEADS_A = 4
HEAD_DIM_A = 64
A_QK = N_HEADS_A * HEAD_DIM_A
A_V = 2 * A_QK
D_INNER_B = 512
HEAD_DIM_B = 64
N_HEADS_B = 8
D_STATE = 128
D_CONV = 5
CONV_DIM_B = 1024
CHUNK = 128
GRID_W = 64
POOL_WINDOWS = (2, 4, 8, 16)
POOL_GROUP_DIM = 256
D_FF = 4096
ROPE_BASE = 10000.0
LN_EPS = 1e-5
DEPTH = 2
ALPHA = (2 * DEPTH) ** 0.25
NEG_BIG = -1e30
LOG2E = 1.4426950408889634

LANES = 128
TOKEN_TILE = 512
ATTN_Q_TILE = 256
FF_CHUNK = 1024
CONV_COL_BLOCK = 256
VMEM_LIMIT = 56 * 1024 * 1024


def _dot(a, b):
    return jnp.dot(a, b, preferred_element_type=F32)


def _dot_nt(a, b):
    return lax.dot_general(a, b, (((1,), (1,)), ((), ())), preferred_element_type=F32)


def _bf16_pieces(a):
    a1 = a.astype(BF16)
    r1 = a - a1.astype(F32)
    a2 = r1.astype(BF16)
    a3 = (r1 - a2.astype(F32)).astype(BF16)
    return a1, a2, a3


def _dot_exact_rhs(t01, a):
    a1, a2, a3 = _bf16_pieces(a)
    return _dot(t01, a1) + _dot(t01, a2) + _dot(t01, a3)


def _sigmoid(x):
    return 1.0 / (1.0 + jnp.exp(-x))


def _silu(x):
    return x * _sigmoid(x)


def _layer_norm(x, g, b):
    mu = jnp.mean(x, axis=-1, keepdims=True)
    xc = x - mu
    var = jnp.mean(xc * xc, axis=-1, keepdims=True)
    return xc * lax.rsqrt(var + LN_EPS) * g + b


def _const_spec(shape):
    nd = len(shape)
    return pl.BlockSpec(shape, lambda *_: (0,) * nd, pipeline_mode=pl.Buffered(1))


def _params(n_axes=1):
    return pltpu.CompilerParams(dimension_semantics=("arbitrary",) * n_axes,
                                vmem_limit_bytes=VMEM_LIMIT)


def _mod_kernel(cond_ref, w_ref, b_ref, o_ref):
    c = cond_ref[...]
    s = _silu(c).astype(BF16)
    o_ref[...] = _dot(s, w_ref[...].astype(BF16)) + b_ref[...]


def _modulation(cond8, w_mod, b_mod):
    tn = 512
    n = w_mod.shape[1]
    out = pl.pallas_call(
        _mod_kernel,
        grid=(n // tn,),
        in_specs=[pl.BlockSpec((8, D_MODEL), lambda j: (0, 0)),
                  pl.BlockSpec((D_MODEL, tn), lambda j: (0, j)),
                  pl.BlockSpec((1, tn), lambda j: (0, j))],
        out_specs=pl.BlockSpec((8, tn), lambda j: (0, j)),
        out_shape=jax.ShapeDtypeStruct((8, n), F32),
        compiler_params=_params(),
        name="modulation",
    )(cond8, w_mod, b_mod.reshape(1, n))
    return out.reshape(8, N_MOD, D_MODEL)


def _mod_spec(row_of_tile):
    return pl.BlockSpec((1, N_MOD, D_MODEL), lambda i: (row_of_tile(i), 0, 0))


def _rope(t, cosf, sinf):
    lane = lax.broadcasted_iota(jnp.int32, t.shape, 1)
    first = (lane & (HEAD_DIM_A - 1)) < HEAD_DIM_A // 2
    width = t.shape[1]
    swapped = jnp.where(first, pltpu.roll(t, width - HEAD_DIM_A // 2, 1),
                        pltpu.roll(t, HEAD_DIM_A // 2, 1))
    return t * cosf + swapped * sinf


def _conv_silu(xbc, cw_ref, cb_ref, cols, seq_len):
    rows = xbc.shape[0]
    pos = lax.rem(lax.broadcasted_iota(jnp.int32, xbc.shape, 0), seq_len)
    half = D_CONV // 2
    acc = cb_ref[:, cols] + cw_ref[half:half + 1, cols] * xbc
    for j in range(D_CONV):
        o = j - half
        if o == 0:
            continue
        shifted = pltpu.roll(xbc, (-o) % rows, 0)
        valid = (pos >= -o) if o < 0 else (pos < seq_len - o)
        acc = acc + cw_ref[j:j + 1, cols] * jnp.where(valid, shifted, 0.0)
    return _silu(acc)


def _softplus(t):
    return jnp.maximum(t, 0.0) + jnp.log(1.0 + jnp.exp(-jnp.abs(t)))


def _inproj0_kernel(*refs, rope, seq_len):
    if rope:
        (x_ref, mod_ref, w_ref, wdt_ref, cw_ref, cb_ref, dtb_ref, cos_ref, sin_ref,
         q_ref, k_ref, v_ref, z_ref, xbc_ref, dt_ref) = refs
    else:
        (x_ref, mod_ref, w_ref, wdt_ref, cw_ref, cb_ref, dtb_ref,
         q_ref, k_ref, v_ref, z_ref, xbc_ref, dt_ref) = refs
    m = mod_ref[0]
    for b in range(x_ref.shape[0] // seq_len):
        rows = slice(b * seq_len, (b + 1) * seq_len)
        h = (x_ref[rows, :] * (1.0 + m[1:2]) + m[0:1]).astype(BF16)
        for c0 in range(0, CONV_DIM_B, CONV_COL_BLOCK):
            cols = slice(c0, c0 + CONV_COL_BLOCK)
            raw = _dot(h, w_ref[:, 2048 + c0:2048 + c0 + CONV_COL_BLOCK])
            xbc_ref[rows, cols] = _conv_silu(raw, cw_ref, cb_ref, cols, seq_len)
        dt_ref[rows, :] = _softplus(_dot(h, wdt_ref[...]) + dtb_ref[...])
        q = _dot(h, w_ref[:, 0:512])
        k = _dot(h, w_ref[:, 512:1024])
        if rope:
            q = _rope(q, cos_ref[...], sin_ref[...])
            k = _rope(k, cos_ref[...], sin_ref[...])
        q_ref[rows, :] = q
        for blk in range(2 * A_QK // LANES):
            t = k[:, blk * LANES:(blk + 1) * LANES].T
            which, head = blk // 2, (blk % 2) * 2
            k_ref[b, which, head] = t[0:HEAD_DIM_A]
            k_ref[b, which, head + 1] = t[HEAD_DIM_A:2 * HEAD_DIM_A]
        v_ref[rows, :] = _dot(h, w_ref[:, 1024:1536])
        z_ref[rows, :] = _dot(h, w_ref[:, 1536:2048])


def _inproj0(x, mod, row_of_tile, w_main, w_dt, conv_w, conv_b, dt_bias, seq_len, tm, rope_tables=None):
    t = x.shape[0]
    assert tm % seq_len == 0
    row = lambda n: pl.BlockSpec((tm, n), lambda i: (i, 0))
    flat = lambda n: jax.ShapeDtypeStruct((t, n), F32)
    consts = [w_main, w_dt, conv_w, conv_b, dt_bias]
    in_specs = [row(D_MODEL), _mod_spec(row_of_tile)] + [_const_spec(cst.shape) for cst in consts]
    args = [x, mod] + consts
    if rope_tables is not None:
        assert tm == seq_len
        tab = pl.BlockSpec((tm, 512), lambda i: (0, 0))
        in_specs += [tab, tab]
        args += list(rope_tables)
    kt_dims = (2, N_HEADS_A, HEAD_DIM_A, seq_len)
    k_spec = pl.BlockSpec((tm // seq_len,) + kt_dims, lambda i: (i, 0, 0, 0, 0))
    k_shape = jax.ShapeDtypeStruct((t // seq_len,) + kt_dims, F32)
    return pl.pallas_call(
        functools.partial(_inproj0_kernel, rope=rope_tables is not None, seq_len=seq_len),
        grid=(t // tm,),
        in_specs=in_specs,
        out_specs=[row(512), k_spec, row(512), row(512), row(CONV_DIM_B), row(LANES)],
        out_shape=[flat(512), k_shape, flat(512), flat(512), flat(CONV_DIM_B), flat(LANES)],
        compiler_params=_params(),
        name="inproj0",
    )(*args)


def _lambda_full(lam_ref, lam_init):
    lv = lam_ref[...]
    return (jnp.exp(jnp.sum(lv[0:1] * lv[1:2], axis=1, keepdims=True))
            - jnp.exp(jnp.sum(lv[2:3] * lv[3:4], axis=1, keepdims=True)) + lam_init)


def _attn_kernel(*refs, n_seg, tq, lam_init):
    q_ref = refs[0]
    kt_refs = refs[1:1 + n_seg]
    v_refs = refs[1 + n_seg:1 + 2 * n_seg]
    lam_ref, sub_ref, o_ref = refs[1 + 2 * n_seg:]
    lam = _lambda_full(lam_ref, lam_init)
    lane = lax.broadcasted_iota(jnp.int32, (1, A_QK), 1)
    masks = [(lane >= h * HEAD_DIM_A) & (lane < (h + 1) * HEAD_DIM_A) for h in range(N_HEADS_A)]
    keys = [r.shape[-1] for r in kt_refs]

    def stacked(qx):
        return jnp.concatenate([jnp.where(m, qx, 0.0) for m in masks], axis=0).astype(BF16)

    def exp_scores(qx, b, which):
        qs = stacked(qx)
        s = [_dot(qs, r[b, which].reshape(A_QK, n).astype(BF16)) for r, n in zip(kt_refs, keys)]
        mx = functools.reduce(jnp.maximum, [jnp.max(x, axis=1, keepdims=True) for x in s])
        return [jnp.exp2(x - mx).astype(BF16) for x in s]

    outs = []
    for b in range(kt_refs[0].shape[0]):
        q = q_ref[b * tq:(b + 1) * tq, :] * (HEAD_DIM_A ** -0.5 * LOG2E)
        e1 = exp_scores(q[:, 0:A_QK], b, 0)
        e2 = exp_scores(q[:, A_QK:2 * A_QK], b, 1)
        heads = []
        for h in range(N_HEADS_A):
            hrows = slice(h * tq, (h + 1) * tq)
            n1 = n2 = None
            for s, (v_ref, n) in enumerate(zip(v_refs, keys)):
                v_ext = jnp.concatenate([v_ref[b * n:(b + 1) * n, h * LANES:(h + 1) * LANES].astype(BF16),
                                         jnp.ones((n, LANES), BF16)], axis=1)
                p1 = _dot(e1[s][hrows], v_ext)
                p2 = _dot(e2[s][hrows], v_ext)
                n1 = p1 if n1 is None else n1 + p1
                n2 = p2 if n2 is None else n2 + p2
            heads.append(n1[:, 0:LANES] / n1[:, LANES:] - lam * (n2[:, 0:LANES] / n2[:, LANES:]))
        o = jnp.concatenate(heads, axis=0)
        ms = jnp.mean(o * o, axis=1, keepdims=True)
        o = (o * lax.rsqrt(ms + 1e-5) * sub_ref[...]) * (1.0 - lam_init)
        outs.append(jnp.concatenate([o[h * tq:(h + 1) * tq] for h in range(N_HEADS_A)], axis=1))
    o_ref[...] = jnp.concatenate(outs, axis=0)


def _attention(q, kts, vs, lam_vecs, subln_w, q_len, tq, seqs_per_step, lam_init):
    n_seq = kts[0].shape[0]
    nb = seqs_per_step
    tiles = q_len // tq
    assert tiles == 1 or nb == 1
    qspec = pl.BlockSpec((nb * tq, 512), lambda b, i: (b * tiles + i, 0))
    ktspecs = [pl.BlockSpec((nb,) + kt.shape[1:], lambda b, i: (b, 0, 0, 0, 0)) for kt in kts]
    vspecs = [pl.BlockSpec((nb * kt.shape[-1], 512), lambda b, i: (b, 0)) for kt in kts]
    return pl.pallas_call(
        functools.partial(_attn_kernel, n_seg=len(kts), tq=tq, lam_init=lam_init),
        grid=(n_seq // nb, tiles),
        in_specs=[qspec] + ktspecs + vspecs + [pl.BlockSpec((4, HEAD_DIM_A), lambda b, i: (0, 0)),
                                             pl.BlockSpec((1, LANES), lambda b, i: (0, 0))],
        out_specs=qspec,
        out_shape=jax.ShapeDtypeStruct(q.shape, F32),
        compiler_params=_params(2),
        name="diff_attention",
    )(q, *kts, *vs, lam_vecs, subln_w.reshape(1, LANES))


def _chunk_loop(n, body):
    if n <= 2:
        for c in range(n):
            body(c)
    else:
        def step(c, carry):
            body(c)
            return carry
        lax.fori_loop(0, n, step, 0)


def _ssd_kernel(*refs, seq_len, has_h0):
    if has_h0:
        (xact, dtv, z_ref, h0_ref, arow_ref, dsk_ref, nw_ref,
         y_ref, ysc, s_sc, cd_sc, ce_sc, htf, htb) = refs
        hfin_ref = None
    else:
        (xact, dtv, z_ref, arow_ref, dsk_ref, nw_ref,
         y_ref, hfin_ref, ysc, s_sc, cd_sc, ce_sc, htf, htb) = refs
        h0_ref = None
    nc = seq_len // CHUNK
    n_pairs = N_HEADS_B // 2
    a_row = -jnp.exp(arow_ref[...])

    row_i = lax.broadcasted_iota(jnp.int32, (CHUNK, CHUNK), 0)
    col_i = lax.broadcasted_iota(jnp.int32, (CHUNK, CHUNK), 1)
    lower = col_i <= row_i
    upper = col_i >= row_i
    tri = (jnp.where(lower, 1.0, 0.0).astype(BF16), jnp.where(upper, 1.0, 0.0).astype(BF16))
    lo_half = col_i < HEAD_DIM_B
    col16_i = lax.broadcasted_iota(jnp.int32, (16, CHUNK), 1)

    def lane_column(mat, idx, lane_ids):
        return jnp.sum(jnp.where(lane_ids == idx, mat, 0.0), axis=1, keepdims=True)

    def split_pair(m):
        zero = jnp.zeros_like(m)
        return jnp.concatenate([jnp.where(lo_half, m, zero), jnp.where(lo_half, zero, m)], axis=0).astype(BF16)

    def local_phase(c):
        rows = pl.ds(pl.multiple_of(c * CHUNK, CHUNK), CHUNK)
        dtc = dtv[rows, :]
        dtt = dtc.T[0:16, :]
        per_dir = []
        for d in range(2):
            col = _dot_exact_rhs(tri[d], dtc * a_row)
            rowm = col.T[0:16, :]
            tot = jnp.broadcast_to(lane_column(rowm, CHUNK - 1 if d == 0 else 0, col16_i),
                                   (16, CHUNK))
            dte = jnp.exp(tot - rowm) * dtt
            per_dir.append((col, rowm, dte, jnp.exp(tot)))
        for g in range(2):
            bg = xact[rows, 512 + g * LANES:512 + (g + 1) * LANES]
            cg = xact[rows, 768 + g * LANES:768 + (g + 1) * LANES]
            cbm = _dot_nt(cg.astype(BF16), bg.astype(BF16))
            bgt = bg.T
            for pq in range(2):
                pi = g * 2 + pq
                lanes = slice(pi * LANES, (pi + 1) * LANES)
                xrhs = split_pair(xact[rows, lanes])
                y_pair = None
                for d in range(2):
                    col, rowm, dte, cdec = per_dir[d]
                    causal = lower if d == 0 else upper
                    m_parts, ce_parts, bw_parts, cd_parts = [], [], [], []
                    for hh in (2 * pi, 2 * pi + 1):
                        hd = 8 * d + hh
                        colb = lane_column(col, hd, col_i)
                        dec = jnp.exp(jnp.where(causal, colb - rowm[hd:hd + 1, :], NEG_BIG))
                        m_parts.append(cbm * dec * dtt[hd:hd + 1, :])
                        ce_parts.append(cg * jnp.exp(colb))
                        bw_parts.append(bgt * dte[hd:hd + 1, :])
                        cd_parts.append(cdec[hd:hd + 1, :])
                    yd = _dot(jnp.concatenate(m_parts, axis=1).astype(BF16), xrhs)
                    y_pair = yd if y_pair is None else y_pair + yd
                    s_sc[c, d, :, lanes] = _dot(jnp.concatenate(bw_parts, axis=1).astype(BF16), xrhs)
                    ce_sc[c, d, :, pi * 2 * LANES:(pi + 1) * 2 * LANES] = (
                        jnp.concatenate(ce_parts, axis=1).astype(BF16))
                    cd_sc[c, d, :, lanes] = jnp.broadcast_to(
                        jnp.where(lo_half[0:1, :], cd_parts[0], cd_parts[1]), (8, LANES))
                ysc[rows, lanes] = y_pair

    _chunk_loop(nc, local_phase)

    for d, ht in ((0, htf), (1, htb)):
        for qd in range(n_pairs):
            lanes = slice(qd * LANES, (qd + 1) * LANES)
            if has_h0:
                ht[:, lanes] = h0_ref[0, d, lanes, :].T
            else:
                ht[:, lanes] = jnp.zeros((D_STATE, LANES), F32)

    def scan_phase(j):
        for d, ht in ((0, htf), (1, htb)):
            c = j if d == 0 else nc - 1 - j
            rows = pl.ds(pl.multiple_of(c * CHUNK, CHUNK), CHUNK)
            for pi in range(n_pairs):
                lanes = slice(pi * LANES, (pi + 1) * LANES)
                hprev = ht[:, lanes]
                ysc[rows, lanes] += _dot(ce_sc[c, d, :, pi * 2 * LANES:(pi + 1) * 2 * LANES], split_pair(hprev))
                ht[:, lanes] = hprev * cd_sc[c, d, 0:1, lanes] + s_sc[c, d, :, lanes]

    _chunk_loop(nc, scan_phase)

    def finish(c):
        rows = pl.ds(pl.multiple_of(c * CHUNK, CHUNK), CHUNK)
        y = ysc[rows, :] + dsk_ref[...] * xact[rows, 0:D_INNER_B]
        y = y * _silu(z_ref[rows, :])
        ms = jnp.mean(y * y, axis=1, keepdims=True)
        y_ref[rows, :] = y * lax.rsqrt(ms + 1e-5) * nw_ref[...]

    _chunk_loop(nc, finish)

    if hfin_ref is not None:
        for d, ht in ((0, htf), (1, htb)):
            for qd in range(n_pairs):
                lanes = slice(qd * LANES, (qd + 1) * LANES)
                hfin_ref[0, d, lanes, :] = ht[:, lanes].T


def _ssd(xbc, dt, z, h0, consts, n_batch, seq_len):
    L = seq_len
    nc = L // CHUNK
    row = lambda n: pl.BlockSpec((L, n), lambda b: (b, 0))
    state_spec = pl.BlockSpec((1, 2, D_INNER_B, D_STATE), lambda b: (b, 0, 0, 0))
    has_h0 = h0 is not None
    in_specs = [row(CONV_DIM_B), row(LANES), row(D_INNER_B)]
    args = [xbc, dt, z]
    if has_h0:
        in_specs.append(state_spec)
        args.append(h0)
    for cst in consts:
        in_specs.append(pl.BlockSpec(cst.shape, lambda b: (0, 0)))
        args.append(cst)
    out_specs = [row(D_INNER_B)]
    out_shape = [jax.ShapeDtypeStruct((n_batch * L, D_INNER_B), F32)]
    if not has_h0:
        out_specs.append(state_spec)
        out_shape.append(jax.ShapeDtypeStruct((n_batch, 2, D_INNER_B, D_STATE), F32))
    scratch = [pltpu.VMEM((L, D_INNER_B), F32),
               pltpu.VMEM((nc, 2, D_STATE, D_INNER_B), F32),
               pltpu.VMEM((nc, 2, 8, D_INNER_B), F32),
               pltpu.VMEM((nc, 2, CHUNK, 2 * D_INNER_B), BF16),
               pltpu.VMEM((D_STATE, D_INNER_B), F32),
               pltpu.VMEM((D_STATE, D_INNER_B), F32)]
    return pl.pallas_call(
        functools.partial(_ssd_kernel, seq_len=L, has_h0=has_h0),
        grid=(n_batch,),
        in_specs=in_specs,
        out_specs=out_specs,
        out_shape=out_shape,
        scratch_shapes=scratch,
        compiler_params=_params(),
        name="bidir_ssd",
    )(*args)


def _tail_kernel(*refs, n_in):
    a_refs = refs[:n_in]
    (x_ref, mod_ref, wout_ref, g1_ref, b1_ref, wup_ref, wdown_ref, g2_ref, b2_ref, o_ref) = refs[n_in:]
    m = mod_ref[0]
    half = x_ref.shape[0] // 2
    n_chunks = D_FF // FF_CHUNK

    def head(rows):
        d = None
        off = 0
        for a_ref in a_refs:
            n = a_ref.shape[1]
            part = _dot(a_ref[rows, :].astype(BF16), wout_ref[off:off + n, :])
            d = part if d is None else d + part
            off += n
        x1 = _layer_norm(ALPHA * x_ref[rows, :] + m[2:3] * d, g1_ref[...], b1_ref[...])
        return x1, (x1 * (1.0 + m[4:5]) + m[3:4]).astype(BF16)

    def up(h, c):
        return _dot(h, wup_ref[:, c * FF_CHUNK:(c + 1) * FF_CHUNK])

    def down(u, acc, c):
        u = jnp.maximum(u, 0.0)
        part = _dot((u * u).astype(BF16), wdown_ref[c * FF_CHUNK:(c + 1) * FF_CHUNK, :])
        return part if acc is None else acc + part

    def finish(rows, x1, acc):
        o_ref[rows, :] = _layer_norm(ALPHA * x1 + m[5:6] * acc, g2_ref[...], b2_ref[...])

    rows_a, rows_b = slice(0, half), slice(half, 2 * half)
    x1_a, h_a = head(rows_a)
    u_next = up(h_a, 0)
    x1_b, h_b = head(rows_b)
    acc_a = None
    for c in range(n_chunks):
        u_cur, u_next = u_next, (up(h_a, c + 1) if c + 1 < n_chunks else up(h_b, 0))
        acc_a = down(u_cur, acc_a, c)
    finish(rows_a, x1_a, acc_a)
    acc_b = None
    for c in range(n_chunks):
        u_cur, u_next = u_next, (up(h_b, c + 1) if c + 1 < n_chunks else None)
        acc_b = down(u_cur, acc_b, c)
    finish(rows_b, x1_b, acc_b)


def _tail(a_list, x, mod, row_of_tile, w_out, ln1_g, ln1_b, w_up, w_down, ln2_g, ln2_b):
    t = x.shape[0]
    tm = TOKEN_TILE
    row = lambda n: pl.BlockSpec((tm, n), lambda i: (i, 0))
    vec = lambda v: v.reshape(1, D_MODEL)
    consts = [w_out, vec(ln1_g), vec(ln1_b), w_up, w_down, vec(ln2_g), vec(ln2_b)]
    return pl.pallas_call(
        functools.partial(_tail_kernel, n_in=len(a_list)),
        grid=(t // tm,),
        in_specs=[row(a.shape[1]) for a in a_list] + [row(D_MODEL), _mod_spec(row_of_tile)]
                 + [_const_spec(cst.shape) for cst in consts],
        out_specs=row(D_MODEL),
        out_shape=jax.ShapeDtypeStruct((t, D_MODEL), F32),
        compiler_params=_params(),
        name="outproj_mlp",
    )(*a_list, x, mod, *consts)


def _pool_kernel(x_ref, mod_ref, win_ref, pw_ref, ps_ref, o_ref, *, seq_len):
    L = seq_len
    m = mod_ref[0]
    t_i = lax.broadcasted_iota(jnp.int32, (L, L), 0)
    s_i = lax.broadcasted_iota(jnp.int32, (L, L), 1)
    t_g = lax.broadcasted_iota(jnp.int32, (L, POOL_GROUP_DIM), 0)
    bands, scales = [], []
    for w in POOL_WINDOWS:
        lo, hi = w // 2, w - w // 2
        bands.append(jnp.where((s_i >= t_i - lo) & (s_i < t_i + hi), 1.0 / w, 0.0).astype(BF16))
        scales.append(w / (jnp.minimum(t_g + hi, L) - jnp.maximum(t_g - lo, 0)).astype(F32))
    n_seq = x_ref.shape[0] // L
    seq_rows = [slice(s * L, (s + 1) * L) for s in range(n_seq)]
    hs = [(x_ref[r, :] * (1.0 + m[1:2]) + m[0:1]).astype(BF16) for r in seq_rows]
    us = [_dot(h, win_ref[...]) for h in hs]
    n_groups = len(POOL_WINDOWS)
    cols = [slice(g * POOL_GROUP_DIM, (g + 1) * POOL_GROUP_DIM) for g in range(n_groups)]

    def window_means(g):
        out = []
        for s in range(n_seq):
            ug = us[s][:, cols[g]]
            ug_hi = ug.astype(BF16)
            ug_lo = (ug - ug_hi.astype(F32)).astype(BF16)
            out.append((ug, _dot(bands[g], ug_hi) + _dot(bands[g], ug_lo)))
        return out

    def mix(g, means):
        return [_dot((wm * scales[g] - ug).astype(BF16), pw_ref[g]) * ps_ref[:, cols[g]] for ug, wm in means]

    mixed = []
    pending = window_means(0)
    for g in range(n_groups):
        nxt = window_means(g + 1) if g + 1 < n_groups else None
        mixed.append(mix(g, pending))
        pending = nxt
    for s in range(n_seq):
        o_ref[seq_rows[s], :] = jnp.concatenate([mixed[g][s] for g in range(n_groups)], axis=1)


def _pool_mixer(x, mod, row_of_tile, w_in, pool_w, pool_scale, seq_len, tm):
    t = x.shape[0]
    row = pl.BlockSpec((tm, D_MODEL), lambda i: (i, 0))
    consts = [w_in, pool_w, pool_scale.reshape(1, D_MODEL)]
    return pl.pallas_call(
        functools.partial(_pool_kernel, seq_len=seq_len),
        grid=(t // tm,),
        in_specs=[row, _mod_spec(row_of_tile)] + [_const_spec(cst.shape) for cst in consts],
        out_specs=row,
        out_shape=jax.ShapeDtypeStruct((t, D_MODEL), F32),
        compiler_params=_params(),
        name="pool_mixer",
    )(x, mod, *consts)


def _rope_tables(rows):
    r, col = jnp.meshgrid(jnp.arange(rows), jnp.arange(GRID_W), indexing="ij")
    r = r.reshape(-1).astype(F32)
    col = col.reshape(-1).astype(F32)
    n_freq = HEAD_DIM_A // 4
    inv = ROPE_BASE ** (-jnp.arange(n_freq, dtype=F32) / n_freq)
    ang = jnp.concatenate([r[:, None] * inv, col[:, None] * inv], -1)
    cos, sin = jnp.cos(ang), jnp.sin(ang)
    reps = 512 // HEAD_DIM_A
    return (jnp.tile(jnp.concatenate([cos, cos], -1), (1, reps)),
            jnp.tile(jnp.concatenate([-sin, sin], -1), (1, reps)))


def kernel(x_prompt, x_sample, cache_k_l0, cache_v_l0, state_ssm_l0, c, c_ctx, w_mod_l0, b_mod_l0, w_in_l0, conv_w_l0, conv_b_l0, a_log_l0, dt_bias_l0, d_skip_l0, ssm_norm_w_l0, lam_q1_l0, lam_k1_l0, lam_q2_l0, lam_k2_l0, subln_w_l0, w_out_l0, ln1_g_l0, ln1_b_l0, w_up_l0, w_down_l0, ln2_g_l0, ln2_b_l0, w_mod_l1, b_mod_l1, w_in_l1, pool_w_l1, pool_scale_l1, w_out_l1, ln1_g_l1, ln1_b_l1, w_up_l1, w_down_l1, ln2_g_l1, ln2_b_l1):
    n_ctx, l_ctx, _ = x_prompt.shape
    n_lat, l_lat, _ = x_sample.shape
    past = cache_k_l0.shape[1]
    xc = x_prompt.reshape(n_ctx * l_ctx, D_MODEL)
    xl = x_sample.reshape(n_lat * l_lat, D_MODEL)

    cond8 = jnp.concatenate([c_ctx[None, :], c, jnp.zeros((8 - 1 - n_lat, D_MODEL), F32)], axis=0)
    ctx_row = lambda i: 0
    lat_tiles = l_lat // TOKEN_TILE
    lat_row = lambda i: 1 + i // lat_tiles

    mod0 = _modulation(cond8, w_mod_l0, b_mod_l0)
    n_main = 4 * A_QK + A_V + D_INNER_B + CONV_DIM_B
    w_main = w_in_l0[:, :n_main].astype(BF16)
    w_dt = jnp.pad(w_in_l0[:, n_main:], ((0, 0), (0, LANES - 2 * N_HEADS_B))).astype(BF16)
    lane_pad = lambda v: jnp.pad(v.reshape(1, -1), ((0, 0), (0, LANES - 2 * N_HEADS_B)))
    conv_consts = (conv_w_l0, conv_b_l0.reshape(1, CONV_DIM_B), lane_pad(dt_bias_l0))
    ssd_consts = [lane_pad(a_log_l0),
                  jnp.repeat(d_skip_l0, HEAD_DIM_B).reshape(1, D_INNER_B),
                  ssm_norm_w_l0.reshape(1, D_INNER_B)]
    lam_vecs = jnp.stack([lam_q1_l0, lam_k1_l0, lam_q2_l0, lam_k2_l0], axis=0)
    lam_init = 0.8 - 0.6 * math.exp(-0.3 * 0)
    tail0 = (w_out_l0.astype(BF16), ln1_g_l0, ln1_b_l0, w_up_l0.astype(BF16), w_down_l0.astype(BF16),
             ln2_g_l0, ln2_b_l0)

    qc, kc, vc, zc, xbc_c, dt_c = _inproj0(xc, mod0, ctx_row, w_main, w_dt, *conv_consts, l_ctx, TOKEN_TILE)
    oa_c = _attention(qc, [kc], [vc], lam_vecs, subln_w_l0, l_ctx, l_ctx, 2, lam_init)
    y_c, new_ssm = _ssd(xbc_c, dt_c, zc, None, ssd_consts, n_ctx, l_ctx)
    xc = _tail([oa_c, y_c], xc, mod0, ctx_row, *tail0)

    ql, kl, vl, zl, xbc_l, dt_l = _inproj0(xl, mod0, lambda i: 1 + i, w_main, w_dt, *conv_consts, l_lat, l_lat,
                                           rope_tables=_rope_tables(l_lat // GRID_W))
    ck = jnp.transpose(cache_k_l0, (0, 2, 3, 4, 1))
    cv = cache_v_l0.reshape(n_lat * past, A_V)
    oa_l = _attention(ql, [ck, kl], [cv, vl], lam_vecs, subln_w_l0, l_lat, ATTN_Q_TILE, 1, lam_init)
    h0 = state_ssm_l0.reshape(n_lat, 2, D_INNER_B, D_STATE)
    (y_l,) = _ssd(xbc_l, dt_l, zl, h0, ssd_consts, n_lat, l_lat)
    xl = _tail([oa_l, y_l], xl, mod0, lat_row, *tail0)

    mod1 = _modulation(cond8, w_mod_l1, b_mod_l1)
    w_in1 = w_in_l1.astype(BF16)
    pool_w = pool_w_l1.astype(BF16)
    tail1 = (w_out_l1.astype(BF16), ln1_g_l1, ln1_b_l1, w_up_l1.astype(BF16), w_down_l1.astype(BF16),
             ln2_g_l1, ln2_b_l1)
    mix_c = _pool_mixer(xc, mod1, ctx_row, w_in1, pool_w, pool_scale_l1, l_ctx, TOKEN_TILE)
    xc = _tail([mix_c], xc, mod1, ctx_row, *tail1)
    mix_l = _pool_mixer(xl, mod1, lambda i: 1 + i, w_in1, pool_w, pool_scale_l1, l_lat, l_lat)
    xl = _tail([mix_l], xl, mod1, lat_row, *tail1)

    return (xc.reshape(n_ctx, l_ctx, D_MODEL), xl.reshape(n_lat, l_lat, D_MODEL),
            jnp.transpose(kc, (0, 4, 1, 2, 3)),
            vc.reshape(n_ctx, l_ctx, N_HEADS_A, 2 * HEAD_DIM_A),
            new_ssm.reshape(n_ctx, 2, N_HEADS_B, HEAD_DIM_B, D_STATE))
```

```python
import functools
import math

import jax
import jax.numpy as jnp
from jax import lax
from jax.experimental import pallas as pl
from jax.experimental.pallas import tpu as pltpu

F32 = jnp.float32
BF16 = jnp.bfloat16

D_MODEL = 1024
N_MOD = 6
N_HEADS_A = 4
HEAD_DIM_A = 64
A_QK = N_HEADS_A * HEAD_DIM_A
A_V = 2 * A_QK
D_INNER_B = 512
HEAD_DIM_B = 64
N_HEADS_B = 8
D_STATE = 128
D_CONV = 5
CONV_DIM_B = 1024
CHUNK = 128
GRID_W = 64
POOL_WINDOWS = (2, 4, 8, 16)
POOL_GROUP_DIM = 256
D_FF = 4096
ROPE_BASE = 10000.0
LN_EPS = 1e-5
DEPTH = 2
ALPHA = (2 * DEPTH) ** 0.25
NEG_BIG = -1e30
LOG2E = 1.4426950408889634

LANES = 128
TOKEN_TILE = 512
ATTN_Q_TILE = 256
CTX_SEQS_PER_STEP = 4
FF_CHUNK = 1024
CONV_COL_BLOCK = 256
VMEM_LIMIT = 56 * 1024 * 1024


def _dot(a, b):
    return jnp.dot(a, b, preferred_element_type=F32)


def _dot_nt(a, b):
    return lax.dot_general(a, b, (((1,), (1,)), ((), ())), preferred_element_type=F32)


def _bf16_pieces(a):
    a1 = a.astype(BF16)
    r1 = a - a1.astype(F32)
    a2 = r1.astype(BF16)
    a3 = (r1 - a2.astype(F32)).astype(BF16)
    return a1, a2, a3


def _dot_exact_rhs(t01, a):
    a1, a2, a3 = _bf16_pieces(a)
    return _dot(t01, a1) + _dot(t01, a2) + _dot(t01, a3)


def _sigmoid(x):
    return 1.0 / (1.0 + jnp.exp(-x))


def _silu(x):
    return x * _sigmoid(x)


def _layer_norm(x, g, b):
    mu = jnp.mean(x, axis=-1, keepdims=True)
    xc = x - mu
    var = jnp.mean(xc * xc, axis=-1, keepdims=True)
    return xc * lax.rsqrt(var + LN_EPS) * g + b


def _const_spec(shape):
    nd = len(shape)
    return pl.BlockSpec(shape, lambda *_: (0,) * nd, pipeline_mode=pl.Buffered(1))


def _params(n_axes=1):
    return pltpu.CompilerParams(dimension_semantics=("arbitrary",) * n_axes,
                                vmem_limit_bytes=VMEM_LIMIT)


def _mod_kernel(cond_ref, w_ref, b_ref, o_ref):
    c = cond_ref[...]
    s = _silu(c).astype(BF16)
    o_ref[...] = _dot(s, w_ref[...].astype(BF16)) + b_ref[...]


def _modulation(cond8, w_mod, b_mod):
    tn = 1024
    n = w_mod.shape[1]
    out = pl.pallas_call(
        _mod_kernel,
        grid=(n // tn,),
        in_specs=[pl.BlockSpec((8, D_MODEL), lambda j: (0, 0)),
                  pl.BlockSpec((D_MODEL, tn), lambda j: (0, j)),
                  pl.BlockSpec((1, tn), lambda j: (0, j))],
        out_specs=pl.BlockSpec((8, tn), lambda j: (0, j)),
        out_shape=jax.ShapeDtypeStruct((8, n), F32),
        compiler_params=_params(),
        name="modulation",
    )(cond8, w_mod, b_mod.reshape(1, n))
    return out.reshape(8, N_MOD, D_MODEL)


def _mod_spec(row_of_tile):
    return pl.BlockSpec((1, N_MOD, D_MODEL), lambda i: (row_of_tile(i), 0, 0))


def _rope(t, cosf, sinf):
    lane = lax.broadcasted_iota(jnp.int32, t.shape, 1)
    first = (lane & (HEAD_DIM_A - 1)) < HEAD_DIM_A // 2
    width = t.shape[1]
    swapped = jnp.where(first, pltpu.roll(t, width - HEAD_DIM_A // 2, 1),
                        pltpu.roll(t, HEAD_DIM_A // 2, 1))
    return t * cosf + swapped * sinf


def _conv_silu(xbc, cw_ref, cb_ref, cols, seq_len):
    rows = xbc.shape[0]
    pos = lax.rem(lax.broadcasted_iota(jnp.int32, xbc.shape, 0), seq_len)
    half = D_CONV // 2
    acc = cb_ref[:, cols] + cw_ref[half:half + 1, cols] * xbc
    for j in range(D_CONV):
        o = j - half
        if o == 0:
            continue
        shifted = pltpu.roll(xbc, (-o) % rows, 0)
        valid = (pos >= -o) if o < 0 else (pos < seq_len - o)
        acc = acc + cw_ref[j:j + 1, cols] * jnp.where(valid, shifted, 0.0)
    return _silu(acc)


def _softplus(t):
    return jnp.maximum(t, 0.0) + jnp.log(1.0 + jnp.exp(-jnp.abs(t)))


def _inproj0_kernel(*refs, rope, seq_len):
    if rope:
        (x_ref, mod_ref, w_ref, wdt_ref, cw_ref, cb_ref, dtb_ref, cos_ref, sin_ref,
         q_ref, k_ref, v_ref, z_ref, xbc_ref, dt_ref) = refs
    else:
        (x_ref, mod_ref, w_ref, wdt_ref, cw_ref, cb_ref, dtb_ref,
         q_ref, k_ref, v_ref, z_ref, xbc_ref, dt_ref) = refs
    m = mod_ref[0]
    for b in range(x_ref.shape[0] // seq_len):
        rows = slice(b * seq_len, (b + 1) * seq_len)
        h = (x_ref[rows, :] * (1.0 + m[1:2]) + m[0:1]).astype(BF16)
        for c0 in range(0, CONV_DIM_B, CONV_COL_BLOCK):
            cols = slice(c0, c0 + CONV_COL_BLOCK)
            raw = _dot(h, w_ref[:, 2048 + c0:2048 + c0 + CONV_COL_BLOCK])
            xbc_ref[rows, cols] = _conv_silu(raw, cw_ref, cb_ref, cols, seq_len)
        dt_ref[rows, :] = _softplus(_dot(h, wdt_ref[...]) + dtb_ref[...])
        q = _dot(h, w_ref[:, 0:512])
        k = _dot(h, w_ref[:, 512:1024])
        if rope:
            q = _rope(q, cos_ref[...], sin_ref[...])
            k = _rope(k, cos_ref[...], sin_ref[...])
        q_ref[rows, :] = q
        for blk in range(2 * A_QK // LANES):
            t = k[:, blk * LANES:(blk + 1) * LANES].T
            which, head = blk // 2, (blk % 2) * 2
            k_ref[b, which, head] = t[0:HEAD_DIM_A]
            k_ref[b, which, head + 1] = t[HEAD_DIM_A:2 * HEAD_DIM_A]
        vals = _dot(h, w_ref[:, 1024:1536])
        for hh in range(N_HEADS_A):
            v_ref[pl.ds(b * seq_len * N_HEADS_A + hh, seq_len, stride=N_HEADS_A), :] = (
                vals[:, hh * LANES:(hh + 1) * LANES])
        z_ref[rows, :] = _dot(h, w_ref[:, 1536:2048])


def _inproj0(x, mod, row_of_tile, w_main, w_dt, conv_w, conv_b, dt_bias, seq_len, tm, rope_tables=None):
    t = x.shape[0]
    assert tm % seq_len == 0
    row = lambda n: pl.BlockSpec((tm, n), lambda i: (i, 0))
    flat = lambda n: jax.ShapeDtypeStruct((t, n), F32)
    consts = [w_main, w_dt, conv_w, conv_b, dt_bias]
    in_specs = [row(D_MODEL), _mod_spec(row_of_tile)] + [_const_spec(cst.shape) for cst in consts]
    args = [x, mod] + consts
    if rope_tables is not None:
        assert tm == seq_len
        tab = pl.BlockSpec((tm, 512), lambda i: (0, 0))
        in_specs += [tab, tab]
        args += list(rope_tables)
    kt_dims = (2, N_HEADS_A, HEAD_DIM_A, seq_len)
    k_spec = pl.BlockSpec((tm // seq_len,) + kt_dims, lambda i: (i, 0, 0, 0, 0))
    k_shape = jax.ShapeDtypeStruct((t // seq_len,) + kt_dims, F32)
    return pl.pallas_call(
        functools.partial(_inproj0_kernel, rope=rope_tables is not None, seq_len=seq_len),
        grid=(t // tm,),
        in_specs=in_specs,
        out_specs=[row(512), k_spec, pl.BlockSpec((tm * N_HEADS_A, LANES), lambda i: (i, 0)),
                   row(512), row(CONV_DIM_B), row(LANES)],
        out_shape=[flat(512), k_shape, jax.ShapeDtypeStruct((t * N_HEADS_A, LANES), F32),
                   flat(512), flat(CONV_DIM_B), flat(LANES)],
        compiler_params=_params(),
        name="inproj0",
    )(*args)


def _lambda_full(lam_ref, lam_init):
    lv = lam_ref[...]
    return (jnp.exp(jnp.sum(lv[0:1] * lv[1:2], axis=1, keepdims=True))
            - jnp.exp(jnp.sum(lv[2:3] * lv[3:4], axis=1, keepdims=True)) + lam_init)


def _attn_kernel(*refs, n_seg, tq, lam_init):
    q_ref = refs[0]
    kt_refs = refs[1:1 + n_seg]
    v_refs = refs[1 + n_seg:1 + 2 * n_seg]
    lam_ref, sub_ref, o_ref = refs[1 + 2 * n_seg:]
    lam = _lambda_full(lam_ref, lam_init)
    lane = lax.broadcasted_iota(jnp.int32, (1, A_QK), 1)
    masks = [(lane >= h * HEAD_DIM_A) & (lane < (h + 1) * HEAD_DIM_A) for h in range(N_HEADS_A)]
    keys = [r.shape[-1] for r in kt_refs]

    def stacked(qx):
        return jnp.concatenate([jnp.where(m, qx, 0.0) for m in masks], axis=0).astype(BF16)

    def exp_scores(qx, b, which):
        qs = stacked(qx)
        s = [_dot(qs, r[b, which].reshape(A_QK, n).astype(BF16)) for r, n in zip(kt_refs, keys)]
        mx = functools.reduce(jnp.maximum, [jnp.max(x, axis=1, keepdims=True) for x in s])
        return [jnp.exp2(x - mx).astype(BF16) for x in s]

    outs = []
    for b in range(kt_refs[0].shape[0]):
        q = q_ref[b * tq:(b + 1) * tq, :] * (HEAD_DIM_A ** -0.5 * LOG2E)
        e1 = exp_scores(q[:, 0:A_QK], b, 0)
        e2 = exp_scores(q[:, A_QK:2 * A_QK], b, 1)
        heads = []
        for h in range(N_HEADS_A):
            hrows = slice(h * tq, (h + 1) * tq)
            n1 = n2 = None
            for s, (v_ref, n) in enumerate(zip(v_refs, keys)):
                v_h = v_ref[pl.ds(b * n * N_HEADS_A + h, n, stride=N_HEADS_A), :]
                v_ext = jnp.concatenate([v_h.astype(BF16),
                                         jnp.ones((n, LANES), BF16)], axis=1)
                p1 = _dot(e1[s][hrows], v_ext)
                p2 = _dot(e2[s][hrows], v_ext)
                n1 = p1 if n1 is None else n1 + p1
                n2 = p2 if n2 is None else n2 + p2
            heads.append(n1[:, 0:LANES] / n1[:, LANES:] - lam * (n2[:, 0:LANES] / n2[:, LANES:]))
        o = jnp.concatenate(heads, axis=0)
        ms = jnp.mean(o * o, axis=1, keepdims=True)
        o = (o * lax.rsqrt(ms + 1e-5) * sub_ref[...]) * (1.0 - lam_init)
        outs.append(jnp.concatenate([o[h * tq:(h + 1) * tq] for h in range(N_HEADS_A)], axis=1))
    o_ref[...] = jnp.concatenate(outs, axis=0)


def _attention(q, kts, vs, lam_vecs, subln_w, q_len, tq, seqs_per_step, lam_init):
    n_seq = kts[0].shape[0]
    nb = seqs_per_step
    tiles = q_len // tq
    assert tiles == 1 or nb == 1
    qspec = pl.BlockSpec((nb * tq, 512), lambda b, i: (b * tiles + i, 0))
    ktspecs = [pl.BlockSpec((nb,) + kt.shape[1:], lambda b, i: (b, 0, 0, 0, 0)) for kt in kts]
    vspecs = [pl.BlockSpec((nb * kt.shape[-1] * N_HEADS_A, LANES), lambda b, i: (b, 0)) for kt in kts]
    return pl.pallas_call(
        functools.partial(_attn_kernel, n_seg=len(kts), tq=tq, lam_init=lam_init),
        grid=(n_seq // nb, tiles),
        in_specs=[qspec] + ktspecs + vspecs + [pl.BlockSpec((4, HEAD_DIM_A), lambda b, i: (0, 0)),
                                             pl.BlockSpec((1, LANES), lambda b, i: (0, 0))],
        out_specs=qspec,
        out_shape=jax.ShapeDtypeStruct(q.shape, F32),
        compiler_params=_params(2),
        name="diff_attention",
    )(q, *kts, *vs, lam_vecs, subln_w.reshape(1, LANES))


def _chunk_loop(n, body, unroll=1):
    if n <= 2:
        for c in range(n):
            body(c)
    else:
        def step(c, carry):
            body(c)
            return carry
        lax.fori_loop(0, n, step, 0, unroll=unroll)


def _ssd_kernel(*refs, seq_len, n_seq, has_h0):
    if has_h0:
        (xact, dtv, z_ref, h0_ref, arow_ref, dsk_ref, nw_ref,
         y_ref, ysc, s_sc, cd_sc, ce_sc, htf, htb) = refs
        hfin_ref = None
    else:
        (xact, dtv, z_ref, arow_ref, dsk_ref, nw_ref,
         y_ref, hfin_ref, ysc, s_sc, cd_sc, ce_sc, htf, htb) = refs
        h0_ref = None
    nc = seq_len // CHUNK
    n_pairs = N_HEADS_B // 2
    a_row = -jnp.exp(arow_ref[...]) * LOG2E

    row_i = lax.broadcasted_iota(jnp.int32, (CHUNK, CHUNK), 0)
    col_i = lax.broadcasted_iota(jnp.int32, (CHUNK, CHUNK), 1)
    lower = col_i <= row_i
    upper = col_i >= row_i
    tri = (jnp.where(lower, 1.0, 0.0).astype(BF16), jnp.where(upper, 1.0, 0.0).astype(BF16))
    lo_half = col_i < HEAD_DIM_B
    col16_i = lax.broadcasted_iota(jnp.int32, (16, CHUNK), 1)

    def lane_column(mat, idx, lane_ids):
        return jnp.sum(jnp.where(lane_ids == idx, mat, 0.0), axis=1, keepdims=True)

    def split_pair(m):
        zero = jnp.zeros_like(m)
        return jnp.concatenate([jnp.where(lo_half, m, zero), jnp.where(lo_half, zero, m)], axis=0).astype(BF16)

    def local_phase(c):
        rows = pl.ds(pl.multiple_of(c * CHUNK, CHUNK), CHUNK)
        dtc = dtv[rows, :]
        dtt = dtc.T[0:16, :]
        per_dir = []
        for d in range(2):
            col = _dot_exact_rhs(tri[d], dtc * a_row)
            rowm = col.T[0:16, :]
            tot = jnp.broadcast_to(lane_column(rowm, CHUNK - 1 if d == 0 else 0, col16_i),
                                   (16, CHUNK))
            dte = jnp.exp2(tot - rowm) * dtt
            per_dir.append((col, rowm - jnp.log2(dtt), dte, jnp.exp2(tot)))
        for g in range(2):
            bg = xact[rows, 512 + g * LANES:512 + (g + 1) * LANES]
            cg = xact[rows, 768 + g * LANES:768 + (g + 1) * LANES]
            cbm = _dot_nt(cg.astype(BF16), bg.astype(BF16))
            bgt = bg.T
            for pq in range(2):
                pi = g * 2 + pq
                lanes = slice(pi * LANES, (pi + 1) * LANES)
                xrhs = split_pair(xact[rows, lanes])
                y_pair = None
                for d in range(2):
                    col, row_dt, dte, cdec = per_dir[d]
                    causal = lower if d == 0 else upper
                    m_parts, ce_parts, bw_parts, cd_parts = [], [], [], []
                    for hh in (2 * pi, 2 * pi + 1):
                        hd = 8 * d + hh
                        colb = lane_column(col, hd, col_i)
                        m_parts.append(cbm * jnp.exp2(jnp.where(causal, colb - row_dt[hd:hd + 1, :], NEG_BIG)))
                        ce_parts.append(cg * jnp.exp2(colb))
                        bw_parts.append(bgt * dte[hd:hd + 1, :])
                        cd_parts.append(cdec[hd:hd + 1, :])
                    yd = _dot(jnp.concatenate(m_parts, axis=1).astype(BF16), xrhs)
                    y_pair = yd if y_pair is None else y_pair + yd
                    s_sc[c, d, :, lanes] = _dot(jnp.concatenate(bw_parts, axis=1).astype(BF16), xrhs)
                    ce_sc[c, d, :, pi * 2 * LANES:(pi + 1) * 2 * LANES] = (
                        jnp.concatenate(ce_parts, axis=1).astype(BF16))
                    cd_sc[c, d, :, lanes] = jnp.broadcast_to(
                        jnp.where(lo_half[0:1, :], cd_parts[0], cd_parts[1]), (8, LANES))
                ysc[rows, lanes] = y_pair

    _chunk_loop(n_seq * nc, local_phase, unroll=2)

    def scan_sequence(sq):
        for d, ht in ((0, htf), (1, htb)):
            for qd in range(n_pairs):
                lanes = slice(qd * LANES, (qd + 1) * LANES)
                if has_h0:
                    ht[:, lanes] = h0_ref[sq, d, lanes, :].T
                else:
                    ht[:, lanes] = jnp.zeros((D_STATE, LANES), F32)

        def scan_phase(j):
            for d, ht in ((0, htf), (1, htb)):
                c = sq * nc + (j if d == 0 else nc - 1 - j)
                rows = pl.ds(pl.multiple_of(c * CHUNK, CHUNK), CHUNK)
                for pi in range(n_pairs):
                    lanes = slice(pi * LANES, (pi + 1) * LANES)
                    hprev = ht[:, lanes]
                    ysc[rows, lanes] += _dot(ce_sc[c, d, :, pi * 2 * LANES:(pi + 1) * 2 * LANES],
                                             split_pair(hprev))
                    ht[:, lanes] = hprev * cd_sc[c, d, 0:1, lanes] + s_sc[c, d, :, lanes]

        _chunk_loop(nc, scan_phase)
        if hfin_ref is not None:
            for d, ht in ((0, htf), (1, htb)):
                for qd in range(n_pairs):
                    lanes = slice(qd * LANES, (qd + 1) * LANES)
                    hfin_ref[sq, d, lanes, :] = ht[:, lanes].T

    if n_seq == 1:
        scan_sequence(0)
    else:
        def seq_step(sq, carry):
            scan_sequence(sq)
            return carry
        lax.fori_loop(0, n_seq, seq_step, 0)

    def finish(c):
        rows = pl.ds(pl.multiple_of(c * CHUNK, CHUNK), CHUNK)
        y = ysc[rows, :] + dsk_ref[...] * xact[rows, 0:D_INNER_B]
        y = y * _silu(z_ref[rows, :])
        ms = jnp.mean(y * y, axis=1, keepdims=True)
        y_ref[rows, :] = y * lax.rsqrt(ms + 1e-5) * nw_ref[...]

    _chunk_loop(n_seq * nc, finish)


def _ssd(xbc, dt, z, h0, consts, n_batch, seq_len, seqs_per_step):
    nb = seqs_per_step
    L = nb * seq_len
    nc = L // CHUNK
    row = lambda n: pl.BlockSpec((L, n), lambda b: (b, 0))
    state_spec = pl.BlockSpec((nb, 2, D_INNER_B, D_STATE), lambda b: (b, 0, 0, 0))
    has_h0 = h0 is not None
    in_specs = [row(CONV_DIM_B), row(LANES), row(D_INNER_B)]
    args = [xbc, dt, z]
    if has_h0:
        in_specs.append(state_spec)
        args.append(h0)
    for cst in consts:
        in_specs.append(pl.BlockSpec(cst.shape, lambda b: (0, 0)))
        args.append(cst)
    out_specs = [row(D_INNER_B)]
    out_shape = [jax.ShapeDtypeStruct((n_batch * seq_len, D_INNER_B), F32)]
    if not has_h0:
        out_specs.append(state_spec)
        out_shape.append(jax.ShapeDtypeStruct((n_batch, 2, D_INNER_B, D_STATE), F32))
    scratch = [pltpu.VMEM((L, D_INNER_B), F32),
               pltpu.VMEM((nc, 2, D_STATE, D_INNER_B), F32),
               pltpu.VMEM((nc, 2, 8, D_INNER_B), F32),
               pltpu.VMEM((nc, 2, CHUNK, 2 * D_INNER_B), BF16),
               pltpu.VMEM((D_STATE, D_INNER_B), F32),
               pltpu.VMEM((D_STATE, D_INNER_B), F32)]
    return pl.pallas_call(
        functools.partial(_ssd_kernel, seq_len=seq_len, n_seq=nb, has_h0=has_h0),
        grid=(n_batch // nb,),
        in_specs=in_specs,
        out_specs=out_specs,
        out_shape=out_shape,
        scratch_shapes=scratch,
        compiler_params=_params(),
        name="bidir_ssd",
    )(*args)


def _tail_kernel(*refs, n_in):
    a_refs = refs[:n_in]
    (x_ref, mod_ref, wout_ref, g1_ref, b1_ref, wup_ref, wdown_ref, g2_ref, b2_ref, o_ref) = refs[n_in:]
    m = mod_ref[0]
    half = x_ref.shape[0] // 2
    n_chunks = D_FF // FF_CHUNK

    def head(rows):
        d = None
        off = 0
        for a_ref in a_refs:
            n = a_ref.shape[1]
            part = _dot(a_ref[rows, :].astype(BF16), wout_ref[off:off + n, :])
            d = part if d is None else d + part
            off += n
        x1 = _layer_norm(ALPHA * x_ref[rows, :] + m[2:3] * d, g1_ref[...], b1_ref[...])
        return x1, (x1 * (1.0 + m[4:5]) + m[3:4]).astype(BF16)

    def up(h, c):
        return _dot(h, wup_ref[:, c * FF_CHUNK:(c + 1) * FF_CHUNK])

    def down(u, acc, c):
        u = jnp.maximum(u, 0.0)
        part = _dot((u * u).astype(BF16), wdown_ref[c * FF_CHUNK:(c + 1) * FF_CHUNK, :])
        return part if acc is None else acc + part

    def finish(rows, x1, acc):
        o_ref[rows, :] = _layer_norm(ALPHA * x1 + m[5:6] * acc, g2_ref[...], b2_ref[...])

    rows_a, rows_b = slice(0, half), slice(half, 2 * half)
    x1_a, h_a = head(rows_a)
    u_next = up(h_a, 0)
    x1_b, h_b = head(rows_b)
    acc_a = None
    for c in range(n_chunks):
        u_cur, u_next = u_next, (up(h_a, c + 1) if c + 1 < n_chunks else up(h_b, 0))
        acc_a = down(u_cur, acc_a, c)
    finish(rows_a, x1_a, acc_a)
    acc_b = None
    for c in range(n_chunks):
        u_cur, u_next = u_next, (up(h_b, c + 1) if c + 1 < n_chunks else None)
        acc_b = down(u_cur, acc_b, c)
    finish(rows_b, x1_b, acc_b)


def _tail(a_list, x, mod, row_of_tile, w_out, ln1_g, ln1_b, w_up, w_down, ln2_g, ln2_b):
    t = x.shape[0]
    tm = TOKEN_TILE
    row = lambda n: pl.BlockSpec((tm, n), lambda i: (i, 0))
    vec = lambda v: v.reshape(1, D_MODEL)
    consts = [w_out, vec(ln1_g), vec(ln1_b), w_up, w_down, vec(ln2_g), vec(ln2_b)]
    return pl.pallas_call(
        functools.partial(_tail_kernel, n_in=len(a_list)),
        grid=(t // tm,),
        in_specs=[row(a.shape[1]) for a in a_list] + [row(D_MODEL), _mod_spec(row_of_tile)]
                 + [_const_spec(cst.shape) for cst in consts],
        out_specs=row(D_MODEL),
        out_shape=jax.ShapeDtypeStruct((t, D_MODEL), F32),
        compiler_params=_params(),
        name="outproj_mlp",
    )(*a_list, x, mod, *consts)


def _pool_kernel(x_ref, mod_ref, win_ref, pw_ref, ps_ref, o_ref, *, seq_len):
    L = seq_len
    m = mod_ref[0]
    t_i = lax.broadcasted_iota(jnp.int32, (L, L), 0)
    s_i = lax.broadcasted_iota(jnp.int32, (L, L), 1)
    t_g = lax.broadcasted_iota(jnp.int32, (L, POOL_GROUP_DIM), 0)
    bands, scales = [], []
    for w in POOL_WINDOWS:
        lo, hi = w // 2, w - w // 2
        bands.append(jnp.where((s_i >= t_i - lo) & (s_i < t_i + hi), 1.0 / w, 0.0).astype(BF16))
        scales.append(w / (jnp.minimum(t_g + hi, L) - jnp.maximum(t_g - lo, 0)).astype(F32))
    n_seq = x_ref.shape[0] // L
    seq_rows = [slice(s * L, (s + 1) * L) for s in range(n_seq)]
    hs = [(x_ref[r, :] * (1.0 + m[1:2]) + m[0:1]).astype(BF16) for r in seq_rows]
    us = [_dot(h, win_ref[...]) for h in hs]
    n_groups = len(POOL_WINDOWS)
    cols = [slice(g * POOL_GROUP_DIM, (g + 1) * POOL_GROUP_DIM) for g in range(n_groups)]

    def window_means(g):
        out = []
        for s in range(n_seq):
            ug = us[s][:, cols[g]]
            ug_hi = ug.astype(BF16)
            ug_lo = (ug - ug_hi.astype(F32)).astype(BF16)
            out.append((ug, _dot(bands[g], ug_hi) + _dot(bands[g], ug_lo)))
        return out

    def mix(g, means):
        return [_dot((wm * scales[g] - ug).astype(BF16), pw_ref[g]) * ps_ref[:, cols[g]] for ug, wm in means]

    mixed = []
    pending = window_means(0)
    for g in range(n_groups):
        nxt = window_means(g + 1) if g + 1 < n_groups else None
        mixed.append(mix(g, pending))
        pending = nxt
    for s in range(n_seq):
        o_ref[seq_rows[s], :] = jnp.concatenate([mixed[g][s] for g in range(n_groups)], axis=1)


def _pool_mixer(x, mod, row_of_tile, w_in, pool_w, pool_scale, seq_len, tm):
    t = x.shape[0]
    row = pl.BlockSpec((tm, D_MODEL), lambda i: (i, 0))
    consts = [w_in, pool_w, pool_scale.reshape(1, D_MODEL)]
    return pl.pallas_call(
        functools.partial(_pool_kernel, seq_len=seq_len),
        grid=(t // tm,),
        in_specs=[row, _mod_spec(row_of_tile)] + [_const_spec(cst.shape) for cst in consts],
        out_specs=row,
        out_shape=jax.ShapeDtypeStruct((t, D_MODEL), F32),
        compiler_params=_params(),
        name="pool_mixer",
    )(x, mod, *consts)


def _rope_tables(rows):
    r, col = jnp.meshgrid(jnp.arange(rows), jnp.arange(GRID_W), indexing="ij")
    r = r.reshape(-1).astype(F32)
    col = col.reshape(-1).astype(F32)
    n_freq = HEAD_DIM_A // 4
    inv = ROPE_BASE ** (-jnp.arange(n_freq, dtype=F32) / n_freq)
    ang = jnp.concatenate([r[:, None] * inv, col[:, None] * inv], -1)
    cos, sin = jnp.cos(ang), jnp.sin(ang)
    reps = 512 // HEAD_DIM_A
    return (jnp.tile(jnp.concatenate([cos, cos], -1), (1, reps)),
            jnp.tile(jnp.concatenate([-sin, sin], -1), (1, reps)))


def kernel(x_prompt, x_sample, cache_k_l0, cache_v_l0, state_ssm_l0, c, c_ctx, w_mod_l0, b_mod_l0, w_in_l0, conv_w_l0, conv_b_l0, a_log_l0, dt_bias_l0, d_skip_l0, ssm_norm_w_l0, lam_q1_l0, lam_k1_l0, lam_q2_l0, lam_k2_l0, subln_w_l0, w_out_l0, ln1_g_l0, ln1_b_l0, w_up_l0, w_down_l0, ln2_g_l0, ln2_b_l0, w_mod_l1, b_mod_l1, w_in_l1, pool_w_l1, pool_scale_l1, w_out_l1, ln1_g_l1, ln1_b_l1, w_up_l1, w_down_l1, ln2_g_l1, ln2_b_l1):
    n_ctx, l_ctx, _ = x_prompt.shape
    n_lat, l_lat, _ = x_sample.shape
    past = cache_k_l0.shape[1]
    xc = x_prompt.reshape(n_ctx * l_ctx, D_MODEL)
    xl = x_sample.reshape(n_lat * l_lat, D_MODEL)

    cond8 = jnp.concatenate([c_ctx[None, :], c, jnp.zeros((8 - 1 - n_lat, D_MODEL), F32)], axis=0)
    ctx_row = lambda i: 0
    lat_tiles = l_lat // TOKEN_TILE
    lat_row = lambda i: 1 + i // lat_tiles

    mod0 = _modulation(cond8, w_mod_l0, b_mod_l0)
    n_main = 4 * A_QK + A_V + D_INNER_B + CONV_DIM_B
    w_main = w_in_l0[:, :n_main].astype(BF16)
    w_dt = jnp.pad(w_in_l0[:, n_main:], ((0, 0), (0, LANES - 2 * N_HEADS_B))).astype(BF16)
    lane_pad = lambda v: jnp.pad(v.reshape(1, -1), ((0, 0), (0, LANES - 2 * N_HEADS_B)))
    conv_consts = (conv_w_l0, conv_b_l0.reshape(1, CONV_DIM_B), lane_pad(dt_bias_l0))
    ssd_consts = [lane_pad(a_log_l0),
                  jnp.repeat(d_skip_l0, HEAD_DIM_B).reshape(1, D_INNER_B),
                  ssm_norm_w_l0.reshape(1, D_INNER_B)]
    lam_vecs = jnp.stack([lam_q1_l0, lam_k1_l0, lam_q2_l0, lam_k2_l0], axis=0)
    lam_init = 0.8 - 0.6 * math.exp(-0.3 * 0)
    tail0 = (w_out_l0.astype(BF16), ln1_g_l0, ln1_b_l0, w_up_l0.astype(BF16), w_down_l0.astype(BF16),
             ln2_g_l0, ln2_b_l0)

    qc, kc, vc, zc, xbc_c, dt_c = _inproj0(xc, mod0, ctx_row, w_main, w_dt, *conv_consts, l_ctx, TOKEN_TILE)
    oa_c = _attention(qc, [kc], [vc], lam_vecs, subln_w_l0, l_ctx, l_ctx, CTX_SEQS_PER_STEP, lam_init)
    y_c, new_ssm = _ssd(xbc_c, dt_c, zc, None, ssd_consts, n_ctx, l_ctx, CTX_SEQS_PER_STEP)
    xc = _tail([oa_c, y_c], xc, mod0, ctx_row, *tail0)

    ql, kl, vl, zl, xbc_l, dt_l = _inproj0(xl, mod0, lambda i: 1 + i, w_main, w_dt, *conv_consts, l_lat, l_lat,
                                           rope_tables=_rope_tables(l_lat // GRID_W))
    ck = jnp.transpose(cache_k_l0, (0, 2, 3, 4, 1))
    cv = cache_v_l0.reshape(n_lat * past * N_HEADS_A, 2 * HEAD_DIM_A)
    oa_l = _attention(ql, [ck, kl], [cv, vl], lam_vecs, subln_w_l0, l_lat, ATTN_Q_TILE, 1, lam_init)
    h0 = state_ssm_l0.reshape(n_lat, 2, D_INNER_B, D_STATE)
    (y_l,) = _ssd(xbc_l, dt_l, zl, h0, ssd_consts, n_lat, l_lat, 1)
    xl = _tail([oa_l, y_l], xl, mod0, lat_row, *tail0)

    mod1 = _modulation(cond8, w_mod_l1, b_mod_l1)
    w_in1 = w_in_l1.astype(BF16)
    pool_w = pool_w_l1.astype(BF16)
    tail1 = (w_out_l1.astype(BF16), ln1_g_l1, ln1_b_l1, w_up_l1.astype(BF16), w_down_l1.astype(BF16),
             ln2_g_l1, ln2_b_l1)
    mix_c = _pool_mixer(xc, mod1, ctx_row, w_in1, pool_w, pool_scale_l1, l_ctx, TOKEN_TILE)
    xc = _tail([mix_c], xc, mod1, ctx_row, *tail1)
    mix_l = _pool_mixer(xl, mod1, lambda i: 1 + i, w_in1, pool_w, pool_scale_l1, l_lat, l_lat)
    xl = _tail([mix_l], xl, mod1, lat_row, *tail1)

    return (xc.reshape(n_ctx, l_ctx, D_MODEL), xl.reshape(n_lat, l_lat, D_MODEL),
            jnp.transpose(kc, (0, 4, 1, 2, 3)),
            vc.reshape(n_ctx, l_ctx, N_HEADS_A, 2 * HEAD_DIM_A),
            new_ssm.reshape(n_ctx, 2, N_HEADS_B, HEAD_DIM_B, D_STATE))
```

```python
import functools
import math

import jax
import jax.numpy as jnp
from jax import lax
from jax.experimental import pallas as pl
from jax.experimental.pallas import tpu as pltpu

F32 = jnp.float32
BF16 = jnp.bfloat16

D_MODEL = 1024
N_MOD = 6
N_HEADS_A = 4
HEAD_DIM_A = 64
A_QK = N_HEADS_A * HEAD_DIM_A
A_V = 2 * A_QK
D_INNER_B = 512
HEAD_DIM_B = 64
N_HEADS_B = 8
D_STATE = 128
D_CONV = 5
CONV_DIM_B = 1024
CHUNK = 128
GRID_W = 64
POOL_WINDOWS = (2, 4, 8, 16)
POOL_GROUP_DIM = 256
D_FF = 4096
ROPE_BASE = 10000.0
LN_EPS = 1e-5
DEPTH = 2
ALPHA = (2 * DEPTH) ** 0.25
NEG_BIG = -1e30
LOG2E = 1.4426950408889634

LANES = 128
TOKEN_TILE = 512
ATTN_Q_TILE = 256
CTX_SEQS_PER_STEP = 4
FF_CHUNK = 1024
CONV_COL_BLOCK = 256
TAIL_SUBTILES = 2
VMEM_LIMIT = 56 * 1024 * 1024


def _dot(a, b):
    return jnp.dot(a, b, preferred_element_type=F32)


def _dot_nt(a, b):
    return lax.dot_general(a, b, (((1,), (1,)), ((), ())), preferred_element_type=F32)


def _bf16_pieces(a):
    a1 = a.astype(BF16)
    r1 = a - a1.astype(F32)
    a2 = r1.astype(BF16)
    a3 = (r1 - a2.astype(F32)).astype(BF16)
    return a1, a2, a3


def _dot_exact_rhs(t01, a):
    a1, a2, a3 = _bf16_pieces(a)
    return _dot(t01, a1) + _dot(t01, a2) + _dot(t01, a3)


def _sigmoid(x):
    return 1.0 / (1.0 + jnp.exp(-x))


def _silu(x):
    return x * _sigmoid(x)


def _layer_norm(x, g, b):
    mu = jnp.mean(x, axis=-1, keepdims=True)
    xc = x - mu
    var = jnp.mean(xc * xc, axis=-1, keepdims=True)
    return xc * lax.rsqrt(var + LN_EPS) * g + b


def _const_spec(shape):
    nd = len(shape)
    return pl.BlockSpec(shape, lambda *_: (0,) * nd, pipeline_mode=pl.Buffered(1))


def _params(n_axes=1):
    return pltpu.CompilerParams(dimension_semantics=("arbitrary",) * n_axes,
                                vmem_limit_bytes=VMEM_LIMIT)


def _mod_kernel(cond_ref, w_ref, b_ref, o_ref):
    c = cond_ref[...]
    s = _silu(c).astype(BF16)
    o_ref[...] = _dot(s, w_ref[...].astype(BF16)) + b_ref[...]


def _modulation(cond8, w_mod, b_mod):
    tn = 1024
    n = w_mod.shape[1]
    out = pl.pallas_call(
        _mod_kernel,
        grid=(n // tn,),
        in_specs=[pl.BlockSpec((8, D_MODEL), lambda j: (0, 0)),
                  pl.BlockSpec((D_MODEL, tn), lambda j: (0, j)),
                  pl.BlockSpec((1, tn), lambda j: (0, j))],
        out_specs=pl.BlockSpec((8, tn), lambda j: (0, j)),
        out_shape=jax.ShapeDtypeStruct((8, n), F32),
        compiler_params=_params(),
        name="modulation",
    )(cond8, w_mod, b_mod.reshape(1, n))
    return out.reshape(8, N_MOD, D_MODEL)


def _mod_spec(row_of_tile):
    return pl.BlockSpec((1, N_MOD, D_MODEL), lambda i: (row_of_tile(i), 0, 0))


def _rope(t, cosf, sinf):
    lane = lax.broadcasted_iota(jnp.int32, t.shape, 1)
    first = (lane & (HEAD_DIM_A - 1)) < HEAD_DIM_A // 2
    width = t.shape[1]
    swapped = jnp.where(first, pltpu.roll(t, width - HEAD_DIM_A // 2, 1),
                        pltpu.roll(t, HEAD_DIM_A // 2, 1))
    return t * cosf + swapped * sinf


def _conv_silu(xbc, cw_ref, cb_ref, cols, seq_len):
    rows = xbc.shape[0]
    pos = lax.rem(lax.broadcasted_iota(jnp.int32, xbc.shape, 0), seq_len)
    half = D_CONV // 2
    acc = cb_ref[:, cols] + cw_ref[half:half + 1, cols] * xbc
    for j in range(D_CONV):
        o = j - half
        if o == 0:
            continue
        shifted = pltpu.roll(xbc, (-o) % rows, 0)
        valid = (pos >= -o) if o < 0 else (pos < seq_len - o)
        acc = acc + cw_ref[j:j + 1, cols] * jnp.where(valid, shifted, 0.0)
    return _silu(acc)


def _softplus(t):
    return jnp.maximum(t, 0.0) + jnp.log(1.0 + jnp.exp(-jnp.abs(t)))


def _inproj0_kernel(*refs, rope, seq_len):
    if rope:
        (x_ref, mod_ref, w_ref, wdt_ref, cw_ref, cb_ref, dtb_ref, cos_ref, sin_ref,
         q_ref, k_ref, v_ref, z_ref, xbc_ref, dt_ref) = refs
    else:
        (x_ref, mod_ref, w_ref, wdt_ref, cw_ref, cb_ref, dtb_ref,
         q_ref, k_ref, v_ref, z_ref, xbc_ref, dt_ref) = refs
    m = mod_ref[0]
    for b in range(x_ref.shape[0] // seq_len):
        rows = slice(b * seq_len, (b + 1) * seq_len)
        h = (x_ref[rows, :] * (1.0 + m[1:2]) + m[0:1]).astype(BF16)
        for c0 in range(0, CONV_DIM_B, CONV_COL_BLOCK):
            cols = slice(c0, c0 + CONV_COL_BLOCK)
            raw = _dot(h, w_ref[:, 2048 + c0:2048 + c0 + CONV_COL_BLOCK])
            xbc_ref[rows, cols] = _conv_silu(raw, cw_ref, cb_ref, cols, seq_len)
        dt_ref[rows, :] = _softplus(_dot(h, wdt_ref[...]) + dtb_ref[...])
        q = _dot(h, w_ref[:, 0:512])
        k = _dot(h, w_ref[:, 512:1024])
        if rope:
            q = _rope(q, cos_ref[...], sin_ref[...])
            k = _rope(k, cos_ref[...], sin_ref[...])
        q_ref[rows, :] = q
        for blk in range(2 * A_QK // LANES):
            t = k[:, blk * LANES:(blk + 1) * LANES].T
            which, head = blk // 2, (blk % 2) * 2
            k_ref[b, which, head] = t[0:HEAD_DIM_A]
            k_ref[b, which, head + 1] = t[HEAD_DIM_A:2 * HEAD_DIM_A]
        vals = _dot(h, w_ref[:, 1024:1536])
        for hh in range(N_HEADS_A):
            v_ref[pl.ds(b * seq_len * N_HEADS_A + hh, seq_len, stride=N_HEADS_A), :] = (
                vals[:, hh * LANES:(hh + 1) * LANES])
        z_ref[rows, :] = _dot(h, w_ref[:, 1536:2048])


def _inproj0(x, mod, row_of_tile, w_main, w_dt, conv_w, conv_b, dt_bias, seq_len, tm, rope_tables=None):
    t = x.shape[0]
    assert tm % seq_len == 0
    row = lambda n: pl.BlockSpec((tm, n), lambda i: (i, 0))
    flat = lambda n: jax.ShapeDtypeStruct((t, n), F32)
    consts = [w_main, w_dt, conv_w, conv_b, dt_bias]
    in_specs = [row(D_MODEL), _mod_spec(row_of_tile)] + [_const_spec(cst.shape) for cst in consts]
    args = [x, mod] + consts
    if rope_tables is not None:
        assert tm == seq_len
        tab = pl.BlockSpec((tm, 512), lambda i: (0, 0))
        in_specs += [tab, tab]
        args += list(rope_tables)
    kt_dims = (2, N_HEADS_A, HEAD_DIM_A, seq_len)
    k_spec = pl.BlockSpec((tm // seq_len,) + kt_dims, lambda i: (i, 0, 0, 0, 0))
    k_shape = jax.ShapeDtypeStruct((t // seq_len,) + kt_dims, F32)
    return pl.pallas_call(
        functools.partial(_inproj0_kernel, rope=rope_tables is not None, seq_len=seq_len),
        grid=(t // tm,),
        in_specs=in_specs,
        out_specs=[row(512), k_spec, pl.BlockSpec((tm * N_HEADS_A, LANES), lambda i: (i, 0)),
                   row(512), row(CONV_DIM_B), row(LANES)],
        out_shape=[flat(512), k_shape, jax.ShapeDtypeStruct((t * N_HEADS_A, LANES), F32),
                   flat(512), flat(CONV_DIM_B), flat(LANES)],
        compiler_params=_params(),
        name="inproj0",
    )(*args)


def _lambda_full(lam_ref, lam_init):
    lv = lam_ref[...]
    return (jnp.exp(jnp.sum(lv[0:1] * lv[1:2], axis=1, keepdims=True))
            - jnp.exp(jnp.sum(lv[2:3] * lv[3:4], axis=1, keepdims=True)) + lam_init)


def _attn_kernel(*refs, n_seg, tq, lam_init):
    q_ref = refs[0]
    kt_refs = refs[1:1 + n_seg]
    v_refs = refs[1 + n_seg:1 + 2 * n_seg]
    lam_ref, sub_ref, o_ref = refs[1 + 2 * n_seg:]
    lam = _lambda_full(lam_ref, lam_init)
    lane = lax.broadcasted_iota(jnp.int32, (1, A_QK), 1)
    masks = [(lane >= h * HEAD_DIM_A) & (lane < (h + 1) * HEAD_DIM_A) for h in range(N_HEADS_A)]
    keys = [r.shape[-1] for r in kt_refs]

    def stacked(qx):
        return jnp.concatenate([jnp.where(m, qx, 0.0) for m in masks], axis=0).astype(BF16)

    def exp_scores(qx, b, which):
        qs = stacked(qx)
        s = [_dot(qs, r[b, which].reshape(A_QK, n).astype(BF16)) for r, n in zip(kt_refs, keys)]
        mx = functools.reduce(jnp.maximum, [jnp.max(x, axis=1, keepdims=True) for x in s])
        return [jnp.exp2(x - mx).astype(BF16) for x in s]

    outs = []
    for b in range(kt_refs[0].shape[0]):
        q = q_ref[b * tq:(b + 1) * tq, :] * (HEAD_DIM_A ** -0.5 * LOG2E)
        e1 = exp_scores(q[:, 0:A_QK], b, 0)
        e2 = exp_scores(q[:, A_QK:2 * A_QK], b, 1)
        heads = []
        for h in range(N_HEADS_A):
            hrows = slice(h * tq, (h + 1) * tq)
            n1 = n2 = None
            for s, (v_ref, n) in enumerate(zip(v_refs, keys)):
                v_h = v_ref[pl.ds(b * n * N_HEADS_A + h, n, stride=N_HEADS_A), :]
                v_ext = jnp.concatenate([v_h.astype(BF16),
                                         jnp.ones((n, LANES), BF16)], axis=1)
                p1 = _dot(e1[s][hrows], v_ext)
                p2 = _dot(e2[s][hrows], v_ext)
                n1 = p1 if n1 is None else n1 + p1
                n2 = p2 if n2 is None else n2 + p2
            heads.append(n1[:, 0:LANES] / n1[:, LANES:] - lam * (n2[:, 0:LANES] / n2[:, LANES:]))
        o = jnp.concatenate(heads, axis=0)
        ms = jnp.mean(o * o, axis=1, keepdims=True)
        o = (o * lax.rsqrt(ms + 1e-5) * sub_ref[...]) * (1.0 - lam_init)
        outs.append(jnp.concatenate([o[h * tq:(h + 1) * tq] for h in range(N_HEADS_A)], axis=1))
    o_ref[...] = jnp.concatenate(outs, axis=0)


def _attention(q, kts, vs, lam_vecs, subln_w, q_len, tq, seqs_per_step, lam_init):
    n_seq = kts[0].shape[0]
    nb = seqs_per_step
    tiles = q_len // tq
    assert tiles == 1 or nb == 1
    qspec = pl.BlockSpec((nb * tq, 512), lambda b, i: (b * tiles + i, 0))
    ktspecs = [pl.BlockSpec((nb,) + kt.shape[1:], lambda b, i: (b, 0, 0, 0, 0)) for kt in kts]
    vspecs = [pl.BlockSpec((nb * kt.shape[-1] * N_HEADS_A, LANES), lambda b, i: (b, 0)) for kt in kts]
    return pl.pallas_call(
        functools.partial(_attn_kernel, n_seg=len(kts), tq=tq, lam_init=lam_init),
        grid=(n_seq // nb, tiles),
        in_specs=[qspec] + ktspecs + vspecs + [pl.BlockSpec((4, HEAD_DIM_A), lambda b, i: (0, 0)),
                                             pl.BlockSpec((1, LANES), lambda b, i: (0, 0))],
        out_specs=qspec,
        out_shape=jax.ShapeDtypeStruct(q.shape, F32),
        compiler_params=_params(2),
        name="diff_attention",
    )(q, *kts, *vs, lam_vecs, subln_w.reshape(1, LANES))


def _chunk_loop(n, body, unroll=1):
    if n <= 2:
        for c in range(n):
            body(c)
    else:
        def step(c, carry):
            body(c)
            return carry
        lax.fori_loop(0, n, step, 0, unroll=unroll)


def _ssd_kernel(*refs, seq_len, n_seq, has_h0):
    if has_h0:
        (xact, dtv, z_ref, h0_ref, arow_ref, dsk_ref, nw_ref,
         y_ref, ysc, s_sc, cd_sc, ce_sc, htf, htb) = refs
        hfin_ref = None
    else:
        (xact, dtv, z_ref, arow_ref, dsk_ref, nw_ref,
         y_ref, hfin_ref, ysc, s_sc, cd_sc, ce_sc, htf, htb) = refs
        h0_ref = None
    nc = seq_len // CHUNK
    n_pairs = N_HEADS_B // 2
    a_row = -jnp.exp(arow_ref[...]) * LOG2E

    row_i = lax.broadcasted_iota(jnp.int32, (CHUNK, CHUNK), 0)
    col_i = lax.broadcasted_iota(jnp.int32, (CHUNK, CHUNK), 1)
    lower = col_i <= row_i
    upper = col_i >= row_i
    tri = (jnp.where(lower, 1.0, 0.0).astype(BF16), jnp.where(upper, 1.0, 0.0).astype(BF16))
    lo_half = col_i < HEAD_DIM_B
    col16_i = lax.broadcasted_iota(jnp.int32, (16, CHUNK), 1)

    def lane_column(mat, idx, lane_ids):
        return jnp.sum(jnp.where(lane_ids == idx, mat, 0.0), axis=1, keepdims=True)

    def split_pair(m):
        zero = jnp.zeros_like(m)
        return jnp.concatenate([jnp.where(lo_half, m, zero), jnp.where(lo_half, zero, m)], axis=0).astype(BF16)

    def local_phase(c):
        rows = pl.ds(pl.multiple_of(c * CHUNK, CHUNK), CHUNK)
        dtc = dtv[rows, :]
        dtt = dtc.T[0:16, :]
        per_dir = []
        for d in range(2):
            col = _dot_exact_rhs(tri[d], dtc * a_row)
            rowm = col.T[0:16, :]
            tot = jnp.broadcast_to(lane_column(rowm, CHUNK - 1 if d == 0 else 0, col16_i),
                                   (16, CHUNK))
            dte = jnp.exp2(tot - rowm) * dtt
            per_dir.append((col, rowm - jnp.log2(dtt), dte, jnp.exp2(tot)))
        for g in range(2):
            bg = xact[rows, 512 + g * LANES:512 + (g + 1) * LANES]
            cg = xact[rows, 768 + g * LANES:768 + (g + 1) * LANES]
            cbm = _dot_nt(cg.astype(BF16), bg.astype(BF16))
            bgt = bg.T
            for pq in range(2):
                pi = g * 2 + pq
                lanes = slice(pi * LANES, (pi + 1) * LANES)
                xrhs = split_pair(xact[rows, lanes])
                y_pair = None
                for d in range(2):
                    col, row_dt, dte, cdec = per_dir[d]
                    causal = lower if d == 0 else upper
                    m_parts, ce_parts, bw_parts, cd_parts = [], [], [], []
                    for hh in (2 * pi, 2 * pi + 1):
                        hd = 8 * d + hh
                        colb = lane_column(col, hd, col_i)
                        m_parts.append(cbm * jnp.exp2(jnp.where(causal, colb - row_dt[hd:hd + 1, :], NEG_BIG)))
                        ce_parts.append(cg * jnp.exp2(colb))
                        bw_parts.append(bgt * dte[hd:hd + 1, :])
                        cd_parts.append(cdec[hd:hd + 1, :])
                    yd = _dot(jnp.concatenate(m_parts, axis=1).astype(BF16), xrhs)
                    y_pair = yd if y_pair is None else y_pair + yd
                    s_sc[c, d, :, lanes] = _dot(jnp.concatenate(bw_parts, axis=1).astype(BF16), xrhs)
                    ce_sc[c, d, :, pi * 2 * LANES:(pi + 1) * 2 * LANES] = (
                        jnp.concatenate(ce_parts, axis=1).astype(BF16))
                    cd_sc[c, d, :, lanes] = jnp.broadcast_to(
                        jnp.where(lo_half[0:1, :], cd_parts[0], cd_parts[1]), (8, LANES))
                ysc[rows, lanes] = y_pair

    _chunk_loop(n_seq * nc, local_phase, unroll=2)

    def scan_sequence(sq):
        for d, ht in ((0, htf), (1, htb)):
            for qd in range(n_pairs):
                lanes = slice(qd * LANES, (qd + 1) * LANES)
                if has_h0:
                    ht[:, lanes] = h0_ref[sq, d, lanes, :].T
                else:
                    ht[:, lanes] = jnp.zeros((D_STATE, LANES), F32)

        def scan_phase(j):
            for d, ht in ((0, htf), (1, htb)):
                c = sq * nc + (j if d == 0 else nc - 1 - j)
                rows = pl.ds(pl.multiple_of(c * CHUNK, CHUNK), CHUNK)
                for pi in range(n_pairs):
                    lanes = slice(pi * LANES, (pi + 1) * LANES)
                    hprev = ht[:, lanes]
                    ysc[rows, lanes] += _dot(ce_sc[c, d, :, pi * 2 * LANES:(pi + 1) * 2 * LANES],
                                             split_pair(hprev))
                    ht[:, lanes] = hprev * cd_sc[c, d, 0:1, lanes] + s_sc[c, d, :, lanes]

        _chunk_loop(nc, scan_phase)
        if hfin_ref is not None:
            for d, ht in ((0, htf), (1, htb)):
                for qd in range(n_pairs):
                    lanes = slice(qd * LANES, (qd + 1) * LANES)
                    hfin_ref[sq, d, lanes, :] = ht[:, lanes].T

    if n_seq == 1:
        scan_sequence(0)
    else:
        def seq_step(sq, carry):
            scan_sequence(sq)
            return carry
        lax.fori_loop(0, n_seq, seq_step, 0)

    def finish(c):
        rows = pl.ds(pl.multiple_of(c * CHUNK, CHUNK), CHUNK)
        y = ysc[rows, :] + dsk_ref[...] * xact[rows, 0:D_INNER_B]
        y = y * _silu(z_ref[rows, :])
        ms = jnp.mean(y * y, axis=1, keepdims=True)
        y_ref[rows, :] = y * lax.rsqrt(ms + 1e-5) * nw_ref[...]

    _chunk_loop(n_seq * nc, finish)


def _ssd(xbc, dt, z, h0, consts, n_batch, seq_len, seqs_per_step):
    nb = seqs_per_step
    L = nb * seq_len
    nc = L // CHUNK
    row = lambda n: pl.BlockSpec((L, n), lambda b: (b, 0))
    state_spec = pl.BlockSpec((nb, 2, D_INNER_B, D_STATE), lambda b: (b, 0, 0, 0))
    has_h0 = h0 is not None
    in_specs = [row(CONV_DIM_B), row(LANES), row(D_INNER_B)]
    args = [xbc, dt, z]
    if has_h0:
        in_specs.append(state_spec)
        args.append(h0)
    for cst in consts:
        in_specs.append(pl.BlockSpec(cst.shape, lambda b: (0, 0)))
        args.append(cst)
    out_specs = [row(D_INNER_B)]
    out_shape = [jax.ShapeDtypeStruct((n_batch * seq_len, D_INNER_B), F32)]
    if not has_h0:
        out_specs.append(state_spec)
        out_shape.append(jax.ShapeDtypeStruct((n_batch, 2, D_INNER_B, D_STATE), F32))
    scratch = [pltpu.VMEM((L, D_INNER_B), F32),
               pltpu.VMEM((nc, 2, D_STATE, D_INNER_B), F32),
               pltpu.VMEM((nc, 2, 8, D_INNER_B), F32),
               pltpu.VMEM((nc, 2, CHUNK, 2 * D_INNER_B), BF16),
               pltpu.VMEM((D_STATE, D_INNER_B), F32),
               pltpu.VMEM((D_STATE, D_INNER_B), F32)]
    return pl.pallas_call(
        functools.partial(_ssd_kernel, seq_len=seq_len, n_seq=nb, has_h0=has_h0),
        grid=(n_batch // nb,),
        in_specs=in_specs,
        out_specs=out_specs,
        out_shape=out_shape,
        scratch_shapes=scratch,
        compiler_params=_params(),
        name="bidir_ssd",
    )(*args)


def _tail_kernel(*refs, n_in):
    a_refs = refs[:n_in]
    (x_ref, mod_ref, wout_ref, g1_ref, b1_ref, wup_ref, wdown_ref, g2_ref, b2_ref, o_ref) = refs[n_in:]
    m = mod_ref[0]
    n_chunks = D_FF // FF_CHUNK

    def head(rows):
        d = None
        off = 0
        for a_ref in a_refs:
            n = a_ref.shape[1]
            part = _dot(a_ref[rows, :].astype(BF16), wout_ref[off:off + n, :])
            d = part if d is None else d + part
            off += n
        x1 = _layer_norm(ALPHA * x_ref[rows, :] + m[2:3] * d, g1_ref[...], b1_ref[...])
        return x1, (x1 * (1.0 + m[4:5]) + m[3:4]).astype(BF16)

    def up(h, c):
        return _dot(h, wup_ref[:, c * FF_CHUNK:(c + 1) * FF_CHUNK])

    def down(u, acc, c):
        u = jnp.maximum(u, 0.0)
        part = _dot((u * u).astype(BF16), wdown_ref[c * FF_CHUNK:(c + 1) * FF_CHUNK, :])
        return part if acc is None else acc + part

    def finish(rows, x1, acc):
        o_ref[rows, :] = _layer_norm(ALPHA * x1 + m[5:6] * acc, g2_ref[...], b2_ref[...])

    k = TAIL_SUBTILES
    sub = x_ref.shape[0] // k
    rows = [slice(t * sub, (t + 1) * sub) for t in range(k)]
    heads = [None] * k
    heads[0] = head(rows[0])
    u_next = up(heads[0][1], 0)
    done = None
    for t in range(k):
        if t + 1 < k:
            heads[t + 1] = head(rows[t + 1])
        acc = None
        for c in range(n_chunks):
            if c + 1 < n_chunks:
                nxt = up(heads[t][1], c + 1)
            else:
                nxt = up(heads[t + 1][1], 0) if t + 1 < k else None
            u_cur, u_next = u_next, nxt
            acc = down(u_cur, acc, c)
            if c == 0 and done is not None:
                finish(*done)
        done = (rows[t], heads[t][0], acc)
    finish(*done)


def _tail(a_list, x, mod, row_of_tile, w_out, ln1_g, ln1_b, w_up, w_down, ln2_g, ln2_b):
    t = x.shape[0]
    tm = TOKEN_TILE
    row = lambda n: pl.BlockSpec((tm, n), lambda i: (i, 0))
    vec = lambda v: v.reshape(1, D_MODEL)
    consts = [w_out, vec(ln1_g), vec(ln1_b), w_up, w_down, vec(ln2_g), vec(ln2_b)]
    return pl.pallas_call(
        functools.partial(_tail_kernel, n_in=len(a_list)),
        grid=(t // tm,),
        in_specs=[row(a.shape[1]) for a in a_list] + [row(D_MODEL), _mod_spec(row_of_tile)]
                 + [_const_spec(cst.shape) for cst in consts],
        out_specs=row(D_MODEL),
        out_shape=jax.ShapeDtypeStruct((t, D_MODEL), F32),
        compiler_params=_params(),
        name="outproj_mlp",
    )(*a_list, x, mod, *consts)


def _pool_kernel(x_ref, mod_ref, win_ref, pw_ref, ps_ref, o_ref, *, seq_len):
    L = seq_len
    m = mod_ref[0]
    t_i = lax.broadcasted_iota(jnp.int32, (L, L), 0)
    s_i = lax.broadcasted_iota(jnp.int32, (L, L), 1)
    t_g = lax.broadcasted_iota(jnp.int32, (L, POOL_GROUP_DIM), 0)
    bands, scales = [], []
    for w in POOL_WINDOWS:
        lo, hi = w // 2, w - w // 2
        bands.append(jnp.where((s_i >= t_i - lo) & (s_i < t_i + hi), 1.0 / w, 0.0).astype(BF16))
        scales.append(w / (jnp.minimum(t_g + hi, L) - jnp.maximum(t_g - lo, 0)).astype(F32))
    n_seq = x_ref.shape[0] // L
    seq_rows = [slice(s * L, (s + 1) * L) for s in range(n_seq)]
    hs = [(x_ref[r, :] * (1.0 + m[1:2]) + m[0:1]).astype(BF16) for r in seq_rows]
    us = [_dot(h, win_ref[...]) for h in hs]
    n_groups = len(POOL_WINDOWS)
    cols = [slice(g * POOL_GROUP_DIM, (g + 1) * POOL_GROUP_DIM) for g in range(n_groups)]

    def window_means(g):
        out = []
        for s in range(n_seq):
            ug = us[s][:, cols[g]]
            ug_hi = ug.astype(BF16)
            ug_lo = (ug - ug_hi.astype(F32)).astype(BF16)
            out.append((ug, _dot(bands[g], ug_hi) + _dot(bands[g], ug_lo)))
        return out

    def mix(g, means):
        return [_dot((wm * scales[g] - ug).astype(BF16), pw_ref[g]) * ps_ref[:, cols[g]] for ug, wm in means]

    mixed = []
    pending = window_means(0)
    for g in range(n_groups):
        nxt = window_means(g + 1) if g + 1 < n_groups else None
        mixed.append(mix(g, pending))
        pending = nxt
    for s in range(n_seq):
        o_ref[seq_rows[s], :] = jnp.concatenate([mixed[g][s] for g in range(n_groups)], axis=1)


def _pool_mixer(x, mod, row_of_tile, w_in, pool_w, pool_scale, seq_len, tm):
    t = x.shape[0]
    row = pl.BlockSpec((tm, D_MODEL), lambda i: (i, 0))
    consts = [w_in, pool_w, pool_scale.reshape(1, D_MODEL)]
    return pl.pallas_call(
        functools.partial(_pool_kernel, seq_len=seq_len),
        grid=(t // tm,),
        in_specs=[row, _mod_spec(row_of_tile)] + [_const_spec(cst.shape) for cst in consts],
        out_specs=row,
        out_shape=jax.ShapeDtypeStruct((t, D_MODEL), F32),
        compiler_params=_params(),
        name="pool_mixer",
    )(x, mod, *consts)


def _rope_tables(rows):
    r, col = jnp.meshgrid(jnp.arange(rows), jnp.arange(GRID_W), indexing="ij")
    r = r.reshape(-1).astype(F32)
    col = col.reshape(-1).astype(F32)
    n_freq = HEAD_DIM_A // 4
    inv = ROPE_BASE ** (-jnp.arange(n_freq, dtype=F32) / n_freq)
    ang = jnp.concatenate([r[:, None] * inv, col[:, None] * inv], -1)
    cos, sin = jnp.cos(ang), jnp.sin(ang)
    reps = 512 // HEAD_DIM_A
    return (jnp.tile(jnp.concatenate([cos, cos], -1), (1, reps)),
            jnp.tile(jnp.concatenate([-sin, sin], -1), (1, reps)))


def kernel(x_prompt, x_sample, cache_k_l0, cache_v_l0, state_ssm_l0, c, c_ctx, w_mod_l0, b_mod_l0, w_in_l0, conv_w_l0, conv_b_l0, a_log_l0, dt_bias_l0, d_skip_l0, ssm_norm_w_l0, lam_q1_l0, lam_k1_l0, lam_q2_l0, lam_k2_l0, subln_w_l0, w_out_l0, ln1_g_l0, ln1_b_l0, w_up_l0, w_down_l0, ln2_g_l0, ln2_b_l0, w_mod_l1, b_mod_l1, w_in_l1, pool_w_l1, pool_scale_l1, w_out_l1, ln1_g_l1, ln1_b_l1, w_up_l1, w_down_l1, ln2_g_l1, ln2_b_l1):
    n_ctx, l_ctx, _ = x_prompt.shape
    n_lat, l_lat, _ = x_sample.shape
    past = cache_k_l0.shape[1]
    xc = x_prompt.reshape(n_ctx * l_ctx, D_MODEL)
    xl = x_sample.reshape(n_lat * l_lat, D_MODEL)

    cond8 = jnp.concatenate([c_ctx[None, :], c, jnp.zeros((8 - 1 - n_lat, D_MODEL), F32)], axis=0)
    ctx_row = lambda i: 0
    lat_tiles = l_lat // TOKEN_TILE
    lat_row = lambda i: 1 + i // lat_tiles

    mod0 = _modulation(cond8, w_mod_l0, b_mod_l0)
    n_main = 4 * A_QK + A_V + D_INNER_B + CONV_DIM_B
    w_main = w_in_l0.astype(BF16)
    w_dt = jnp.pad(w_in_l0[:, n_main:], ((0, 0), (0, LANES - 2 * N_HEADS_B))).astype(BF16)
    lane_pad = lambda v: jnp.pad(v.reshape(1, -1), ((0, 0), (0, LANES - 2 * N_HEADS_B)))
    conv_consts = (conv_w_l0, conv_b_l0.reshape(1, CONV_DIM_B), lane_pad(dt_bias_l0))
    ssd_consts = [lane_pad(a_log_l0),
                  jnp.repeat(d_skip_l0, HEAD_DIM_B).reshape(1, D_INNER_B),
                  ssm_norm_w_l0.reshape(1, D_INNER_B)]
    lam_vecs = jnp.stack([lam_q1_l0, lam_k1_l0, lam_q2_l0, lam_k2_l0], axis=0)
    lam_init = 0.8 - 0.6 * math.exp(-0.3 * 0)
    tail0 = (w_out_l0.astype(BF16), ln1_g_l0, ln1_b_l0, w_up_l0.astype(BF16), w_down_l0.astype(BF16),
             ln2_g_l0, ln2_b_l0)

    qc, kc, vc, zc, xbc_c, dt_c = _inproj0(xc, mod0, ctx_row, w_main, w_dt, *conv_consts, l_ctx, TOKEN_TILE)
    oa_c = _attention(qc, [kc], [vc], lam_vecs, subln_w_l0, l_ctx, l_ctx, CTX_SEQS_PER_STEP, lam_init)
    y_c, new_ssm = _ssd(xbc_c, dt_c, zc, None, ssd_consts, n_ctx, l_ctx, 1)
    xc = _tail([oa_c, y_c], xc, mod0, ctx_row, *tail0)

    ql, kl, vl, zl, xbc_l, dt_l = _inproj0(xl, mod0, lambda i: 1 + i, w_main, w_dt, *conv_consts, l_lat, l_lat,
                                           rope_tables=_rope_tables(l_lat // GRID_W))
    ck = jnp.transpose(cache_k_l0, (0, 2, 3, 4, 1))
    cv = cache_v_l0.reshape(n_lat * past * N_HEADS_A, 2 * HEAD_DIM_A)
    oa_l = _attention(ql, [ck, kl], [cv, vl], lam_vecs, subln_w_l0, l_lat, ATTN_Q_TILE, 1, lam_init)
    h0 = state_ssm_l0.reshape(n_lat, 2, D_INNER_B, D_STATE)
    (y_l,) = _ssd(xbc_l, dt_l, zl, h0, ssd_consts, n_lat, l_lat, 1)
    xl = _tail([oa_l, y_l], xl, mod0, lat_row, *tail0)

    mod1 = _modulation(cond8, w_mod_l1, b_mod_l1)
    w_in1 = w_in_l1.astype(BF16)
    pool_w = pool_w_l1.astype(BF16)
    tail1 = (w_out_l1.astype(BF16), ln1_g_l1, ln1_b_l1, w_up_l1.astype(BF16), w_down_l1.astype(BF16),
             ln2_g_l1, ln2_b_l1)
    mix_c = _pool_mixer(xc, mod1, ctx_row, w_in1, pool_w, pool_scale_l1, l_ctx, TOKEN_TILE)
    xc = _tail([mix_c], xc, mod1, ctx_row, *tail1)
    mix_l = _pool_mixer(xl, mod1, lambda i: 1 + i, w_in1, pool_w, pool_scale_l1, l_lat, l_lat)
    xl = _tail([mix_l], xl, mod1, lat_row, *tail1)

    return (xc.reshape(n_ctx, l_ctx, D_MODEL), xl.reshape(n_lat, l_lat, D_MODEL),
            jnp.transpose(kc, (0, 4, 1, 2, 3)),
            vc.reshape(n_ctx, l_ctx, N_HEADS_A, 2 * HEAD_DIM_A),
            new_ssm.reshape(n_ctx, 2, N_HEADS_B, HEAD_DIM_B, D_STATE))
```

```python
import functools
import math

import jax
import jax.numpy as jnp
from jax import lax
from jax.experimental import pallas as pl
from jax.experimental.pallas import tpu as pltpu

F32 = jnp.float32
BF16 = jnp.bfloat16

D_MODEL = 1024
N_MOD = 6
N_HEADS_A = 4
HEAD_DIM_A = 64
A_QK = N_HEADS_A * HEAD_DIM_A
A_V = 2 * A_QK
D_INNER_B = 512
HEAD_DIM_B = 64
N_HEADS_B = 8
D_STATE = 128
D_CONV = 5
CONV_DIM_B = 1024
CHUNK = 128
GRID_W = 64
POOL_WINDOWS = (2, 4, 8, 16)
POOL_GROUP_DIM = 256
D_FF = 4096
ROPE_BASE = 10000.0
LN_EPS = 1e-5
DEPTH = 2
ALPHA = (2 * DEPTH) ** 0.25
NEG_BIG = -1e30
LOG2E = 1.4426950408889634

LANES = 128
TOKEN_TILE = 512
ATTN_Q_TILE = 256
CTX_SEQS_PER_STEP = 4
FF_CHUNK = 1024
CONV_COL_BLOCK = 256
TAIL_SUBTILES = 2
VMEM_LIMIT = 56 * 1024 * 1024


def _dot(a, b):
    return jnp.dot(a, b, preferred_element_type=F32)


def _dot_nt(a, b):
    return lax.dot_general(a, b, (((1,), (1,)), ((), ())), preferred_element_type=F32)


def _bf16_pieces(a):
    a1 = a.astype(BF16)
    r1 = a - a1.astype(F32)
    a2 = r1.astype(BF16)
    a3 = (r1 - a2.astype(F32)).astype(BF16)
    return a1, a2, a3


def _dot_exact_rhs(t01, a):
    a1, a2, a3 = _bf16_pieces(a)
    return _dot(t01, a1) + _dot(t01, a2) + _dot(t01, a3)


def _sigmoid(x):
    return 1.0 / (1.0 + jnp.exp(-x))


def _silu(x):
    return x * _sigmoid(x)


def _layer_norm(x, g, b):
    mu = jnp.mean(x, axis=-1, keepdims=True)
    xc = x - mu
    var = jnp.mean(xc * xc, axis=-1, keepdims=True)
    return xc * lax.rsqrt(var + LN_EPS) * g + b


def _const_spec(shape):
    nd = len(shape)
    return pl.BlockSpec(shape, lambda *_: (0,) * nd, pipeline_mode=pl.Buffered(1))


def _params(n_axes=1):
    return pltpu.CompilerParams(dimension_semantics=("arbitrary",) * n_axes,
                                vmem_limit_bytes=VMEM_LIMIT)


def _mod_kernel(cond_ref, w_ref, b_ref, o_ref):
    c = cond_ref[...]
    s = _silu(c).astype(BF16)
    o_ref[...] = _dot(s, w_ref[...].astype(BF16)) + b_ref[...]


def _modulation(cond8, w_mod, b_mod):
    tn = 1024
    n = w_mod.shape[1]
    out = pl.pallas_call(
        _mod_kernel,
        grid=(n // tn,),
        in_specs=[pl.BlockSpec((8, D_MODEL), lambda j: (0, 0)),
                  pl.BlockSpec((D_MODEL, tn), lambda j: (0, j)),
                  pl.BlockSpec((1, tn), lambda j: (0, j))],
        out_specs=pl.BlockSpec((8, tn), lambda j: (0, j)),
        out_shape=jax.ShapeDtypeStruct((8, n), F32),
        compiler_params=_params(),
        name="modulation",
    )(cond8, w_mod, b_mod.reshape(1, n))
    return out.reshape(8, N_MOD, D_MODEL)


def _mod_spec(row_of_tile):
    return pl.BlockSpec((1, N_MOD, D_MODEL), lambda i: (row_of_tile(i), 0, 0))


def _rope(t, cosf, sinf):
    lane = lax.broadcasted_iota(jnp.int32, t.shape, 1)
    first = (lane & (HEAD_DIM_A - 1)) < HEAD_DIM_A // 2
    width = t.shape[1]
    swapped = jnp.where(first, pltpu.roll(t, width - HEAD_DIM_A // 2, 1),
                        pltpu.roll(t, HEAD_DIM_A // 2, 1))
    return t * cosf + swapped * sinf


def _conv_silu(xbc, cw_ref, cb_ref, cols, seq_len):
    assert xbc.shape[0] == seq_len
    half = D_CONV // 2
    edge = 8
    offsets = [j - half for j in range(D_CONV) if j != half]
    shifted = {o: pltpu.roll(xbc, (-o) % seq_len, 0) for o in offsets}

    def taps(lo, hi, masked):
        acc = cb_ref[:, cols] + cw_ref[half:half + 1, cols] * xbc[lo:hi]
        pos = lo + lax.broadcasted_iota(jnp.int32, (hi - lo, xbc.shape[1]), 0)
        for o in offsets:
            s = shifted[o][lo:hi]
            if masked:
                s = jnp.where((pos >= -o) if o < 0 else (pos < seq_len - o), s, 0.0)
            acc = acc + cw_ref[o + half:o + half + 1, cols] * s
        return acc

    acc = jnp.concatenate([taps(0, edge, True), taps(edge, seq_len - edge, False),
                           taps(seq_len - edge, seq_len, True)], axis=0)
    return _silu(acc)


def _softplus(t):
    return jnp.maximum(t, 0.0) + jnp.log(1.0 + jnp.exp(-jnp.abs(t)))


def _inproj0_kernel(*refs, rope, seq_len):
    if rope:
        (x_ref, mod_ref, w_ref, wdt_ref, cw_ref, cb_ref, dtb_ref, cos_ref, sin_ref,
         q_ref, k_ref, v_ref, z_ref, xbc_ref, dt_ref) = refs
    else:
        (x_ref, mod_ref, w_ref, wdt_ref, cw_ref, cb_ref, dtb_ref,
         q_ref, k_ref, v_ref, z_ref, xbc_ref, dt_ref) = refs
    m = mod_ref[0]
    for b in range(x_ref.shape[0] // seq_len):
        rows = slice(b * seq_len, (b + 1) * seq_len)
        h = (x_ref[rows, :] * (1.0 + m[1:2]) + m[0:1]).astype(BF16)
        for c0 in range(0, CONV_DIM_B, CONV_COL_BLOCK):
            cols = slice(c0, c0 + CONV_COL_BLOCK)
            raw = _dot(h, w_ref[:, 2048 + c0:2048 + c0 + CONV_COL_BLOCK])
            xbc_ref[rows, cols] = _conv_silu(raw, cw_ref, cb_ref, cols, seq_len)
        dt_ref[rows, :] = _softplus(_dot(h, wdt_ref[...]) + dtb_ref[...])
        q = _dot(h, w_ref[:, 0:512])
        k = _dot(h, w_ref[:, 512:1024])
        if rope:
            q = _rope(q, cos_ref[...], sin_ref[...])
            k = _rope(k, cos_ref[...], sin_ref[...])
        q_ref[rows, :] = q
        for blk in range(2 * A_QK // LANES):
            t = k[:, blk * LANES:(blk + 1) * LANES].T
            which, head = blk // 2, (blk % 2) * 2
            k_ref[b, which, head] = t[0:HEAD_DIM_A]
            k_ref[b, which, head + 1] = t[HEAD_DIM_A:2 * HEAD_DIM_A]
        vals = _dot(h, w_ref[:, 1024:1536])
        for hh in range(N_HEADS_A):
            v_ref[pl.ds(b * seq_len * N_HEADS_A + hh, seq_len, stride=N_HEADS_A), :] = (
                vals[:, hh * LANES:(hh + 1) * LANES])
        z_ref[rows, :] = _dot(h, w_ref[:, 1536:2048])


def _inproj0(x, mod, row_of_tile, w_main, w_dt, conv_w, conv_b, dt_bias, seq_len, tm, rope_tables=None):
    t = x.shape[0]
    assert tm % seq_len == 0
    row = lambda n: pl.BlockSpec((tm, n), lambda i: (i, 0))
    flat = lambda n: jax.ShapeDtypeStruct((t, n), F32)
    consts = [w_main, w_dt, conv_w, conv_b, dt_bias]
    in_specs = [row(D_MODEL), _mod_spec(row_of_tile)] + [_const_spec(cst.shape) for cst in consts]
    args = [x, mod] + consts
    if rope_tables is not None:
        assert tm == seq_len
        tab = pl.BlockSpec((tm, 512), lambda i: (0, 0))
        in_specs += [tab, tab]
        args += list(rope_tables)
    kt_dims = (2, N_HEADS_A, HEAD_DIM_A, seq_len)
    k_spec = pl.BlockSpec((tm // seq_len,) + kt_dims, lambda i: (i, 0, 0, 0, 0))
    k_shape = jax.ShapeDtypeStruct((t // seq_len,) + kt_dims, F32)
    return pl.pallas_call(
        functools.partial(_inproj0_kernel, rope=rope_tables is not None, seq_len=seq_len),
        grid=(t // tm,),
        in_specs=in_specs,
        out_specs=[row(512), k_spec, pl.BlockSpec((tm * N_HEADS_A, LANES), lambda i: (i, 0)),
                   row(512), row(CONV_DIM_B), row(LANES)],
        out_shape=[flat(512), k_shape, jax.ShapeDtypeStruct((t * N_HEADS_A, LANES), F32),
                   flat(512), flat(CONV_DIM_B), flat(LANES)],
        compiler_params=_params(),
        name="inproj0",
    )(*args)


def _lambda_full(lam_ref, lam_init):
    lv = lam_ref[...]
    return (jnp.exp(jnp.sum(lv[0:1] * lv[1:2], axis=1, keepdims=True))
            - jnp.exp(jnp.sum(lv[2:3] * lv[3:4], axis=1, keepdims=True)) + lam_init)


def _attn_kernel(*refs, n_seg, tq, lam_init):
    q_ref = refs[0]
    kt_refs = refs[1:1 + n_seg]
    v_refs = refs[1 + n_seg:1 + 2 * n_seg]
    lam_ref, sub_ref, o_ref = refs[1 + 2 * n_seg:]
    lam = _lambda_full(lam_ref, lam_init)
    lane = lax.broadcasted_iota(jnp.int32, (1, A_QK), 1)
    masks = [(lane >= h * HEAD_DIM_A) & (lane < (h + 1) * HEAD_DIM_A) for h in range(N_HEADS_A)]
    keys = [r.shape[-1] for r in kt_refs]

    def stacked(qx):
        return jnp.concatenate([jnp.where(m, qx, 0.0) for m in masks], axis=0).astype(BF16)

    def exp_scores(qx, b, which):
        qs = stacked(qx)
        s = [_dot(qs, r[b, which].reshape(A_QK, n).astype(BF16)) for r, n in zip(kt_refs, keys)]
        mx = functools.reduce(jnp.maximum, [jnp.max(x, axis=1, keepdims=True) for x in s])
        return [jnp.exp2(x - mx).astype(BF16) for x in s]

    outs = []
    for b in range(kt_refs[0].shape[0]):
        q = q_ref[b * tq:(b + 1) * tq, :] * (HEAD_DIM_A ** -0.5 * LOG2E)
        e1 = exp_scores(q[:, 0:A_QK], b, 0)
        e2 = exp_scores(q[:, A_QK:2 * A_QK], b, 1)
        heads = []
        for h in range(N_HEADS_A):
            hrows = slice(h * tq, (h + 1) * tq)
            n1 = n2 = None
            for s, (v_ref, n) in enumerate(zip(v_refs, keys)):
                v_h = v_ref[pl.ds(b * n * N_HEADS_A + h, n, stride=N_HEADS_A), :]
                v_ext = jnp.concatenate([v_h.astype(BF16),
                                         jnp.ones((n, LANES), BF16)], axis=1)
                p1 = _dot(e1[s][hrows], v_ext)
                p2 = _dot(e2[s][hrows], v_ext)
                n1 = p1 if n1 is None else n1 + p1
                n2 = p2 if n2 is None else n2 + p2
            heads.append(n1[:, 0:LANES] / n1[:, LANES:] - lam * (n2[:, 0:LANES] / n2[:, LANES:]))
        o = jnp.concatenate(heads, axis=0)
        ms = jnp.mean(o * o, axis=1, keepdims=True)
        o = (o * lax.rsqrt(ms + 1e-5) * sub_ref[...]) * (1.0 - lam_init)
        outs.append(jnp.concatenate([o[h * tq:(h + 1) * tq] for h in range(N_HEADS_A)], axis=1))
    o_ref[...] = jnp.concatenate(outs, axis=0)


def _attention(q, kts, vs, lam_vecs, subln_w, q_len, tq, seqs_per_step, lam_init):
    n_seq = kts[0].shape[0]
    nb = seqs_per_step
    tiles = q_len // tq
    assert tiles == 1 or nb == 1
    qspec = pl.BlockSpec((nb * tq, 512), lambda b, i: (b * tiles + i, 0))
    ktspecs = [pl.BlockSpec((nb,) + kt.shape[1:], lambda b, i: (b, 0, 0, 0, 0)) for kt in kts]
    vspecs = [pl.BlockSpec((nb * kt.shape[-1] * N_HEADS_A, LANES), lambda b, i: (b, 0)) for kt in kts]
    return pl.pallas_call(
        functools.partial(_attn_kernel, n_seg=len(kts), tq=tq, lam_init=lam_init),
        grid=(n_seq // nb, tiles),
        in_specs=[qspec] + ktspecs + vspecs + [pl.BlockSpec((4, HEAD_DIM_A), lambda b, i: (0, 0)),
                                             pl.BlockSpec((1, LANES), lambda b, i: (0, 0))],
        out_specs=qspec,
        out_shape=jax.ShapeDtypeStruct(q.shape, F32),
        compiler_params=_params(2),
        name="diff_attention",
    )(q, *kts, *vs, lam_vecs, subln_w.reshape(1, LANES))


def _chunk_loop(n, body, unroll=1):
    if n <= 2:
        for c in range(n):
            body(c)
    else:
        def step(c, carry):
            body(c)
            return carry
        lax.fori_loop(0, n, step, 0, unroll=unroll)


def _ssd_kernel(*refs, seq_len, n_seq, has_h0):
    if has_h0:
        (xact, dtv, z_ref, h0_ref, arow_ref, dsk_ref, nw_ref,
         y_ref, ysc, s_sc, cd_sc, ce_sc, htf, htb) = refs
        hfin_ref = None
    else:
        (xact, dtv, z_ref, arow_ref, dsk_ref, nw_ref,
         y_ref, hfin_ref, ysc, s_sc, cd_sc, ce_sc, htf, htb) = refs
        h0_ref = None
    nc = seq_len // CHUNK
    n_pairs = N_HEADS_B // 2
    a_row = -jnp.exp(arow_ref[...]) * LOG2E

    row_i = lax.broadcasted_iota(jnp.int32, (CHUNK, CHUNK), 0)
    col_i = lax.broadcasted_iota(jnp.int32, (CHUNK, CHUNK), 1)
    lower = col_i <= row_i
    upper = col_i >= row_i
    tri = (jnp.where(lower, 1.0, 0.0).astype(BF16), jnp.where(upper, 1.0, 0.0).astype(BF16))
    lo_half = col_i < HEAD_DIM_B

    def split_pair(m):
        zero = jnp.zeros_like(m)
        return jnp.concatenate([jnp.where(lo_half, m, zero), jnp.where(lo_half, zero, m)], axis=0).astype(BF16)

    def decay_terms(c):
        rows = pl.ds(pl.multiple_of(c * CHUNK, CHUNK), CHUNK)
        dtc = dtv[rows, :]
        dtt = dtc.T[0:16, :]
        per_dir = []
        for d in range(2):
            col = _dot_exact_rhs(tri[d], dtc * a_row)
            rowm = col.T[0:16, :]
            far = CHUNK - 1 if d == 0 else 0
            tot = jnp.broadcast_to(rowm[:, far:far + 1], (16, CHUNK))
            dte = jnp.exp2(tot - rowm) * dtt
            per_dir.append((col, rowm - jnp.log2(dtt), dte, jnp.exp2(tot)))
        return per_dir

    def local_phase(c, per_dir=None):
        rows = pl.ds(pl.multiple_of(c * CHUNK, CHUNK), CHUNK)
        if per_dir is None:
            per_dir = decay_terms(c)
        for g in range(2):
            bg = xact[rows, 512 + g * LANES:512 + (g + 1) * LANES]
            cg = xact[rows, 768 + g * LANES:768 + (g + 1) * LANES]
            cbm = _dot_nt(cg.astype(BF16), bg.astype(BF16))
            bgt = bg.T
            for pq in range(2):
                pi = g * 2 + pq
                lanes = slice(pi * LANES, (pi + 1) * LANES)
                xrhs = split_pair(xact[rows, lanes])
                y_pair = None
                for d in range(2):
                    col, row_dt, dte, cdec = per_dir[d]
                    causal = lower if d == 0 else upper
                    m_parts, ce_parts, bw_parts, cd_parts = [], [], [], []
                    for hh in (2 * pi, 2 * pi + 1):
                        hd = 8 * d + hh
                        colb = jnp.sum(jnp.where(col_i == hd, col, 0.0), axis=1, keepdims=True)
                        m_parts.append(cbm * jnp.exp2(jnp.where(causal, colb - row_dt[hd:hd + 1, :], NEG_BIG)))
                        ce_parts.append(cg * jnp.exp2(colb))
                        bw_parts.append(bgt * dte[hd:hd + 1, :])
                        cd_parts.append(cdec[hd:hd + 1, :])
                    yd = _dot(jnp.concatenate(m_parts, axis=1).astype(BF16), xrhs)
                    y_pair = yd if y_pair is None else y_pair + yd
                    s_sc[c, d, :, lanes] = _dot(jnp.concatenate(bw_parts, axis=1).astype(BF16), xrhs)
                    ce_sc[c, d, :, pi * 2 * LANES:(pi + 1) * 2 * LANES] = (
                        jnp.concatenate(ce_parts, axis=1).astype(BF16))
                    cd_sc[c, d, :, lanes] = jnp.broadcast_to(
                        jnp.where(lo_half[0:1, :], cd_parts[0], cd_parts[1]), (8, LANES))
                ysc[rows, lanes] = y_pair

    if n_seq * nc <= 2:
        terms = [decay_terms(c) for c in range(n_seq * nc)]
        for c in range(n_seq * nc):
            local_phase(c, terms[c])
    else:
        _chunk_loop(n_seq * nc, local_phase, unroll=2)

    def scan_sequence(sq):
        for d, ht in ((0, htf), (1, htb)):
            for qd in range(n_pairs):
                lanes = slice(qd * LANES, (qd + 1) * LANES)
                if has_h0:
                    ht[:, lanes] = h0_ref[sq, d, lanes, :].T
                else:
                    ht[:, lanes] = jnp.zeros((D_STATE, LANES), F32)

        def scan_phase(j):
            for d, ht in ((0, htf), (1, htb)):
                c = sq * nc + (j if d == 0 else nc - 1 - j)
                rows = pl.ds(pl.multiple_of(c * CHUNK, CHUNK), CHUNK)
                for pi in range(n_pairs):
                    lanes = slice(pi * LANES, (pi + 1) * LANES)
                    hprev = ht[:, lanes]
                    ysc[rows, lanes] += _dot(ce_sc[c, d, :, pi * 2 * LANES:(pi + 1) * 2 * LANES],
                                             split_pair(hprev))
                    ht[:, lanes] = hprev * cd_sc[c, d, 0:1, lanes] + s_sc[c, d, :, lanes]

        _chunk_loop(nc, scan_phase)
        if hfin_ref is not None:
            for d, ht in ((0, htf), (1, htb)):
                for qd in range(n_pairs):
                    lanes = slice(qd * LANES, (qd + 1) * LANES)
                    hfin_ref[sq, d, lanes, :] = ht[:, lanes].T

    if n_seq == 1:
        scan_sequence(0)
    else:
        def seq_step(sq, carry):
            scan_sequence(sq)
            return carry
        lax.fori_loop(0, n_seq, seq_step, 0)

    def finish(c):
        rows = pl.ds(pl.multiple_of(c * CHUNK, CHUNK), CHUNK)
        y = ysc[rows, :] + dsk_ref[...] * xact[rows, 0:D_INNER_B]
        y = y * _silu(z_ref[rows, :])
        ms = jnp.mean(y * y, axis=1, keepdims=True)
        y_ref[rows, :] = y * lax.rsqrt(ms + 1e-5) * nw_ref[...]

    _chunk_loop(n_seq * nc, finish)


def _ssd(xbc, dt, z, h0, consts, n_batch, seq_len, seqs_per_step):
    nb = seqs_per_step
    L = nb * seq_len
    nc = L // CHUNK
    row = lambda n: pl.BlockSpec((L, n), lambda b: (b, 0))
    state_spec = pl.BlockSpec((nb, 2, D_INNER_B, D_STATE), lambda b: (b, 0, 0, 0))
    has_h0 = h0 is not None
    in_specs = [row(CONV_DIM_B), row(LANES), row(D_INNER_B)]
    args = [xbc, dt, z]
    if has_h0:
        in_specs.append(state_spec)
        args.append(h0)
    for cst in consts:
        in_specs.append(pl.BlockSpec(cst.shape, lambda b: (0, 0)))
        args.append(cst)
    out_specs = [row(D_INNER_B)]
    out_shape = [jax.ShapeDtypeStruct((n_batch * seq_len, D_INNER_B), F32)]
    if not has_h0:
        out_specs.append(state_spec)
        out_shape.append(jax.ShapeDtypeStruct((n_batch, 2, D_INNER_B, D_STATE), F32))
    scratch = [pltpu.VMEM((L, D_INNER_B), F32),
               pltpu.VMEM((nc, 2, D_STATE, D_INNER_B), F32),
               pltpu.VMEM((nc, 2, 8, D_INNER_B), F32),
               pltpu.VMEM((nc, 2, CHUNK, 2 * D_INNER_B), BF16),
               pltpu.VMEM((D_STATE, D_INNER_B), F32),
               pltpu.VMEM((D_STATE, D_INNER_B), F32)]
    return pl.pallas_call(
        functools.partial(_ssd_kernel, seq_len=seq_len, n_seq=nb, has_h0=has_h0),
        grid=(n_batch // nb,),
        in_specs=in_specs,
        out_specs=out_specs,
        out_shape=out_shape,
        scratch_shapes=scratch,
        compiler_params=_params(),
        name="bidir_ssd",
    )(*args)


def _tail_kernel(*refs, n_in):
    a_refs = refs[:n_in]
    (x_ref, mod_ref, wout_ref, g1_ref, b1_ref, wup_ref, wdown_ref, g2_ref, b2_ref, o_ref) = refs[n_in:]
    m = mod_ref[0]
    n_chunks = D_FF // FF_CHUNK

    def head(rows):
        d = None
        off = 0
        for a_ref in a_refs:
            n = a_ref.shape[1]
            part = _dot(a_ref[rows, :].astype(BF16), wout_ref[off:off + n, :])
            d = part if d is None else d + part
            off += n
        x1 = _layer_norm(ALPHA * x_ref[rows, :] + m[2:3] * d, g1_ref[...], b1_ref[...])
        return x1, (x1 * (1.0 + m[4:5]) + m[3:4]).astype(BF16)

    def up(h, c):
        return _dot(h, wup_ref[:, c * FF_CHUNK:(c + 1) * FF_CHUNK])

    def down(u, acc, c):
        u = jnp.maximum(u, 0.0)
        part = _dot((u * u).astype(BF16), wdown_ref[c * FF_CHUNK:(c + 1) * FF_CHUNK, :])
        return part if acc is None else acc + part

    def finish(rows, x1, acc):
        o_ref[rows, :] = _layer_norm(ALPHA * x1 + m[5:6] * acc, g2_ref[...], b2_ref[...])

    k = TAIL_SUBTILES
    sub = x_ref.shape[0] // k
    rows = [slice(t * sub, (t + 1) * sub) for t in range(k)]
    heads = [None] * k
    heads[0] = head(rows[0])
    u_next = up(heads[0][1], 0)
    done = None
    for t in range(k):
        if t + 1 < k:
            heads[t + 1] = head(rows[t + 1])
        acc = None
        for c in range(n_chunks):
            if c + 1 < n_chunks:
                nxt = up(heads[t][1], c + 1)
            else:
                nxt = up(heads[t + 1][1], 0) if t + 1 < k else None
            u_cur, u_next = u_next, nxt
            acc = down(u_cur, acc, c)
            if c == 0 and done is not None:
                finish(*done)
        done = (rows[t], heads[t][0], acc)
    finish(*done)


def _tail(a_list, x, mod, row_of_tile, w_out, ln1_g, ln1_b, w_up, w_down, ln2_g, ln2_b):
    t = x.shape[0]
    tm = TOKEN_TILE
    row = lambda n: pl.BlockSpec((tm, n), lambda i: (i, 0))
    vec = lambda v: v.reshape(1, D_MODEL)
    consts = [w_out, vec(ln1_g), vec(ln1_b), w_up, w_down, vec(ln2_g), vec(ln2_b)]
    return pl.pallas_call(
        functools.partial(_tail_kernel, n_in=len(a_list)),
        grid=(t // tm,),
        in_specs=[row(a.shape[1]) for a in a_list] + [row(D_MODEL), _mod_spec(row_of_tile)]
                 + [_const_spec(cst.shape) for cst in consts],
        out_specs=row(D_MODEL),
        out_shape=jax.ShapeDtypeStruct((t, D_MODEL), F32),
        compiler_params=_params(),
        name="outproj_mlp",
    )(*a_list, x, mod, *consts)


def _pool_kernel(x_ref, mod_ref, win_ref, pw_ref, ps_ref, o_ref, *, seq_len):
    L = seq_len
    m = mod_ref[0]
    t_i = lax.broadcasted_iota(jnp.int32, (L, L), 0)
    s_i = lax.broadcasted_iota(jnp.int32, (L, L), 1)
    t_g = lax.broadcasted_iota(jnp.int32, (L, POOL_GROUP_DIM), 0)
    bands, scales = [], []
    for w in POOL_WINDOWS:
        lo, hi = w // 2, w - w // 2
        bands.append(jnp.where((s_i >= t_i - lo) & (s_i < t_i + hi), 1.0 / w, 0.0).astype(BF16))
        scales.append(w / (jnp.minimum(t_g + hi, L) - jnp.maximum(t_g - lo, 0)).astype(F32))
    n_seq = x_ref.shape[0] // L
    seq_rows = [slice(s * L, (s + 1) * L) for s in range(n_seq)]
    hs = [(x_ref[r, :] * (1.0 + m[1:2]) + m[0:1]).astype(BF16) for r in seq_rows]
    us = [_dot(h, win_ref[...]) for h in hs]
    n_groups = len(POOL_WINDOWS)
    cols = [slice(g * POOL_GROUP_DIM, (g + 1) * POOL_GROUP_DIM) for g in range(n_groups)]

    def window_means(g):
        out = []
        for s in range(n_seq):
            ug = us[s][:, cols[g]]
            ug_hi = ug.astype(BF16)
            ug_lo = (ug - ug_hi.astype(F32)).astype(BF16)
            out.append((ug, _dot(bands[g], ug_hi) + _dot(bands[g], ug_lo)))
        return out

    def mix(g, means):
        return [_dot((wm * scales[g] - ug).astype(BF16), pw_ref[g]) * ps_ref[:, cols[g]] for ug, wm in means]

    mixed = []
    pending = window_means(0)
    for g in range(n_groups):
        nxt = window_means(g + 1) if g + 1 < n_groups else None
        mixed.append(mix(g, pending))
        pending = nxt
    for s in range(n_seq):
        o_ref[seq_rows[s], :] = jnp.concatenate([mixed[g][s] for g in range(n_groups)], axis=1)


def _pool_mixer(x, mod, row_of_tile, w_in, pool_w, pool_scale, seq_len, tm):
    t = x.shape[0]
    row = pl.BlockSpec((tm, D_MODEL), lambda i: (i, 0))
    consts = [w_in, pool_w, pool_scale.reshape(1, D_MODEL)]
    return pl.pallas_call(
        functools.partial(_pool_kernel, seq_len=seq_len),
        grid=(t // tm,),
        in_specs=[row, _mod_spec(row_of_tile)] + [_const_spec(cst.shape) for cst in consts],
        out_specs=row,
        out_shape=jax.ShapeDtypeStruct((t, D_MODEL), F32),
        compiler_params=_params(),
        name="pool_mixer",
    )(x, mod, *consts)


def _rope_tables(rows):
    r, col = jnp.meshgrid(jnp.arange(rows), jnp.arange(GRID_W), indexing="ij")
    r = r.reshape(-1).astype(F32)
    col = col.reshape(-1).astype(F32)
    n_freq = HEAD_DIM_A // 4
    inv = ROPE_BASE ** (-jnp.arange(n_freq, dtype=F32) / n_freq)
    ang = jnp.concatenate([r[:, None] * inv, col[:, None] * inv], -1)
    cos, sin = jnp.cos(ang), jnp.sin(ang)
    reps = 512 // HEAD_DIM_A
    return (jnp.tile(jnp.concatenate([cos, cos], -1), (1, reps)),
            jnp.tile(jnp.concatenate([-sin, sin], -1), (1, reps)))


def kernel(x_prompt, x_sample, cache_k_l0, cache_v_l0, state_ssm_l0, c, c_ctx, w_mod_l0, b_mod_l0, w_in_l0, conv_w_l0, conv_b_l0, a_log_l0, dt_bias_l0, d_skip_l0, ssm_norm_w_l0, lam_q1_l0, lam_k1_l0, lam_q2_l0, lam_k2_l0, subln_w_l0, w_out_l0, ln1_g_l0, ln1_b_l0, w_up_l0, w_down_l0, ln2_g_l0, ln2_b_l0, w_mod_l1, b_mod_l1, w_in_l1, pool_w_l1, pool_scale_l1, w_out_l1, ln1_g_l1, ln1_b_l1, w_up_l1, w_down_l1, ln2_g_l1, ln2_b_l1):
    n_ctx, l_ctx, _ = x_prompt.shape
    n_lat, l_lat, _ = x_sample.shape
    past = cache_k_l0.shape[1]
    xc = x_prompt.reshape(n_ctx * l_ctx, D_MODEL)
    xl = x_sample.reshape(n_lat * l_lat, D_MODEL)

    cond8 = jnp.concatenate([c_ctx[None, :], c, jnp.zeros((8 - 1 - n_lat, D_MODEL), F32)], axis=0)
    ctx_row = lambda i: 0
    lat_tiles = l_lat // TOKEN_TILE
    lat_row = lambda i: 1 + i // lat_tiles

    mod0 = _modulation(cond8, w_mod_l0, b_mod_l0)
    n_main = 4 * A_QK + A_V + D_INNER_B + CONV_DIM_B
    w_main = w_in_l0.astype(BF16)
    w_dt = jnp.pad(w_in_l0[:, n_main:], ((0, 0), (0, LANES - 2 * N_HEADS_B))).astype(BF16)
    lane_pad = lambda v: jnp.pad(v.reshape(1, -1), ((0, 0), (0, LANES - 2 * N_HEADS_B)))
    conv_consts = (conv_w_l0, conv_b_l0.reshape(1, CONV_DIM_B), lane_pad(dt_bias_l0))
    ssd_consts = [lane_pad(a_log_l0),
                  jnp.repeat(d_skip_l0, HEAD_DIM_B).reshape(1, D_INNER_B),
                  ssm_norm_w_l0.reshape(1, D_INNER_B)]
    lam_vecs = jnp.stack([lam_q1_l0, lam_k1_l0, lam_q2_l0, lam_k2_l0], axis=0)
    lam_init = 0.8 - 0.6 * math.exp(-0.3 * 0)
    tail0 = (w_out_l0.astype(BF16), ln1_g_l0, ln1_b_l0, w_up_l0.astype(BF16), w_down_l0.astype(BF16),
             ln2_g_l0, ln2_b_l0)

    qc, kc, vc, zc, xbc_c, dt_c = _inproj0(xc, mod0, ctx_row, w_main, w_dt, *conv_consts, l_ctx, TOKEN_TILE)
    oa_c = _attention(qc, [kc], [vc], lam_vecs, subln_w_l0, l_ctx, l_ctx, CTX_SEQS_PER_STEP, lam_init)
    y_c, new_ssm = _ssd(xbc_c, dt_c, zc, None, ssd_consts, n_ctx, l_ctx, 1)
    xc = _tail([oa_c, y_c], xc, mod0, ctx_row, *tail0)

    ql, kl, vl, zl, xbc_l, dt_l = _inproj0(xl, mod0, lambda i: 1 + i, w_main, w_dt, *conv_consts, l_lat, l_lat,
                                           rope_tables=_rope_tables(l_lat // GRID_W))
    ck = jnp.transpose(cache_k_l0, (0, 2, 3, 4, 1))
    cv = cache_v_l0.reshape(n_lat * past * N_HEADS_A, 2 * HEAD_DIM_A)
    oa_l = _attention(ql, [ck, kl], [cv, vl], lam_vecs, subln_w_l0, l_lat, ATTN_Q_TILE, 1, lam_init)
    h0 = state_ssm_l0.reshape(n_lat, 2, D_INNER_B, D_STATE)
    (y_l,) = _ssd(xbc_l, dt_l, zl, h0, ssd_consts, n_lat, l_lat, 1)
    xl = _tail([oa_l, y_l], xl, mod0, lat_row, *tail0)

    mod1 = _modulation(cond8, w_mod_l1, b_mod_l1)
    w_in1 = w_in_l1.astype(BF16)
    pool_w = pool_w_l1.astype(BF16)
    tail1 = (w_out_l1.astype(BF16), ln1_g_l1, ln1_b_l1, w_up_l1.astype(BF16), w_down_l1.astype(BF16),
             ln2_g_l1, ln2_b_l1)
    mix_c = _pool_mixer(xc, mod1, ctx_row, w_in1, pool_w, pool_scale_l1, l_ctx, TOKEN_TILE)
    xc = _tail([mix_c], xc, mod1, ctx_row, *tail1)
    mix_l = _pool_mixer(xl, mod1, lambda i: 1 + i, w_in1, pool_w, pool_scale_l1, l_lat, l_lat)
    xl = _tail([mix_l], xl, mod1, lat_row, *tail1)

    return (xc.reshape(n_ctx, l_ctx, D_MODEL), xl.reshape(n_lat, l_lat, D_MODEL),
            jnp.transpose(kc, (0, 4, 1, 2, 3)),
            vc.reshape(n_ctx, l_ctx, N_HEADS_A, 2 * HEAD_DIM_A),
            new_ssm.reshape(n_ctx, 2, N_HEADS_B, HEAD_DIM_B, D_STATE))
```

```python
import functools
import math

import jax
import jax.numpy as jnp
import numpy as np
from jax import lax
from jax.experimental import pallas as pl
from jax.experimental.pallas import tpu as pltpu

F32 = jnp.float32
BF16 = jnp.bfloat16

D_MODEL = 1024
N_MOD = 6
N_HEADS_A = 4
HEAD_DIM_A = 64
A_QK = N_HEADS_A * HEAD_DIM_A
A_V = 2 * A_QK
D_INNER_B = 512
HEAD_DIM_B = 64
N_HEADS_B = 8
D_STATE = 128
D_CONV = 5
CONV_DIM_B = 1024
CHUNK = 128
GRID_W = 64
POOL_WINDOWS = (2, 4, 8, 16)
POOL_GROUP_DIM = 256
D_FF = 4096
ROPE_BASE = 10000.0
LN_EPS = 1e-5
DEPTH = 2
ALPHA = (2 * DEPTH) ** 0.25
NEG_BIG = -1e30
LOG2E = 1.4426950408889634

LANES = 128
TOKEN_TILE = 512
ATTN_Q_TILE = 256
CTX_SEQS_PER_STEP = 4
FF_CHUNK = 1024
CONV_COL_BLOCK = 256
TAIL_SPLIT = (256, 256)
VMEM_LIMIT = 56 * 1024 * 1024


def _dot(a, b):
    return jnp.dot(a, b, preferred_element_type=F32)


def _dot_nt(a, b):
    return lax.dot_general(a, b, (((1,), (1,)), ((), ())), preferred_element_type=F32)


def _bf16_pieces(a):
    a1 = a.astype(BF16)
    r1 = a - a1.astype(F32)
    a2 = r1.astype(BF16)
    a3 = (r1 - a2.astype(F32)).astype(BF16)
    return a1, a2, a3


def _dot_exact_rhs(t01, a):
    a1, a2, a3 = _bf16_pieces(a)
    return _dot(t01, a1) + _dot(t01, a2) + _dot(t01, a3)


def _sigmoid(x):
    return 1.0 / (1.0 + jnp.exp(-x))


def _silu(x):
    return x * _sigmoid(x)


def _layer_norm(x, g, b):
    mu = jnp.mean(x, axis=-1, keepdims=True)
    xc = x - mu
    var = jnp.mean(xc * xc, axis=-1, keepdims=True)
    return xc * lax.rsqrt(var + LN_EPS) * g + b


def _const_spec(shape):
    nd = len(shape)
    return pl.BlockSpec(shape, lambda *_: (0,) * nd, pipeline_mode=pl.Buffered(1))


def _params(n_axes=1):
    return pltpu.CompilerParams(dimension_semantics=("arbitrary",) * n_axes,
                                vmem_limit_bytes=VMEM_LIMIT)


def _mod_kernel(cond_ref, w_ref, b_ref, o_ref):
    c = cond_ref[...]
    s = _silu(c).astype(BF16)
    o_ref[...] = _dot(s, w_ref[...].astype(BF16)) + b_ref[...]


def _modulation(cond8, w_mod, b_mod):
    tn = 1024
    n = w_mod.shape[1]
    out = pl.pallas_call(
        _mod_kernel,
        grid=(n // tn,),
        in_specs=[pl.BlockSpec((8, D_MODEL), lambda j: (0, 0)),
                  pl.BlockSpec((D_MODEL, tn), lambda j: (0, j)),
                  pl.BlockSpec((1, tn), lambda j: (0, j))],
        out_specs=pl.BlockSpec((8, tn), lambda j: (0, j)),
        out_shape=jax.ShapeDtypeStruct((8, n), F32),
        compiler_params=_params(),
        name="modulation",
    )(cond8, w_mod, b_mod.reshape(1, n))
    return out.reshape(8, N_MOD, D_MODEL)


def _mod_spec(row_of_tile):
    return pl.BlockSpec((1, N_MOD, D_MODEL), lambda i: (row_of_tile(i), 0, 0))


def _rope(t, cosf, sinf):
    lane = lax.broadcasted_iota(jnp.int32, t.shape, 1)
    first = (lane & (HEAD_DIM_A - 1)) < HEAD_DIM_A // 2
    width = t.shape[1]
    swapped = jnp.where(first, pltpu.roll(t, width - HEAD_DIM_A // 2, 1),
                        pltpu.roll(t, HEAD_DIM_A // 2, 1))
    return t * cosf + swapped * sinf


def _conv_silu(xbc, cw_ref, cb_ref, cols, seq_len):
    assert xbc.shape[0] == seq_len
    half = D_CONV // 2
    edge = 8
    offsets = [j - half for j in range(D_CONV) if j != half]
    shifted = {o: pltpu.roll(xbc, (-o) % seq_len, 0) for o in offsets}

    def taps(lo, hi, masked):
        acc = cb_ref[:, cols] + cw_ref[half:half + 1, cols] * xbc[lo:hi]
        pos = lo + lax.broadcasted_iota(jnp.int32, (hi - lo, xbc.shape[1]), 0)
        for o in offsets:
            s = shifted[o][lo:hi]
            if masked:
                s = jnp.where((pos >= -o) if o < 0 else (pos < seq_len - o), s, 0.0)
            acc = acc + cw_ref[o + half:o + half + 1, cols] * s
        return acc

    acc = jnp.concatenate([taps(0, edge, True), taps(edge, seq_len - edge, False),
                           taps(seq_len - edge, seq_len, True)], axis=0)
    return _silu(acc)


def _softplus(t):
    return jnp.maximum(t, 0.0) + jnp.log(1.0 + jnp.exp(-jnp.abs(t)))


def _inproj0_kernel(*refs, rope, seq_len):
    if rope:
        (x_ref, mod_ref, w_ref, wdt_ref, cw_ref, cb_ref, dtb_ref, cos_ref, sin_ref,
         q_ref, k_ref, v_ref, z_ref, xbc_ref, dt_ref) = refs
    else:
        (x_ref, mod_ref, w_ref, wdt_ref, cw_ref, cb_ref, dtb_ref,
         q_ref, k_ref, v_ref, z_ref, xbc_ref, dt_ref) = refs
    m = mod_ref[0]
    for b in range(x_ref.shape[0] // seq_len):
        rows = slice(b * seq_len, (b + 1) * seq_len)
        h = (x_ref[rows, :] * (1.0 + m[1:2]) + m[0:1]).astype(BF16)
        for c0 in range(0, CONV_DIM_B, CONV_COL_BLOCK):
            cols = slice(c0, c0 + CONV_COL_BLOCK)
            raw = _dot(h, w_ref[:, 2048 + c0:2048 + c0 + CONV_COL_BLOCK])
            xbc_ref[rows, cols] = _conv_silu(raw, cw_ref, cb_ref, cols, seq_len)
        dt_ref[rows, :] = _softplus(_dot(h, wdt_ref[...]) + dtb_ref[...])
        q = _dot(h, w_ref[:, 0:512])
        k = _dot(h, w_ref[:, 512:1024])
        if rope:
            q = _rope(q, cos_ref[...], sin_ref[...])
            k = _rope(k, cos_ref[...], sin_ref[...])
        q_ref[rows, :] = q
        for blk in range(2 * A_QK // LANES):
            t = k[:, blk * LANES:(blk + 1) * LANES].T
            which, head = blk // 2, (blk % 2) * 2
            k_ref[b, which, head] = t[0:HEAD_DIM_A]
            k_ref[b, which, head + 1] = t[HEAD_DIM_A:2 * HEAD_DIM_A]
        vals = _dot(h, w_ref[:, 1024:1536])
        for hh in range(N_HEADS_A):
            v_ref[pl.ds(b * seq_len * N_HEADS_A + hh, seq_len, stride=N_HEADS_A), :] = (
                vals[:, hh * LANES:(hh + 1) * LANES])
        z_ref[rows, :] = _dot(h, w_ref[:, 1536:2048])


def _inproj0(x, mod, row_of_tile, w_main, w_dt, conv_w, conv_b, dt_bias, seq_len, tm, rope_tables=None):
    t = x.shape[0]
    assert tm % seq_len == 0
    row = lambda n: pl.BlockSpec((tm, n), lambda i: (i, 0))
    flat = lambda n: jax.ShapeDtypeStruct((t, n), F32)
    consts = [w_main, w_dt, conv_w, conv_b, dt_bias]
    in_specs = [row(D_MODEL), _mod_spec(row_of_tile)] + [_const_spec(cst.shape) for cst in consts]
    args = [x, mod] + consts
    if rope_tables is not None:
        assert tm == seq_len
        in_specs += [_const_spec(tab.shape) for tab in rope_tables]
        args += list(rope_tables)
    kt_dims = (2, N_HEADS_A, HEAD_DIM_A, seq_len)
    k_spec = pl.BlockSpec((tm // seq_len,) + kt_dims, lambda i: (i, 0, 0, 0, 0))
    k_shape = jax.ShapeDtypeStruct((t // seq_len,) + kt_dims, F32)
    return pl.pallas_call(
        functools.partial(_inproj0_kernel, rope=rope_tables is not None, seq_len=seq_len),
        grid=(t // tm,),
        in_specs=in_specs,
        out_specs=[row(512), k_spec, pl.BlockSpec((tm * N_HEADS_A, LANES), lambda i: (i, 0)),
                   row(512), row(CONV_DIM_B), row(LANES)],
        out_shape=[flat(512), k_shape, jax.ShapeDtypeStruct((t * N_HEADS_A, LANES), F32),
                   flat(512), flat(CONV_DIM_B), flat(LANES)],
        compiler_params=_params(),
        name="inproj0",
    )(*args)


def _lambda_full(lam_ref, lam_init):
    lv = lam_ref[...]
    return (jnp.exp(jnp.sum(lv[0:1] * lv[1:2], axis=1, keepdims=True))
            - jnp.exp(jnp.sum(lv[2:3] * lv[3:4], axis=1, keepdims=True)) + lam_init)


def _attn_kernel(*refs, n_seg, tq, lam_init):
    q_ref = refs[0]
    kt_refs = refs[1:1 + n_seg]
    v_refs = refs[1 + n_seg:1 + 2 * n_seg]
    lam_ref, sub_ref, o_ref = refs[1 + 2 * n_seg:]
    lam = _lambda_full(lam_ref, lam_init)
    lane = lax.broadcasted_iota(jnp.int32, (1, A_QK), 1)
    masks = [(lane >= h * HEAD_DIM_A) & (lane < (h + 1) * HEAD_DIM_A) for h in range(N_HEADS_A)]
    keys = [r.shape[-1] for r in kt_refs]

    def stacked(qx):
        return jnp.concatenate([jnp.where(m, qx, 0.0) for m in masks], axis=0).astype(BF16)

    def exp_scores(qx, b, which):
        qs = stacked(qx)
        s = [_dot(qs, r[b, which].reshape(A_QK, n).astype(BF16)) for r, n in zip(kt_refs, keys)]
        mx = functools.reduce(jnp.maximum, [jnp.max(x, axis=1, keepdims=True) for x in s])
        return [jnp.exp2(x - mx).astype(BF16) for x in s]

    outs = []
    for b in range(kt_refs[0].shape[0]):
        q = q_ref[b * tq:(b + 1) * tq, :] * (HEAD_DIM_A ** -0.5 * LOG2E)
        e1 = exp_scores(q[:, 0:A_QK], b, 0)
        e2 = exp_scores(q[:, A_QK:2 * A_QK], b, 1)
        heads = []
        for h in range(N_HEADS_A):
            hrows = slice(h * tq, (h + 1) * tq)
            n1 = n2 = None
            for s, (v_ref, n) in enumerate(zip(v_refs, keys)):
                v_h = v_ref[pl.ds(b * n * N_HEADS_A + h, n, stride=N_HEADS_A), :]
                v_ext = jnp.concatenate([v_h.astype(BF16),
                                         jnp.ones((n, LANES), BF16)], axis=1)
                p1 = _dot(e1[s][hrows], v_ext)
                p2 = _dot(e2[s][hrows], v_ext)
                n1 = p1 if n1 is None else n1 + p1
                n2 = p2 if n2 is None else n2 + p2
            heads.append(n1[:, 0:LANES] / n1[:, LANES:] - lam * (n2[:, 0:LANES] / n2[:, LANES:]))
        o = jnp.concatenate(heads, axis=0)
        ms = jnp.mean(o * o, axis=1, keepdims=True)
        o = (o * lax.rsqrt(ms + 1e-5) * sub_ref[...]) * (1.0 - lam_init)
        outs.append(jnp.concatenate([o[h * tq:(h + 1) * tq] for h in range(N_HEADS_A)], axis=1))
    o_ref[...] = jnp.concatenate(outs, axis=0)


def _attention(q, kts, vs, lam_vecs, subln_w, q_len, tq, seqs_per_step, lam_init):
    n_seq = kts[0].shape[0]
    nb = seqs_per_step
    tiles = q_len // tq
    assert tiles == 1 or nb == 1
    qspec = pl.BlockSpec((nb * tq, 512), lambda b, i: (b * tiles + i, 0))
    ktspecs = [pl.BlockSpec((nb,) + kt.shape[1:], lambda b, i: (b, 0, 0, 0, 0)) for kt in kts]
    vspecs = [pl.BlockSpec((nb * kt.shape[-1] * N_HEADS_A, LANES), lambda b, i: (b, 0)) for kt in kts]
    return pl.pallas_call(
        functools.partial(_attn_kernel, n_seg=len(kts), tq=tq, lam_init=lam_init),
        grid=(n_seq // nb, tiles),
        in_specs=[qspec] + ktspecs + vspecs + [pl.BlockSpec((4, HEAD_DIM_A), lambda b, i: (0, 0)),
                                             pl.BlockSpec((1, LANES), lambda b, i: (0, 0))],
        out_specs=qspec,
        out_shape=jax.ShapeDtypeStruct(q.shape, F32),
        compiler_params=_params(2),
        name="diff_attention",
    )(q, *kts, *vs, lam_vecs, subln_w.reshape(1, LANES))


def _chunk_loop(n, body, unroll=1):
    if n <= 2:
        for c in range(n):
            body(c)
    else:
        def step(c, carry):
            body(c)
            return carry
        lax.fori_loop(0, n, step, 0, unroll=unroll)


def _ssd_kernel(*refs, seq_len, n_seq, has_h0):
    if has_h0:
        (xact, dtv, z_ref, h0_ref, arow_ref, dsk_ref, nw_ref,
         y_ref, ysc, s_sc, cd_sc, ce_sc, htf, htb) = refs
        hfin_ref = None
    else:
        (xact, dtv, z_ref, arow_ref, dsk_ref, nw_ref,
         y_ref, hfin_ref, ysc, s_sc, cd_sc, ce_sc, htf, htb) = refs
        h0_ref = None
    nc = seq_len // CHUNK
    n_pairs = N_HEADS_B // 2
    a_row = -jnp.exp(arow_ref[...]) * LOG2E

    row_i = lax.broadcasted_iota(jnp.int32, (CHUNK, CHUNK), 0)
    col_i = lax.broadcasted_iota(jnp.int32, (CHUNK, CHUNK), 1)
    lower = col_i <= row_i
    upper = col_i >= row_i
    tri = (jnp.where(lower, 1.0, 0.0).astype(BF16), jnp.where(upper, 1.0, 0.0).astype(BF16))
    lo_half = col_i < HEAD_DIM_B

    def split_pair(m):
        zero = jnp.zeros_like(m)
        return jnp.concatenate([jnp.where(lo_half, m, zero), jnp.where(lo_half, zero, m)], axis=0).astype(BF16)

    def decay_terms(c):
        rows = pl.ds(pl.multiple_of(c * CHUNK, CHUNK), CHUNK)
        dtc = dtv[rows, :]
        dtt = dtc.T[0:16, :]
        per_dir = []
        for d in range(2):
            col = _dot_exact_rhs(tri[d], dtc * a_row)
            rowm = col.T[0:16, :]
            far = CHUNK - 1 if d == 0 else 0
            tot = jnp.broadcast_to(rowm[:, far:far + 1], (16, CHUNK))
            dte = jnp.exp2(tot - rowm) * dtt
            per_dir.append((col, rowm - jnp.log2(dtt), dte, jnp.exp2(tot)))
        return per_dir

    def local_phase(c, per_dir=None):
        rows = pl.ds(pl.multiple_of(c * CHUNK, CHUNK), CHUNK)
        if per_dir is None:
            per_dir = decay_terms(c)
        for g in range(2):
            bg = xact[rows, 512 + g * LANES:512 + (g + 1) * LANES]
            cg = xact[rows, 768 + g * LANES:768 + (g + 1) * LANES]
            cbm = _dot_nt(cg.astype(BF16), bg.astype(BF16))
            bgt = bg.T
            for pq in range(2):
                pi = g * 2 + pq
                lanes = slice(pi * LANES, (pi + 1) * LANES)
                xrhs = split_pair(xact[rows, lanes])
                y_pair = None
                for d in range(2):
                    col, row_dt, dte, cdec = per_dir[d]
                    causal = lower if d == 0 else upper
                    m_parts, ce_parts, bw_parts, cd_parts = [], [], [], []
                    for hh in (2 * pi, 2 * pi + 1):
                        hd = 8 * d + hh
                        colb = jnp.sum(jnp.where(col_i == hd, col, 0.0), axis=1, keepdims=True)
                        m_parts.append(cbm * jnp.exp2(jnp.where(causal, colb - row_dt[hd:hd + 1, :], NEG_BIG)))
                        ce_parts.append(cg * jnp.exp2(colb))
                        bw_parts.append(bgt * dte[hd:hd + 1, :])
                        cd_parts.append(cdec[hd:hd + 1, :])
                    yd = _dot(jnp.concatenate(m_parts, axis=1).astype(BF16), xrhs)
                    y_pair = yd if y_pair is None else y_pair + yd
                    s_sc[c, d, :, lanes] = _dot(jnp.concatenate(bw_parts, axis=1).astype(BF16), xrhs)
                    ce_sc[c, d, :, pi * 2 * LANES:(pi + 1) * 2 * LANES] = (
                        jnp.concatenate(ce_parts, axis=1).astype(BF16))
                    cd_sc[c, d, :, lanes] = jnp.broadcast_to(
                        jnp.where(lo_half[0:1, :], cd_parts[0], cd_parts[1]), (8, LANES))
                ysc[rows, lanes] = y_pair

    if n_seq * nc <= 2:
        terms = [decay_terms(c) for c in range(n_seq * nc)]
        for c in range(n_seq * nc):
            local_phase(c, terms[c])
    else:
        _chunk_loop(n_seq * nc, local_phase, unroll=2)

    def scan_sequence(sq):
        for d, ht in ((0, htf), (1, htb)):
            for qd in range(n_pairs):
                lanes = slice(qd * LANES, (qd + 1) * LANES)
                if has_h0:
                    ht[:, lanes] = h0_ref[sq, d, lanes, :].T
                else:
                    ht[:, lanes] = jnp.zeros((D_STATE, LANES), F32)

        def scan_phase(j):
            for d, ht in ((0, htf), (1, htb)):
                c = sq * nc + (j if d == 0 else nc - 1 - j)
                rows = pl.ds(pl.multiple_of(c * CHUNK, CHUNK), CHUNK)
                for pi in range(n_pairs):
                    lanes = slice(pi * LANES, (pi + 1) * LANES)
                    hprev = ht[:, lanes]
                    ysc[rows, lanes] += _dot(ce_sc[c, d, :, pi * 2 * LANES:(pi + 1) * 2 * LANES],
                                             split_pair(hprev))
                    ht[:, lanes] = hprev * cd_sc[c, d, 0:1, lanes] + s_sc[c, d, :, lanes]

        _chunk_loop(nc, scan_phase)
        if hfin_ref is not None:
            for d, ht in ((0, htf), (1, htb)):
                for qd in range(n_pairs):
                    lanes = slice(qd * LANES, (qd + 1) * LANES)
                    hfin_ref[sq, d, lanes, :] = ht[:, lanes].T

    if n_seq == 1:
        scan_sequence(0)
    else:
        def seq_step(sq, carry):
            scan_sequence(sq)
            return carry
        lax.fori_loop(0, n_seq, seq_step, 0)

    def finish(c):
        rows = pl.ds(pl.multiple_of(c * CHUNK, CHUNK), CHUNK)
        y = ysc[rows, :] + dsk_ref[...] * xact[rows, 0:D_INNER_B]
        y = y * _silu(z_ref[rows, :])
        ms = jnp.mean(y * y, axis=1, keepdims=True)
        y_ref[rows, :] = y * lax.rsqrt(ms + 1e-5) * nw_ref[...]

    _chunk_loop(n_seq * nc, finish)


def _ssd(xbc, dt, z, h0, consts, n_batch, seq_len, seqs_per_step):
    nb = seqs_per_step
    L = nb * seq_len
    nc = L // CHUNK
    row = lambda n: pl.BlockSpec((L, n), lambda b: (b, 0))
    state_spec = pl.BlockSpec((nb, 2, D_INNER_B, D_STATE), lambda b: (b, 0, 0, 0))
    has_h0 = h0 is not None
    in_specs = [row(CONV_DIM_B), row(LANES), row(D_INNER_B)]
    args = [xbc, dt, z]
    if has_h0:
        in_specs.append(state_spec)
        args.append(h0)
    for cst in consts:
        in_specs.append(pl.BlockSpec(cst.shape, lambda b: (0, 0)))
        args.append(cst)
    out_specs = [row(D_INNER_B)]
    out_shape = [jax.ShapeDtypeStruct((n_batch * seq_len, D_INNER_B), F32)]
    if not has_h0:
        out_specs.append(state_spec)
        out_shape.append(jax.ShapeDtypeStruct((n_batch, 2, D_INNER_B, D_STATE), F32))
    scratch = [pltpu.VMEM((L, D_INNER_B), F32),
               pltpu.VMEM((nc, 2, D_STATE, D_INNER_B), F32),
               pltpu.VMEM((nc, 2, 8, D_INNER_B), F32),
               pltpu.VMEM((nc, 2, CHUNK, 2 * D_INNER_B), BF16),
               pltpu.VMEM((D_STATE, D_INNER_B), F32),
               pltpu.VMEM((D_STATE, D_INNER_B), F32)]
    return pl.pallas_call(
        functools.partial(_ssd_kernel, seq_len=seq_len, n_seq=nb, has_h0=has_h0),
        grid=(n_batch // nb,),
        in_specs=in_specs,
        out_specs=out_specs,
        out_shape=out_shape,
        scratch_shapes=scratch,
        compiler_params=_params(),
        name="bidir_ssd",
    )(*args)


def _tail_kernel(*refs, n_in):
    a_refs = refs[:n_in]
    (x_ref, mod_ref, wout_ref, g1_ref, b1_ref, wup_ref, wdown_ref, g2_ref, b2_ref, o_ref) = refs[n_in:]
    m = mod_ref[0]
    n_chunks = D_FF // FF_CHUNK

    def out_proj(rows):
        d = None
        off = 0
        for a_ref in a_refs:
            n = a_ref.shape[1]
            part = _dot(a_ref[rows, :].astype(BF16), wout_ref[off:off + n, :])
            d = part if d is None else d + part
            off += n
        return d

    def head(rows, d):
        x1 = _layer_norm(ALPHA * x_ref[rows, :] + m[2:3] * d, g1_ref[...], b1_ref[...])
        return x1, (x1 * (1.0 + m[4:5]) + m[3:4]).astype(BF16)

    def up(h, c):
        return _dot(h, wup_ref[:, c * FF_CHUNK:(c + 1) * FF_CHUNK])

    def down(u, acc, c):
        u = jnp.maximum(u, 0.0)
        part = _dot((u * u).astype(BF16), wdown_ref[c * FF_CHUNK:(c + 1) * FF_CHUNK, :])
        return part if acc is None else acc + part

    def finish(rows, x1, acc):
        o_ref[rows, :] = _layer_norm(ALPHA * x1 + m[5:6] * acc, g2_ref[...], b2_ref[...])

    assert sum(TAIL_SPLIT) == x_ref.shape[0]
    k = len(TAIL_SPLIT)
    starts = [sum(TAIL_SPLIT[:t]) for t in range(k)]
    rows = [slice(s, s + n) for s, n in zip(starts, TAIL_SPLIT)]
    projected = [out_proj(r) for r in rows]
    heads = [None] * k
    heads[0] = head(rows[0], projected[0])
    u_next = up(heads[0][1], 0)
    done = None
    for t in range(k):
        if t + 1 < k:
            heads[t + 1] = head(rows[t + 1], projected[t + 1])
        acc = None
        for c in range(n_chunks):
            if c + 1 < n_chunks:
                nxt = up(heads[t][1], c + 1)
            else:
                nxt = up(heads[t + 1][1], 0) if t + 1 < k else None
            u_cur, u_next = u_next, nxt
            acc = down(u_cur, acc, c)
            if c == 0 and done is not None:
                finish(*done)
        done = (rows[t], heads[t][0], acc)
    finish(*done)


def _tail(a_list, x, mod, row_of_tile, w_out, ln1_g, ln1_b, w_up, w_down, ln2_g, ln2_b):
    t = x.shape[0]
    tm = TOKEN_TILE
    row = lambda n: pl.BlockSpec((tm, n), lambda i: (i, 0))
    vec = lambda v: v.reshape(1, D_MODEL)
    consts = [w_out, vec(ln1_g), vec(ln1_b), w_up, w_down, vec(ln2_g), vec(ln2_b)]
    return pl.pallas_call(
        functools.partial(_tail_kernel, n_in=len(a_list)),
        grid=(t // tm,),
        in_specs=[row(a.shape[1]) for a in a_list] + [row(D_MODEL), _mod_spec(row_of_tile)]
                 + [_const_spec(cst.shape) for cst in consts],
        out_specs=row(D_MODEL),
        out_shape=jax.ShapeDtypeStruct((t, D_MODEL), F32),
        compiler_params=_params(),
        name="outproj_mlp",
    )(*a_list, x, mod, *consts)


def _pool_kernel(x_ref, mod_ref, win_ref, pw_ref, ps_ref, o_ref, *, seq_len):
    L = seq_len
    m = mod_ref[0]
    t_i = lax.broadcasted_iota(jnp.int32, (L, L), 0)
    s_i = lax.broadcasted_iota(jnp.int32, (L, L), 1)
    t_g = lax.broadcasted_iota(jnp.int32, (L, POOL_GROUP_DIM), 0)
    bands, scales = [], []
    for w in POOL_WINDOWS:
        lo, hi = w // 2, w - w // 2
        bands.append(jnp.where((s_i >= t_i - lo) & (s_i < t_i + hi), 1.0 / w, 0.0).astype(BF16))
        scales.append(w / (jnp.minimum(t_g + hi, L) - jnp.maximum(t_g - lo, 0)).astype(F32))
    n_seq = x_ref.shape[0] // L
    seq_rows = [slice(s * L, (s + 1) * L) for s in range(n_seq)]
    hs = [(x_ref[r, :] * (1.0 + m[1:2]) + m[0:1]).astype(BF16) for r in seq_rows]
    us = [_dot(h, win_ref[...]) for h in hs]
    n_groups = len(POOL_WINDOWS)
    cols = [slice(g * POOL_GROUP_DIM, (g + 1) * POOL_GROUP_DIM) for g in range(n_groups)]

    def window_means(g):
        out = []
        for s in range(n_seq):
            ug = us[s][:, cols[g]]
            ug_hi = ug.astype(BF16)
            ug_lo = (ug - ug_hi.astype(F32)).astype(BF16)
            out.append((ug, _dot(bands[g], ug_hi) + _dot(bands[g], ug_lo)))
        return out

    def mix(g, means):
        return [_dot((wm * scales[g] - ug).astype(BF16), pw_ref[g]) * ps_ref[:, cols[g]] for ug, wm in means]

    mixed = []
    pending = window_means(0)
    for g in range(n_groups):
        nxt = window_means(g + 1) if g + 1 < n_groups else None
        mixed.append(mix(g, pending))
        pending = nxt
    for s in range(n_seq):
        o_ref[seq_rows[s], :] = jnp.concatenate([mixed[g][s] for g in range(n_groups)], axis=1)


def _pool_mixer(x, mod, row_of_tile, w_in, pool_w, pool_scale, seq_len, tm):
    t = x.shape[0]
    row = pl.BlockSpec((tm, D_MODEL), lambda i: (i, 0))
    consts = [w_in, pool_w, pool_scale.reshape(1, D_MODEL)]
    return pl.pallas_call(
        functools.partial(_pool_kernel, seq_len=seq_len),
        grid=(t // tm,),
        in_specs=[row, _mod_spec(row_of_tile)] + [_const_spec(cst.shape) for cst in consts],
        out_specs=row,
        out_shape=jax.ShapeDtypeStruct((t, D_MODEL), F32),
        compiler_params=_params(),
        name="pool_mixer",
    )(x, mod, *consts)


def _rope_tables(rows):
    r, col = np.meshgrid(np.arange(rows), np.arange(GRID_W), indexing="ij")
    r = r.reshape(-1).astype(np.float64)
    col = col.reshape(-1).astype(np.float64)
    n_freq = HEAD_DIM_A // 4
    inv = ROPE_BASE ** (-np.arange(n_freq, dtype=np.float64) / n_freq)
    ang = np.concatenate([r[:, None] * inv, col[:, None] * inv], -1)
    reps = 512 // HEAD_DIM_A
    cosf = np.tile(np.concatenate([np.cos(ang), np.cos(ang)], -1), (1, reps)).astype(np.float32)
    sinf = np.tile(np.concatenate([-np.sin(ang), np.sin(ang)], -1), (1, reps)).astype(np.float32)
    return jnp.asarray(cosf), jnp.asarray(sinf)


def kernel(x_prompt, x_sample, cache_k_l0, cache_v_l0, state_ssm_l0, c, c_ctx, w_mod_l0, b_mod_l0, w_in_l0, conv_w_l0, conv_b_l0, a_log_l0, dt_bias_l0, d_skip_l0, ssm_norm_w_l0, lam_q1_l0, lam_k1_l0, lam_q2_l0, lam_k2_l0, subln_w_l0, w_out_l0, ln1_g_l0, ln1_b_l0, w_up_l0, w_down_l0, ln2_g_l0, ln2_b_l0, w_mod_l1, b_mod_l1, w_in_l1, pool_w_l1, pool_scale_l1, w_out_l1, ln1_g_l1, ln1_b_l1, w_up_l1, w_down_l1, ln2_g_l1, ln2_b_l1):
    n_ctx, l_ctx, _ = x_prompt.shape
    n_lat, l_lat, _ = x_sample.shape
    past = cache_k_l0.shape[1]
    xc = x_prompt.reshape(n_ctx * l_ctx, D_MODEL)
    xl = x_sample.reshape(n_lat * l_lat, D_MODEL)

    cond8 = jnp.concatenate([c_ctx[None, :], c, jnp.zeros((8 - 1 - n_lat, D_MODEL), F32)], axis=0)
    ctx_row = lambda i: 0
    lat_tiles = l_lat // TOKEN_TILE
    lat_row = lambda i: 1 + i // lat_tiles

    mod0 = _modulation(cond8, w_mod_l0, b_mod_l0)
    n_main = 4 * A_QK + A_V + D_INNER_B + CONV_DIM_B
    w_main = w_in_l0.astype(BF16)
    w_dt = jnp.pad(w_in_l0[:, n_main:], ((0, 0), (0, LANES - 2 * N_HEADS_B))).astype(BF16)
    lane_pad = lambda v: jnp.pad(v.reshape(1, -1), ((0, 0), (0, LANES - 2 * N_HEADS_B)))
    conv_consts = (conv_w_l0, conv_b_l0.reshape(1, CONV_DIM_B), lane_pad(dt_bias_l0))
    ssd_consts = [lane_pad(a_log_l0),
                  jnp.repeat(d_skip_l0, HEAD_DIM_B).reshape(1, D_INNER_B),
                  ssm_norm_w_l0.reshape(1, D_INNER_B)]
    lam_vecs = jnp.stack([lam_q1_l0, lam_k1_l0, lam_q2_l0, lam_k2_l0], axis=0)
    lam_init = 0.8 - 0.6 * math.exp(-0.3 * 0)
    tail0 = (w_out_l0.astype(BF16), ln1_g_l0, ln1_b_l0, w_up_l0.astype(BF16), w_down_l0.astype(BF16),
             ln2_g_l0, ln2_b_l0)

    qc, kc, vc, zc, xbc_c, dt_c = _inproj0(xc, mod0, ctx_row, w_main, w_dt, *conv_consts, l_ctx, 2 * TOKEN_TILE)
    oa_c = _attention(qc, [kc], [vc], lam_vecs, subln_w_l0, l_ctx, l_ctx, CTX_SEQS_PER_STEP, lam_init)
    y_c, new_ssm = _ssd(xbc_c, dt_c, zc, None, ssd_consts, n_ctx, l_ctx, 1)
    xc = _tail([oa_c, y_c], xc, mod0, ctx_row, *tail0)

    ql, kl, vl, zl, xbc_l, dt_l = _inproj0(xl, mod0, lambda i: 1 + i, w_main, w_dt, *conv_consts, l_lat, l_lat,
                                           rope_tables=_rope_tables(l_lat // GRID_W))
    ck = jnp.transpose(cache_k_l0, (0, 2, 3, 4, 1))
    cv = cache_v_l0.reshape(n_lat * past * N_HEADS_A, 2 * HEAD_DIM_A)
    oa_l = _attention(ql, [ck, kl], [cv, vl], lam_vecs, subln_w_l0, l_lat, ATTN_Q_TILE, 1, lam_init)
    h0 = state_ssm_l0.reshape(n_lat, 2, D_INNER_B, D_STATE)
    (y_l,) = _ssd(xbc_l, dt_l, zl, h0, ssd_consts, n_lat, l_lat, 1)
    xl = _tail([oa_l, y_l], xl, mod0, lat_row, *tail0)

    mod1 = _modulation(cond8, w_mod_l1, b_mod_l1)
    w_in1 = w_in_l1.astype(BF16)
    pool_w = pool_w_l1.astype(BF16)
    tail1 = (w_out_l1.astype(BF16), ln1_g_l1, ln1_b_l1, w_up_l1.astype(BF16), w_down_l1.astype(BF16),
             ln2_g_l1, ln2_b_l1)
    mix_c = _pool_mixer(xc, mod1, ctx_row, w_in1, pool_w, pool_scale_l1, l_ctx, TOKEN_TILE)
    xc = _tail([mix_c], xc, mod1, ctx_row, *tail1)
    mix_l = _pool_mixer(xl, mod1, lambda i: 1 + i, w_in1, pool_w, pool_scale_l1, l_lat, l_lat)
    xl = _tail([mix_l], xl, mod1, lat_row, *tail1)

    return (xc.reshape(n_ctx, l_ctx, D_MODEL), xl.reshape(n_lat, l_lat, D_MODEL),
            jnp.transpose(kc, (0, 4, 1, 2, 3)),
            vc.reshape(n_ctx, l_ctx, N_HEADS_A, 2 * HEAD_DIM_A),
            new_ssm.reshape(n_ctx, 2, N_HEADS_B, HEAD_DIM_B, D_STATE))
```

```python
import functools
import math

import jax
import jax.numpy as jnp
import numpy as np
from jax import lax
from jax.experimental import pallas as pl
from jax.experimental.pallas import tpu as pltpu

F32 = jnp.float32
BF16 = jnp.bfloat16

D_MODEL = 1024
N_MOD = 6
N_HEADS_A = 4
HEAD_DIM_A = 64
A_QK = N_HEADS_A * HEAD_DIM_A
A_V = 2 * A_QK
D_INNER_B = 512
HEAD_DIM_B = 64
N_HEADS_B = 8
D_STATE = 128
D_CONV = 5
CONV_DIM_B = 1024
CHUNK = 128
GRID_W = 64
POOL_WINDOWS = (2, 4, 8, 16)
POOL_GROUP_DIM = 256
D_FF = 4096
ROPE_BASE = 10000.0
LN_EPS = 1e-5
DEPTH = 2
ALPHA = (2 * DEPTH) ** 0.25
NEG_BIG = -1e30
LOG2E = 1.4426950408889634

LANES = 128
TOKEN_TILE = 512
ATTN_Q_TILE = 256
CTX_SEQS_PER_STEP = 4
FF_CHUNK = 1024
CONV_COL_BLOCK = 256
TAIL_SPLIT = (256, 256)
VMEM_LIMIT = 56 * 1024 * 1024
TAIL_VMEM_LIMIT = 58 * 1024 * 1024
TAIL_STAGE_BYTES = 1024 * 1024


def _dot(a, b):
    return jnp.dot(a, b, preferred_element_type=F32)


def _dot_nt(a, b):
    return lax.dot_general(a, b, (((1,), (1,)), ((), ())), preferred_element_type=F32)


def _bf16_pieces(a):
    a1 = a.astype(BF16)
    r1 = a - a1.astype(F32)
    a2 = r1.astype(BF16)
    a3 = (r1 - a2.astype(F32)).astype(BF16)
    return a1, a2, a3


def _dot_exact_rhs(t01, a):
    a1, a2, a3 = _bf16_pieces(a)
    return _dot(t01, a1) + _dot(t01, a2) + _dot(t01, a3)


def _sigmoid(x):
    return 1.0 / (1.0 + jnp.exp(-x))


def _silu(x):
    return x * _sigmoid(x)


def _layer_norm(x, g, b):
    mu = jnp.mean(x, axis=-1, keepdims=True)
    xc = x - mu
    var = jnp.mean(xc * xc, axis=-1, keepdims=True)
    return xc * lax.rsqrt(var + LN_EPS) * g + b


def _const_spec(shape):
    nd = len(shape)
    return pl.BlockSpec(shape, lambda *_: (0,) * nd, pipeline_mode=pl.Buffered(1))


def _params(n_axes=1):
    return pltpu.CompilerParams(dimension_semantics=("arbitrary",) * n_axes,
                                vmem_limit_bytes=VMEM_LIMIT)


def _mod_kernel(cond_ref, w_ref, b_ref, o_ref):
    c = cond_ref[...]
    s = _silu(c).astype(BF16)
    o_ref[...] = _dot(s, w_ref[...].astype(BF16)) + b_ref[...]


def _modulation(cond8, w_mod, b_mod):
    tn = 1024
    n = w_mod.shape[1]
    out = pl.pallas_call(
        _mod_kernel,
        grid=(n // tn,),
        in_specs=[pl.BlockSpec((8, D_MODEL), lambda j: (0, 0)),
                  pl.BlockSpec((D_MODEL, tn), lambda j: (0, j)),
                  pl.BlockSpec((1, tn), lambda j: (0, j))],
        out_specs=pl.BlockSpec((8, tn), lambda j: (0, j)),
        out_shape=jax.ShapeDtypeStruct((8, n), F32),
        compiler_params=_params(),
        name="modulation",
    )(cond8, w_mod, b_mod.reshape(1, n))
    return out.reshape(8, N_MOD, D_MODEL)


def _mod_spec(row_of_tile):
    return pl.BlockSpec((1, N_MOD, D_MODEL), lambda i: (row_of_tile(i), 0, 0))


def _rope(t, cosf, sinf):
    lane = lax.broadcasted_iota(jnp.int32, t.shape, 1)
    first = (lane & (HEAD_DIM_A - 1)) < HEAD_DIM_A // 2
    width = t.shape[1]
    swapped = jnp.where(first, pltpu.roll(t, width - HEAD_DIM_A // 2, 1),
                        pltpu.roll(t, HEAD_DIM_A // 2, 1))
    return t * cosf + swapped * sinf


def _conv_silu(xbc, cw_ref, cb_ref, cols, seq_len):
    assert xbc.shape[0] == seq_len
    half = D_CONV // 2
    edge = 8
    offsets = [j - half for j in range(D_CONV) if j != half]
    shifted = {o: pltpu.roll(xbc, (-o) % seq_len, 0) for o in offsets}

    def taps(lo, hi, masked):
        acc = cb_ref[:, cols] + cw_ref[half:half + 1, cols] * xbc[lo:hi]
        pos = lo + lax.broadcasted_iota(jnp.int32, (hi - lo, xbc.shape[1]), 0)
        for o in offsets:
            s = shifted[o][lo:hi]
            if masked:
                s = jnp.where((pos >= -o) if o < 0 else (pos < seq_len - o), s, 0.0)
            acc = acc + cw_ref[o + half:o + half + 1, cols] * s
        return acc

    acc = jnp.concatenate([taps(0, edge, True), taps(edge, seq_len - edge, False),
                           taps(seq_len - edge, seq_len, True)], axis=0)
    return _silu(acc)


def _softplus(t):
    return jnp.maximum(t, 0.0) + jnp.log(1.0 + jnp.exp(-jnp.abs(t)))


def _inproj0_kernel(*refs, rope, seq_len):
    if rope:
        (x_ref, mod_ref, w_ref, wdt_ref, cw_ref, cb_ref, dtb_ref, cos_ref, sin_ref,
         q_ref, k_ref, v_ref, z_ref, xbc_ref, dt_ref) = refs
    else:
        (x_ref, mod_ref, w_ref, wdt_ref, cw_ref, cb_ref, dtb_ref,
         q_ref, k_ref, v_ref, z_ref, xbc_ref, dt_ref) = refs
    m = mod_ref[0]
    tasks = []
    for b in range(x_ref.shape[0] // seq_len):
        rows = slice(b * seq_len, (b + 1) * seq_len)
        h = (x_ref[rows, :] * (1.0 + m[1:2]) + m[0:1]).astype(BF16)

        for c0 in range(0, CONV_DIM_B, CONV_COL_BLOCK):
            cols = slice(c0, c0 + CONV_COL_BLOCK)

            def conv_mm(h=h, c0=c0):
                return _dot(h, w_ref[:, 2048 + c0:2048 + c0 + CONV_COL_BLOCK])

            def conv_vec(raw, rows=rows, cols=cols):
                xbc_ref[rows, cols] = _conv_silu(raw, cw_ref, cb_ref, cols, seq_len)

            tasks.append((conv_mm, conv_vec))

        def qk_mm(h=h):
            return _dot(h, wdt_ref[...]), _dot(h, w_ref[:, 0:512]), _dot(h, w_ref[:, 512:1024])

        def qk_vec(res, rows=rows, b=b):
            dt_raw, q, k = res
            dt_ref[rows, :] = _softplus(dt_raw + dtb_ref[...])
            if rope:
                q = _rope(q, cos_ref[...], sin_ref[...])
                k = _rope(k, cos_ref[...], sin_ref[...])
            q_ref[rows, :] = q
            for blk in range(2 * A_QK // LANES):
                t = k[:, blk * LANES:(blk + 1) * LANES].T
                which, head = blk // 2, (blk % 2) * 2
                k_ref[b, which, head] = t[0:HEAD_DIM_A]
                k_ref[b, which, head + 1] = t[HEAD_DIM_A:2 * HEAD_DIM_A]

        def vz_mm(h=h):
            return _dot(h, w_ref[:, 1024:1536]), _dot(h, w_ref[:, 1536:2048])

        def vz_vec(res, rows=rows, b=b):
            vals, z = res
            for hh in range(N_HEADS_A):
                v_ref[pl.ds(b * seq_len * N_HEADS_A + hh, seq_len, stride=N_HEADS_A), :] = (
                    vals[:, hh * LANES:(hh + 1) * LANES])
            z_ref[rows, :] = z

        tasks += [(qk_mm, qk_vec), (vz_mm, vz_vec)]

    pending = None
    for mm, vec in tasks:
        res = mm()
        if pending is not None:
            pending[0](pending[1])
        pending = (vec, res)
    pending[0](pending[1])


def _inproj0(x, mod, row_of_tile, w_main, w_dt, conv_w, conv_b, dt_bias, seq_len, tm, rope_tables=None):
    t = x.shape[0]
    assert tm % seq_len == 0
    row = lambda n: pl.BlockSpec((tm, n), lambda i: (i, 0))
    flat = lambda n: jax.ShapeDtypeStruct((t, n), F32)
    consts = [w_main, w_dt, conv_w, conv_b, dt_bias]
    in_specs = [row(D_MODEL), _mod_spec(row_of_tile)] + [_const_spec(cst.shape) for cst in consts]
    args = [x, mod] + consts
    if rope_tables is not None:
        assert tm == seq_len
        in_specs += [_const_spec(tab.shape) for tab in rope_tables]
        args += list(rope_tables)
    kt_dims = (2, N_HEADS_A, HEAD_DIM_A, seq_len)
    k_spec = pl.BlockSpec((tm // seq_len,) + kt_dims, lambda i: (i, 0, 0, 0, 0))
    k_shape = jax.ShapeDtypeStruct((t // seq_len,) + kt_dims, F32)
    return pl.pallas_call(
        functools.partial(_inproj0_kernel, rope=rope_tables is not None, seq_len=seq_len),
        grid=(t // tm,),
        in_specs=in_specs,
        out_specs=[row(512), k_spec, pl.BlockSpec((tm * N_HEADS_A, LANES), lambda i: (i, 0)),
                   row(512), row(CONV_DIM_B), row(LANES)],
        out_shape=[flat(512), k_shape, jax.ShapeDtypeStruct((t * N_HEADS_A, LANES), F32),
                   flat(512), flat(CONV_DIM_B), flat(LANES)],
        compiler_params=_params(),
        name="inproj0",
    )(*args)


def _lambda_full(lam_ref, lam_init):
    lv = lam_ref[...]
    return (jnp.exp(jnp.sum(lv[0:1] * lv[1:2], axis=1, keepdims=True))
            - jnp.exp(jnp.sum(lv[2:3] * lv[3:4], axis=1, keepdims=True)) + lam_init)


def _attn_kernel(*refs, n_seg, tq, lam_init):
    q_ref = refs[0]
    kt_refs = refs[1:1 + n_seg]
    v_refs = refs[1 + n_seg:1 + 2 * n_seg]
    lam_ref, sub_ref, o_ref = refs[1 + 2 * n_seg:]
    lam = _lambda_full(lam_ref, lam_init)
    lane = lax.broadcasted_iota(jnp.int32, (1, A_QK), 1)
    masks = [(lane >= h * HEAD_DIM_A) & (lane < (h + 1) * HEAD_DIM_A) for h in range(N_HEADS_A)]
    keys = [r.shape[-1] for r in kt_refs]

    def stacked(qx):
        return jnp.concatenate([jnp.where(m, qx, 0.0) for m in masks], axis=0).astype(BF16)

    def exp_scores(qx, b, which):
        qs = stacked(qx)
        s = [_dot(qs, r[b, which].reshape(A_QK, n).astype(BF16)) for r, n in zip(kt_refs, keys)]
        mx = functools.reduce(jnp.maximum, [jnp.max(x, axis=1, keepdims=True) for x in s])
        return [jnp.exp2(x - mx).astype(BF16) for x in s]

    outs = []
    for b in range(kt_refs[0].shape[0]):
        q = q_ref[b * tq:(b + 1) * tq, :] * (HEAD_DIM_A ** -0.5 * LOG2E)
        e1 = exp_scores(q[:, 0:A_QK], b, 0)
        e2 = exp_scores(q[:, A_QK:2 * A_QK], b, 1)
        heads = []
        for h in range(N_HEADS_A):
            hrows = slice(h * tq, (h + 1) * tq)
            n1 = n2 = None
            for s, (v_ref, n) in enumerate(zip(v_refs, keys)):
                v_h = v_ref[pl.ds(b * n * N_HEADS_A + h, n, stride=N_HEADS_A), :]
                v_ext = jnp.concatenate([v_h.astype(BF16),
                                         jnp.ones((n, LANES), BF16)], axis=1)
                p1 = _dot(e1[s][hrows], v_ext)
                p2 = _dot(e2[s][hrows], v_ext)
                n1 = p1 if n1 is None else n1 + p1
                n2 = p2 if n2 is None else n2 + p2
            heads.append(n1[:, 0:LANES] / n1[:, LANES:] - lam * (n2[:, 0:LANES] / n2[:, LANES:]))
        o = jnp.concatenate(heads, axis=0)
        ms = jnp.mean(o * o, axis=1, keepdims=True)
        o = (o * lax.rsqrt(ms + 1e-5) * sub_ref[...]) * (1.0 - lam_init)
        outs.append(jnp.concatenate([o[h * tq:(h + 1) * tq] for h in range(N_HEADS_A)], axis=1))
    o_ref[...] = jnp.concatenate(outs, axis=0)


def _attention(q, kts, vs, lam_vecs, subln_w, q_len, tq, seqs_per_step, lam_init):
    n_seq = kts[0].shape[0]
    nb = seqs_per_step
    tiles = q_len // tq
    assert tiles == 1 or nb == 1
    qspec = pl.BlockSpec((nb * tq, 512), lambda b, i: (b * tiles + i, 0))
    ktspecs = [pl.BlockSpec((nb,) + kt.shape[1:], lambda b, i: (b, 0, 0, 0, 0)) for kt in kts]
    vspecs = [pl.BlockSpec((nb * kt.shape[-1] * N_HEADS_A, LANES), lambda b, i: (b, 0)) for kt in kts]
    return pl.pallas_call(
        functools.partial(_attn_kernel, n_seg=len(kts), tq=tq, lam_init=lam_init),
        grid=(n_seq // nb, tiles),
        in_specs=[qspec] + ktspecs + vspecs + [pl.BlockSpec((4, HEAD_DIM_A), lambda b, i: (0, 0)),
                                             pl.BlockSpec((1, LANES), lambda b, i: (0, 0))],
        out_specs=qspec,
        out_shape=jax.ShapeDtypeStruct(q.shape, F32),
        compiler_params=_params(2),
        name="diff_attention",
    )(q, *kts, *vs, lam_vecs, subln_w.reshape(1, LANES))


def _chunk_loop(n, body, unroll=1):
    if n <= 2:
        for c in range(n):
            body(c)
    else:
        def step(c, carry):
            body(c)
            return carry
        lax.fori_loop(0, n, step, 0, unroll=unroll)


def _ssd_kernel(*refs, seq_len, n_seq, has_h0):
    if has_h0:
        (xact, dtv, z_ref, h0_ref, arow_ref, dsk_ref, nw_ref,
         y_ref, ysc, s_sc, cd_sc, ce_sc, htf, htb) = refs
        hfin_ref = None
    else:
        (xact, dtv, z_ref, arow_ref, dsk_ref, nw_ref,
         y_ref, hfin_ref, ysc, s_sc, cd_sc, ce_sc, htf, htb) = refs
        h0_ref = None
    nc = seq_len // CHUNK
    n_pairs = N_HEADS_B // 2
    a_row = -jnp.exp(arow_ref[...]) * LOG2E

    row_i = lax.broadcasted_iota(jnp.int32, (CHUNK, CHUNK), 0)
    col_i = lax.broadcasted_iota(jnp.int32, (CHUNK, CHUNK), 1)
    lower = col_i <= row_i
    upper = col_i >= row_i
    tri = (jnp.where(lower, 1.0, 0.0).astype(BF16), jnp.where(upper, 1.0, 0.0).astype(BF16))
    lo_half = col_i < HEAD_DIM_B

    def split_pair(m):
        zero = jnp.zeros_like(m)
        return jnp.concatenate([jnp.where(lo_half, m, zero), jnp.where(lo_half, zero, m)], axis=0).astype(BF16)

    def decay_terms(c):
        rows = pl.ds(pl.multiple_of(c * CHUNK, CHUNK), CHUNK)
        dtc = dtv[rows, :]
        dtt = dtc.T[0:16, :]
        per_dir = []
        for d in range(2):
            col = _dot_exact_rhs(tri[d], dtc * a_row)
            rowm = col.T[0:16, :]
            far = CHUNK - 1 if d == 0 else 0
            tot = jnp.broadcast_to(rowm[:, far:far + 1], (16, CHUNK))
            dte = jnp.exp2(tot - rowm) * dtt
            per_dir.append((col, rowm - jnp.log2(dtt), dte, jnp.exp2(tot)))
        return per_dir

    def local_phase(c, per_dir=None):
        rows = pl.ds(pl.multiple_of(c * CHUNK, CHUNK), CHUNK)
        if per_dir is None:
            per_dir = decay_terms(c)
        for g in range(2):
            bg = xact[rows, 512 + g * LANES:512 + (g + 1) * LANES]
            cg = xact[rows, 768 + g * LANES:768 + (g + 1) * LANES]
            cbm = _dot_nt(cg.astype(BF16), bg.astype(BF16))
            bgt = bg.T
            for pq in range(2):
                pi = g * 2 + pq
                lanes = slice(pi * LANES, (pi + 1) * LANES)
                xrhs = split_pair(xact[rows, lanes])
                y_pair = None
                for d in range(2):
                    col, row_dt, dte, cdec = per_dir[d]
                    causal = lower if d == 0 else upper
                    m_parts, ce_parts, bw_parts, cd_parts = [], [], [], []
                    for hh in (2 * pi, 2 * pi + 1):
                        hd = 8 * d + hh
                        colb = jnp.sum(jnp.where(col_i == hd, col, 0.0), axis=1, keepdims=True)
                        m_parts.append(cbm * jnp.exp2(jnp.where(causal, colb - row_dt[hd:hd + 1, :], NEG_BIG)))
                        ce_parts.append(cg * jnp.exp2(colb))
                        bw_parts.append(bgt * dte[hd:hd + 1, :])
                        cd_parts.append(cdec[hd:hd + 1, :])
                    yd = _dot(jnp.concatenate(m_parts, axis=1).astype(BF16), xrhs)
                    y_pair = yd if y_pair is None else y_pair + yd
                    s_sc[c, d, :, lanes] = _dot(jnp.concatenate(bw_parts, axis=1).astype(BF16), xrhs)
                    ce_sc[c, d, :, pi * 2 * LANES:(pi + 1) * 2 * LANES] = (
                        jnp.concatenate(ce_parts, axis=1).astype(BF16))
                    cd_sc[c, d, :, lanes] = jnp.broadcast_to(
                        jnp.where(lo_half[0:1, :], cd_parts[0], cd_parts[1]), (8, LANES))
                ysc[rows, lanes] = y_pair

    if n_seq * nc <= 2:
        terms = [decay_terms(c) for c in range(n_seq * nc)]
        for c in range(n_seq * nc):
            local_phase(c, terms[c])
    else:
        _chunk_loop(n_seq * nc, local_phase, unroll=2)

    def scan_sequence(sq):
        for d, ht in ((0, htf), (1, htb)):
            for qd in range(n_pairs):
                lanes = slice(qd * LANES, (qd + 1) * LANES)
                if has_h0:
                    ht[:, lanes] = h0_ref[sq, d, lanes, :].T
                else:
                    ht[:, lanes] = jnp.zeros((D_STATE, LANES), F32)

        def scan_phase(j):
            for d, ht in ((0, htf), (1, htb)):
                c = sq * nc + (j if d == 0 else nc - 1 - j)
                rows = pl.ds(pl.multiple_of(c * CHUNK, CHUNK), CHUNK)
                for pi in range(n_pairs):
                    lanes = slice(pi * LANES, (pi + 1) * LANES)
                    hprev = ht[:, lanes]
                    ysc[rows, lanes] += _dot(ce_sc[c, d, :, pi * 2 * LANES:(pi + 1) * 2 * LANES],
                                             split_pair(hprev))
                    ht[:, lanes] = hprev * cd_sc[c, d, 0:1, lanes] + s_sc[c, d, :, lanes]

        _chunk_loop(nc, scan_phase)
        if hfin_ref is not None:
            for d, ht in ((0, htf), (1, htb)):
                for qd in range(n_pairs):
                    lanes = slice(qd * LANES, (qd + 1) * LANES)
                    hfin_ref[sq, d, lanes, :] = ht[:, lanes].T

    if n_seq == 1:
        scan_sequence(0)
    else:
        def seq_step(sq, carry):
            scan_sequence(sq)
            return carry
        lax.fori_loop(0, n_seq, seq_step, 0)

    def finish(c):
        rows = pl.ds(pl.multiple_of(c * CHUNK, CHUNK), CHUNK)
        y = ysc[rows, :] + dsk_ref[...] * xact[rows, 0:D_INNER_B]
        y = y * _silu(z_ref[rows, :])
        ms = jnp.mean(y * y, axis=1, keepdims=True)
        y_ref[rows, :] = y * lax.rsqrt(ms + 1e-5) * nw_ref[...]

    _chunk_loop(n_seq * nc, finish)


def _ssd(xbc, dt, z, h0, consts, n_batch, seq_len, seqs_per_step):
    nb = seqs_per_step
    L = nb * seq_len
    nc = L // CHUNK
    row = lambda n: pl.BlockSpec((L, n), lambda b: (b, 0))
    state_spec = pl.BlockSpec((nb, 2, D_INNER_B, D_STATE), lambda b: (b, 0, 0, 0))
    has_h0 = h0 is not None
    in_specs = [row(CONV_DIM_B), row(LANES), row(D_INNER_B)]
    args = [xbc, dt, z]
    if has_h0:
        in_specs.append(state_spec)
        args.append(h0)
    for cst in consts:
        in_specs.append(pl.BlockSpec(cst.shape, lambda b: (0, 0)))
        args.append(cst)
    out_specs = [row(D_INNER_B)]
    out_shape = [jax.ShapeDtypeStruct((n_batch * seq_len, D_INNER_B), F32)]
    if not has_h0:
        out_specs.append(state_spec)
        out_shape.append(jax.ShapeDtypeStruct((n_batch, 2, D_INNER_B, D_STATE), F32))
    scratch = [pltpu.VMEM((L, D_INNER_B), F32),
               pltpu.VMEM((nc, 2, D_STATE, D_INNER_B), F32),
               pltpu.VMEM((nc, 2, 8, D_INNER_B), F32),
               pltpu.VMEM((nc, 2, CHUNK, 2 * D_INNER_B), BF16),
               pltpu.VMEM((D_STATE, D_INNER_B), F32),
               pltpu.VMEM((D_STATE, D_INNER_B), F32)]
    return pl.pallas_call(
        functools.partial(_ssd_kernel, seq_len=seq_len, n_seq=nb, has_h0=has_h0),
        grid=(n_batch // nb,),
        in_specs=in_specs,
        out_specs=out_specs,
        out_shape=out_shape,
        scratch_shapes=scratch,
        compiler_params=_params(),
        name="bidir_ssd",
    )(*args)


def _stage_bf16(src_hbm, dst, stage, sem):
    rows_per = stage.shape[1]
    n = src_hbm.shape[0] // rows_per

    def copy(c):
        return pltpu.make_async_copy(src_hbm.at[pl.ds(c * rows_per, rows_per), :], stage.at[c % 2], sem.at[c % 2])

    copy(0).start()
    for c in range(n):
        if c + 1 < n:
            copy(c + 1).start()
        copy(c).wait()
        dst[c * rows_per:(c + 1) * rows_per, :] = stage[c % 2].astype(BF16)


def _tail_kernel(*refs, n_in, n_ctx_tiles):
    ctx_a, lat_a = refs[:n_in], refs[n_in:2 * n_in]
    (xc_ref, xl_ref, mod_ref, wout_hbm, wup_hbm, wdown_hbm, g1_ref, b1_ref, g2_ref, b2_ref,
     oc_ref, ol_ref, wout_ref, wup_ref, wdown_ref, stage_wide, stage_tall, sem) = refs[2 * n_in:]
    step = pl.program_id(0)

    @pl.when(step == 0)
    def _():
        _stage_bf16(wout_hbm, wout_ref, stage_tall, sem)
        _stage_bf16(wup_hbm, wup_ref, stage_wide, sem)
        _stage_bf16(wdown_hbm, wdown_ref, stage_tall, sem)

    tile = functools.partial(_mlp_tile, mod_ref=mod_ref, wout_ref=wout_ref, g1_ref=g1_ref, b1_ref=b1_ref,
                             wup_ref=wup_ref, wdown_ref=wdown_ref, g2_ref=g2_ref, b2_ref=b2_ref)

    @pl.when(step < n_ctx_tiles)
    def _():
        tile(ctx_a, xc_ref, oc_ref)

    @pl.when(step >= n_ctx_tiles)
    def _():
        tile(lat_a, xl_ref, ol_ref)


def _mlp_tile(a_refs, x_ref, o_ref, *, mod_ref, wout_ref, g1_ref, b1_ref, wup_ref, wdown_ref, g2_ref, b2_ref):
    m = mod_ref[0]
    n_chunks = D_FF // FF_CHUNK

    def out_proj(rows):
        d = None
        off = 0
        for a_ref in a_refs:
            n = a_ref.shape[1]
            part = _dot(a_ref[rows, :].astype(BF16), wout_ref[off:off + n, :])
            d = part if d is None else d + part
            off += n
        return d

    def head(rows, d):
        x1 = _layer_norm(ALPHA * x_ref[rows, :] + m[2:3] * d, g1_ref[...], b1_ref[...])
        return x1, (x1 * (1.0 + m[4:5]) + m[3:4]).astype(BF16)

    def up(h, c):
        return _dot(h, wup_ref[:, c * FF_CHUNK:(c + 1) * FF_CHUNK])

    def down(u, acc, c):
        u = jnp.maximum(u, 0.0)
        part = _dot((u * u).astype(BF16), wdown_ref[c * FF_CHUNK:(c + 1) * FF_CHUNK, :])
        return part if acc is None else acc + part

    def finish(rows, x1, acc):
        o_ref[rows, :] = _layer_norm(ALPHA * x1 + m[5:6] * acc, g2_ref[...], b2_ref[...])

    assert sum(TAIL_SPLIT) == x_ref.shape[0]
    k = len(TAIL_SPLIT)
    starts = [sum(TAIL_SPLIT[:t]) for t in range(k)]
    rows = [slice(s, s + n) for s, n in zip(starts, TAIL_SPLIT)]
    projected = [out_proj(r) for r in rows]
    heads = [None] * k
    heads[0] = head(rows[0], projected[0])
    u_next = up(heads[0][1], 0)
    done = None
    for t in range(k):
        if t + 1 < k:
            heads[t + 1] = head(rows[t + 1], projected[t + 1])
        acc = None
        for c in range(n_chunks):
            if c + 1 < n_chunks:
                nxt = up(heads[t][1], c + 1)
            else:
                nxt = up(heads[t + 1][1], 0) if t + 1 < k else None
            u_cur, u_next = u_next, nxt
            acc = down(u_cur, acc, c)
            if c == 0 and done is not None:
                finish(*done)
        done = (rows[t], heads[t][0], acc)
    finish(*done)


def _tail(a_ctx, a_lat, x_ctx, x_lat, mod, lat_seq_len, w_out, ln1_g, ln1_b, w_up, w_down, ln2_g, ln2_b):
    tm = TOKEN_TILE
    n_ctx = x_ctx.shape[0] // tm
    n_lat = x_lat.shape[0] // tm
    tiles_per_lat_seq = lat_seq_len // tm
    ctx_row = lambda n: pl.BlockSpec((tm, n), lambda i: (jnp.minimum(i, n_ctx - 1), 0))
    lat_row = lambda n: pl.BlockSpec((tm, n), lambda i: (jnp.maximum(i - n_ctx, 0), 0), pipeline_mode=pl.Buffered(1))
    lat_out = pl.BlockSpec((tm, D_MODEL), lambda i: (jnp.maximum(i - n_ctx, 0), 0))
    mod_spec = _mod_spec(lambda i: jnp.where(i < n_ctx, 0, 1 + (i - n_ctx) // tiles_per_lat_seq))
    hbm = pl.BlockSpec(memory_space=pl.ANY)
    vec = lambda v: v.reshape(1, D_MODEL)
    ln = [vec(ln1_g), vec(ln1_b), vec(ln2_g), vec(ln2_b)]
    stage_rows = TAIL_STAGE_BYTES // 4
    scratch = [pltpu.VMEM(w_out.shape, BF16), pltpu.VMEM(w_up.shape, BF16), pltpu.VMEM(w_down.shape, BF16),
               pltpu.VMEM((2, stage_rows // D_FF, D_FF), F32), pltpu.VMEM((2, stage_rows // D_MODEL, D_MODEL), F32),
               pltpu.SemaphoreType.DMA((2,))]
    return pl.pallas_call(
        functools.partial(_tail_kernel, n_in=len(a_ctx), n_ctx_tiles=n_ctx),
        grid=(n_ctx + n_lat,),
        in_specs=[ctx_row(a.shape[1]) for a in a_ctx] + [lat_row(a.shape[1]) for a in a_lat]
                 + [ctx_row(D_MODEL), lat_row(D_MODEL), mod_spec, hbm, hbm, hbm]
                 + [_const_spec(v.shape) for v in ln],
        out_specs=[ctx_row(D_MODEL), lat_out],
        out_shape=[jax.ShapeDtypeStruct(x_ctx.shape, F32), jax.ShapeDtypeStruct(x_lat.shape, F32)],
        scratch_shapes=scratch,
        compiler_params=pltpu.CompilerParams(dimension_semantics=("arbitrary",), vmem_limit_bytes=TAIL_VMEM_LIMIT),
        name="outproj_mlp",
    )(*a_ctx, *a_lat, x_ctx, x_lat, mod, w_out, w_up, w_down, *ln)


def _pool_kernel(x_ref, mod_ref, win_ref, pw_ref, ps_ref, o_ref, *, seq_len):
    L = seq_len
    m = mod_ref[0]
    t_i = lax.broadcasted_iota(jnp.int32, (L, L), 0)
    s_i = lax.broadcasted_iota(jnp.int32, (L, L), 1)
    t_g = lax.broadcasted_iota(jnp.int32, (L, POOL_GROUP_DIM), 0)
    bands, scales = [], []
    for w in POOL_WINDOWS:
        lo, hi = w // 2, w - w // 2
        bands.append(jnp.where((s_i >= t_i - lo) & (s_i < t_i + hi), 1.0 / w, 0.0).astype(BF16))
        scales.append(w / (jnp.minimum(t_g + hi, L) - jnp.maximum(t_g - lo, 0)).astype(F32))
    n_seq = x_ref.shape[0] // L
    seq_rows = [slice(s * L, (s + 1) * L) for s in range(n_seq)]
    hs = [(x_ref[r, :] * (1.0 + m[1:2]) + m[0:1]).astype(BF16) for r in seq_rows]
    us = [_dot(h, win_ref[...]) for h in hs]
    n_groups = len(POOL_WINDOWS)
    cols = [slice(g * POOL_GROUP_DIM, (g + 1) * POOL_GROUP_DIM) for g in range(n_groups)]

    def window_means(g):
        out = []
        for s in range(n_seq):
            ug = us[s][:, cols[g]]
            ug_hi = ug.astype(BF16)
            ug_lo = (ug - ug_hi.astype(F32)).astype(BF16)
            out.append((ug, _dot(bands[g], ug_hi) + _dot(bands[g], ug_lo)))
        return out

    def mix(g, means):
        return [_dot((wm * scales[g] - ug).astype(BF16), pw_ref[g]) * ps_ref[:, cols[g]] for ug, wm in means]

    mixed = []
    pending = window_means(0)
    for g in range(n_groups):
        nxt = window_means(g + 1) if g + 1 < n_groups else None
        mixed.append(mix(g, pending))
        pending = nxt
    for s in range(n_seq):
        o_ref[seq_rows[s], :] = jnp.concatenate([mixed[g][s] for g in range(n_groups)], axis=1)


def _pool_mixer(x, mod, row_of_tile, w_in, pool_w, pool_scale, seq_len, tm):
    t = x.shape[0]
    row = pl.BlockSpec((tm, D_MODEL), lambda i: (i, 0))
    consts = [w_in, pool_w, pool_scale.reshape(1, D_MODEL)]
    return pl.pallas_call(
        functools.partial(_pool_kernel, seq_len=seq_len),
        grid=(t // tm,),
        in_specs=[row, _mod_spec(row_of_tile)] + [_const_spec(cst.shape) for cst in consts],
        out_specs=row,
        out_shape=jax.ShapeDtypeStruct((t, D_MODEL), F32),
        compiler_params=_params(),
        name="pool_mixer",
    )(x, mod, *consts)


def _rope_tables(rows):
    r, col = np.meshgrid(np.arange(rows), np.arange(GRID_W), indexing="ij")
    r = r.reshape(-1).astype(np.float64)
    col = col.reshape(-1).astype(np.float64)
    n_freq = HEAD_DIM_A // 4
    inv = ROPE_BASE ** (-np.arange(n_freq, dtype=np.float64) / n_freq)
    ang = np.concatenate([r[:, None] * inv, col[:, None] * inv], -1)
    reps = 512 // HEAD_DIM_A
    cosf = np.tile(np.concatenate([np.cos(ang), np.cos(ang)], -1), (1, reps)).astype(np.float32)
    sinf = np.tile(np.concatenate([-np.sin(ang), np.sin(ang)], -1), (1, reps)).astype(np.float32)
    return jnp.asarray(cosf), jnp.asarray(sinf)


def kernel(x_prompt, x_sample, cache_k_l0, cache_v_l0, state_ssm_l0, c, c_ctx, w_mod_l0, b_mod_l0, w_in_l0, conv_w_l0, conv_b_l0, a_log_l0, dt_bias_l0, d_skip_l0, ssm_norm_w_l0, lam_q1_l0, lam_k1_l0, lam_q2_l0, lam_k2_l0, subln_w_l0, w_out_l0, ln1_g_l0, ln1_b_l0, w_up_l0, w_down_l0, ln2_g_l0, ln2_b_l0, w_mod_l1, b_mod_l1, w_in_l1, pool_w_l1, pool_scale_l1, w_out_l1, ln1_g_l1, ln1_b_l1, w_up_l1, w_down_l1, ln2_g_l1, ln2_b_l1):
    n_ctx, l_ctx, _ = x_prompt.shape
    n_lat, l_lat, _ = x_sample.shape
    past = cache_k_l0.shape[1]
    xc = x_prompt.reshape(n_ctx * l_ctx, D_MODEL)
    xl = x_sample.reshape(n_lat * l_lat, D_MODEL)

    cond8 = jnp.concatenate([c_ctx[None, :], c, jnp.zeros((8 - 1 - n_lat, D_MODEL), F32)], axis=0)
    ctx_row = lambda i: 0
    lat_tiles = l_lat // TOKEN_TILE
    lat_row = lambda i: 1 + i // lat_tiles

    mod0 = _modulation(cond8, w_mod_l0, b_mod_l0)
    n_main = 4 * A_QK + A_V + D_INNER_B + CONV_DIM_B
    w_main = w_in_l0.astype(BF16)
    w_dt = jnp.pad(w_in_l0[:, n_main:], ((0, 0), (0, LANES - 2 * N_HEADS_B))).astype(BF16)
    lane_pad = lambda v: jnp.pad(v.reshape(1, -1), ((0, 0), (0, LANES - 2 * N_HEADS_B)))
    conv_consts = (conv_w_l0, conv_b_l0.reshape(1, CONV_DIM_B), lane_pad(dt_bias_l0))
    ssd_consts = [lane_pad(a_log_l0),
                  jnp.repeat(d_skip_l0, HEAD_DIM_B).reshape(1, D_INNER_B),
                  ssm_norm_w_l0.reshape(1, D_INNER_B)]
    lam_vecs = jnp.stack([lam_q1_l0, lam_k1_l0, lam_q2_l0, lam_k2_l0], axis=0)
    lam_init = 0.8 - 0.6 * math.exp(-0.3 * 0)
    tail0 = (w_out_l0, ln1_g_l0, ln1_b_l0, w_up_l0, w_down_l0, ln2_g_l0, ln2_b_l0)

    qc, kc, vc, zc, xbc_c, dt_c = _inproj0(xc, mod0, ctx_row, w_main, w_dt, *conv_consts, l_ctx, 2 * TOKEN_TILE)
    oa_c = _attention(qc, [kc], [vc], lam_vecs, subln_w_l0, l_ctx, l_ctx, CTX_SEQS_PER_STEP, lam_init)
    y_c, new_ssm = _ssd(xbc_c, dt_c, zc, None, ssd_consts, n_ctx, l_ctx, 1)

    ql, kl, vl, zl, xbc_l, dt_l = _inproj0(xl, mod0, lambda i: 1 + i, w_main, w_dt, *conv_consts, l_lat, l_lat,
                                           rope_tables=_rope_tables(l_lat // GRID_W))
    ck = jnp.transpose(cache_k_l0, (0, 2, 3, 4, 1))
    cv = cache_v_l0.reshape(n_lat * past * N_HEADS_A, 2 * HEAD_DIM_A)
    oa_l = _attention(ql, [ck, kl], [cv, vl], lam_vecs, subln_w_l0, l_lat, ATTN_Q_TILE, 1, lam_init)
    h0 = state_ssm_l0.reshape(n_lat, 2, D_INNER_B, D_STATE)
    (y_l,) = _ssd(xbc_l, dt_l, zl, h0, ssd_consts, n_lat, l_lat, 1)
    xc, xl = _tail([oa_c, y_c], [oa_l, y_l], xc, xl, mod0, l_lat, *tail0)

    mod1 = _modulation(cond8, w_mod_l1, b_mod_l1)
    w_in1 = w_in_l1.astype(BF16)
    pool_w = pool_w_l1.astype(BF16)
    tail1 = (w_out_l1, ln1_g_l1, ln1_b_l1, w_up_l1, w_down_l1, ln2_g_l1, ln2_b_l1)
    mix_c = _pool_mixer(xc, mod1, ctx_row, w_in1, pool_w, pool_scale_l1, l_ctx, TOKEN_TILE)
    mix_l = _pool_mixer(xl, mod1, lambda i: 1 + i, w_in1, pool_w, pool_scale_l1, l_lat, l_lat)
    xc, xl = _tail([mix_c], [mix_l], xc, xl, mod1, l_lat, *tail1)

    return (xc.reshape(n_ctx, l_ctx, D_MODEL), xl.reshape(n_lat, l_lat, D_MODEL),
            jnp.transpose(kc, (0, 4, 1, 2, 3)),
            vc.reshape(n_ctx, l_ctx, N_HEADS_A, 2 * HEAD_DIM_A),
            new_ssm.reshape(n_ctx, 2, N_HEADS_B, HEAD_DIM_B, D_STATE))
```

```python
import functools
import math

import jax
import jax.numpy as jnp
import numpy as np
from jax import lax
from jax.experimental import pallas as pl
from jax.experimental.pallas import tpu as pltpu

F32 = jnp.float32
BF16 = jnp.bfloat16

D_MODEL = 1024
N_MOD = 6
N_HEADS_A = 4
HEAD_DIM_A = 64
A_QK = N_HEADS_A * HEAD_DIM_A
A_V = 2 * A_QK
D_INNER_B = 512
HEAD_DIM_B = 64
N_HEADS_B = 8
D_STATE = 128
D_CONV = 5
CONV_DIM_B = 1024
CHUNK = 128
GRID_W = 64
POOL_WINDOWS = (2, 4, 8, 16)
POOL_GROUP_DIM = 256
D_FF = 4096
ROPE_BASE = 10000.0
LN_EPS = 1e-5
DEPTH = 2
ALPHA = (2 * DEPTH) ** 0.25
NEG_BIG = -1e30
LOG2E = 1.4426950408889634

LANES = 128
TOKEN_TILE = 512
ATTN_Q_TILE = 256
CTX_SEQS_PER_STEP = 4
FF_CHUNK = 1024
CONV_COL_BLOCK = 256
TAIL_SPLIT = (256, 256)
VMEM_LIMIT = 56 * 1024 * 1024


def _dot(a, b):
    return jnp.dot(a, b, preferred_element_type=F32)


def _dot_nt(a, b):
    return lax.dot_general(a, b, (((1,), (1,)), ((), ())), preferred_element_type=F32)


def _bf16_pieces(a):
    a1 = a.astype(BF16)
    r1 = a - a1.astype(F32)
    a2 = r1.astype(BF16)
    a3 = (r1 - a2.astype(F32)).astype(BF16)
    return a1, a2, a3


def _dot_exact_rhs(t01, a):
    a1, a2, a3 = _bf16_pieces(a)
    return _dot(t01, a1) + _dot(t01, a2) + _dot(t01, a3)


def _sigmoid(x):
    return 1.0 / (1.0 + jnp.exp(-x))


def _silu(x):
    return x * _sigmoid(x)


def _layer_norm(x, g, b):
    mu = jnp.mean(x, axis=-1, keepdims=True)
    xc = x - mu
    var = jnp.mean(xc * xc, axis=-1, keepdims=True)
    return xc * lax.rsqrt(var + LN_EPS) * g + b


def _const_spec(shape):
    nd = len(shape)
    return pl.BlockSpec(shape, lambda *_: (0,) * nd, pipeline_mode=pl.Buffered(1))


def _params(n_axes=1):
    return pltpu.CompilerParams(dimension_semantics=("arbitrary",) * n_axes,
                                vmem_limit_bytes=VMEM_LIMIT)


def _mod_kernel(cond_ref, w_ref, b_ref, o_ref):
    c = cond_ref[...]
    s = _silu(c).astype(BF16)
    o_ref[...] = _dot(s, w_ref[...].astype(BF16)) + b_ref[...]


def _modulation(cond8, w_mod, b_mod):
    tn = 1024
    n = w_mod.shape[1]
    out = pl.pallas_call(
        _mod_kernel,
        grid=(n // tn,),
        in_specs=[pl.BlockSpec((8, D_MODEL), lambda j: (0, 0)),
                  pl.BlockSpec((D_MODEL, tn), lambda j: (0, j)),
                  pl.BlockSpec((1, tn), lambda j: (0, j))],
        out_specs=pl.BlockSpec((8, tn), lambda j: (0, j)),
        out_shape=jax.ShapeDtypeStruct((8, n), F32),
        compiler_params=_params(),
        name="modulation",
    )(cond8, w_mod, b_mod.reshape(1, n))
    return out.reshape(8, N_MOD, D_MODEL)


def _mod_spec(row_of_tile):
    return pl.BlockSpec((1, N_MOD, D_MODEL), lambda i: (row_of_tile(i), 0, 0))


def _rope(t, cosf, sinf):
    lane = lax.broadcasted_iota(jnp.int32, t.shape, 1)
    first = (lane & (HEAD_DIM_A - 1)) < HEAD_DIM_A // 2
    width = t.shape[1]
    swapped = jnp.where(first, pltpu.roll(t, width - HEAD_DIM_A // 2, 1),
                        pltpu.roll(t, HEAD_DIM_A // 2, 1))
    return t * cosf + swapped * sinf


def _conv_silu(xbc, cw_ref, cb_ref, cols, seq_len):
    assert xbc.shape[0] == seq_len
    half = D_CONV // 2
    edge = 8
    offsets = [j - half for j in range(D_CONV) if j != half]
    shifted = {o: pltpu.roll(xbc, (-o) % seq_len, 0) for o in offsets}

    def taps(lo, hi, masked):
        acc = cb_ref[:, cols] + cw_ref[half:half + 1, cols] * xbc[lo:hi]
        pos = lo + lax.broadcasted_iota(jnp.int32, (hi - lo, xbc.shape[1]), 0)
        for o in offsets:
            s = shifted[o][lo:hi]
            if masked:
                s = jnp.where((pos >= -o) if o < 0 else (pos < seq_len - o), s, 0.0)
            acc = acc + cw_ref[o + half:o + half + 1, cols] * s
        return acc

    acc = jnp.concatenate([taps(0, edge, True), taps(edge, seq_len - edge, False),
                           taps(seq_len - edge, seq_len, True)], axis=0)
    return _silu(acc)


def _softplus(t):
    return jnp.maximum(t, 0.0) + jnp.log(1.0 + jnp.exp(-jnp.abs(t)))


def _ride_along_specs(mats, n_steps):
    specs, shapes = [], []
    for w in mats:
        assert w.shape[0] % n_steps == 0
        specs.append(pl.BlockSpec((w.shape[0] // n_steps, w.shape[1]), lambda i: (i, 0)))
        shapes.append(jax.ShapeDtypeStruct(w.shape, BF16))
    return specs, shapes


def _ride_along_cast(src_refs, dst_refs):
    for src, dst in zip(src_refs, dst_refs):
        dst[...] = src[...].astype(BF16)


def _inproj0_kernel(*refs, rope, seq_len, n_cast):
    n_in = 9 if rope else 7
    ins, cast_in = refs[:n_in], refs[n_in:n_in + n_cast]
    (q_ref, k_ref, v_ref, z_ref, xbc_ref, dt_ref) = refs[n_in + n_cast:n_in + n_cast + 6]
    cast_out = refs[n_in + n_cast + 6:]
    if rope:
        x_ref, mod_ref, w_ref, wdt_ref, cw_ref, cb_ref, dtb_ref, cos_ref, sin_ref = ins
    else:
        x_ref, mod_ref, w_ref, wdt_ref, cw_ref, cb_ref, dtb_ref = ins
    _ride_along_cast(cast_in, cast_out)
    m = mod_ref[0]
    tasks = []
    for b in range(x_ref.shape[0] // seq_len):
        rows = slice(b * seq_len, (b + 1) * seq_len)
        h = (x_ref[rows, :] * (1.0 + m[1:2]) + m[0:1]).astype(BF16)

        for c0 in range(0, CONV_DIM_B, CONV_COL_BLOCK):
            cols = slice(c0, c0 + CONV_COL_BLOCK)

            def conv_mm(h=h, c0=c0):
                return _dot(h, w_ref[:, 2048 + c0:2048 + c0 + CONV_COL_BLOCK])

            def conv_vec(raw, rows=rows, cols=cols):
                xbc_ref[rows, cols] = _conv_silu(raw, cw_ref, cb_ref, cols, seq_len)

            tasks.append((conv_mm, conv_vec))

        def qk_mm(h=h):
            return _dot(h, wdt_ref[...]), _dot(h, w_ref[:, 0:512]), _dot(h, w_ref[:, 512:1024])

        def qk_vec(res, rows=rows, b=b):
            dt_raw, q, k = res
            dt_ref[rows, :] = _softplus(dt_raw + dtb_ref[...])
            if rope:
                q = _rope(q, cos_ref[...], sin_ref[...])
                k = _rope(k, cos_ref[...], sin_ref[...])
            q_ref[rows, :] = q
            for blk in range(2 * A_QK // LANES):
                t = k[:, blk * LANES:(blk + 1) * LANES].T
                which, head = blk // 2, (blk % 2) * 2
                k_ref[b, which, head] = t[0:HEAD_DIM_A]
                k_ref[b, which, head + 1] = t[HEAD_DIM_A:2 * HEAD_DIM_A]

        def vz_mm(h=h):
            return _dot(h, w_ref[:, 1024:1536]), _dot(h, w_ref[:, 1536:2048])

        def vz_vec(res, rows=rows, b=b):
            vals, z = res
            for hh in range(N_HEADS_A):
                v_ref[pl.ds(b * seq_len * N_HEADS_A + hh, seq_len, stride=N_HEADS_A), :] = (
                    vals[:, hh * LANES:(hh + 1) * LANES])
            z_ref[rows, :] = z

        tasks += [(qk_mm, qk_vec), (vz_mm, vz_vec)]

    pending = None
    for mm, vec in tasks:
        res = mm()
        if pending is not None:
            pending[0](pending[1])
        pending = (vec, res)
    pending[0](pending[1])


def _inproj0(x, mod, row_of_tile, w_main, w_dt, conv_w, conv_b, dt_bias, seq_len, tm, rope_tables=None,
             ride_along=()):
    t = x.shape[0]
    assert tm % seq_len == 0
    row = lambda n: pl.BlockSpec((tm, n), lambda i: (i, 0))
    flat = lambda n: jax.ShapeDtypeStruct((t, n), F32)
    consts = [w_main, w_dt, conv_w, conv_b, dt_bias]
    in_specs = [row(D_MODEL), _mod_spec(row_of_tile)] + [_const_spec(cst.shape) for cst in consts]
    args = [x, mod] + consts
    if rope_tables is not None:
        assert tm == seq_len
        in_specs += [_const_spec(tab.shape) for tab in rope_tables]
        args += list(rope_tables)
    kt_dims = (2, N_HEADS_A, HEAD_DIM_A, seq_len)
    k_spec = pl.BlockSpec((tm // seq_len,) + kt_dims, lambda i: (i, 0, 0, 0, 0))
    k_shape = jax.ShapeDtypeStruct((t // seq_len,) + kt_dims, F32)
    cast_specs, cast_shapes = _ride_along_specs(ride_along, t // tm)
    outs = pl.pallas_call(
        functools.partial(_inproj0_kernel, rope=rope_tables is not None, seq_len=seq_len, n_cast=len(ride_along)),
        grid=(t // tm,),
        in_specs=in_specs + cast_specs,
        out_specs=[row(512), k_spec, pl.BlockSpec((tm * N_HEADS_A, LANES), lambda i: (i, 0)),
                   row(512), row(CONV_DIM_B), row(LANES)] + cast_specs,
        out_shape=[flat(512), k_shape, jax.ShapeDtypeStruct((t * N_HEADS_A, LANES), F32),
                   flat(512), flat(CONV_DIM_B), flat(LANES)] + cast_shapes,
        compiler_params=_params(),
        name="inproj0",
    )(*args, *ride_along)
    return outs[:6], outs[6:]


def _lambda_full(lam_ref, lam_init):
    lv = lam_ref[...]
    return (jnp.exp(jnp.sum(lv[0:1] * lv[1:2], axis=1, keepdims=True))
            - jnp.exp(jnp.sum(lv[2:3] * lv[3:4], axis=1, keepdims=True)) + lam_init)


def _attn_kernel(*refs, n_seg, tq, lam_init):
    q_ref = refs[0]
    kt_refs = refs[1:1 + n_seg]
    v_refs = refs[1 + n_seg:1 + 2 * n_seg]
    lam_ref, sub_ref, o_ref = refs[1 + 2 * n_seg:]
    lam = _lambda_full(lam_ref, lam_init)
    lane = lax.broadcasted_iota(jnp.int32, (1, A_QK), 1)
    masks = [(lane >= h * HEAD_DIM_A) & (lane < (h + 1) * HEAD_DIM_A) for h in range(N_HEADS_A)]
    keys = [r.shape[-1] for r in kt_refs]

    def stacked(qx):
        return jnp.concatenate([jnp.where(m, qx, 0.0) for m in masks], axis=0).astype(BF16)

    def exp_scores(qx, b, which):
        qs = stacked(qx)
        s = [_dot(qs, r[b, which].reshape(A_QK, n).astype(BF16)) for r, n in zip(kt_refs, keys)]
        mx = functools.reduce(jnp.maximum, [jnp.max(x, axis=1, keepdims=True) for x in s])
        return [jnp.exp2(x - mx).astype(BF16) for x in s]

    outs = []
    for b in range(kt_refs[0].shape[0]):
        q = q_ref[b * tq:(b + 1) * tq, :] * (HEAD_DIM_A ** -0.5 * LOG2E)
        e1 = exp_scores(q[:, 0:A_QK], b, 0)
        e2 = exp_scores(q[:, A_QK:2 * A_QK], b, 1)
        heads = []
        for h in range(N_HEADS_A):
            hrows = slice(h * tq, (h + 1) * tq)
            n1 = n2 = None
            for s, (v_ref, n) in enumerate(zip(v_refs, keys)):
                v_h = v_ref[pl.ds(b * n * N_HEADS_A + h, n, stride=N_HEADS_A), :]
                v_ext = jnp.concatenate([v_h.astype(BF16),
                                         jnp.ones((n, LANES), BF16)], axis=1)
                p1 = _dot(e1[s][hrows], v_ext)
                p2 = _dot(e2[s][hrows], v_ext)
                n1 = p1 if n1 is None else n1 + p1
                n2 = p2 if n2 is None else n2 + p2
            heads.append(n1[:, 0:LANES] / n1[:, LANES:] - lam * (n2[:, 0:LANES] / n2[:, LANES:]))
        o = jnp.concatenate(heads, axis=0)
        ms = jnp.mean(o * o, axis=1, keepdims=True)
        o = (o * lax.rsqrt(ms + 1e-5) * sub_ref[...]) * (1.0 - lam_init)
        outs.append(jnp.concatenate([o[h * tq:(h + 1) * tq] for h in range(N_HEADS_A)], axis=1))
    o_ref[...] = jnp.concatenate(outs, axis=0)


def _attention(q, kts, vs, lam_vecs, subln_w, q_len, tq, seqs_per_step, lam_init):
    n_seq = kts[0].shape[0]
    nb = seqs_per_step
    tiles = q_len // tq
    assert tiles == 1 or nb == 1
    qspec = pl.BlockSpec((nb * tq, 512), lambda b, i: (b * tiles + i, 0))
    ktspecs = [pl.BlockSpec((nb,) + kt.shape[1:], lambda b, i: (b, 0, 0, 0, 0)) for kt in kts]
    vspecs = [pl.BlockSpec((nb * kt.shape[-1] * N_HEADS_A, LANES), lambda b, i: (b, 0)) for kt in kts]
    return pl.pallas_call(
        functools.partial(_attn_kernel, n_seg=len(kts), tq=tq, lam_init=lam_init),
        grid=(n_seq // nb, tiles),
        in_specs=[qspec] + ktspecs + vspecs + [pl.BlockSpec((4, HEAD_DIM_A), lambda b, i: (0, 0)),
                                             pl.BlockSpec((1, LANES), lambda b, i: (0, 0))],
        out_specs=qspec,
        out_shape=jax.ShapeDtypeStruct(q.shape, F32),
        compiler_params=_params(2),
        name="diff_attention",
    )(q, *kts, *vs, lam_vecs, subln_w.reshape(1, LANES))


def _chunk_loop(n, body, unroll=1):
    if n <= 2:
        for c in range(n):
            body(c)
    else:
        def step(c, carry):
            body(c)
            return carry
        lax.fori_loop(0, n, step, 0, unroll=unroll)


def _ssd_kernel(*refs, seq_len, n_seq, has_h0, n_cast):
    n_in = 7 if has_h0 else 6
    n_out = 1 if has_h0 else 2
    ins, cast_in = refs[:n_in], refs[n_in:n_in + n_cast]
    outs = refs[n_in + n_cast:n_in + n_cast + n_out]
    cast_out = refs[n_in + n_cast + n_out:n_in + 2 * n_cast + n_out]
    ysc, s_sc, cd_sc, ce_sc, htf, htb = refs[n_in + 2 * n_cast + n_out:]
    if has_h0:
        xact, dtv, z_ref, h0_ref, arow_ref, dsk_ref, nw_ref = ins
        (y_ref,), hfin_ref = outs, None
    else:
        xact, dtv, z_ref, arow_ref, dsk_ref, nw_ref = ins
        (y_ref, hfin_ref), h0_ref = outs, None
    _ride_along_cast(cast_in, cast_out)
    nc = seq_len // CHUNK
    n_pairs = N_HEADS_B // 2
    a_row = -jnp.exp(arow_ref[...]) * LOG2E

    row_i = lax.broadcasted_iota(jnp.int32, (CHUNK, CHUNK), 0)
    col_i = lax.broadcasted_iota(jnp.int32, (CHUNK, CHUNK), 1)
    lower = col_i <= row_i
    upper = col_i >= row_i
    tri = (jnp.where(lower, 1.0, 0.0).astype(BF16), jnp.where(upper, 1.0, 0.0).astype(BF16))
    lo_half = col_i < HEAD_DIM_B

    def split_pair(m):
        zero = jnp.zeros_like(m)
        return jnp.concatenate([jnp.where(lo_half, m, zero), jnp.where(lo_half, zero, m)], axis=0).astype(BF16)

    def decay_terms(c):
        rows = pl.ds(pl.multiple_of(c * CHUNK, CHUNK), CHUNK)
        dtc = dtv[rows, :]
        dtt = dtc.T[0:16, :]
        per_dir = []
        for d in range(2):
            col = _dot_exact_rhs(tri[d], dtc * a_row)
            rowm = col.T[0:16, :]
            far = CHUNK - 1 if d == 0 else 0
            tot = jnp.broadcast_to(rowm[:, far:far + 1], (16, CHUNK))
            dte = jnp.exp2(tot - rowm) * dtt
            per_dir.append((col, rowm - jnp.log2(dtt), dte, jnp.exp2(tot)))
        return per_dir

    def local_phase(c, per_dir=None):
        rows = pl.ds(pl.multiple_of(c * CHUNK, CHUNK), CHUNK)
        if per_dir is None:
            per_dir = decay_terms(c)
        for g in range(2):
            bg = xact[rows, 512 + g * LANES:512 + (g + 1) * LANES]
            cg = xact[rows, 768 + g * LANES:768 + (g + 1) * LANES]
            cbm = _dot_nt(cg.astype(BF16), bg.astype(BF16))
            bgt = bg.T
            for pq in range(2):
                pi = g * 2 + pq
                lanes = slice(pi * LANES, (pi + 1) * LANES)
                xrhs = split_pair(xact[rows, lanes])
                y_pair = None
                for d in range(2):
                    col, row_dt, dte, cdec = per_dir[d]
                    causal = lower if d == 0 else upper
                    m_parts, ce_parts, bw_parts, cd_parts = [], [], [], []
                    for hh in (2 * pi, 2 * pi + 1):
                        hd = 8 * d + hh
                        colb = jnp.sum(jnp.where(col_i == hd, col, 0.0), axis=1, keepdims=True)
                        m_parts.append(cbm * jnp.exp2(jnp.where(causal, colb - row_dt[hd:hd + 1, :], NEG_BIG)))
                        ce_parts.append(cg * jnp.exp2(colb))
                        bw_parts.append(bgt * dte[hd:hd + 1, :])
                        cd_parts.append(cdec[hd:hd + 1, :])
                    yd = _dot(jnp.concatenate(m_parts, axis=1).astype(BF16), xrhs)
                    y_pair = yd if y_pair is None else y_pair + yd
                    s_sc[c, d, :, lanes] = _dot(jnp.concatenate(bw_parts, axis=1).astype(BF16), xrhs)
                    ce_sc[c, d, :, pi * 2 * LANES:(pi + 1) * 2 * LANES] = (
                        jnp.concatenate(ce_parts, axis=1).astype(BF16))
                    cd_sc[c, d, :, lanes] = jnp.broadcast_to(
                        jnp.where(lo_half[0:1, :], cd_parts[0], cd_parts[1]), (8, LANES))
                ysc[rows, lanes] = y_pair

    if n_seq * nc <= 2:
        terms = [decay_terms(c) for c in range(n_seq * nc)]
        for c in range(n_seq * nc):
            local_phase(c, terms[c])
    else:
        _chunk_loop(n_seq * nc, local_phase, unroll=2)

    def scan_sequence(sq):
        for d, ht in ((0, htf), (1, htb)):
            for qd in range(n_pairs):
                lanes = slice(qd * LANES, (qd + 1) * LANES)
                if has_h0:
                    ht[:, lanes] = h0_ref[sq, d, lanes, :].T
                else:
                    ht[:, lanes] = jnp.zeros((D_STATE, LANES), F32)

        def scan_phase(j):
            for d, ht in ((0, htf), (1, htb)):
                c = sq * nc + (j if d == 0 else nc - 1 - j)
                rows = pl.ds(pl.multiple_of(c * CHUNK, CHUNK), CHUNK)
                for pi in range(n_pairs):
                    lanes = slice(pi * LANES, (pi + 1) * LANES)
                    hprev = ht[:, lanes]
                    ysc[rows, lanes] += _dot(ce_sc[c, d, :, pi * 2 * LANES:(pi + 1) * 2 * LANES],
                                             split_pair(hprev))
                    ht[:, lanes] = hprev * cd_sc[c, d, 0:1, lanes] + s_sc[c, d, :, lanes]

        _chunk_loop(nc, scan_phase)
        if hfin_ref is not None:
            for d, ht in ((0, htf), (1, htb)):
                for qd in range(n_pairs):
                    lanes = slice(qd * LANES, (qd + 1) * LANES)
                    hfin_ref[sq, d, lanes, :] = ht[:, lanes].T

    if n_seq == 1:
        scan_sequence(0)
    else:
        def seq_step(sq, carry):
            scan_sequence(sq)
            return carry
        lax.fori_loop(0, n_seq, seq_step, 0)

    def finish(c):
        rows = pl.ds(pl.multiple_of(c * CHUNK, CHUNK), CHUNK)
        y = ysc[rows, :] + dsk_ref[...] * xact[rows, 0:D_INNER_B]
        y = y * _silu(z_ref[rows, :])
        ms = jnp.mean(y * y, axis=1, keepdims=True)
        y_ref[rows, :] = y * lax.rsqrt(ms + 1e-5) * nw_ref[...]

    _chunk_loop(n_seq * nc, finish)


def _ssd(xbc, dt, z, h0, consts, n_batch, seq_len, seqs_per_step, ride_along=()):
    nb = seqs_per_step
    L = nb * seq_len
    nc = L // CHUNK
    row = lambda n: pl.BlockSpec((L, n), lambda b: (b, 0))
    state_spec = pl.BlockSpec((nb, 2, D_INNER_B, D_STATE), lambda b: (b, 0, 0, 0))
    has_h0 = h0 is not None
    in_specs = [row(CONV_DIM_B), row(LANES), row(D_INNER_B)]
    args = [xbc, dt, z]
    if has_h0:
        in_specs.append(state_spec)
        args.append(h0)
    for cst in consts:
        in_specs.append(pl.BlockSpec(cst.shape, lambda b: (0, 0)))
        args.append(cst)
    out_specs = [row(D_INNER_B)]
    out_shape = [jax.ShapeDtypeStruct((n_batch * seq_len, D_INNER_B), F32)]
    if not has_h0:
        out_specs.append(state_spec)
        out_shape.append(jax.ShapeDtypeStruct((n_batch, 2, D_INNER_B, D_STATE), F32))
    scratch = [pltpu.VMEM((L, D_INNER_B), F32),
               pltpu.VMEM((nc, 2, D_STATE, D_INNER_B), F32),
               pltpu.VMEM((nc, 2, 8, D_INNER_B), F32),
               pltpu.VMEM((nc, 2, CHUNK, 2 * D_INNER_B), BF16),
               pltpu.VMEM((D_STATE, D_INNER_B), F32),
               pltpu.VMEM((D_STATE, D_INNER_B), F32)]
    cast_specs, cast_shapes = _ride_along_specs(ride_along, n_batch // nb)
    n_main_out = len(out_specs)
    outs = pl.pallas_call(
        functools.partial(_ssd_kernel, seq_len=seq_len, n_seq=nb, has_h0=has_h0, n_cast=len(ride_along)),
        grid=(n_batch // nb,),
        in_specs=in_specs + cast_specs,
        out_specs=out_specs + cast_specs,
        out_shape=out_shape + cast_shapes,
        scratch_shapes=scratch,
        compiler_params=_params(),
        name="bidir_ssd",
    )(*args, *ride_along)
    return outs[:n_main_out], outs[n_main_out:]


def _tail_kernel(*refs, n_in, n_ctx_tiles):
    ctx_a, lat_a = refs[:n_in], refs[n_in:2 * n_in]
    (xc_ref, xl_ref, mod_ref, wout_ref, wup_ref, wdown_ref, g1_ref, b1_ref, g2_ref, b2_ref,
     oc_ref, ol_ref) = refs[2 * n_in:]
    step = pl.program_id(0)

    tile = functools.partial(_mlp_tile, mod_ref=mod_ref, wout_ref=wout_ref, g1_ref=g1_ref, b1_ref=b1_ref,
                             wup_ref=wup_ref, wdown_ref=wdown_ref, g2_ref=g2_ref, b2_ref=b2_ref)

    @pl.when(step < n_ctx_tiles)
    def _():
        tile(ctx_a, xc_ref, oc_ref)

    @pl.when(step >= n_ctx_tiles)
    def _():
        tile(lat_a, xl_ref, ol_ref)


def _mlp_tile(a_refs, x_ref, o_ref, *, mod_ref, wout_ref, g1_ref, b1_ref, wup_ref, wdown_ref, g2_ref, b2_ref):
    m = mod_ref[0]
    n_chunks = D_FF // FF_CHUNK

    def out_proj(rows):
        d = None
        off = 0
        for a_ref in a_refs:
            n = a_ref.shape[1]
            part = _dot(a_ref[rows, :].astype(BF16), wout_ref[off:off + n, :])
            d = part if d is None else d + part
            off += n
        return d

    def head(rows, d):
        x1 = _layer_norm(ALPHA * x_ref[rows, :] + m[2:3] * d, g1_ref[...], b1_ref[...])
        return x1, (x1 * (1.0 + m[4:5]) + m[3:4]).astype(BF16)

    def up(h, c):
        return _dot(h, wup_ref[:, c * FF_CHUNK:(c + 1) * FF_CHUNK])

    def down(u, acc, c):
        u = jnp.maximum(u, 0.0)
        part = _dot((u * u).astype(BF16), wdown_ref[c * FF_CHUNK:(c + 1) * FF_CHUNK, :])
        return part if acc is None else acc + part

    def finish(rows, x1, acc):
        o_ref[rows, :] = _layer_norm(ALPHA * x1 + m[5:6] * acc, g2_ref[...], b2_ref[...])

    assert sum(TAIL_SPLIT) == x_ref.shape[0]
    k = len(TAIL_SPLIT)
    starts = [sum(TAIL_SPLIT[:t]) for t in range(k)]
    rows = [slice(s, s + n) for s, n in zip(starts, TAIL_SPLIT)]
    projected = [out_proj(r) for r in rows]
    heads = [None] * k
    heads[0] = head(rows[0], projected[0])
    u_next = up(heads[0][1], 0)
    done = None
    for t in range(k):
        if t + 1 < k:
            heads[t + 1] = head(rows[t + 1], projected[t + 1])
        acc = None
        for c in range(n_chunks):
            if c + 1 < n_chunks:
                nxt = up(heads[t][1], c + 1)
            else:
                nxt = up(heads[t + 1][1], 0) if t + 1 < k else None
            u_cur, u_next = u_next, nxt
            acc = down(u_cur, acc, c)
            if c == 0 and done is not None:
                finish(*done)
        done = (rows[t], heads[t][0], acc)
    finish(*done)


def _tail(a_ctx, a_lat, x_ctx, x_lat, mod, lat_seq_len, w_out, ln1_g, ln1_b, w_up, w_down, ln2_g, ln2_b):
    tm = TOKEN_TILE
    n_ctx = x_ctx.shape[0] // tm
    n_lat = x_lat.shape[0] // tm
    tiles_per_lat_seq = lat_seq_len // tm
    ctx_row = lambda n: pl.BlockSpec((tm, n), lambda i: (jnp.minimum(i, n_ctx - 1), 0))
    lat_row = lambda n: pl.BlockSpec((tm, n), lambda i: (jnp.maximum(i - n_ctx, 0), 0), pipeline_mode=pl.Buffered(1))
    lat_out = pl.BlockSpec((tm, D_MODEL), lambda i: (jnp.maximum(i - n_ctx, 0), 0))
    mod_spec = _mod_spec(lambda i: jnp.where(i < n_ctx, 0, 1 + (i - n_ctx) // tiles_per_lat_seq))
    vec = lambda v: v.reshape(1, D_MODEL)
    consts = [w_out, w_up, w_down, vec(ln1_g), vec(ln1_b), vec(ln2_g), vec(ln2_b)]
    return pl.pallas_call(
        functools.partial(_tail_kernel, n_in=len(a_ctx), n_ctx_tiles=n_ctx),
        grid=(n_ctx + n_lat,),
        in_specs=[ctx_row(a.shape[1]) for a in a_ctx] + [lat_row(a.shape[1]) for a in a_lat]
                 + [ctx_row(D_MODEL), lat_row(D_MODEL), mod_spec] + [_const_spec(v.shape) for v in consts],
        out_specs=[ctx_row(D_MODEL), lat_out],
        out_shape=[jax.ShapeDtypeStruct(x_ctx.shape, F32), jax.ShapeDtypeStruct(x_lat.shape, F32)],
        compiler_params=_params(),
        name="outproj_mlp",
    )(*a_ctx, *a_lat, x_ctx, x_lat, mod, *consts)


def _pool_kernel(x_ref, mod_ref, win_ref, pw_ref, ps_ref, o_ref, *, seq_len):
    L = seq_len
    m = mod_ref[0]
    t_i = lax.broadcasted_iota(jnp.int32, (L, L), 0)
    s_i = lax.broadcasted_iota(jnp.int32, (L, L), 1)
    t_g = lax.broadcasted_iota(jnp.int32, (L, POOL_GROUP_DIM), 0)
    bands, scales = [], []
    for w in POOL_WINDOWS:
        lo, hi = w // 2, w - w // 2
        bands.append(jnp.where((s_i >= t_i - lo) & (s_i < t_i + hi), 1.0 / w, 0.0).astype(BF16))
        scales.append(w / (jnp.minimum(t_g + hi, L) - jnp.maximum(t_g - lo, 0)).astype(F32))
    n_seq = x_ref.shape[0] // L
    seq_rows = [slice(s * L, (s + 1) * L) for s in range(n_seq)]
    hs = [(x_ref[r, :] * (1.0 + m[1:2]) + m[0:1]).astype(BF16) for r in seq_rows]
    us = [_dot(h, win_ref[...]) for h in hs]
    n_groups = len(POOL_WINDOWS)
    cols = [slice(g * POOL_GROUP_DIM, (g + 1) * POOL_GROUP_DIM) for g in range(n_groups)]

    def window_means(g):
        out = []
        for s in range(n_seq):
            ug = us[s][:, cols[g]]
            ug_hi = ug.astype(BF16)
            ug_lo = (ug - ug_hi.astype(F32)).astype(BF16)
            out.append((ug, _dot(bands[g], ug_hi) + _dot(bands[g], ug_lo)))
        return out

    def mix(g, means):
        return [_dot((wm * scales[g] - ug).astype(BF16), pw_ref[g]) * ps_ref[:, cols[g]] for ug, wm in means]

    mixed = []
    pending = window_means(0)
    for g in range(n_groups):
        nxt = window_means(g + 1) if g + 1 < n_groups else None
        mixed.append(mix(g, pending))
        pending = nxt
    for s in range(n_seq):
        o_ref[seq_rows[s], :] = jnp.concatenate([mixed[g][s] for g in range(n_groups)], axis=1)


def _pool_mixer(x, mod, row_of_tile, w_in, pool_w, pool_scale, seq_len, tm):
    t = x.shape[0]
    row = pl.BlockSpec((tm, D_MODEL), lambda i: (i, 0))
    consts = [w_in, pool_w, pool_scale.reshape(1, D_MODEL)]
    return pl.pallas_call(
        functools.partial(_pool_kernel, seq_len=seq_len),
        grid=(t // tm,),
        in_specs=[row, _mod_spec(row_of_tile)] + [_const_spec(cst.shape) for cst in consts],
        out_specs=row,
        out_shape=jax.ShapeDtypeStruct((t, D_MODEL), F32),
        compiler_params=_params(),
        name="pool_mixer",
    )(x, mod, *consts)


def _rope_tables(rows):
    r, col = np.meshgrid(np.arange(rows), np.arange(GRID_W), indexing="ij")
    r = r.reshape(-1).astype(np.float64)
    col = col.reshape(-1).astype(np.float64)
    n_freq = HEAD_DIM_A // 4
    inv = ROPE_BASE ** (-np.arange(n_freq, dtype=np.float64) / n_freq)
    ang = np.concatenate([r[:, None] * inv, col[:, None] * inv], -1)
    reps = 512 // HEAD_DIM_A
    cosf = np.tile(np.concatenate([np.cos(ang), np.cos(ang)], -1), (1, reps)).astype(np.float32)
    sinf = np.tile(np.concatenate([-np.sin(ang), np.sin(ang)], -1), (1, reps)).astype(np.float32)
    return jnp.asarray(cosf), jnp.asarray(sinf)


def kernel(x_prompt, x_sample, cache_k_l0, cache_v_l0, state_ssm_l0, c, c_ctx, w_mod_l0, b_mod_l0, w_in_l0, conv_w_l0, conv_b_l0, a_log_l0, dt_bias_l0, d_skip_l0, ssm_norm_w_l0, lam_q1_l0, lam_k1_l0, lam_q2_l0, lam_k2_l0, subln_w_l0, w_out_l0, ln1_g_l0, ln1_b_l0, w_up_l0, w_down_l0, ln2_g_l0, ln2_b_l0, w_mod_l1, b_mod_l1, w_in_l1, pool_w_l1, pool_scale_l1, w_out_l1, ln1_g_l1, ln1_b_l1, w_up_l1, w_down_l1, ln2_g_l1, ln2_b_l1):
    n_ctx, l_ctx, _ = x_prompt.shape
    n_lat, l_lat, _ = x_sample.shape
    past = cache_k_l0.shape[1]
    xc = x_prompt.reshape(n_ctx * l_ctx, D_MODEL)
    xl = x_sample.reshape(n_lat * l_lat, D_MODEL)

    cond8 = jnp.concatenate([c_ctx[None, :], c, jnp.zeros((8 - 1 - n_lat, D_MODEL), F32)], axis=0)
    ctx_row = lambda i: 0

    mod0 = _modulation(cond8, w_mod_l0, b_mod_l0)
    n_main = 4 * A_QK + A_V + D_INNER_B + CONV_DIM_B
    w_main = w_in_l0.astype(BF16)
    w_dt = jnp.pad(w_in_l0[:, n_main:], ((0, 0), (0, LANES - 2 * N_HEADS_B))).astype(BF16)
    lane_pad = lambda v: jnp.pad(v.reshape(1, -1), ((0, 0), (0, LANES - 2 * N_HEADS_B)))
    conv_consts = (conv_w_l0, conv_b_l0.reshape(1, CONV_DIM_B), lane_pad(dt_bias_l0))
    ssd_consts = [lane_pad(a_log_l0),
                  jnp.repeat(d_skip_l0, HEAD_DIM_B).reshape(1, D_INNER_B),
                  ssm_norm_w_l0.reshape(1, D_INNER_B)]
    lam_vecs = jnp.stack([lam_q1_l0, lam_k1_l0, lam_q2_l0, lam_k2_l0], axis=0)
    lam_init = 0.8 - 0.6 * math.exp(-0.3 * 0)

    (qc, kc, vc, zc, xbc_c, dt_c), (w_out0, w_up0, w_down0) = _inproj0(
        xc, mod0, ctx_row, w_main, w_dt, *conv_consts, l_ctx, TOKEN_TILE, ride_along=(w_out_l0, w_up_l0, w_down_l0))
    oa_c = _attention(qc, [kc], [vc], lam_vecs, subln_w_l0, l_ctx, l_ctx, CTX_SEQS_PER_STEP, lam_init)
    (y_c, new_ssm), (w_out1, w_up1, w_down1, w_in1, pool_w) = _ssd(
        xbc_c, dt_c, zc, None, ssd_consts, n_ctx, l_ctx, 1,
        ride_along=(w_out_l1, w_up_l1, w_down_l1, w_in_l1, pool_w_l1.reshape(-1, POOL_GROUP_DIM)))
    pool_w = pool_w.reshape(pool_w_l1.shape)

    (ql, kl, vl, zl, xbc_l, dt_l), _ = _inproj0(xl, mod0, lambda i: 1 + i, w_main, w_dt, *conv_consts, l_lat, l_lat,
                                                rope_tables=_rope_tables(l_lat // GRID_W))
    ck = jnp.transpose(cache_k_l0, (0, 2, 3, 4, 1))
    cv = cache_v_l0.reshape(n_lat * past * N_HEADS_A, 2 * HEAD_DIM_A)
    oa_l = _attention(ql, [ck, kl], [cv, vl], lam_vecs, subln_w_l0, l_lat, ATTN_Q_TILE, 1, lam_init)
    h0 = state_ssm_l0.reshape(n_lat, 2, D_INNER_B, D_STATE)
    ((y_l,), _) = _ssd(xbc_l, dt_l, zl, h0, ssd_consts, n_lat, l_lat, 1)
    tail0 = (w_out0, ln1_g_l0, ln1_b_l0, w_up0, w_down0, ln2_g_l0, ln2_b_l0)
    xc, xl = _tail([oa_c, y_c], [oa_l, y_l], xc, xl, mod0, l_lat, *tail0)

    mod1 = _modulation(cond8, w_mod_l1, b_mod_l1)
    tail1 = (w_out1, ln1_g_l1, ln1_b_l1, w_up1, w_down1, ln2_g_l1, ln2_b_l1)
    mix_c = _pool_mixer(xc, mod1, ctx_row, w_in1, pool_w, pool_scale_l1, l_ctx, TOKEN_TILE)
    mix_l = _pool_mixer(xl, mod1, lambda i: 1 + i, w_in1, pool_w, pool_scale_l1, l_lat, l_lat)
    xc, xl = _tail([mix_c], [mix_l], xc, xl, mod1, l_lat, *tail1)

    return (xc.reshape(n_ctx, l_ctx, D_MODEL), xl.reshape(n_lat, l_lat, D_MODEL),
            jnp.transpose(kc, (0, 4, 1, 2, 3)),
            vc.reshape(n_ctx, l_ctx, N_HEADS_A, 2 * HEAD_DIM_A),
            new_ssm.reshape(n_ctx, 2, N_HEADS_B, HEAD_DIM_B, D_STATE))
```

```python
import functools
import math

import jax
import jax.numpy as jnp
import numpy as np
from jax import lax
from jax.experimental import pallas as pl
from jax.experimental.pallas import tpu as pltpu

F32 = jnp.float32
BF16 = jnp.bfloat16

D_MODEL = 1024
N_MOD = 6
N_HEADS_A = 4
HEAD_DIM_A = 64
A_QK = N_HEADS_A * HEAD_DIM_A
A_V = 2 * A_QK
D_INNER_B = 512
HEAD_DIM_B = 64
N_HEADS_B = 8
D_STATE = 128
D_CONV = 5
CONV_DIM_B = 1024
CHUNK = 128
GRID_W = 64
POOL_WINDOWS = (2, 4, 8, 16)
POOL_GROUP_DIM = 256
D_FF = 4096
ROPE_BASE = 10000.0
LN_EPS = 1e-5
DEPTH = 2
ALPHA = (2 * DEPTH) ** 0.25
NEG_BIG = -1e30
LOG2E = 1.4426950408889634

LANES = 128
TOKEN_TILE = 512
ATTN_Q_TILE = 256
CTX_SEQS_PER_STEP = 4
FF_CHUNK = 1024
CONV_COL_BLOCK = 256
TAIL_SPLIT = (256, 256)
VMEM_LIMIT = 56 * 1024 * 1024


def _dot(a, b):
    return jnp.dot(a, b, preferred_element_type=F32)


def _dot_nt(a, b):
    return lax.dot_general(a, b, (((1,), (1,)), ((), ())), preferred_element_type=F32)


def _bf16_pieces(a):
    a1 = a.astype(BF16)
    r1 = a - a1.astype(F32)
    a2 = r1.astype(BF16)
    a3 = (r1 - a2.astype(F32)).astype(BF16)
    return a1, a2, a3


def _dot_exact_rhs(t01, a):
    a1, a2, a3 = _bf16_pieces(a)
    return _dot(t01, a1) + _dot(t01, a2) + _dot(t01, a3)


def _sigmoid(x):
    return 1.0 / (1.0 + jnp.exp(-x))


def _silu(x):
    return x * _sigmoid(x)


def _layer_norm(x, g, b):
    mu = jnp.mean(x, axis=-1, keepdims=True)
    xc = x - mu
    var = jnp.mean(xc * xc, axis=-1, keepdims=True)
    return xc * lax.rsqrt(var + LN_EPS) * g + b


def _const_spec(shape):
    nd = len(shape)
    return pl.BlockSpec(shape, lambda *_: (0,) * nd, pipeline_mode=pl.Buffered(1))


def _params(n_axes=1):
    return pltpu.CompilerParams(dimension_semantics=("arbitrary",) * n_axes,
                                vmem_limit_bytes=VMEM_LIMIT)


def _mod_kernel(condt_ref, w_ref, b_ref, o_ref, *, n_rows):
    st = _silu(condt_ref[...])
    cols = [jnp.broadcast_to(st[:, r:r + 1], (D_MODEL, LANES)) for r in range(n_rows)]
    row_i = lax.broadcasted_iota(jnp.int32, (8, LANES), 0)
    for blk in range(w_ref.shape[1] // LANES):
        lanes = slice(blk * LANES, (blk + 1) * LANES)
        w = w_ref[:, lanes]
        out = jnp.zeros((8, LANES), F32)
        for r in range(n_rows):
            acc = jnp.sum(w * cols[r], axis=0, keepdims=True) + b_ref[:, lanes]
            out = jnp.where(row_i == r, acc, out)
        o_ref[:, lanes] = out


def _modulation(cond8, n_rows, w_mod, b_mod):
    tn = 1024
    n = w_mod.shape[1]
    out = pl.pallas_call(
        functools.partial(_mod_kernel, n_rows=n_rows),
        grid=(n // tn,),
        in_specs=[pl.BlockSpec((D_MODEL, 8), lambda j: (0, 0)),
                  pl.BlockSpec((D_MODEL, tn), lambda j: (0, j)),
                  pl.BlockSpec((1, tn), lambda j: (0, j))],
        out_specs=pl.BlockSpec((8, tn), lambda j: (0, j)),
        out_shape=jax.ShapeDtypeStruct((8, n), F32),
        compiler_params=_params(),
        name="modulation",
    )(cond8.T, w_mod, b_mod.reshape(1, n))
    return out.reshape(8, N_MOD, D_MODEL)


def _mod_spec(row_of_tile):
    return pl.BlockSpec((1, N_MOD, D_MODEL), lambda i: (row_of_tile(i), 0, 0))


def _rope(t, cosf, sinf):
    lane = lax.broadcasted_iota(jnp.int32, t.shape, 1)
    first = (lane & (HEAD_DIM_A - 1)) < HEAD_DIM_A // 2
    width = t.shape[1]
    swapped = jnp.where(first, pltpu.roll(t, width - HEAD_DIM_A // 2, 1),
                        pltpu.roll(t, HEAD_DIM_A // 2, 1))
    return t * cosf + swapped * sinf


def _conv_silu(xbc, cw_ref, cb_ref, cols, seq_len):
    assert xbc.shape[0] == seq_len
    half = D_CONV // 2
    edge = 8
    offsets = [j - half for j in range(D_CONV) if j != half]
    shifted = {o: pltpu.roll(xbc, (-o) % seq_len, 0) for o in offsets}

    def taps(lo, hi, masked):
        acc = cb_ref[:, cols] + cw_ref[half:half + 1, cols] * xbc[lo:hi]
        pos = lo + lax.broadcasted_iota(jnp.int32, (hi - lo, xbc.shape[1]), 0)
        for o in offsets:
            s = shifted[o][lo:hi]
            if masked:
                s = jnp.where((pos >= -o) if o < 0 else (pos < seq_len - o), s, 0.0)
            acc = acc + cw_ref[o + half:o + half + 1, cols] * s
        return acc

    acc = jnp.concatenate([taps(0, edge, True), taps(edge, seq_len - edge, False),
                           taps(seq_len - edge, seq_len, True)], axis=0)
    return _silu(acc)


def _softplus(t):
    return jnp.maximum(t, 0.0) + jnp.log(1.0 + jnp.exp(-jnp.abs(t)))


def _ride_along_specs(mats, n_steps):
    specs, shapes = [], []
    for w in mats:
        assert w.shape[0] % n_steps == 0
        specs.append(pl.BlockSpec((w.shape[0] // n_steps, w.shape[1]), lambda i: (i, 0)))
        shapes.append(jax.ShapeDtypeStruct(w.shape, BF16))
    return specs, shapes


def _ride_along_cast(src_refs, dst_refs):
    for src, dst in zip(src_refs, dst_refs):
        dst[...] = src[...].astype(BF16)


def _inproj0_kernel(*refs, rope, seq_len, n_cast):
    n_in = 9 if rope else 7
    ins, cast_in = refs[:n_in], refs[n_in:n_in + n_cast]
    (q_ref, k_ref, v_ref, z_ref, xbc_ref, dt_ref) = refs[n_in + n_cast:n_in + n_cast + 6]
    cast_out = refs[n_in + n_cast + 6:]
    if rope:
        x_ref, mod_ref, w_ref, wdt_ref, cw_ref, cb_ref, dtb_ref, cos_ref, sin_ref = ins
    else:
        x_ref, mod_ref, w_ref, wdt_ref, cw_ref, cb_ref, dtb_ref = ins
    _ride_along_cast(cast_in, cast_out)
    m = mod_ref[0]
    tasks = []
    for b in range(x_ref.shape[0] // seq_len):
        rows = slice(b * seq_len, (b + 1) * seq_len)
        h = (x_ref[rows, :] * (1.0 + m[1:2]) + m[0:1]).astype(BF16)

        for c0 in range(0, CONV_DIM_B, CONV_COL_BLOCK):
            cols = slice(c0, c0 + CONV_COL_BLOCK)

            def conv_mm(h=h, c0=c0):
                return _dot(h, w_ref[:, 2048 + c0:2048 + c0 + CONV_COL_BLOCK])

            def conv_vec(raw, rows=rows, cols=cols):
                xbc_ref[rows, cols] = _conv_silu(raw, cw_ref, cb_ref, cols, seq_len)

            tasks.append((conv_mm, conv_vec))

        def qk_mm(h=h):
            return _dot(h, wdt_ref[...]), _dot(h, w_ref[:, 0:512]), _dot(h, w_ref[:, 512:1024])

        def qk_vec(res, rows=rows, b=b):
            dt_raw, q, k = res
            dt_ref[rows, :] = _softplus(dt_raw + dtb_ref[...])
            if rope:
                q = _rope(q, cos_ref[...], sin_ref[...])
                k = _rope(k, cos_ref[...], sin_ref[...])
            q_ref[rows, :] = q
            for blk in range(2 * A_QK // LANES):
                t = k[:, blk * LANES:(blk + 1) * LANES].T
                which, head = blk // 2, (blk % 2) * 2
                k_ref[b, which, head] = t[0:HEAD_DIM_A]
                k_ref[b, which, head + 1] = t[HEAD_DIM_A:2 * HEAD_DIM_A]

        def vz_mm(h=h):
            return _dot(h, w_ref[:, 1024:1536]), _dot(h, w_ref[:, 1536:2048])

        def vz_vec(res, rows=rows, b=b):
            vals, z = res
            for hh in range(N_HEADS_A):
                v_ref[pl.ds(b * seq_len * N_HEADS_A + hh, seq_len, stride=N_HEADS_A), :] = (
                    vals[:, hh * LANES:(hh + 1) * LANES])
            z_ref[rows, :] = z

        tasks += [(qk_mm, qk_vec), (vz_mm, vz_vec)]

    pending = None
    for mm, vec in tasks:
        res = mm()
        if pending is not None:
            pending[0](pending[1])
        pending = (vec, res)
    pending[0](pending[1])


def _inproj0(x, mod, row_of_tile, w_main, w_dt, conv_w, conv_b, dt_bias, seq_len, tm, rope_tables=None,
             ride_along=()):
    t = x.shape[0]
    assert tm % seq_len == 0
    row = lambda n: pl.BlockSpec((tm, n), lambda i: (i, 0))
    flat = lambda n: jax.ShapeDtypeStruct((t, n), F32)
    consts = [w_main, w_dt, conv_w, conv_b, dt_bias]
    in_specs = [row(D_MODEL), _mod_spec(row_of_tile)] + [_const_spec(cst.shape) for cst in consts]
    args = [x, mod] + consts
    if rope_tables is not None:
        assert tm == seq_len
        in_specs += [_const_spec(tab.shape) for tab in rope_tables]
        args += list(rope_tables)
    kt_dims = (2, N_HEADS_A, HEAD_DIM_A, seq_len)
    k_spec = pl.BlockSpec((tm // seq_len,) + kt_dims, lambda i: (i, 0, 0, 0, 0))
    k_shape = jax.ShapeDtypeStruct((t // seq_len,) + kt_dims, F32)
    cast_specs, cast_shapes = _ride_along_specs(ride_along, t // tm)
    outs = pl.pallas_call(
        functools.partial(_inproj0_kernel, rope=rope_tables is not None, seq_len=seq_len, n_cast=len(ride_along)),
        grid=(t // tm,),
        in_specs=in_specs + cast_specs,
        out_specs=[row(512), k_spec, pl.BlockSpec((tm * N_HEADS_A, LANES), lambda i: (i, 0)),
                   row(512), row(CONV_DIM_B), row(LANES)] + cast_specs,
        out_shape=[flat(512), k_shape, jax.ShapeDtypeStruct((t * N_HEADS_A, LANES), F32),
                   flat(512), flat(CONV_DIM_B), flat(LANES)] + cast_shapes,
        compiler_params=_params(),
        name="inproj0",
    )(*args, *ride_along)
    return outs[:6], outs[6:]


def _lambda_full(lam_ref, lam_init):
    lv = lam_ref[...]
    return (jnp.exp(jnp.sum(lv[0:1] * lv[1:2], axis=1, keepdims=True))
            - jnp.exp(jnp.sum(lv[2:3] * lv[3:4], axis=1, keepdims=True)) + lam_init)


def _attn_kernel(*refs, n_seg, tq, lam_init):
    q_ref = refs[0]
    kt_refs = refs[1:1 + n_seg]
    v_refs = refs[1 + n_seg:1 + 2 * n_seg]
    lam_ref, sub_ref, o_ref = refs[1 + 2 * n_seg:]
    lam = _lambda_full(lam_ref, lam_init)
    lane = lax.broadcasted_iota(jnp.int32, (1, A_QK), 1)
    masks = [(lane >= h * HEAD_DIM_A) & (lane < (h + 1) * HEAD_DIM_A) for h in range(N_HEADS_A)]
    keys = [r.shape[-1] for r in kt_refs]

    def stacked(qx):
        return jnp.concatenate([jnp.where(m, qx, 0.0) for m in masks], axis=0).astype(BF16)

    def exp_scores(qx, b, which):
        qs = stacked(qx)
        s = [_dot(qs, r[b, which].reshape(A_QK, n).astype(BF16)) for r, n in zip(kt_refs, keys)]
        mx = functools.reduce(jnp.maximum, [jnp.max(x, axis=1, keepdims=True) for x in s])
        return [jnp.exp2(x - mx).astype(BF16) for x in s]

    outs = []
    for b in range(kt_refs[0].shape[0]):
        q = q_ref[b * tq:(b + 1) * tq, :] * (HEAD_DIM_A ** -0.5 * LOG2E)
        e1 = exp_scores(q[:, 0:A_QK], b, 0)
        e2 = exp_scores(q[:, A_QK:2 * A_QK], b, 1)
        heads = []
        for h in range(N_HEADS_A):
            hrows = slice(h * tq, (h + 1) * tq)
            n1 = n2 = None
            for s, (v_ref, n) in enumerate(zip(v_refs, keys)):
                v_h = v_ref[pl.ds(b * n * N_HEADS_A + h, n, stride=N_HEADS_A), :]
                v_ext = jnp.concatenate([v_h.astype(BF16),
                                         jnp.ones((n, LANES), BF16)], axis=1)
                p1 = _dot(e1[s][hrows], v_ext)
                p2 = _dot(e2[s][hrows], v_ext)
                n1 = p1 if n1 is None else n1 + p1
                n2 = p2 if n2 is None else n2 + p2
            heads.append(n1[:, 0:LANES] / n1[:, LANES:] - lam * (n2[:, 0:LANES] / n2[:, LANES:]))
        o = jnp.concatenate(heads, axis=0)
        ms = jnp.mean(o * o, axis=1, keepdims=True)
        o = (o * lax.rsqrt(ms + 1e-5) * sub_ref[...]) * (1.0 - lam_init)
        outs.append(jnp.concatenate([o[h * tq:(h + 1) * tq] for h in range(N_HEADS_A)], axis=1))
    o_ref[...] = jnp.concatenate(outs, axis=0)


def _attention(q, kts, vs, lam_vecs, subln_w, q_len, tq, seqs_per_step, lam_init):
    n_seq = kts[0].shape[0]
    nb = seqs_per_step
    tiles = q_len // tq
    assert tiles == 1 or nb == 1
    qspec = pl.BlockSpec((nb * tq, 512), lambda b, i: (b * tiles + i, 0))
    ktspecs = [pl.BlockSpec((nb,) + kt.shape[1:], lambda b, i: (b, 0, 0, 0, 0)) for kt in kts]
    vspecs = [pl.BlockSpec((nb * kt.shape[-1] * N_HEADS_A, LANES), lambda b, i: (b, 0)) for kt in kts]
    return pl.pallas_call(
        functools.partial(_attn_kernel, n_seg=len(kts), tq=tq, lam_init=lam_init),
        grid=(n_seq // nb, tiles),
        in_specs=[qspec] + ktspecs + vspecs + [pl.BlockSpec((4, HEAD_DIM_A), lambda b, i: (0, 0)),
                                             pl.BlockSpec((1, LANES), lambda b, i: (0, 0))],
        out_specs=qspec,
        out_shape=jax.ShapeDtypeStruct(q.shape, F32),
        compiler_params=_params(2),
        name="diff_attention",
    )(q, *kts, *vs, lam_vecs, subln_w.reshape(1, LANES))


def _chunk_loop(n, body, unroll=1):
    if n <= 2:
        for c in range(n):
            body(c)
    else:
        def step(c, carry):
            body(c)
            return carry
        lax.fori_loop(0, n, step, 0, unroll=unroll)


def _ssd_kernel(*refs, seq_len, n_seq, has_h0, n_cast):
    n_in = 7 if has_h0 else 6
    n_out = 1 if has_h0 else 2
    ins, cast_in = refs[:n_in], refs[n_in:n_in + n_cast]
    outs = refs[n_in + n_cast:n_in + n_cast + n_out]
    cast_out = refs[n_in + n_cast + n_out:n_in + 2 * n_cast + n_out]
    ysc, s_sc, cd_sc, ce_sc, htf, htb = refs[n_in + 2 * n_cast + n_out:]
    if has_h0:
        xact, dtv, z_ref, h0_ref, arow_ref, dsk_ref, nw_ref = ins
        (y_ref,), hfin_ref = outs, None
    else:
        xact, dtv, z_ref, arow_ref, dsk_ref, nw_ref = ins
        (y_ref, hfin_ref), h0_ref = outs, None
    _ride_along_cast(cast_in, cast_out)
    nc = seq_len // CHUNK
    n_pairs = N_HEADS_B // 2
    a_row = -jnp.exp(arow_ref[...]) * LOG2E

    row_i = lax.broadcasted_iota(jnp.int32, (CHUNK, CHUNK), 0)
    col_i = lax.broadcasted_iota(jnp.int32, (CHUNK, CHUNK), 1)
    lower = col_i <= row_i
    upper = col_i >= row_i
    tri = (jnp.where(lower, 1.0, 0.0).astype(BF16), jnp.where(upper, 1.0, 0.0).astype(BF16))
    lo_half = col_i < HEAD_DIM_B

    def split_pair(m):
        zero = jnp.zeros_like(m)
        return jnp.concatenate([jnp.where(lo_half, m, zero), jnp.where(lo_half, zero, m)], axis=0).astype(BF16)

    def decay_terms(c):
        rows = pl.ds(pl.multiple_of(c * CHUNK, CHUNK), CHUNK)
        dtc = dtv[rows, :]
        dtt = dtc.T[0:16, :]
        per_dir = []
        for d in range(2):
            col = _dot_exact_rhs(tri[d], dtc * a_row)
            rowm = col.T[0:16, :]
            far = CHUNK - 1 if d == 0 else 0
            tot = jnp.broadcast_to(rowm[:, far:far + 1], (16, CHUNK))
            dte = jnp.exp2(tot - rowm) * dtt
            per_dir.append((col, rowm - jnp.log2(dtt), dte, jnp.exp2(tot)))
        return per_dir

    def local_phase(c, per_dir=None):
        rows = pl.ds(pl.multiple_of(c * CHUNK, CHUNK), CHUNK)
        if per_dir is None:
            per_dir = decay_terms(c)
        for g in range(2):
            bg = xact[rows, 512 + g * LANES:512 + (g + 1) * LANES]
            cg = xact[rows, 768 + g * LANES:768 + (g + 1) * LANES]
            cbm = _dot_nt(cg.astype(BF16), bg.astype(BF16))
            bgt = bg.T
            for pq in range(2):
                pi = g * 2 + pq
                lanes = slice(pi * LANES, (pi + 1) * LANES)
                xrhs = split_pair(xact[rows, lanes])
                y_pair = None
                for d in range(2):
                    col, row_dt, dte, cdec = per_dir[d]
                    causal = lower if d == 0 else upper
                    m_parts, ce_parts, bw_parts, cd_parts = [], [], [], []
                    for hh in (2 * pi, 2 * pi + 1):
                        hd = 8 * d + hh
                        colb = jnp.sum(jnp.where(col_i == hd, col, 0.0), axis=1, keepdims=True)
                        m_parts.append(cbm * jnp.exp2(jnp.where(causal, colb - row_dt[hd:hd + 1, :], NEG_BIG)))
                        ce_parts.append(cg * jnp.exp2(colb))
                        bw_parts.append(bgt * dte[hd:hd + 1, :])
                        cd_parts.append(cdec[hd:hd + 1, :])
                    yd = _dot(jnp.concatenate(m_parts, axis=1).astype(BF16), xrhs)
                    y_pair = yd if y_pair is None else y_pair + yd
                    s_sc[c, d, :, lanes] = _dot(jnp.concatenate(bw_parts, axis=1).astype(BF16), xrhs)
                    ce_sc[c, d, :, pi * 2 * LANES:(pi + 1) * 2 * LANES] = (
                        jnp.concatenate(ce_parts, axis=1).astype(BF16))
                    cd_sc[c, d, :, lanes] = jnp.broadcast_to(
                        jnp.where(lo_half[0:1, :], cd_parts[0], cd_parts[1]), (8, LANES))
                ysc[rows, lanes] = y_pair

    if n_seq * nc <= 2:
        terms = [decay_terms(c) for c in range(n_seq * nc)]
        for c in range(n_seq * nc):
            local_phase(c, terms[c])
    else:
        _chunk_loop(n_seq * nc, local_phase, unroll=2)

    def scan_sequence(sq):
        for d, ht in ((0, htf), (1, htb)):
            for qd in range(n_pairs):
                lanes = slice(qd * LANES, (qd + 1) * LANES)
                if has_h0:
                    ht[:, lanes] = h0_ref[sq, d, lanes, :].T
                else:
                    ht[:, lanes] = jnp.zeros((D_STATE, LANES), F32)

        def scan_phase(j):
            for d, ht in ((0, htf), (1, htb)):
                c = sq * nc + (j if d == 0 else nc - 1 - j)
                rows = pl.ds(pl.multiple_of(c * CHUNK, CHUNK), CHUNK)
                for pi in range(n_pairs):
                    lanes = slice(pi * LANES, (pi + 1) * LANES)
                    hprev = ht[:, lanes]
                    ysc[rows, lanes] += _dot(ce_sc[c, d, :, pi * 2 * LANES:(pi + 1) * 2 * LANES],
                                             split_pair(hprev))
                    ht[:, lanes] = hprev * cd_sc[c, d, 0:1, lanes] + s_sc[c, d, :, lanes]

        _chunk_loop(nc, scan_phase)
        if hfin_ref is not None:
            for d, ht in ((0, htf), (1, htb)):
                for qd in range(n_pairs):
                    lanes = slice(qd * LANES, (qd + 1) * LANES)
                    hfin_ref[sq, d, lanes, :] = ht[:, lanes].T

    if n_seq == 1:
        scan_sequence(0)
    else:
        def seq_step(sq, carry):
            scan_sequence(sq)
            return carry
        lax.fori_loop(0, n_seq, seq_step, 0)

    def finish(c):
        rows = pl.ds(pl.multiple_of(c * CHUNK, CHUNK), CHUNK)
        y = ysc[rows, :] + dsk_ref[...] * xact[rows, 0:D_INNER_B]
        y = y * _silu(z_ref[rows, :])
        ms = jnp.mean(y * y, axis=1, keepdims=True)
        y_ref[rows, :] = y * lax.rsqrt(ms + 1e-5) * nw_ref[...]

    _chunk_loop(n_seq * nc, finish)


def _ssd(xbc, dt, z, h0, consts, n_batch, seq_len, seqs_per_step, ride_along=()):
    nb = seqs_per_step
    L = nb * seq_len
    nc = L // CHUNK
    row = lambda n: pl.BlockSpec((L, n), lambda b: (b, 0))
    state_spec = pl.BlockSpec((nb, 2, D_INNER_B, D_STATE), lambda b: (b, 0, 0, 0))
    has_h0 = h0 is not None
    in_specs = [row(CONV_DIM_B), row(LANES), row(D_INNER_B)]
    args = [xbc, dt, z]
    if has_h0:
        in_specs.append(state_spec)
        args.append(h0)
    for cst in consts:
        in_specs.append(pl.BlockSpec(cst.shape, lambda b: (0, 0)))
        args.append(cst)
    out_specs = [row(D_INNER_B)]
    out_shape = [jax.ShapeDtypeStruct((n_batch * seq_len, D_INNER_B), F32)]
    if not has_h0:
        out_specs.append(state_spec)
        out_shape.append(jax.ShapeDtypeStruct((n_batch, 2, D_INNER_B, D_STATE), F32))
    scratch = [pltpu.VMEM((L, D_INNER_B), F32),
               pltpu.VMEM((nc, 2, D_STATE, D_INNER_B), F32),
               pltpu.VMEM((nc, 2, 8, D_INNER_B), F32),
               pltpu.VMEM((nc, 2, CHUNK, 2 * D_INNER_B), BF16),
               pltpu.VMEM((D_STATE, D_INNER_B), F32),
               pltpu.VMEM((D_STATE, D_INNER_B), F32)]
    cast_specs, cast_shapes = _ride_along_specs(ride_along, n_batch // nb)
    n_main_out = len(out_specs)
    outs = pl.pallas_call(
        functools.partial(_ssd_kernel, seq_len=seq_len, n_seq=nb, has_h0=has_h0, n_cast=len(ride_along)),
        grid=(n_batch // nb,),
        in_specs=in_specs + cast_specs,
        out_specs=out_specs + cast_specs,
        out_shape=out_shape + cast_shapes,
        scratch_shapes=scratch,
        compiler_params=_params(),
        name="bidir_ssd",
    )(*args, *ride_along)
    return outs[:n_main_out], outs[n_main_out:]


def _tail_kernel(*refs, n_in, n_ctx_tiles):
    ctx_a, lat_a = refs[:n_in], refs[n_in:2 * n_in]
    (xc_ref, xl_ref, mod_ref, wout_ref, wup_ref, wdown_ref, g1_ref, b1_ref, g2_ref, b2_ref,
     oc_ref, ol_ref) = refs[2 * n_in:]
    step = pl.program_id(0)

    tile = functools.partial(_mlp_tile, mod_ref=mod_ref, wout_ref=wout_ref, g1_ref=g1_ref, b1_ref=b1_ref,
                             wup_ref=wup_ref, wdown_ref=wdown_ref, g2_ref=g2_ref, b2_ref=b2_ref)

    @pl.when(step < n_ctx_tiles)
    def _():
        tile(ctx_a, xc_ref, oc_ref)

    @pl.when(step >= n_ctx_tiles)
    def _():
        tile(lat_a, xl_ref, ol_ref)


def _mlp_tile(a_refs, x_ref, o_ref, *, mod_ref, wout_ref, g1_ref, b1_ref, wup_ref, wdown_ref, g2_ref, b2_ref):
    m = mod_ref[0]
    n_chunks = D_FF // FF_CHUNK

    def out_proj(rows):
        d = None
        off = 0
        for a_ref in a_refs:
            n = a_ref.shape[1]
            part = _dot(a_ref[rows, :].astype(BF16), wout_ref[off:off + n, :])
            d = part if d is None else d + part
            off += n
        return d

    def head(rows, d):
        x1 = _layer_norm(ALPHA * x_ref[rows, :] + m[2:3] * d, g1_ref[...], b1_ref[...])
        return x1, (x1 * (1.0 + m[4:5]) + m[3:4]).astype(BF16)

    def up(h, c):
        return _dot(h, wup_ref[:, c * FF_CHUNK:(c + 1) * FF_CHUNK])

    def down(u, acc, c):
        u = jnp.maximum(u, 0.0)
        part = _dot((u * u).astype(BF16), wdown_ref[c * FF_CHUNK:(c + 1) * FF_CHUNK, :])
        return part if acc is None else acc + part

    def finish(rows, x1, acc):
        o_ref[rows, :] = _layer_norm(ALPHA * x1 + m[5:6] * acc, g2_ref[...], b2_ref[...])

    assert sum(TAIL_SPLIT) == x_ref.shape[0]
    k = len(TAIL_SPLIT)
    starts = [sum(TAIL_SPLIT[:t]) for t in range(k)]
    rows = [slice(s, s + n) for s, n in zip(starts, TAIL_SPLIT)]
    projected = [out_proj(r) for r in rows]
    heads = [None] * k
    heads[0] = head(rows[0], projected[0])
    u_next = up(heads[0][1], 0)
    done = None
    for t in range(k):
        if t + 1 < k:
            heads[t + 1] = head(rows[t + 1], projected[t + 1])
        acc = None
        for c in range(n_chunks):
            if c + 1 < n_chunks:
                nxt = up(heads[t][1], c + 1)
            else:
                nxt = up(heads[t + 1][1], 0) if t + 1 < k else None
            u_cur, u_next = u_next, nxt
            acc = down(u_cur, acc, c)
            if c == 0 and done is not None:
                finish(*done)
        done = (rows[t], heads[t][0], acc)
    finish(*done)


def _tail(a_ctx, a_lat, x_ctx, x_lat, mod, lat_seq_len, w_out, ln1_g, ln1_b, w_up, w_down, ln2_g, ln2_b):
    tm = TOKEN_TILE
    n_ctx = x_ctx.shape[0] // tm
    n_lat = x_lat.shape[0] // tm
    tiles_per_lat_seq = lat_seq_len // tm
    ctx_row = lambda n: pl.BlockSpec((tm, n), lambda i: (jnp.minimum(i, n_ctx - 1), 0))
    lat_row = lambda n: pl.BlockSpec((tm, n), lambda i: (jnp.maximum(i - n_ctx, 0), 0), pipeline_mode=pl.Buffered(1))
    lat_out = pl.BlockSpec((tm, D_MODEL), lambda i: (jnp.maximum(i - n_ctx, 0), 0))
    mod_spec = _mod_spec(lambda i: jnp.where(i < n_ctx, 0, 1 + (i - n_ctx) // tiles_per_lat_seq))
    vec = lambda v: v.reshape(1, D_MODEL)
    consts = [w_out, w_up, w_down, vec(ln1_g), vec(ln1_b), vec(ln2_g), vec(ln2_b)]
    return pl.pallas_call(
        functools.partial(_tail_kernel, n_in=len(a_ctx), n_ctx_tiles=n_ctx),
        grid=(n_ctx + n_lat,),
        in_specs=[ctx_row(a.shape[1]) for a in a_ctx] + [lat_row(a.shape[1]) for a in a_lat]
                 + [ctx_row(D_MODEL), lat_row(D_MODEL), mod_spec] + [_const_spec(v.shape) for v in consts],
        out_specs=[ctx_row(D_MODEL), lat_out],
        out_shape=[jax.ShapeDtypeStruct(x_ctx.shape, F32), jax.ShapeDtypeStruct(x_lat.shape, F32)],
        compiler_params=_params(),
        name="outproj_mlp",
    )(*a_ctx, *a_lat, x_ctx, x_lat, mod, *consts)


def _pool_kernel(x_ref, mod_ref, win_ref, pw_ref, ps_ref, o_ref, *, seq_len):
    L = seq_len
    m = mod_ref[0]
    t_i = lax.broadcasted_iota(jnp.int32, (L, L), 0)
    s_i = lax.broadcasted_iota(jnp.int32, (L, L), 1)
    t_g = lax.broadcasted_iota(jnp.int32, (L, POOL_GROUP_DIM), 0)
    bands, scales = [], []
    for w in POOL_WINDOWS:
        lo, hi = w // 2, w - w // 2
        bands.append(jnp.where((s_i >= t_i - lo) & (s_i < t_i + hi), 1.0 / w, 0.0).astype(BF16))
        scales.append(w / (jnp.minimum(t_g + hi, L) - jnp.maximum(t_g - lo, 0)).astype(F32))
    n_seq = x_ref.shape[0] // L
    seq_rows = [slice(s * L, (s + 1) * L) for s in range(n_seq)]
    hs = [(x_ref[r, :] * (1.0 + m[1:2]) + m[0:1]).astype(BF16) for r in seq_rows]
    us = [_dot(h, win_ref[...]) for h in hs]
    n_groups = len(POOL_WINDOWS)
    cols = [slice(g * POOL_GROUP_DIM, (g + 1) * POOL_GROUP_DIM) for g in range(n_groups)]

    def window_means(g):
        out = []
        for s in range(n_seq):
            ug = us[s][:, cols[g]]
            ug_hi = ug.astype(BF16)
            ug_lo = (ug - ug_hi.astype(F32)).astype(BF16)
            out.append((ug, _dot(bands[g], ug_hi) + _dot(bands[g], ug_lo)))
        return out

    def mix(g, means):
        return [_dot((wm * scales[g] - ug).astype(BF16), pw_ref[g]) * ps_ref[:, cols[g]] for ug, wm in means]

    mixed = []
    pending = window_means(0)
    for g in range(n_groups):
        nxt = window_means(g + 1) if g + 1 < n_groups else None
        mixed.append(mix(g, pending))
        pending = nxt
    for s in range(n_seq):
        o_ref[seq_rows[s], :] = jnp.concatenate([mixed[g][s] for g in range(n_groups)], axis=1)


def _pool_mixer(x, mod, row_of_tile, w_in, pool_w, pool_scale, seq_len, tm):
    t = x.shape[0]
    row = pl.BlockSpec((tm, D_MODEL), lambda i: (i, 0))
    consts = [w_in, pool_w, pool_scale.reshape(1, D_MODEL)]
    return pl.pallas_call(
        functools.partial(_pool_kernel, seq_len=seq_len),
        grid=(t // tm,),
        in_specs=[row, _mod_spec(row_of_tile)] + [_const_spec(cst.shape) for cst in consts],
        out_specs=row,
        out_shape=jax.ShapeDtypeStruct((t, D_MODEL), F32),
        compiler_params=_params(),
        name="pool_mixer",
    )(x, mod, *consts)


def _rope_tables(rows):
    r, col = np.meshgrid(np.arange(rows), np.arange(GRID_W), indexing="ij")
    r = r.reshape(-1).astype(np.float64)
    col = col.reshape(-1).astype(np.float64)
    n_freq = HEAD_DIM_A // 4
    inv = ROPE_BASE ** (-np.arange(n_freq, dtype=np.float64) / n_freq)
    ang = np.concatenate([r[:, None] * inv, col[:, None] * inv], -1)
    reps = 512 // HEAD_DIM_A
    cosf = np.tile(np.concatenate([np.cos(ang), np.cos(ang)], -1), (1, reps)).astype(np.float32)
    sinf = np.tile(np.concatenate([-np.sin(ang), np.sin(ang)], -1), (1, reps)).astype(np.float32)
    return jnp.asarray(cosf), jnp.asarray(sinf)


def kernel(x_prompt, x_sample, cache_k_l0, cache_v_l0, state_ssm_l0, c, c_ctx, w_mod_l0, b_mod_l0, w_in_l0, conv_w_l0, conv_b_l0, a_log_l0, dt_bias_l0, d_skip_l0, ssm_norm_w_l0, lam_q1_l0, lam_k1_l0, lam_q2_l0, lam_k2_l0, subln_w_l0, w_out_l0, ln1_g_l0, ln1_b_l0, w_up_l0, w_down_l0, ln2_g_l0, ln2_b_l0, w_mod_l1, b_mod_l1, w_in_l1, pool_w_l1, pool_scale_l1, w_out_l1, ln1_g_l1, ln1_b_l1, w_up_l1, w_down_l1, ln2_g_l1, ln2_b_l1):
    n_ctx, l_ctx, _ = x_prompt.shape
    n_lat, l_lat, _ = x_sample.shape
    past = cache_k_l0.shape[1]
    xc = x_prompt.reshape(n_ctx * l_ctx, D_MODEL)
    xl = x_sample.reshape(n_lat * l_lat, D_MODEL)

    cond8 = jnp.concatenate([c_ctx[None, :], c, jnp.zeros((8 - 1 - n_lat, D_MODEL), F32)], axis=0)
    ctx_row = lambda i: 0

    mod0 = _modulation(cond8, 1 + n_lat, w_mod_l0, b_mod_l0)
    n_main = 4 * A_QK + A_V + D_INNER_B + CONV_DIM_B
    w_main = w_in_l0.astype(BF16)
    w_dt = jnp.pad(w_in_l0[:, n_main:], ((0, 0), (0, LANES - 2 * N_HEADS_B))).astype(BF16)
    lane_pad = lambda v: jnp.pad(v.reshape(1, -1), ((0, 0), (0, LANES - 2 * N_HEADS_B)))
    conv_consts = (conv_w_l0, conv_b_l0.reshape(1, CONV_DIM_B), lane_pad(dt_bias_l0))
    ssd_consts = [lane_pad(a_log_l0),
                  jnp.repeat(d_skip_l0, HEAD_DIM_B).reshape(1, D_INNER_B),
                  ssm_norm_w_l0.reshape(1, D_INNER_B)]
    lam_vecs = jnp.stack([lam_q1_l0, lam_k1_l0, lam_q2_l0, lam_k2_l0], axis=0)
    lam_init = 0.8 - 0.6 * math.exp(-0.3 * 0)

    (qc, kc, vc, zc, xbc_c, dt_c), (w_out0, w_up0, w_down0) = _inproj0(
        xc, mod0, ctx_row, w_main, w_dt, *conv_consts, l_ctx, TOKEN_TILE, ride_along=(w_out_l0, w_up_l0, w_down_l0))
    oa_c = _attention(qc, [kc], [vc], lam_vecs, subln_w_l0, l_ctx, l_ctx, CTX_SEQS_PER_STEP, lam_init)
    (y_c, new_ssm), (w_out1, w_up1, w_down1, w_in1, pool_w) = _ssd(
        xbc_c, dt_c, zc, None, ssd_consts, n_ctx, l_ctx, 1,
        ride_along=(w_out_l1, w_up_l1, w_down_l1, w_in_l1, pool_w_l1.reshape(-1, POOL_GROUP_DIM)))
    pool_w = pool_w.reshape(pool_w_l1.shape)

    (ql, kl, vl, zl, xbc_l, dt_l), _ = _inproj0(xl, mod0, lambda i: 1 + i, w_main, w_dt, *conv_consts, l_lat, l_lat,
                                                rope_tables=_rope_tables(l_lat // GRID_W))
    ck = jnp.transpose(cache_k_l0, (0, 2, 3, 4, 1))
    cv = cache_v_l0.reshape(n_lat * past * N_HEADS_A, 2 * HEAD_DIM_A)
    oa_l = _attention(ql, [ck, kl], [cv, vl], lam_vecs, subln_w_l0, l_lat, ATTN_Q_TILE, 1, lam_init)
    h0 = state_ssm_l0.reshape(n_lat, 2, D_INNER_B, D_STATE)
    ((y_l,), _) = _ssd(xbc_l, dt_l, zl, h0, ssd_consts, n_lat, l_lat, 1)
    tail0 = (w_out0, ln1_g_l0, ln1_b_l0, w_up0, w_down0, ln2_g_l0, ln2_b_l0)
    xc, xl = _tail([oa_c, y_c], [oa_l, y_l], xc, xl, mod0, l_lat, *tail0)

    mod1 = _modulation(cond8, 1 + n_lat, w_mod_l1, b_mod_l1)
    tail1 = (w_out1, ln1_g_l1, ln1_b_l1, w_up1, w_down1, ln2_g_l1, ln2_b_l1)
    mix_c = _pool_mixer(xc, mod1, ctx_row, w_in1, pool_w, pool_scale_l1, l_ctx, TOKEN_TILE)
    mix_l = _pool_mixer(xl, mod1, lambda i: 1 + i, w_in1, pool_w, pool_scale_l1, l_lat, l_lat)
    xc, xl = _tail([mix_c], [mix_l], xc, xl, mod1, l_lat, *tail1)

    return (xc.reshape(n_ctx, l_ctx, D_MODEL), xl.reshape(n_lat, l_lat, D_MODEL),
            jnp.transpose(kc, (0, 4, 1, 2, 3)),
            vc.reshape(n_ctx, l_ctx, N_HEADS_A, 2 * HEAD_DIM_A),
            new_ssm.reshape(n_ctx, 2, N_HEADS_B, HEAD_DIM_B, D_STATE))
```

```python
import functools
import math

import jax
import jax.numpy as jnp
import numpy as np
from jax import lax
from jax.experimental import pallas as pl
from jax.experimental.pallas import tpu as pltpu

F32 = jnp.float32
BF16 = jnp.bfloat16

D_MODEL = 1024
N_MOD = 6
N_HEADS_A = 4
HEAD_DIM_A = 64
A_QK = N_HEADS_A * HEAD_DIM_A
A_V = 2 * A_QK
D_INNER_B = 512
HEAD_DIM_B = 64
N_HEADS_B = 8
D_STATE = 128
D_CONV = 5
CONV_DIM_B = 1024
CHUNK = 128
GRID_W = 64
POOL_WINDOWS = (2, 4, 8, 16)
POOL_GROUP_DIM = 256
D_FF = 4096
ROPE_BASE = 10000.0
LN_EPS = 1e-5
DEPTH = 2
ALPHA = (2 * DEPTH) ** 0.25
NEG_BIG = -1e30
LOG2E = 1.4426950408889634

LANES = 128
TOKEN_TILE = 512
ATTN_Q_TILE = 256
CTX_SEQS_PER_STEP = 4
MOD_ROW_BLOCK = 128
FF_CHUNK = 1024
CONV_COL_BLOCK = 256
TAIL_SPLIT = (256, 256)
VMEM_LIMIT = 56 * 1024 * 1024


def _dot(a, b):
    return jnp.dot(a, b, preferred_element_type=F32)


def _dot_nt(a, b):
    return lax.dot_general(a, b, (((1,), (1,)), ((), ())), preferred_element_type=F32)


def _bf16_pieces(a):
    a1 = a.astype(BF16)
    r1 = a - a1.astype(F32)
    a2 = r1.astype(BF16)
    a3 = (r1 - a2.astype(F32)).astype(BF16)
    return a1, a2, a3


def _dot_exact_rhs(t01, a):
    a1, a2, a3 = _bf16_pieces(a)
    return _dot(t01, a1) + _dot(t01, a2) + _dot(t01, a3)


def _sigmoid(x):
    return 1.0 / (1.0 + jnp.exp(-x))


def _silu(x):
    return x * _sigmoid(x)


def _layer_norm(x, g, b):
    mu = jnp.mean(x, axis=-1, keepdims=True)
    xc = x - mu
    var = jnp.mean(xc * xc, axis=-1, keepdims=True)
    return xc * lax.rsqrt(var + LN_EPS) * g + b


def _const_spec(shape):
    nd = len(shape)
    return pl.BlockSpec(shape, lambda *_: (0,) * nd, pipeline_mode=pl.Buffered(1))


def _params(n_axes=1):
    return pltpu.CompilerParams(dimension_semantics=("arbitrary",) * n_axes,
                                vmem_limit_bytes=VMEM_LIMIT)


def _mod_kernel(condt_ref, w_ref, b_ref, o_ref, *, n_rows):
    k = pl.program_id(0)
    kb = w_ref.shape[0]
    st = _silu(condt_ref[pl.ds(pl.multiple_of(k * kb, kb), kb), :])
    cols = [jnp.broadcast_to(st[:, r:r + 1], (kb, LANES)) for r in range(n_rows)]
    row_i = lax.broadcasted_iota(jnp.int32, (8, LANES), 0)

    @pl.when(k == 0)
    def _():
        o_ref[...] = jnp.where(lax.broadcasted_iota(jnp.int32, o_ref.shape, 0) < n_rows, b_ref[...], 0.0)

    for blk in range(w_ref.shape[1] // LANES):
        lanes = slice(blk * LANES, (blk + 1) * LANES)
        w = w_ref[:, lanes]
        part = jnp.zeros((8, LANES), F32)
        for r in range(n_rows):
            part = jnp.where(row_i == r, jnp.sum(w * cols[r], axis=0, keepdims=True), part)
        o_ref[:, lanes] += part


def _modulation(cond8, n_rows, w_mod, b_mod):
    kb = MOD_ROW_BLOCK
    n = w_mod.shape[1]
    out = pl.pallas_call(
        functools.partial(_mod_kernel, n_rows=n_rows),
        grid=(D_MODEL // kb,),
        in_specs=[pl.BlockSpec((D_MODEL, 8), lambda k: (0, 0)),
                  pl.BlockSpec((kb, n), lambda k: (k, 0)),
                  pl.BlockSpec((1, n), lambda k: (0, 0))],
        out_specs=pl.BlockSpec((8, n), lambda k: (0, 0)),
        out_shape=jax.ShapeDtypeStruct((8, n), F32),
        compiler_params=_params(),
        name="modulation",
    )(cond8.T, w_mod, b_mod.reshape(1, n))
    return out.reshape(8, N_MOD, D_MODEL)


def _mod_spec(row_of_tile):
    return pl.BlockSpec((1, N_MOD, D_MODEL), lambda i: (row_of_tile(i), 0, 0))


def _rope(t, cosf, sinf):
    lane = lax.broadcasted_iota(jnp.int32, t.shape, 1)
    first = (lane & (HEAD_DIM_A - 1)) < HEAD_DIM_A // 2
    width = t.shape[1]
    swapped = jnp.where(first, pltpu.roll(t, width - HEAD_DIM_A // 2, 1),
                        pltpu.roll(t, HEAD_DIM_A // 2, 1))
    return t * cosf + swapped * sinf


def _conv_silu(xbc, cw_ref, cb_ref, cols, seq_len):
    assert xbc.shape[0] == seq_len
    half = D_CONV // 2
    edge = 8
    offsets = [j - half for j in range(D_CONV) if j != half]
    shifted = {o: pltpu.roll(xbc, (-o) % seq_len, 0) for o in offsets}

    def taps(lo, hi, masked):
        acc = cb_ref[:, cols] + cw_ref[half:half + 1, cols] * xbc[lo:hi]
        pos = lo + lax.broadcasted_iota(jnp.int32, (hi - lo, xbc.shape[1]), 0)
        for o in offsets:
            s = shifted[o][lo:hi]
            if masked:
                s = jnp.where((pos >= -o) if o < 0 else (pos < seq_len - o), s, 0.0)
            acc = acc + cw_ref[o + half:o + half + 1, cols] * s
        return acc

    acc = jnp.concatenate([taps(0, edge, True), taps(edge, seq_len - edge, False),
                           taps(seq_len - edge, seq_len, True)], axis=0)
    return _silu(acc)


def _softplus(t):
    return jnp.maximum(t, 0.0) + jnp.log(1.0 + jnp.exp(-jnp.abs(t)))


def _ride_along_specs(mats, n_steps):
    specs, shapes = [], []
    for w in mats:
        assert w.shape[0] % n_steps == 0
        specs.append(pl.BlockSpec((w.shape[0] // n_steps, w.shape[1]), lambda i: (i, 0)))
        shapes.append(jax.ShapeDtypeStruct(w.shape, BF16))
    return specs, shapes


def _ride_along_cast(src_refs, dst_refs):
    for src, dst in zip(src_refs, dst_refs):
        dst[...] = src[...].astype(BF16)


def _inproj0_kernel(*refs, rope, seq_len, n_cast):
    n_in = 9 if rope else 7
    ins, cast_in = refs[:n_in], refs[n_in:n_in + n_cast]
    (q_ref, k_ref, v_ref, z_ref, xbc_ref, dt_ref) = refs[n_in + n_cast:n_in + n_cast + 6]
    cast_out = refs[n_in + n_cast + 6:]
    if rope:
        x_ref, mod_ref, w_ref, wdt_ref, cw_ref, cb_ref, dtb_ref, cos_ref, sin_ref = ins
    else:
        x_ref, mod_ref, w_ref, wdt_ref, cw_ref, cb_ref, dtb_ref = ins
    _ride_along_cast(cast_in, cast_out)
    m = mod_ref[0]
    tasks = []
    for b in range(x_ref.shape[0] // seq_len):
        rows = slice(b * seq_len, (b + 1) * seq_len)
        h = (x_ref[rows, :] * (1.0 + m[1:2]) + m[0:1]).astype(BF16)

        for c0 in range(0, CONV_DIM_B, CONV_COL_BLOCK):
            cols = slice(c0, c0 + CONV_COL_BLOCK)

            def conv_mm(h=h, c0=c0):
                return _dot(h, w_ref[:, 2048 + c0:2048 + c0 + CONV_COL_BLOCK])

            def conv_vec(raw, rows=rows, cols=cols):
                xbc_ref[rows, cols] = _conv_silu(raw, cw_ref, cb_ref, cols, seq_len)

            tasks.append((conv_mm, conv_vec))

        def qk_mm(h=h):
            return _dot(h, wdt_ref[...]), _dot(h, w_ref[:, 0:512]), _dot(h, w_ref[:, 512:1024])

        def qk_vec(res, rows=rows, b=b):
            dt_raw, q, k = res
            dt_ref[rows, :] = _softplus(dt_raw + dtb_ref[...])
            if rope:
                q = _rope(q, cos_ref[...], sin_ref[...])
                k = _rope(k, cos_ref[...], sin_ref[...])
            q_ref[rows, :] = q
            for blk in range(2 * A_QK // LANES):
                t = k[:, blk * LANES:(blk + 1) * LANES].T
                which, head = blk // 2, (blk % 2) * 2
                k_ref[b, which, head] = t[0:HEAD_DIM_A]
                k_ref[b, which, head + 1] = t[HEAD_DIM_A:2 * HEAD_DIM_A]

        def vz_mm(h=h):
            return _dot(h, w_ref[:, 1024:1536]), _dot(h, w_ref[:, 1536:2048])

        def vz_vec(res, rows=rows, b=b):
            vals, z = res
            for hh in range(N_HEADS_A):
                v_ref[pl.ds(b * seq_len * N_HEADS_A + hh, seq_len, stride=N_HEADS_A), :] = (
                    vals[:, hh * LANES:(hh + 1) * LANES])
            z_ref[rows, :] = z

        tasks += [(qk_mm, qk_vec), (vz_mm, vz_vec)]

    pending = None
    for mm, vec in tasks:
        res = mm()
        if pending is not None:
            pending[0](pending[1])
        pending = (vec, res)
    pending[0](pending[1])


def _inproj0(x, mod, row_of_tile, w_main, w_dt, conv_w, conv_b, dt_bias, seq_len, tm, rope_tables=None,
             ride_along=()):
    t = x.shape[0]
    assert tm % seq_len == 0
    row = lambda n: pl.BlockSpec((tm, n), lambda i: (i, 0))
    flat = lambda n: jax.ShapeDtypeStruct((t, n), F32)
    consts = [w_main, w_dt, conv_w, conv_b, dt_bias]
    in_specs = [row(D_MODEL), _mod_spec(row_of_tile)] + [_const_spec(cst.shape) for cst in consts]
    args = [x, mod] + consts
    if rope_tables is not None:
        assert tm == seq_len
        in_specs += [_const_spec(tab.shape) for tab in rope_tables]
        args += list(rope_tables)
    kt_dims = (2, N_HEADS_A, HEAD_DIM_A, seq_len)
    k_spec = pl.BlockSpec((tm // seq_len,) + kt_dims, lambda i: (i, 0, 0, 0, 0))
    k_shape = jax.ShapeDtypeStruct((t // seq_len,) + kt_dims, F32)
    cast_specs, cast_shapes = _ride_along_specs(ride_along, t // tm)
    outs = pl.pallas_call(
        functools.partial(_inproj0_kernel, rope=rope_tables is not None, seq_len=seq_len, n_cast=len(ride_along)),
        grid=(t // tm,),
        in_specs=in_specs + cast_specs,
        out_specs=[row(512), k_spec, pl.BlockSpec((tm * N_HEADS_A, LANES), lambda i: (i, 0)),
                   row(512), row(CONV_DIM_B), row(LANES)] + cast_specs,
        out_shape=[flat(512), k_shape, jax.ShapeDtypeStruct((t * N_HEADS_A, LANES), F32),
                   flat(512), flat(CONV_DIM_B), flat(LANES)] + cast_shapes,
        compiler_params=_params(),
        name="inproj0",
    )(*args, *ride_along)
    return outs[:6], outs[6:]


def _lambda_full(lam_ref, lam_init):
    lv = lam_ref[...]
    return (jnp.exp(jnp.sum(lv[0:1] * lv[1:2], axis=1, keepdims=True))
            - jnp.exp(jnp.sum(lv[2:3] * lv[3:4], axis=1, keepdims=True)) + lam_init)


def _attn_kernel(*refs, n_seg, tq, lam_init):
    q_ref = refs[0]
    kt_refs = refs[1:1 + n_seg]
    v_refs = refs[1 + n_seg:1 + 2 * n_seg]
    lam_ref, sub_ref, o_ref = refs[1 + 2 * n_seg:]
    lam = _lambda_full(lam_ref, lam_init)
    lane = lax.broadcasted_iota(jnp.int32, (1, A_QK), 1)
    masks = [(lane >= h * HEAD_DIM_A) & (lane < (h + 1) * HEAD_DIM_A) for h in range(N_HEADS_A)]
    keys = [r.shape[-1] for r in kt_refs]

    def stacked(qx):
        return jnp.concatenate([jnp.where(m, qx, 0.0) for m in masks], axis=0).astype(BF16)

    def exp_scores(qx, b, which):
        qs = stacked(qx)
        s = [_dot(qs, r[b, which].reshape(A_QK, n).astype(BF16)) for r, n in zip(kt_refs, keys)]
        mx = functools.reduce(jnp.maximum, [jnp.max(x, axis=1, keepdims=True) for x in s])
        return [jnp.exp2(x - mx).astype(BF16) for x in s]

    outs = []
    for b in range(kt_refs[0].shape[0]):
        q = q_ref[b * tq:(b + 1) * tq, :] * (HEAD_DIM_A ** -0.5 * LOG2E)
        e1 = exp_scores(q[:, 0:A_QK], b, 0)
        e2 = exp_scores(q[:, A_QK:2 * A_QK], b, 1)
        heads = []
        for h in range(N_HEADS_A):
            hrows = slice(h * tq, (h + 1) * tq)
            n1 = n2 = None
            for s, (v_ref, n) in enumerate(zip(v_refs, keys)):
                v_h = v_ref[pl.ds(b * n * N_HEADS_A + h, n, stride=N_HEADS_A), :]
                v_ext = jnp.concatenate([v_h.astype(BF16),
                                         jnp.ones((n, LANES), BF16)], axis=1)
                p1 = _dot(e1[s][hrows], v_ext)
                p2 = _dot(e2[s][hrows], v_ext)
                n1 = p1 if n1 is None else n1 + p1
                n2 = p2 if n2 is None else n2 + p2
            heads.append(n1[:, 0:LANES] / n1[:, LANES:] - lam * (n2[:, 0:LANES] / n2[:, LANES:]))
        o = jnp.concatenate(heads, axis=0)
        ms = jnp.mean(o * o, axis=1, keepdims=True)
        o = (o * lax.rsqrt(ms + 1e-5) * sub_ref[...]) * (1.0 - lam_init)
        outs.append(jnp.concatenate([o[h * tq:(h + 1) * tq] for h in range(N_HEADS_A)], axis=1))
    o_ref[...] = jnp.concatenate(outs, axis=0)


def _attention(q, kts, vs, lam_vecs, subln_w, q_len, tq, seqs_per_step, lam_init):
    n_seq = kts[0].shape[0]
    nb = seqs_per_step
    tiles = q_len // tq
    assert tiles == 1 or nb == 1
    qspec = pl.BlockSpec((nb * tq, 512), lambda b, i: (b * tiles + i, 0))
    ktspecs = [pl.BlockSpec((nb,) + kt.shape[1:], lambda b, i: (b, 0, 0, 0, 0)) for kt in kts]
    vspecs = [pl.BlockSpec((nb * kt.shape[-1] * N_HEADS_A, LANES), lambda b, i: (b, 0)) for kt in kts]
    return pl.pallas_call(
        functools.partial(_attn_kernel, n_seg=len(kts), tq=tq, lam_init=lam_init),
        grid=(n_seq // nb, tiles),
        in_specs=[qspec] + ktspecs + vspecs + [pl.BlockSpec((4, HEAD_DIM_A), lambda b, i: (0, 0)),
                                             pl.BlockSpec((1, LANES), lambda b, i: (0, 0))],
        out_specs=qspec,
        out_shape=jax.ShapeDtypeStruct(q.shape, F32),
        compiler_params=_params(2),
        name="diff_attention",
    )(q, *kts, *vs, lam_vecs, subln_w.reshape(1, LANES))


def _chunk_loop(n, body, unroll=1):
    if n <= 2:
        for c in range(n):
            body(c)
    else:
        def step(c, carry):
            body(c)
            return carry
        lax.fori_loop(0, n, step, 0, unroll=unroll)


def _ssd_kernel(*refs, seq_len, n_seq, has_h0, n_cast):
    n_in = 7 if has_h0 else 6
    n_out = 1 if has_h0 else 2
    ins, cast_in = refs[:n_in], refs[n_in:n_in + n_cast]
    outs = refs[n_in + n_cast:n_in + n_cast + n_out]
    cast_out = refs[n_in + n_cast + n_out:n_in + 2 * n_cast + n_out]
    ysc, s_sc, cd_sc, ce_sc, htf, htb = refs[n_in + 2 * n_cast + n_out:]
    if has_h0:
        xact, dtv, z_ref, h0_ref, arow_ref, dsk_ref, nw_ref = ins
        (y_ref,), hfin_ref = outs, None
    else:
        xact, dtv, z_ref, arow_ref, dsk_ref, nw_ref = ins
        (y_ref, hfin_ref), h0_ref = outs, None
    _ride_along_cast(cast_in, cast_out)
    nc = seq_len // CHUNK
    n_pairs = N_HEADS_B // 2
    a_row = -jnp.exp(arow_ref[...]) * LOG2E

    row_i = lax.broadcasted_iota(jnp.int32, (CHUNK, CHUNK), 0)
    col_i = lax.broadcasted_iota(jnp.int32, (CHUNK, CHUNK), 1)
    lower = col_i <= row_i
    upper = col_i >= row_i
    tri = (jnp.where(lower, 1.0, 0.0).astype(BF16), jnp.where(upper, 1.0, 0.0).astype(BF16))
    lo_half = col_i < HEAD_DIM_B

    def split_pair(m):
        zero = jnp.zeros_like(m)
        return jnp.concatenate([jnp.where(lo_half, m, zero), jnp.where(lo_half, zero, m)], axis=0).astype(BF16)

    def decay_terms(c):
        rows = pl.ds(pl.multiple_of(c * CHUNK, CHUNK), CHUNK)
        dtc = dtv[rows, :]
        dtt = dtc.T[0:16, :]
        per_dir = []
        for d in range(2):
            col = _dot_exact_rhs(tri[d], dtc * a_row)
            rowm = col.T[0:16, :]
            far = CHUNK - 1 if d == 0 else 0
            tot = jnp.broadcast_to(rowm[:, far:far + 1], (16, CHUNK))
            dte = jnp.exp2(tot - rowm) * dtt
            per_dir.append((col, rowm - jnp.log2(dtt), dte, jnp.exp2(tot)))
        return per_dir

    def local_phase(c, per_dir=None):
        rows = pl.ds(pl.multiple_of(c * CHUNK, CHUNK), CHUNK)
        if per_dir is None:
            per_dir = decay_terms(c)
        for g in range(2):
            bg = xact[rows, 512 + g * LANES:512 + (g + 1) * LANES]
            cg = xact[rows, 768 + g * LANES:768 + (g + 1) * LANES]
            cbm = _dot_nt(cg.astype(BF16), bg.astype(BF16))
            bgt = bg.T
            for pq in range(2):
                pi = g * 2 + pq
                lanes = slice(pi * LANES, (pi + 1) * LANES)
                xrhs = split_pair(xact[rows, lanes])
                y_pair = None
                for d in range(2):
                    col, row_dt, dte, cdec = per_dir[d]
                    causal = lower if d == 0 else upper
                    m_parts, ce_parts, bw_parts, cd_parts = [], [], [], []
                    for hh in (2 * pi, 2 * pi + 1):
                        hd = 8 * d + hh
                        colb = jnp.sum(jnp.where(col_i == hd, col, 0.0), axis=1, keepdims=True)
                        m_parts.append(cbm * jnp.exp2(jnp.where(causal, colb - row_dt[hd:hd + 1, :], NEG_BIG)))
                        ce_parts.append(cg * jnp.exp2(colb))
                        bw_parts.append(bgt * dte[hd:hd + 1, :])
                        cd_parts.append(cdec[hd:hd + 1, :])
                    yd = _dot(jnp.concatenate(m_parts, axis=1).astype(BF16), xrhs)
                    y_pair = yd if y_pair is None else y_pair + yd
                    s_sc[c, d, :, lanes] = _dot(jnp.concatenate(bw_parts, axis=1).astype(BF16), xrhs)
                    ce_sc[c, d, :, pi * 2 * LANES:(pi + 1) * 2 * LANES] = (
                        jnp.concatenate(ce_parts, axis=1).astype(BF16))
                    cd_sc[c, d, :, lanes] = jnp.broadcast_to(
                        jnp.where(lo_half[0:1, :], cd_parts[0], cd_parts[1]), (8, LANES))
                ysc[rows, lanes] = y_pair

    if n_seq * nc <= 2:
        terms = [decay_terms(c) for c in range(n_seq * nc)]
        for c in range(n_seq * nc):
            local_phase(c, terms[c])
    else:
        _chunk_loop(n_seq * nc, local_phase, unroll=2)

    def scan_sequence(sq):
        for d, ht in ((0, htf), (1, htb)):
            for qd in range(n_pairs):
                lanes = slice(qd * LANES, (qd + 1) * LANES)
                if has_h0:
                    ht[:, lanes] = h0_ref[sq, d, lanes, :].T
                else:
                    ht[:, lanes] = jnp.zeros((D_STATE, LANES), F32)

        def scan_phase(j):
            for d, ht in ((0, htf), (1, htb)):
                c = sq * nc + (j if d == 0 else nc - 1 - j)
                rows = pl.ds(pl.multiple_of(c * CHUNK, CHUNK), CHUNK)
                for pi in range(n_pairs):
                    lanes = slice(pi * LANES, (pi + 1) * LANES)
                    hprev = ht[:, lanes]
                    ysc[rows, lanes] += _dot(ce_sc[c, d, :, pi * 2 * LANES:(pi + 1) * 2 * LANES],
                                             split_pair(hprev))
                    ht[:, lanes] = hprev * cd_sc[c, d, 0:1, lanes] + s_sc[c, d, :, lanes]

        _chunk_loop(nc, scan_phase)
        if hfin_ref is not None:
            for d, ht in ((0, htf), (1, htb)):
                for qd in range(n_pairs):
                    lanes = slice(qd * LANES, (qd + 1) * LANES)
                    hfin_ref[sq, d, lanes, :] = ht[:, lanes].T

    if n_seq == 1:
        scan_sequence(0)
    else:
        def seq_step(sq, carry):
            scan_sequence(sq)
            return carry
        lax.fori_loop(0, n_seq, seq_step, 0)

    def finish(c):
        rows = pl.ds(pl.multiple_of(c * CHUNK, CHUNK), CHUNK)
        y = ysc[rows, :] + dsk_ref[...] * xact[rows, 0:D_INNER_B]
        y = y * _silu(z_ref[rows, :])
        ms = jnp.mean(y * y, axis=1, keepdims=True)
        y_ref[rows, :] = y * lax.rsqrt(ms + 1e-5) * nw_ref[...]

    _chunk_loop(n_seq * nc, finish)


def _ssd(xbc, dt, z, h0, consts, n_batch, seq_len, seqs_per_step, ride_along=()):
    nb = seqs_per_step
    L = nb * seq_len
    nc = L // CHUNK
    row = lambda n: pl.BlockSpec((L, n), lambda b: (b, 0))
    state_spec = pl.BlockSpec((nb, 2, D_INNER_B, D_STATE), lambda b: (b, 0, 0, 0))
    has_h0 = h0 is not None
    in_specs = [row(CONV_DIM_B), row(LANES), row(D_INNER_B)]
    args = [xbc, dt, z]
    if has_h0:
        in_specs.append(state_spec)
        args.append(h0)
    for cst in consts:
        in_specs.append(pl.BlockSpec(cst.shape, lambda b: (0, 0)))
        args.append(cst)
    out_specs = [row(D_INNER_B)]
    out_shape = [jax.ShapeDtypeStruct((n_batch * seq_len, D_INNER_B), F32)]
    if not has_h0:
        out_specs.append(state_spec)
        out_shape.append(jax.ShapeDtypeStruct((n_batch, 2, D_INNER_B, D_STATE), F32))
    scratch = [pltpu.VMEM((L, D_INNER_B), F32),
               pltpu.VMEM((nc, 2, D_STATE, D_INNER_B), F32),
               pltpu.VMEM((nc, 2, 8, D_INNER_B), F32),
               pltpu.VMEM((nc, 2, CHUNK, 2 * D_INNER_B), BF16),
               pltpu.VMEM((D_STATE, D_INNER_B), F32),
               pltpu.VMEM((D_STATE, D_INNER_B), F32)]
    cast_specs, cast_shapes = _ride_along_specs(ride_along, n_batch // nb)
    n_main_out = len(out_specs)
    outs = pl.pallas_call(
        functools.partial(_ssd_kernel, seq_len=seq_len, n_seq=nb, has_h0=has_h0, n_cast=len(ride_along)),
        grid=(n_batch // nb,),
        in_specs=in_specs + cast_specs,
        out_specs=out_specs + cast_specs,
        out_shape=out_shape + cast_shapes,
        scratch_shapes=scratch,
        compiler_params=_params(),
        name="bidir_ssd",
    )(*args, *ride_along)
    return outs[:n_main_out], outs[n_main_out:]


def _tail_kernel(*refs, n_in, n_ctx_tiles):
    ctx_a, lat_a = refs[:n_in], refs[n_in:2 * n_in]
    (xc_ref, xl_ref, mod_ref, wout_ref, wup_ref, wdown_ref, g1_ref, b1_ref, g2_ref, b2_ref,
     oc_ref, ol_ref) = refs[2 * n_in:]
    step = pl.program_id(0)

    tile = functools.partial(_mlp_tile, mod_ref=mod_ref, wout_ref=wout_ref, g1_ref=g1_ref, b1_ref=b1_ref,
                             wup_ref=wup_ref, wdown_ref=wdown_ref, g2_ref=g2_ref, b2_ref=b2_ref)

    @pl.when(step < n_ctx_tiles)
    def _():
        tile(ctx_a, xc_ref, oc_ref)

    @pl.when(step >= n_ctx_tiles)
    def _():
        tile(lat_a, xl_ref, ol_ref)


def _mlp_tile(a_refs, x_ref, o_ref, *, mod_ref, wout_ref, g1_ref, b1_ref, wup_ref, wdown_ref, g2_ref, b2_ref):
    m = mod_ref[0]
    n_chunks = D_FF // FF_CHUNK

    def out_proj(rows):
        d = None
        off = 0
        for a_ref in a_refs:
            n = a_ref.shape[1]
            part = _dot(a_ref[rows, :].astype(BF16), wout_ref[off:off + n, :])
            d = part if d is None else d + part
            off += n
        return d

    def head(rows, d):
        x1 = _layer_norm(ALPHA * x_ref[rows, :] + m[2:3] * d, g1_ref[...], b1_ref[...])
        return x1, (x1 * (1.0 + m[4:5]) + m[3:4]).astype(BF16)

    def up(h, c):
        return _dot(h, wup_ref[:, c * FF_CHUNK:(c + 1) * FF_CHUNK])

    def down(u, acc, c):
        u = jnp.maximum(u, 0.0)
        part = _dot((u * u).astype(BF16), wdown_ref[c * FF_CHUNK:(c + 1) * FF_CHUNK, :])
        return part if acc is None else acc + part

    def finish(rows, x1, acc):
        o_ref[rows, :] = _layer_norm(ALPHA * x1 + m[5:6] * acc, g2_ref[...], b2_ref[...])

    assert sum(TAIL_SPLIT) == x_ref.shape[0]
    k = len(TAIL_SPLIT)
    starts = [sum(TAIL_SPLIT[:t]) for t in range(k)]
    rows = [slice(s, s + n) for s, n in zip(starts, TAIL_SPLIT)]
    projected = [out_proj(r) for r in rows]
    heads = [None] * k
    heads[0] = head(rows[0], projected[0])
    u_next = up(heads[0][1], 0)
    done = None
    for t in range(k):
        if t + 1 < k:
            heads[t + 1] = head(rows[t + 1], projected[t + 1])
        acc = None
        for c in range(n_chunks):
            if c + 1 < n_chunks:
                nxt = up(heads[t][1], c + 1)
            else:
                nxt = up(heads[t + 1][1], 0) if t + 1 < k else None
            u_cur, u_next = u_next, nxt
            acc = down(u_cur, acc, c)
            if c == 0 and done is not None:
                finish(*done)
        done = (rows[t], heads[t][0], acc)
    finish(*done)


def _tail(a_ctx, a_lat, x_ctx, x_lat, mod, lat_seq_len, w_out, ln1_g, ln1_b, w_up, w_down, ln2_g, ln2_b):
    tm = TOKEN_TILE
    n_ctx = x_ctx.shape[0] // tm
    n_lat = x_lat.shape[0] // tm
    tiles_per_lat_seq = lat_seq_len // tm
    ctx_row = lambda n: pl.BlockSpec((tm, n), lambda i: (jnp.minimum(i, n_ctx - 1), 0))
    lat_row = lambda n: pl.BlockSpec((tm, n), lambda i: (jnp.maximum(i - n_ctx, 0), 0), pipeline_mode=pl.Buffered(1))
    lat_out = pl.BlockSpec((tm, D_MODEL), lambda i: (jnp.maximum(i - n_ctx, 0), 0))
    mod_spec = _mod_spec(lambda i: jnp.where(i < n_ctx, 0, 1 + (i - n_ctx) // tiles_per_lat_seq))
    vec = lambda v: v.reshape(1, D_MODEL)
    consts = [w_out, w_up, w_down, vec(ln1_g), vec(ln1_b), vec(ln2_g), vec(ln2_b)]
    return pl.pallas_call(
        functools.partial(_tail_kernel, n_in=len(a_ctx), n_ctx_tiles=n_ctx),
        grid=(n_ctx + n_lat,),
        in_specs=[ctx_row(a.shape[1]) for a in a_ctx] + [lat_row(a.shape[1]) for a in a_lat]
                 + [ctx_row(D_MODEL), lat_row(D_MODEL), mod_spec] + [_const_spec(v.shape) for v in consts],
        out_specs=[ctx_row(D_MODEL), lat_out],
        out_shape=[jax.ShapeDtypeStruct(x_ctx.shape, F32), jax.ShapeDtypeStruct(x_lat.shape, F32)],
        compiler_params=_params(),
        name="outproj_mlp",
    )(*a_ctx, *a_lat, x_ctx, x_lat, mod, *consts)


def _pool_kernel(x_ref, mod_ref, win_ref, pw_ref, ps_ref, o_ref, *, seq_len):
    L = seq_len
    m = mod_ref[0]
    t_i = lax.broadcasted_iota(jnp.int32, (L, L), 0)
    s_i = lax.broadcasted_iota(jnp.int32, (L, L), 1)
    t_g = lax.broadcasted_iota(jnp.int32, (L, POOL_GROUP_DIM), 0)
    bands, scales = [], []
    for w in POOL_WINDOWS:
        lo, hi = w // 2, w - w // 2
        bands.append(jnp.where((s_i >= t_i - lo) & (s_i < t_i + hi), 1.0 / w, 0.0).astype(BF16))
        scales.append(w / (jnp.minimum(t_g + hi, L) - jnp.maximum(t_g - lo, 0)).astype(F32))
    n_seq = x_ref.shape[0] // L
    seq_rows = [slice(s * L, (s + 1) * L) for s in range(n_seq)]
    hs = [(x_ref[r, :] * (1.0 + m[1:2]) + m[0:1]).astype(BF16) for r in seq_rows]
    us = [_dot(h, win_ref[...]) for h in hs]
    n_groups = len(POOL_WINDOWS)
    cols = [slice(g * POOL_GROUP_DIM, (g + 1) * POOL_GROUP_DIM) for g in range(n_groups)]

    def window_means(g):
        out = []
        for s in range(n_seq):
            ug = us[s][:, cols[g]]
            ug_hi = ug.astype(BF16)
            ug_lo = (ug - ug_hi.astype(F32)).astype(BF16)
            out.append((ug, _dot(bands[g], ug_hi) + _dot(bands[g], ug_lo)))
        return out

    def mix(g, means):
        return [_dot((wm * scales[g] - ug).astype(BF16), pw_ref[g]) * ps_ref[:, cols[g]] for ug, wm in means]

    mixed = []
    pending = window_means(0)
    for g in range(n_groups):
        nxt = window_means(g + 1) if g + 1 < n_groups else None
        mixed.append(mix(g, pending))
        pending = nxt
    for s in range(n_seq):
        o_ref[seq_rows[s], :] = jnp.concatenate([mixed[g][s] for g in range(n_groups)], axis=1)


def _pool_mixer(x, mod, row_of_tile, w_in, pool_w, pool_scale, seq_len, tm):
    t = x.shape[0]
    row = pl.BlockSpec((tm, D_MODEL), lambda i: (i, 0))
    consts = [w_in, pool_w, pool_scale.reshape(1, D_MODEL)]
    return pl.pallas_call(
        functools.partial(_pool_kernel, seq_len=seq_len),
        grid=(t // tm,),
        in_specs=[row, _mod_spec(row_of_tile)] + [_const_spec(cst.shape) for cst in consts],
        out_specs=row,
        out_shape=jax.ShapeDtypeStruct((t, D_MODEL), F32),
        compiler_params=_params(),
        name="pool_mixer",
    )(x, mod, *consts)


def _rope_tables(rows):
    r, col = np.meshgrid(np.arange(rows), np.arange(GRID_W), indexing="ij")
    r = r.reshape(-1).astype(np.float64)
    col = col.reshape(-1).astype(np.float64)
    n_freq = HEAD_DIM_A // 4
    inv = ROPE_BASE ** (-np.arange(n_freq, dtype=np.float64) / n_freq)
    ang = np.concatenate([r[:, None] * inv, col[:, None] * inv], -1)
    reps = 512 // HEAD_DIM_A
    cosf = np.tile(np.concatenate([np.cos(ang), np.cos(ang)], -1), (1, reps)).astype(np.float32)
    sinf = np.tile(np.concatenate([-np.sin(ang), np.sin(ang)], -1), (1, reps)).astype(np.float32)
    return jnp.asarray(cosf), jnp.asarray(sinf)


def kernel(x_prompt, x_sample, cache_k_l0, cache_v_l0, state_ssm_l0, c, c_ctx, w_mod_l0, b_mod_l0, w_in_l0, conv_w_l0, conv_b_l0, a_log_l0, dt_bias_l0, d_skip_l0, ssm_norm_w_l0, lam_q1_l0, lam_k1_l0, lam_q2_l0, lam_k2_l0, subln_w_l0, w_out_l0, ln1_g_l0, ln1_b_l0, w_up_l0, w_down_l0, ln2_g_l0, ln2_b_l0, w_mod_l1, b_mod_l1, w_in_l1, pool_w_l1, pool_scale_l1, w_out_l1, ln1_g_l1, ln1_b_l1, w_up_l1, w_down_l1, ln2_g_l1, ln2_b_l1):
    n_ctx, l_ctx, _ = x_prompt.shape
    n_lat, l_lat, _ = x_sample.shape
    past = cache_k_l0.shape[1]
    xc = x_prompt.reshape(n_ctx * l_ctx, D_MODEL)
    xl = x_sample.reshape(n_lat * l_lat, D_MODEL)

    cond8 = jnp.concatenate([c_ctx[None, :], c, jnp.zeros((8 - 1 - n_lat, D_MODEL), F32)], axis=0)
    ctx_row = lambda i: 0

    mod0 = _modulation(cond8, 1 + n_lat, w_mod_l0, b_mod_l0)
    n_main = 4 * A_QK + A_V + D_INNER_B + CONV_DIM_B
    w_main = w_in_l0.astype(BF16)
    w_dt = jnp.pad(w_in_l0[:, n_main:], ((0, 0), (0, LANES - 2 * N_HEADS_B))).astype(BF16)
    lane_pad = lambda v: jnp.pad(v.reshape(1, -1), ((0, 0), (0, LANES - 2 * N_HEADS_B)))
    conv_consts = (conv_w_l0, conv_b_l0.reshape(1, CONV_DIM_B), lane_pad(dt_bias_l0))
    ssd_consts = [lane_pad(a_log_l0),
                  jnp.repeat(d_skip_l0, HEAD_DIM_B).reshape(1, D_INNER_B),
                  ssm_norm_w_l0.reshape(1, D_INNER_B)]
    lam_vecs = jnp.stack([lam_q1_l0, lam_k1_l0, lam_q2_l0, lam_k2_l0], axis=0)
    lam_init = 0.8 - 0.6 * math.exp(-0.3 * 0)

    (qc, kc, vc, zc, xbc_c, dt_c), (w_out0, w_up0, w_down0) = _inproj0(
        xc, mod0, ctx_row, w_main, w_dt, *conv_consts, l_ctx, TOKEN_TILE, ride_along=(w_out_l0, w_up_l0, w_down_l0))
    oa_c = _attention(qc, [kc], [vc], lam_vecs, subln_w_l0, l_ctx, l_ctx, CTX_SEQS_PER_STEP, lam_init)
    (y_c, new_ssm), (w_out1, w_up1, w_down1, w_in1, pool_w) = _ssd(
        xbc_c, dt_c, zc, None, ssd_consts, n_ctx, l_ctx, 1,
        ride_along=(w_out_l1, w_up_l1, w_down_l1, w_in_l1, pool_w_l1.reshape(-1, POOL_GROUP_DIM)))
    pool_w = pool_w.reshape(pool_w_l1.shape)

    (ql, kl, vl, zl, xbc_l, dt_l), _ = _inproj0(xl, mod0, lambda i: 1 + i, w_main, w_dt, *conv_consts, l_lat, l_lat,
                                                rope_tables=_rope_tables(l_lat // GRID_W))
    ck = jnp.transpose(cache_k_l0, (0, 2, 3, 4, 1))
    cv = cache_v_l0.reshape(n_lat * past * N_HEADS_A, 2 * HEAD_DIM_A)
    oa_l = _attention(ql, [ck, kl], [cv, vl], lam_vecs, subln_w_l0, l_lat, ATTN_Q_TILE, 1, lam_init)
    h0 = state_ssm_l0.reshape(n_lat, 2, D_INNER_B, D_STATE)
    ((y_l,), _) = _ssd(xbc_l, dt_l, zl, h0, ssd_consts, n_lat, l_lat, 1)
    tail0 = (w_out0, ln1_g_l0, ln1_b_l0, w_up0, w_down0, ln2_g_l0, ln2_b_l0)
    xc, xl = _tail([oa_c, y_c], [oa_l, y_l], xc, xl, mod0, l_lat, *tail0)

    mod1 = _modulation(cond8, 1 + n_lat, w_mod_l1, b_mod_l1)
    tail1 = (w_out1, ln1_g_l1, ln1_b_l1, w_up1, w_down1, ln2_g_l1, ln2_b_l1)
    mix_c = _pool_mixer(xc, mod1, ctx_row, w_in1, pool_w, pool_scale_l1, l_ctx, TOKEN_TILE)
    mix_l = _pool_mixer(xl, mod1, lambda i: 1 + i, w_in1, pool_w, pool_scale_l1, l_lat, l_lat)
    xc, xl = _tail([mix_c], [mix_l], xc, xl, mod1, l_lat, *tail1)

    return (xc.reshape(n_ctx, l_ctx, D_MODEL), xl.reshape(n_lat, l_lat, D_MODEL),
            jnp.transpose(kc, (0, 4, 1, 2, 3)),
            vc.reshape(n_ctx, l_ctx, N_HEADS_A, 2 * HEAD_DIM_A),
            new_ssm.reshape(n_ctx, 2, N_HEADS_B, HEAD_DIM_B, D_STATE))
```

```python
import functools
import math

import jax
import jax.numpy as jnp
import numpy as np
from jax import lax
from jax.experimental import pallas as pl
from jax.experimental.pallas import tpu as pltpu

F32 = jnp.float32
BF16 = jnp.bfloat16

D_MODEL = 1024
N_MOD = 6
N_HEADS_A = 4
HEAD_DIM_A = 64
A_QK = N_HEADS_A * HEAD_DIM_A
A_V = 2 * A_QK
D_INNER_B = 512
HEAD_DIM_B = 64
N_HEADS_B = 8
D_STATE = 128
D_CONV = 5
CONV_DIM_B = 1024
CHUNK = 128
GRID_W = 64
POOL_WINDOWS = (2, 4, 8, 16)
POOL_GROUP_DIM = 256
D_FF = 4096
ROPE_BASE = 10000.0
LN_EPS = 1e-5
DEPTH = 2
ALPHA = (2 * DEPTH) ** 0.25
NEG_BIG = -1e30
LOG2E = 1.4426950408889634

LANES = 128
TOKEN_TILE = 512
ATTN_Q_TILE = 256
CTX_SEQS_PER_STEP = 4
FF_CHUNK = 1024
CONV_COL_BLOCK = 256
TAIL_SPLIT = (256, 256)
VMEM_LIMIT = 56 * 1024 * 1024


def _dot(a, b):
    return jnp.dot(a, b, preferred_element_type=F32)


def _dot_nt(a, b):
    return lax.dot_general(a, b, (((1,), (1,)), ((), ())), preferred_element_type=F32)


def _bf16_pieces(a):
    a1 = a.astype(BF16)
    r1 = a - a1.astype(F32)
    a2 = r1.astype(BF16)
    a3 = (r1 - a2.astype(F32)).astype(BF16)
    return a1, a2, a3


def _dot_exact_rhs(t01, a):
    a1, a2, a3 = _bf16_pieces(a)
    return _dot(t01, a1) + _dot(t01, a2) + _dot(t01, a3)


def _sigmoid(x):
    return 1.0 / (1.0 + jnp.exp(-x))


def _silu(x):
    return x * _sigmoid(x)


def _layer_norm(x, g, b):
    mu = jnp.mean(x, axis=-1, keepdims=True)
    xc = x - mu
    var = jnp.mean(xc * xc, axis=-1, keepdims=True)
    return xc * lax.rsqrt(var + LN_EPS) * g + b


def _const_spec(shape):
    nd = len(shape)
    return pl.BlockSpec(shape, lambda *_: (0,) * nd, pipeline_mode=pl.Buffered(1))


def _params(n_axes=1):
    return pltpu.CompilerParams(dimension_semantics=("arbitrary",) * n_axes,
                                vmem_limit_bytes=VMEM_LIMIT)


def _mod_kernel(cond_ref, w_ref, b_ref, o_ref):
    c = cond_ref[...]
    s = _silu(c).astype(BF16)
    o_ref[...] = _dot(s, w_ref[...].astype(BF16)) + b_ref[...]


def _modulation(cond8, w_mod, b_mod):
    tn = 1024
    n = w_mod.shape[1]
    out = pl.pallas_call(
        _mod_kernel,
        grid=(n // tn,),
        in_specs=[pl.BlockSpec((8, D_MODEL), lambda j: (0, 0)),
                  pl.BlockSpec((D_MODEL, tn), lambda j: (0, j)),
                  pl.BlockSpec((1, tn), lambda j: (0, j))],
        out_specs=pl.BlockSpec((8, tn), lambda j: (0, j)),
        out_shape=jax.ShapeDtypeStruct((8, n), F32),
        compiler_params=_params(),
        name="modulation",
    )(cond8, w_mod, b_mod.reshape(1, n))
    return out.reshape(8, N_MOD, D_MODEL)


def _mod_spec(row_of_tile):
    return pl.BlockSpec((1, N_MOD, D_MODEL), lambda i: (row_of_tile(i), 0, 0))


def _rope(t, cosf, sinf):
    lane = lax.broadcasted_iota(jnp.int32, t.shape, 1)
    first = (lane & (HEAD_DIM_A - 1)) < HEAD_DIM_A // 2
    width = t.shape[1]
    swapped = jnp.where(first, pltpu.roll(t, width - HEAD_DIM_A // 2, 1),
                        pltpu.roll(t, HEAD_DIM_A // 2, 1))
    return t * cosf + swapped * sinf


def _conv_silu(xbc, cw_ref, cb_ref, cols, seq_len):
    assert xbc.shape[0] == seq_len
    half = D_CONV // 2
    edge = 8
    offsets = [j - half for j in range(D_CONV) if j != half]
    shifted = {o: pltpu.roll(xbc, (-o) % seq_len, 0) for o in offsets}

    def taps(lo, hi, masked):
        acc = cb_ref[:, cols] + cw_ref[half:half + 1, cols] * xbc[lo:hi]
        pos = lo + lax.broadcasted_iota(jnp.int32, (hi - lo, xbc.shape[1]), 0)
        for o in offsets:
            s = shifted[o][lo:hi]
            if masked:
                s = jnp.where((pos >= -o) if o < 0 else (pos < seq_len - o), s, 0.0)
            acc = acc + cw_ref[o + half:o + half + 1, cols] * s
        return acc

    acc = jnp.concatenate([taps(0, edge, True), taps(edge, seq_len - edge, False),
                           taps(seq_len - edge, seq_len, True)], axis=0)
    return _silu(acc)


def _softplus(t):
    return jnp.maximum(t, 0.0) + jnp.log(1.0 + jnp.exp(-jnp.abs(t)))


def _ride_along_specs(mats, n_steps):
    specs, shapes = [], []
    for w in mats:
        assert w.shape[0] % n_steps == 0
        specs.append(pl.BlockSpec((w.shape[0] // n_steps, w.shape[1]), lambda i: (i, 0)))
        shapes.append(jax.ShapeDtypeStruct(w.shape, BF16))
    return specs, shapes


def _ride_along_cast(src_refs, dst_refs):
    for src, dst in zip(src_refs, dst_refs):
        dst[...] = src[...].astype(BF16)


def _inproj0_kernel(*refs, rope, seq_len, n_cast):
    n_in = 9 if rope else 7
    ins, cast_in = refs[:n_in], refs[n_in:n_in + n_cast]
    (q_ref, k_ref, v_ref, z_ref, xbc_ref, dt_ref) = refs[n_in + n_cast:n_in + n_cast + 6]
    cast_out = refs[n_in + n_cast + 6:]
    if rope:
        x_ref, mod_ref, w_ref, wdt_ref, cw_ref, cb_ref, dtb_ref, cos_ref, sin_ref = ins
    else:
        x_ref, mod_ref, w_ref, wdt_ref, cw_ref, cb_ref, dtb_ref = ins
    _ride_along_cast(cast_in, cast_out)
    m = mod_ref[0]
    tasks = []
    for b in range(x_ref.shape[0] // seq_len):
        rows = slice(b * seq_len, (b + 1) * seq_len)
        h = (x_ref[rows, :] * (1.0 + m[1:2]) + m[0:1]).astype(BF16)

        for c0 in range(0, CONV_DIM_B, CONV_COL_BLOCK):
            cols = slice(c0, c0 + CONV_COL_BLOCK)

            def conv_mm(h=h, c0=c0):
                return _dot(h, w_ref[:, 2048 + c0:2048 + c0 + CONV_COL_BLOCK])

            def conv_vec(raw, rows=rows, cols=cols):
                xbc_ref[rows, cols] = _conv_silu(raw, cw_ref, cb_ref, cols, seq_len)

            tasks.append((conv_mm, conv_vec))

        def qk_mm(h=h):
            return _dot(h, wdt_ref[...]), _dot(h, w_ref[:, 0:512]), _dot(h, w_ref[:, 512:1024])

        def qk_vec(res, rows=rows, b=b):
            dt_raw, q, k = res
            dt_ref[rows, :] = _softplus(dt_raw + dtb_ref[...])
            if rope:
                q = _rope(q, cos_ref[...], sin_ref[...])
                k = _rope(k, cos_ref[...], sin_ref[...])
            q_ref[rows, :] = q
            for blk in range(2 * A_QK // LANES):
                t = k[:, blk * LANES:(blk + 1) * LANES].T
                which, head = blk // 2, (blk % 2) * 2
                k_ref[b, which, head] = t[0:HEAD_DIM_A]
                k_ref[b, which, head + 1] = t[HEAD_DIM_A:2 * HEAD_DIM_A]

        def vz_mm(h=h):
            return _dot(h, w_ref[:, 1024:1536]), _dot(h, w_ref[:, 1536:2048])

        def vz_vec(res, rows=rows, b=b):
            vals, z = res
            for hh in range(N_HEADS_A):
                v_ref[pl.ds(b * seq_len * N_HEADS_A + hh, seq_len, stride=N_HEADS_A), :] = (
                    vals[:, hh * LANES:(hh + 1) * LANES])
            z_ref[rows, :] = z

        tasks += [(qk_mm, qk_vec), (vz_mm, vz_vec)]

    pending = None
    for mm, vec in tasks:
        res = mm()
        if pending is not None:
            pending[0](pending[1])
        pending = (vec, res)
    pending[0](pending[1])


def _inproj0(x, mod, row_of_tile, w_main, w_dt, conv_w, conv_b, dt_bias, seq_len, tm, rope_tables=None,
             ride_along=()):
    t = x.shape[0]
    assert tm % seq_len == 0
    row = lambda n: pl.BlockSpec((tm, n), lambda i: (i, 0))
    flat = lambda n: jax.ShapeDtypeStruct((t, n), F32)
    consts = [w_main, w_dt, conv_w, conv_b, dt_bias]
    in_specs = [row(D_MODEL), _mod_spec(row_of_tile)] + [_const_spec(cst.shape) for cst in consts]
    args = [x, mod] + consts
    if rope_tables is not None:
        assert tm == seq_len
        in_specs += [_const_spec(tab.shape) for tab in rope_tables]
        args += list(rope_tables)
    kt_dims = (2, N_HEADS_A, HEAD_DIM_A, seq_len)
    k_spec = pl.BlockSpec((tm // seq_len,) + kt_dims, lambda i: (i, 0, 0, 0, 0))
    k_shape = jax.ShapeDtypeStruct((t // seq_len,) + kt_dims, F32)
    cast_specs, cast_shapes = _ride_along_specs(ride_along, t // tm)
    outs = pl.pallas_call(
        functools.partial(_inproj0_kernel, rope=rope_tables is not None, seq_len=seq_len, n_cast=len(ride_along)),
        grid=(t // tm,),
        in_specs=in_specs + cast_specs,
        out_specs=[row(512), k_spec, pl.BlockSpec((tm * N_HEADS_A, LANES), lambda i: (i, 0)),
                   row(512), row(CONV_DIM_B), row(LANES)] + cast_specs,
        out_shape=[flat(512), k_shape, jax.ShapeDtypeStruct((t * N_HEADS_A, LANES), F32),
                   flat(512), flat(CONV_DIM_B), flat(LANES)] + cast_shapes,
        compiler_params=_params(),
        name="inproj0",
    )(*args, *ride_along)
    return outs[:6], outs[6:]


def _lambda_full(lam_ref, lam_init):
    lv = lam_ref[...]
    return (jnp.exp(jnp.sum(lv[0:1] * lv[1:2], axis=1, keepdims=True))
            - jnp.exp(jnp.sum(lv[2:3] * lv[3:4], axis=1, keepdims=True)) + lam_init)


def _attn_kernel(*refs, n_seg, tq, lam_init):
    q_ref = refs[0]
    kt_refs = refs[1:1 + n_seg]
    v_refs = refs[1 + n_seg:1 + 2 * n_seg]
    lam_ref, sub_ref, o_ref = refs[1 + 2 * n_seg:]
    lam = _lambda_full(lam_ref, lam_init)
    lane = lax.broadcasted_iota(jnp.int32, (1, A_QK), 1)
    masks = [(lane >= h * HEAD_DIM_A) & (lane < (h + 1) * HEAD_DIM_A) for h in range(N_HEADS_A)]
    keys = [r.shape[-1] for r in kt_refs]

    def stacked(qx):
        return jnp.concatenate([jnp.where(m, qx, 0.0) for m in masks], axis=0).astype(BF16)

    def exp_scores(qx, b, which):
        qs = stacked(qx)
        s = [_dot(qs, r[b, which].reshape(A_QK, n).astype(BF16)) for r, n in zip(kt_refs, keys)]
        mx = functools.reduce(jnp.maximum, [jnp.max(x, axis=1, keepdims=True) for x in s])
        return [jnp.exp2(x - mx).astype(BF16) for x in s]

    outs = []
    for b in range(kt_refs[0].shape[0]):
        q = q_ref[b * tq:(b + 1) * tq, :] * (HEAD_DIM_A ** -0.5 * LOG2E)
        e1 = exp_scores(q[:, 0:A_QK], b, 0)
        e2 = exp_scores(q[:, A_QK:2 * A_QK], b, 1)
        heads = []
        for h in range(N_HEADS_A):
            hrows = slice(h * tq, (h + 1) * tq)
            n1 = n2 = None
            for s, (v_ref, n) in enumerate(zip(v_refs, keys)):
                v_h = v_ref[pl.ds(b * n * N_HEADS_A + h, n, stride=N_HEADS_A), :]
                v_ext = jnp.concatenate([v_h.astype(BF16),
                                         jnp.ones((n, LANES), BF16)], axis=1)
                p1 = _dot(e1[s][hrows], v_ext)
                p2 = _dot(e2[s][hrows], v_ext)
                n1 = p1 if n1 is None else n1 + p1
                n2 = p2 if n2 is None else n2 + p2
            heads.append(n1[:, 0:LANES] / n1[:, LANES:] - lam * (n2[:, 0:LANES] / n2[:, LANES:]))
        o = jnp.concatenate(heads, axis=0)
        ms = jnp.mean(o * o, axis=1, keepdims=True)
        o = (o * lax.rsqrt(ms + 1e-5) * sub_ref[...]) * (1.0 - lam_init)
        outs.append(jnp.concatenate([o[h * tq:(h + 1) * tq] for h in range(N_HEADS_A)], axis=1))
    o_ref[...] = jnp.concatenate(outs, axis=0)


def _attention(q, kts, vs, lam_vecs, subln_w, q_len, tq, seqs_per_step, lam_init):
    n_seq = kts[0].shape[0]
    nb = seqs_per_step
    tiles = q_len // tq
    assert tiles == 1 or nb == 1
    qspec = pl.BlockSpec((nb * tq, 512), lambda b, i: (b * tiles + i, 0))
    ktspecs = [pl.BlockSpec((nb,) + kt.shape[1:], lambda b, i: (b, 0, 0, 0, 0)) for kt in kts]
    vspecs = [pl.BlockSpec((nb * kt.shape[-1] * N_HEADS_A, LANES), lambda b, i: (b, 0)) for kt in kts]
    return pl.pallas_call(
        functools.partial(_attn_kernel, n_seg=len(kts), tq=tq, lam_init=lam_init),
        grid=(n_seq // nb, tiles),
        in_specs=[qspec] + ktspecs + vspecs + [pl.BlockSpec((4, HEAD_DIM_A), lambda b, i: (0, 0)),
                                             pl.BlockSpec((1, LANES), lambda b, i: (0, 0))],
        out_specs=qspec,
        out_shape=jax.ShapeDtypeStruct(q.shape, F32),
        compiler_params=_params(2),
        name="diff_attention",
    )(q, *kts, *vs, lam_vecs, subln_w.reshape(1, LANES))


def _chunk_loop(n, body, unroll=1):
    if n <= 2:
        for c in range(n):
            body(c)
    else:
        def step(c, carry):
            body(c)
            return carry
        lax.fori_loop(0, n, step, 0, unroll=unroll)


def _ssd_kernel(*refs, seq_len, n_seq, has_h0, n_cast):
    n_in = 7 if has_h0 else 6
    n_out = 1 if has_h0 else 2
    ins, cast_in = refs[:n_in], refs[n_in:n_in + n_cast]
    outs = refs[n_in + n_cast:n_in + n_cast + n_out]
    cast_out = refs[n_in + n_cast + n_out:n_in + 2 * n_cast + n_out]
    ysc, s_sc, cd_sc, ce_sc, htf, htb = refs[n_in + 2 * n_cast + n_out:]
    if has_h0:
        xact, dtv, z_ref, h0_ref, arow_ref, dsk_ref, nw_ref = ins
        (y_ref,), hfin_ref = outs, None
    else:
        xact, dtv, z_ref, arow_ref, dsk_ref, nw_ref = ins
        (y_ref, hfin_ref), h0_ref = outs, None
    _ride_along_cast(cast_in, cast_out)
    nc = seq_len // CHUNK
    n_pairs = N_HEADS_B // 2
    a_row = -jnp.exp(arow_ref[...]) * LOG2E

    row_i = lax.broadcasted_iota(jnp.int32, (CHUNK, CHUNK), 0)
    col_i = lax.broadcasted_iota(jnp.int32, (CHUNK, CHUNK), 1)
    lower = col_i <= row_i
    upper = col_i >= row_i
    tri = (jnp.where(lower, 1.0, 0.0).astype(BF16), jnp.where(upper, 1.0, 0.0).astype(BF16))
    lo_half = col_i < HEAD_DIM_B

    def split_pair(m):
        zero = jnp.zeros_like(m)
        return jnp.concatenate([jnp.where(lo_half, m, zero), jnp.where(lo_half, zero, m)], axis=0).astype(BF16)

    def decay_terms(c):
        rows = pl.ds(pl.multiple_of(c * CHUNK, CHUNK), CHUNK)
        dtc = dtv[rows, :]
        dtt = dtc.T[0:16, :]
        per_dir = []
        for d in range(2):
            col = _dot_exact_rhs(tri[d], dtc * a_row)
            rowm = col.T[0:16, :]
            far = CHUNK - 1 if d == 0 else 0
            tot = jnp.broadcast_to(rowm[:, far:far + 1], (16, CHUNK))
            dte = jnp.exp2(tot - rowm) * dtt
            per_dir.append((col, rowm - jnp.log2(dtt), dte, jnp.exp2(tot)))
        return per_dir

    def local_phase(c, per_dir=None):
        rows = pl.ds(pl.multiple_of(c * CHUNK, CHUNK), CHUNK)
        if per_dir is None:
            per_dir = decay_terms(c)
        for g in range(2):
            bg = xact[rows, 512 + g * LANES:512 + (g + 1) * LANES]
            cg = xact[rows, 768 + g * LANES:768 + (g + 1) * LANES]
            cbm = _dot_nt(cg.astype(BF16), bg.astype(BF16))
            bgt = bg.T
            for pq in range(2):
                pi = g * 2 + pq
                lanes = slice(pi * LANES, (pi + 1) * LANES)
                xrhs = split_pair(xact[rows, lanes])
                y_pair = None
                for d in range(2):
                    col, row_dt, dte, cdec = per_dir[d]
                    causal = lower if d == 0 else upper
                    m_parts, ce_parts, bw_parts, cd_parts = [], [], [], []
                    for hh in (2 * pi, 2 * pi + 1):
                        hd = 8 * d + hh
                        colb = jnp.sum(jnp.where(col_i == hd, col, 0.0), axis=1, keepdims=True)
                        m_parts.append(cbm * jnp.exp2(jnp.where(causal, colb - row_dt[hd:hd + 1, :], NEG_BIG)))
                        ce_parts.append(cg * jnp.exp2(colb))
                        bw_parts.append(bgt * dte[hd:hd + 1, :])
                        cd_parts.append(cdec[hd:hd + 1, :])
                    yd = _dot(jnp.concatenate(m_parts, axis=1).astype(BF16), xrhs)
                    y_pair = yd if y_pair is None else y_pair + yd
                    s_sc[c, d, :, lanes] = _dot(jnp.concatenate(bw_parts, axis=1).astype(BF16), xrhs)
                    ce_sc[c, d, :, pi * 2 * LANES:(pi + 1) * 2 * LANES] = (
                        jnp.concatenate(ce_parts, axis=1).astype(BF16))
                    cd_sc[c, d, :, lanes] = jnp.broadcast_to(
                        jnp.where(lo_half[0:1, :], cd_parts[0], cd_parts[1]), (8, LANES))
                ysc[rows, lanes] = y_pair

    if n_seq * nc <= 2:
        terms = [decay_terms(c) for c in range(n_seq * nc)]
        for c in range(n_seq * nc):
            local_phase(c, terms[c])
    else:
        _chunk_loop(n_seq * nc, local_phase, unroll=2)

    def scan_sequence(sq):
        for d, ht in ((0, htf), (1, htb)):
            for qd in range(n_pairs):
                lanes = slice(qd * LANES, (qd + 1) * LANES)
                if has_h0:
                    ht[:, lanes] = h0_ref[sq, d, lanes, :].T
                else:
                    ht[:, lanes] = jnp.zeros((D_STATE, LANES), F32)

        def scan_phase(j):
            for d, ht in ((0, htf), (1, htb)):
                c = sq * nc + (j if d == 0 else nc - 1 - j)
                rows = pl.ds(pl.multiple_of(c * CHUNK, CHUNK), CHUNK)
                for pi in range(n_pairs):
                    lanes = slice(pi * LANES, (pi + 1) * LANES)
                    hprev = ht[:, lanes]
                    ysc[rows, lanes] += _dot(ce_sc[c, d, :, pi * 2 * LANES:(pi + 1) * 2 * LANES],
                                             split_pair(hprev))
                    ht[:, lanes] = hprev * cd_sc[c, d, 0:1, lanes] + s_sc[c, d, :, lanes]

        _chunk_loop(nc, scan_phase)
        if hfin_ref is not None:
            for d, ht in ((0, htf), (1, htb)):
                for qd in range(n_pairs):
                    lanes = slice(qd * LANES, (qd + 1) * LANES)
                    hfin_ref[sq, d, lanes, :] = ht[:, lanes].T

    if n_seq == 1:
        scan_sequence(0)
    else:
        def seq_step(sq, carry):
            scan_sequence(sq)
            return carry
        lax.fori_loop(0, n_seq, seq_step, 0)

    def finish(c):
        rows = pl.ds(pl.multiple_of(c * CHUNK, CHUNK), CHUNK)
        y = ysc[rows, :] + dsk_ref[...] * xact[rows, 0:D_INNER_B]
        y = y * _silu(z_ref[rows, :])
        ms = jnp.mean(y * y, axis=1, keepdims=True)
        y_ref[rows, :] = y * lax.rsqrt(ms + 1e-5) * nw_ref[...]

    _chunk_loop(n_seq * nc, finish)


def _ssd(xbc, dt, z, h0, consts, n_batch, seq_len, seqs_per_step, ride_along=()):
    nb = seqs_per_step
    L = nb * seq_len
    nc = L // CHUNK
    row = lambda n: pl.BlockSpec((L, n), lambda b: (b, 0))
    state_spec = pl.BlockSpec((nb, 2, D_INNER_B, D_STATE), lambda b: (b, 0, 0, 0))
    has_h0 = h0 is not None
    in_specs = [row(CONV_DIM_B), row(LANES), row(D_INNER_B)]
    args = [xbc, dt, z]
    if has_h0:
        in_specs.append(state_spec)
        args.append(h0)
    for cst in consts:
        in_specs.append(pl.BlockSpec(cst.shape, lambda b: (0, 0)))
        args.append(cst)
    out_specs = [row(D_INNER_B)]
    out_shape = [jax.ShapeDtypeStruct((n_batch * seq_len, D_INNER_B), F32)]
    if not has_h0:
        out_specs.append(state_spec)
        out_shape.append(jax.ShapeDtypeStruct((n_batch, 2, D_INNER_B, D_STATE), F32))
    scratch = [pltpu.VMEM((L, D_INNER_B), F32),
               pltpu.VMEM((nc, 2, D_STATE, D_INNER_B), F32),
               pltpu.VMEM((nc, 2, 8, D_INNER_B), F32),
               pltpu.VMEM((nc, 2, CHUNK, 2 * D_INNER_B), BF16),
               pltpu.VMEM((D_STATE, D_INNER_B), F32),
               pltpu.VMEM((D_STATE, D_INNER_B), F32)]
    cast_specs, cast_shapes = _ride_along_specs(ride_along, n_batch // nb)
    n_main_out = len(out_specs)
    outs = pl.pallas_call(
        functools.partial(_ssd_kernel, seq_len=seq_len, n_seq=nb, has_h0=has_h0, n_cast=len(ride_along)),
        grid=(n_batch // nb,),
        in_specs=in_specs + cast_specs,
        out_specs=out_specs + cast_specs,
        out_shape=out_shape + cast_shapes,
        scratch_shapes=scratch,
        compiler_params=_params(),
        name="bidir_ssd",
    )(*args, *ride_along)
    return outs[:n_main_out], outs[n_main_out:]


def _tail_kernel(*refs, n_in, n_ctx_tiles):
    ctx_a, lat_a = refs[:n_in], refs[n_in:2 * n_in]
    (xc_ref, xl_ref, mod_ref, wout_ref, wup_ref, wdown_ref, g1_ref, b1_ref, g2_ref, b2_ref,
     oc_ref, ol_ref) = refs[2 * n_in:]
    step = pl.program_id(0)

    tile = functools.partial(_mlp_tile, mod_ref=mod_ref, wout_ref=wout_ref, g1_ref=g1_ref, b1_ref=b1_ref,
                             wup_ref=wup_ref, wdown_ref=wdown_ref, g2_ref=g2_ref, b2_ref=b2_ref)

    @pl.when(step < n_ctx_tiles)
    def _():
        tile(ctx_a, xc_ref, oc_ref)

    @pl.when(step >= n_ctx_tiles)
    def _():
        tile(lat_a, xl_ref, ol_ref)


def _mlp_tile(a_refs, x_ref, o_ref, *, mod_ref, wout_ref, g1_ref, b1_ref, wup_ref, wdown_ref, g2_ref, b2_ref):
    m = mod_ref[0]
    n_chunks = D_FF // FF_CHUNK

    def out_proj(rows):
        d = None
        off = 0
        for a_ref in a_refs:
            n = a_ref.shape[1]
            part = _dot(a_ref[rows, :].astype(BF16), wout_ref[off:off + n, :])
            d = part if d is None else d + part
            off += n
        return d

    def head(rows, d):
        x1 = _layer_norm(ALPHA * x_ref[rows, :] + m[2:3] * d, g1_ref[...], b1_ref[...])
        return x1, (x1 * (1.0 + m[4:5]) + m[3:4]).astype(BF16)

    def up(h, c):
        return _dot(h, wup_ref[:, c * FF_CHUNK:(c + 1) * FF_CHUNK])

    def down(u, acc, c):
        u = jnp.maximum(u, 0.0)
        part = _dot((u * u).astype(BF16), wdown_ref[c * FF_CHUNK:(c + 1) * FF_CHUNK, :])
        return part if acc is None else acc + part

    def finish(rows, x1, acc):
        o_ref[rows, :] = _layer_norm(ALPHA * x1 + m[5:6] * acc, g2_ref[...], b2_ref[...])

    assert sum(TAIL_SPLIT) == x_ref.shape[0]
    k = len(TAIL_SPLIT)
    starts = [sum(TAIL_SPLIT[:t]) for t in range(k)]
    rows = [slice(s, s + n) for s, n in zip(starts, TAIL_SPLIT)]
    projected = [out_proj(r) for r in rows]
    heads = [None] * k
    heads[0] = head(rows[0], projected[0])
    u_next = up(heads[0][1], 0)
    done = None
    for t in range(k):
        if t + 1 < k:
            heads[t + 1] = head(rows[t + 1], projected[t + 1])
        acc = None
        for c in range(n_chunks):
            if c + 1 < n_chunks:
                nxt = up(heads[t][1], c + 1)
            else:
                nxt = up(heads[t + 1][1], 0) if t + 1 < k else None
            u_cur, u_next = u_next, nxt
            acc = down(u_cur, acc, c)
            if c == 0 and done is not None:
                finish(*done)
        done = (rows[t], heads[t][0], acc)
    finish(*done)


def _tail(a_ctx, a_lat, x_ctx, x_lat, mod, lat_seq_len, w_out, ln1_g, ln1_b, w_up, w_down, ln2_g, ln2_b):
    tm = TOKEN_TILE
    n_ctx = x_ctx.shape[0] // tm
    n_lat = x_lat.shape[0] // tm
    tiles_per_lat_seq = lat_seq_len // tm
    ctx_row = lambda n: pl.BlockSpec((tm, n), lambda i: (jnp.minimum(i, n_ctx - 1), 0))
    lat_row = lambda n: pl.BlockSpec((tm, n), lambda i: (jnp.maximum(i - n_ctx, 0), 0))
    lat_out = pl.BlockSpec((tm, D_MODEL), lambda i: (jnp.maximum(i - n_ctx, 0), 0))
    mod_spec = _mod_spec(lambda i: jnp.where(i < n_ctx, 0, 1 + (i - n_ctx) // tiles_per_lat_seq))
    vec = lambda v: v.reshape(1, D_MODEL)
    consts = [w_out, w_up, w_down, vec(ln1_g), vec(ln1_b), vec(ln2_g), vec(ln2_b)]
    return pl.pallas_call(
        functools.partial(_tail_kernel, n_in=len(a_ctx), n_ctx_tiles=n_ctx),
        grid=(n_ctx + n_lat,),
        in_specs=[ctx_row(a.shape[1]) for a in a_ctx] + [lat_row(a.shape[1]) for a in a_lat]
                 + [ctx_row(D_MODEL), lat_row(D_MODEL), mod_spec] + [_const_spec(v.shape) for v in consts],
        out_specs=[ctx_row(D_MODEL), lat_out],
        out_shape=[jax.ShapeDtypeStruct(x_ctx.shape, F32), jax.ShapeDtypeStruct(x_lat.shape, F32)],
        compiler_params=_params(),
        name="outproj_mlp",
    )(*a_ctx, *a_lat, x_ctx, x_lat, mod, *consts)


def _pool_kernel(x_ref, mod_ref, win_ref, pw_ref, ps_ref, o_ref, *, seq_len):
    L = seq_len
    m = mod_ref[0]
    t_i = lax.broadcasted_iota(jnp.int32, (L, L), 0)
    s_i = lax.broadcasted_iota(jnp.int32, (L, L), 1)
    t_g = lax.broadcasted_iota(jnp.int32, (L, POOL_GROUP_DIM), 0)
    bands, scales = [], []
    for w in POOL_WINDOWS:
        lo, hi = w // 2, w - w // 2
        bands.append(jnp.where((s_i >= t_i - lo) & (s_i < t_i + hi), 1.0 / w, 0.0).astype(BF16))
        scales.append(w / (jnp.minimum(t_g + hi, L) - jnp.maximum(t_g - lo, 0)).astype(F32))
    n_seq = x_ref.shape[0] // L
    seq_rows = [slice(s * L, (s + 1) * L) for s in range(n_seq)]
    hs = [(x_ref[r, :] * (1.0 + m[1:2]) + m[0:1]).astype(BF16) for r in seq_rows]
    us = [_dot(h, win_ref[...]) for h in hs]
    n_groups = len(POOL_WINDOWS)
    cols = [slice(g * POOL_GROUP_DIM, (g + 1) * POOL_GROUP_DIM) for g in range(n_groups)]

    def window_means(g):
        out = []
        for s in range(n_seq):
            ug = us[s][:, cols[g]]
            ug_hi = ug.astype(BF16)
            ug_lo = (ug - ug_hi.astype(F32)).astype(BF16)
            out.append((ug, _dot(bands[g], ug_hi) + _dot(bands[g], ug_lo)))
        return out

    def mix(g, means):
        return [_dot((wm * scales[g] - ug).astype(BF16), pw_ref[g]) * ps_ref[:, cols[g]] for ug, wm in means]

    mixed = []
    pending = window_means(0)
    for g in range(n_groups):
        nxt = window_means(g + 1) if g + 1 < n_groups else None
        mixed.append(mix(g, pending))
        pending = nxt
    for s in range(n_seq):
        o_ref[seq_rows[s], :] = jnp.concatenate([mixed[g][s] for g in range(n_groups)], axis=1)


def _pool_mixer(x, mod, row_of_tile, w_in, pool_w, pool_scale, seq_len, tm):
    t = x.shape[0]
    row = pl.BlockSpec((tm, D_MODEL), lambda i: (i, 0))
    consts = [w_in, pool_w, pool_scale.reshape(1, D_MODEL)]
    return pl.pallas_call(
        functools.partial(_pool_kernel, seq_len=seq_len),
        grid=(t // tm,),
        in_specs=[row, _mod_spec(row_of_tile)] + [_const_spec(cst.shape) for cst in consts],
        out_specs=row,
        out_shape=jax.ShapeDtypeStruct((t, D_MODEL), F32),
        compiler_params=_params(),
        name="pool_mixer",
    )(x, mod, *consts)


def _rope_tables(rows):
    r, col = np.meshgrid(np.arange(rows), np.arange(GRID_W), indexing="ij")
    r = r.reshape(-1).astype(np.float64)
    col = col.reshape(-1).astype(np.float64)
    n_freq = HEAD_DIM_A // 4
    inv = ROPE_BASE ** (-np.arange(n_freq, dtype=np.float64) / n_freq)
    ang = np.concatenate([r[:, None] * inv, col[:, None] * inv], -1)
    reps = 512 // HEAD_DIM_A
    cosf = np.tile(np.concatenate([np.cos(ang), np.cos(ang)], -1), (1, reps)).astype(np.float32)
    sinf = np.tile(np.concatenate([-np.sin(ang), np.sin(ang)], -1), (1, reps)).astype(np.float32)
    return jnp.asarray(cosf), jnp.asarray(sinf)


def kernel(x_prompt, x_sample, cache_k_l0, cache_v_l0, state_ssm_l0, c, c_ctx, w_mod_l0, b_mod_l0, w_in_l0, conv_w_l0, conv_b_l0, a_log_l0, dt_bias_l0, d_skip_l0, ssm_norm_w_l0, lam_q1_l0, lam_k1_l0, lam_q2_l0, lam_k2_l0, subln_w_l0, w_out_l0, ln1_g_l0, ln1_b_l0, w_up_l0, w_down_l0, ln2_g_l0, ln2_b_l0, w_mod_l1, b_mod_l1, w_in_l1, pool_w_l1, pool_scale_l1, w_out_l1, ln1_g_l1, ln1_b_l1, w_up_l1, w_down_l1, ln2_g_l1, ln2_b_l1):
    n_ctx, l_ctx, _ = x_prompt.shape
    n_lat, l_lat, _ = x_sample.shape
    past = cache_k_l0.shape[1]
    xc = x_prompt.reshape(n_ctx * l_ctx, D_MODEL)
    xl = x_sample.reshape(n_lat * l_lat, D_MODEL)

    cond8 = jnp.concatenate([c_ctx[None, :], c, jnp.zeros((8 - 1 - n_lat, D_MODEL), F32)], axis=0)
    ctx_row = lambda i: 0

    mod0 = _modulation(cond8, w_mod_l0, b_mod_l0)
    n_main = 4 * A_QK + A_V + D_INNER_B + CONV_DIM_B
    w_main = w_in_l0.astype(BF16)
    w_dt = jnp.pad(w_in_l0[:, n_main:], ((0, 0), (0, LANES - 2 * N_HEADS_B))).astype(BF16)
    lane_pad = lambda v: jnp.pad(v.reshape(1, -1), ((0, 0), (0, LANES - 2 * N_HEADS_B)))
    conv_consts = (conv_w_l0, conv_b_l0.reshape(1, CONV_DIM_B), lane_pad(dt_bias_l0))
    ssd_consts = [lane_pad(a_log_l0),
                  jnp.repeat(d_skip_l0, HEAD_DIM_B).reshape(1, D_INNER_B),
                  ssm_norm_w_l0.reshape(1, D_INNER_B)]
    lam_vecs = jnp.stack([lam_q1_l0, lam_k1_l0, lam_q2_l0, lam_k2_l0], axis=0)
    lam_init = 0.8 - 0.6 * math.exp(-0.3 * 0)

    (qc, kc, vc, zc, xbc_c, dt_c), (w_out0, w_up0, w_down0, w_out1, w_up1, w_down1, w_in1, pool_w) = _inproj0(
        xc, mod0, ctx_row, w_main, w_dt, *conv_consts, l_ctx, TOKEN_TILE,
        ride_along=(w_out_l0, w_up_l0, w_down_l0, w_out_l1, w_up_l1, w_down_l1, w_in_l1,
                    pool_w_l1.reshape(-1, POOL_GROUP_DIM)))
    pool_w = pool_w.reshape(pool_w_l1.shape)
    oa_c = _attention(qc, [kc], [vc], lam_vecs, subln_w_l0, l_ctx, l_ctx, CTX_SEQS_PER_STEP, lam_init)
    (y_c, new_ssm), _ = _ssd(xbc_c, dt_c, zc, None, ssd_consts, n_ctx, l_ctx, 1)

    (ql, kl, vl, zl, xbc_l, dt_l), _ = _inproj0(xl, mod0, lambda i: 1 + i, w_main, w_dt, *conv_consts, l_lat, l_lat,
                                                rope_tables=_rope_tables(l_lat // GRID_W))
    ck = jnp.transpose(cache_k_l0, (0, 2, 3, 4, 1))
    cv = cache_v_l0.reshape(n_lat * past * N_HEADS_A, 2 * HEAD_DIM_A)
    oa_l = _attention(ql, [ck, kl], [cv, vl], lam_vecs, subln_w_l0, l_lat, ATTN_Q_TILE, 1, lam_init)
    h0 = state_ssm_l0.reshape(n_lat, 2, D_INNER_B, D_STATE)
    ((y_l,), _) = _ssd(xbc_l, dt_l, zl, h0, ssd_consts, n_lat, l_lat, 1)
    tail0 = (w_out0, ln1_g_l0, ln1_b_l0, w_up0, w_down0, ln2_g_l0, ln2_b_l0)
    xc, xl = _tail([oa_c, y_c], [oa_l, y_l], xc, xl, mod0, l_lat, *tail0)

    mod1 = _modulation(cond8, w_mod_l1, b_mod_l1)
    tail1 = (w_out1, ln1_g_l1, ln1_b_l1, w_up1, w_down1, ln2_g_l1, ln2_b_l1)
    mix_c = _pool_mixer(xc, mod1, ctx_row, w_in1, pool_w, pool_scale_l1, l_ctx, TOKEN_TILE)
    mix_l = _pool_mixer(xl, mod1, lambda i: 1 + i, w_in1, pool_w, pool_scale_l1, l_lat, l_lat)
    xc, xl = _tail([mix_c], [mix_l], xc, xl, mod1, l_lat, *tail1)

    return (xc.reshape(n_ctx, l_ctx, D_MODEL), xl.reshape(n_lat, l_lat, D_MODEL),
            jnp.transpose(kc, (0, 4, 1, 2, 3)),
            vc.reshape(n_ctx, l_ctx, N_HEADS_A, 2 * HEAD_DIM_A),
            new_ssm.reshape(n_ctx, 2, N_HEADS_B, HEAD_DIM_B, D_STATE))
```

```python
import functools
import math

import jax
import jax.numpy as jnp
import numpy as np
from jax import lax
from jax.experimental import pallas as pl
from jax.experimental.pallas import tpu as pltpu

F32 = jnp.float32
BF16 = jnp.bfloat16

D_MODEL = 1024
N_MOD = 6
N_HEADS_A = 4
HEAD_DIM_A = 64
A_QK = N_HEADS_A * HEAD_DIM_A
A_V = 2 * A_QK
D_INNER_B = 512
HEAD_DIM_B = 64
N_HEADS_B = 8
D_STATE = 128
D_CONV = 5
CONV_DIM_B = 1024
CHUNK = 128
GRID_W = 64
POOL_WINDOWS = (2, 4, 8, 16)
POOL_GROUP_DIM = 256
D_FF = 4096
ROPE_BASE = 10000.0
LN_EPS = 1e-5
DEPTH = 2
ALPHA = (2 * DEPTH) ** 0.25
NEG_BIG = -1e30
LOG2E = 1.4426950408889634

LANES = 128
TOKEN_TILE = 512
ATTN_Q_TILE = 256
CTX_SEQS_PER_STEP = 4
FF_CHUNK = 1024
CONV_COL_BLOCK = 256
TAIL_SPLIT = (256, 256)
VMEM_LIMIT = 56 * 1024 * 1024


def _dot(a, b):
    return jnp.dot(a, b, preferred_element_type=F32)


def _dot_nt(a, b):
    return lax.dot_general(a, b, (((1,), (1,)), ((), ())), preferred_element_type=F32)


def _bf16_pieces(a):
    a1 = a.astype(BF16)
    r1 = a - a1.astype(F32)
    a2 = r1.astype(BF16)
    a3 = (r1 - a2.astype(F32)).astype(BF16)
    return a1, a2, a3


def _dot_exact_rhs(t01, a):
    a1, a2, a3 = _bf16_pieces(a)
    return _dot(t01, a1) + _dot(t01, a2) + _dot(t01, a3)


def _sigmoid(x):
    return 1.0 / (1.0 + jnp.exp(-x))


def _silu(x):
    return x * _sigmoid(x)


def _layer_norm(x, g, b):
    mu = jnp.mean(x, axis=-1, keepdims=True)
    xc = x - mu
    var = jnp.mean(xc * xc, axis=-1, keepdims=True)
    return xc * lax.rsqrt(var + LN_EPS) * g + b


def _const_spec(shape):
    nd = len(shape)
    return pl.BlockSpec(shape, lambda *_: (0,) * nd, pipeline_mode=pl.Buffered(1))


def _params(n_axes=1):
    return pltpu.CompilerParams(dimension_semantics=("arbitrary",) * n_axes,
                                vmem_limit_bytes=VMEM_LIMIT)


def _mod_kernel(cond_ref, w_ref, b_ref, o_ref):
    c = cond_ref[...]
    s = _silu(c).astype(BF16)
    o_ref[...] = _dot(s, w_ref[...].astype(BF16)) + b_ref[...]


def _modulation(cond8, w_mod, b_mod):
    tn = 1024
    n = w_mod.shape[1]
    out = pl.pallas_call(
        _mod_kernel,
        grid=(n // tn,),
        in_specs=[pl.BlockSpec((8, D_MODEL), lambda j: (0, 0)),
                  pl.BlockSpec((D_MODEL, tn), lambda j: (0, j)),
                  pl.BlockSpec((1, tn), lambda j: (0, j))],
        out_specs=pl.BlockSpec((8, tn), lambda j: (0, j)),
        out_shape=jax.ShapeDtypeStruct((8, n), F32),
        compiler_params=_params(),
        name="modulation",
    )(cond8, w_mod, b_mod.reshape(1, n))
    return out.reshape(8, N_MOD, D_MODEL)


def _mod_spec(row_of_tile):
    return pl.BlockSpec((1, N_MOD, D_MODEL), lambda i: (row_of_tile(i), 0, 0))


def _rope(t, cosf, sinf):
    lane = lax.broadcasted_iota(jnp.int32, t.shape, 1)
    first = (lane & (HEAD_DIM_A - 1)) < HEAD_DIM_A // 2
    width = t.shape[1]
    swapped = jnp.where(first, pltpu.roll(t, width - HEAD_DIM_A // 2, 1),
                        pltpu.roll(t, HEAD_DIM_A // 2, 1))
    return t * cosf + swapped * sinf


def _conv_silu(xbc, cw_ref, cb_ref, cols, seq_len):
    assert xbc.shape[0] == seq_len
    half = D_CONV // 2
    edge = 8
    offsets = [j - half for j in range(D_CONV) if j != half]
    shifted = {o: pltpu.roll(xbc, (-o) % seq_len, 0) for o in offsets}

    def taps(lo, hi, masked):
        acc = cb_ref[:, cols] + cw_ref[half:half + 1, cols] * xbc[lo:hi]
        pos = lo + lax.broadcasted_iota(jnp.int32, (hi - lo, xbc.shape[1]), 0)
        for o in offsets:
            s = shifted[o][lo:hi]
            if masked:
                s = jnp.where((pos >= -o) if o < 0 else (pos < seq_len - o), s, 0.0)
            acc = acc + cw_ref[o + half:o + half + 1, cols] * s
        return acc

    acc = jnp.concatenate([taps(0, edge, True), taps(edge, seq_len - edge, False),
                           taps(seq_len - edge, seq_len, True)], axis=0)
    return _silu(acc)


def _softplus(t):
    return jnp.maximum(t, 0.0) + jnp.log(1.0 + jnp.exp(-jnp.abs(t)))


def _ride_along_specs(mats, n_steps):
    specs, shapes = [], []
    for w in mats:
        assert w.shape[0] % n_steps == 0
        specs.append(pl.BlockSpec((w.shape[0] // n_steps, w.shape[1]), lambda i: (i, 0)))
        shapes.append(jax.ShapeDtypeStruct(w.shape, BF16))
    return specs, shapes


def _ride_along_cast(src_refs, dst_refs):
    for src, dst in zip(src_refs, dst_refs):
        dst[...] = src[...].astype(BF16)


def _inproj0_kernel(*refs, rope, seq_len, n_cast):
    n_in = 9 if rope else 7
    ins, cast_in = refs[:n_in], refs[n_in:n_in + n_cast]
    (q_ref, k_ref, v_ref, z_ref, xbc_ref, dt_ref) = refs[n_in + n_cast:n_in + n_cast + 6]
    cast_out = refs[n_in + n_cast + 6:]
    if rope:
        x_ref, mod_ref, w_ref, wdt_ref, cw_ref, cb_ref, dtb_ref, cos_ref, sin_ref = ins
    else:
        x_ref, mod_ref, w_ref, wdt_ref, cw_ref, cb_ref, dtb_ref = ins
    _ride_along_cast(cast_in, cast_out)
    m = mod_ref[0]
    tasks = []
    for b in range(x_ref.shape[0] // seq_len):
        rows = slice(b * seq_len, (b + 1) * seq_len)
        h = (x_ref[rows, :] * (1.0 + m[1:2]) + m[0:1]).astype(BF16)

        for c0 in range(0, CONV_DIM_B, CONV_COL_BLOCK):
            cols = slice(c0, c0 + CONV_COL_BLOCK)

            def conv_mm(h=h, c0=c0):
                return _dot(h, w_ref[:, 2048 + c0:2048 + c0 + CONV_COL_BLOCK])

            def conv_vec(raw, rows=rows, cols=cols):
                xbc_ref[rows, cols] = _conv_silu(raw, cw_ref, cb_ref, cols, seq_len)

            tasks.append((conv_mm, conv_vec))

        def qk_mm(h=h):
            return _dot(h, wdt_ref[...]), _dot(h, w_ref[:, 0:512]), _dot(h, w_ref[:, 512:1024])

        def qk_vec(res, rows=rows, b=b):
            dt_raw, q, k = res
            dt_ref[rows, :] = _softplus(dt_raw + dtb_ref[...])
            if rope:
                q = _rope(q, cos_ref[...], sin_ref[...])
                k = _rope(k, cos_ref[...], sin_ref[...])
            q_ref[rows, :] = q
            for blk in range(2 * A_QK // LANES):
                t = k[:, blk * LANES:(blk + 1) * LANES].T
                which, head = blk // 2, (blk % 2) * 2
                k_ref[b, which, head] = t[0:HEAD_DIM_A]
                k_ref[b, which, head + 1] = t[HEAD_DIM_A:2 * HEAD_DIM_A]

        def vz_mm(h=h):
            return _dot(h, w_ref[:, 1024:1536]), _dot(h, w_ref[:, 1536:2048])

        def vz_vec(res, rows=rows, b=b):
            vals, z = res
            for hh in range(N_HEADS_A):
                v_ref[pl.ds(b * seq_len * N_HEADS_A + hh, seq_len, stride=N_HEADS_A), :] = (
                    vals[:, hh * LANES:(hh + 1) * LANES])
            z_ref[rows, :] = z

        tasks += [(qk_mm, qk_vec), (vz_mm, vz_vec)]

    pending = None
    for mm, vec in tasks:
        res = mm()
        if pending is not None:
            pending[0](pending[1])
        pending = (vec, res)
    pending[0](pending[1])


def _inproj0(x, mod, row_of_tile, w_main, w_dt, conv_w, conv_b, dt_bias, seq_len, tm, rope_tables=None,
             ride_along=()):
    t = x.shape[0]
    assert tm % seq_len == 0
    row = lambda n: pl.BlockSpec((tm, n), lambda i: (i, 0))
    flat = lambda n: jax.ShapeDtypeStruct((t, n), F32)
    consts = [w_main, w_dt, conv_w, conv_b, dt_bias]
    in_specs = [row(D_MODEL), _mod_spec(row_of_tile)] + [_const_spec(cst.shape) for cst in consts]
    args = [x, mod] + consts
    if rope_tables is not None:
        assert tm == seq_len
        in_specs += [_const_spec(tab.shape) for tab in rope_tables]
        args += list(rope_tables)
    kt_dims = (2, N_HEADS_A, HEAD_DIM_A, seq_len)
    k_spec = pl.BlockSpec((tm // seq_len,) + kt_dims, lambda i: (i, 0, 0, 0, 0))
    k_shape = jax.ShapeDtypeStruct((t // seq_len,) + kt_dims, F32)
    cast_specs, cast_shapes = _ride_along_specs(ride_along, t // tm)
    outs = pl.pallas_call(
        functools.partial(_inproj0_kernel, rope=rope_tables is not None, seq_len=seq_len, n_cast=len(ride_along)),
        grid=(t // tm,),
        in_specs=in_specs + cast_specs,
        out_specs=[row(512), k_spec, pl.BlockSpec((tm * N_HEADS_A, LANES), lambda i: (i, 0)),
                   row(512), row(CONV_DIM_B), row(LANES)] + cast_specs,
        out_shape=[flat(512), k_shape, jax.ShapeDtypeStruct((t * N_HEADS_A, LANES), F32),
                   flat(512), flat(CONV_DIM_B), flat(LANES)] + cast_shapes,
        compiler_params=_params(),
        name="inproj0",
    )(*args, *ride_along)
    return outs[:6], outs[6:]


def _lambda_full(lam_ref, lam_init):
    lv = lam_ref[...]
    return (jnp.exp(jnp.sum(lv[0:1] * lv[1:2], axis=1, keepdims=True))
            - jnp.exp(jnp.sum(lv[2:3] * lv[3:4], axis=1, keepdims=True)) + lam_init)


def _attn_kernel(*refs, n_seg, tq, lam_init):
    q_ref = refs[0]
    kt_refs = refs[1:1 + n_seg]
    v_refs = refs[1 + n_seg:1 + 2 * n_seg]
    lam_ref, sub_ref, o_ref = refs[1 + 2 * n_seg:]
    lam = _lambda_full(lam_ref, lam_init)
    lane = lax.broadcasted_iota(jnp.int32, (1, A_QK), 1)
    masks = [(lane >= h * HEAD_DIM_A) & (lane < (h + 1) * HEAD_DIM_A) for h in range(N_HEADS_A)]
    keys = [r.shape[-1] for r in kt_refs]

    def stacked(qx):
        return jnp.concatenate([jnp.where(m, qx, 0.0) for m in masks], axis=0).astype(BF16)

    def exp_scores(qx, b, which):
        qs = stacked(qx)
        s = [_dot(qs, r[b, which].reshape(A_QK, n).astype(BF16)) for r, n in zip(kt_refs, keys)]
        mx = functools.reduce(jnp.maximum, [jnp.max(x, axis=1, keepdims=True) for x in s])
        return [jnp.exp2(x - mx).astype(BF16) for x in s]

    outs = []
    for b in range(kt_refs[0].shape[0]):
        q = q_ref[b * tq:(b + 1) * tq, :] * (HEAD_DIM_A ** -0.5 * LOG2E)
        e1 = exp_scores(q[:, 0:A_QK], b, 0)
        e2 = exp_scores(q[:, A_QK:2 * A_QK], b, 1)
        heads = []
        for h in range(N_HEADS_A):
            hrows = slice(h * tq, (h + 1) * tq)
            n1 = n2 = None
            for s, (v_ref, n) in enumerate(zip(v_refs, keys)):
                v_h = v_ref[pl.ds(b * n * N_HEADS_A + h, n, stride=N_HEADS_A), :]
                v_ext = jnp.concatenate([v_h.astype(BF16),
                                         jnp.ones((n, LANES), BF16)], axis=1)
                p1 = _dot(e1[s][hrows], v_ext)
                p2 = _dot(e2[s][hrows], v_ext)
                n1 = p1 if n1 is None else n1 + p1
                n2 = p2 if n2 is None else n2 + p2
            heads.append(n1[:, 0:LANES] / n1[:, LANES:] - lam * (n2[:, 0:LANES] / n2[:, LANES:]))
        o = jnp.concatenate(heads, axis=0)
        ms = jnp.mean(o * o, axis=1, keepdims=True)
        o = (o * lax.rsqrt(ms + 1e-5) * sub_ref[...]) * (1.0 - lam_init)
        outs.append(jnp.concatenate([o[h * tq:(h + 1) * tq] for h in range(N_HEADS_A)], axis=1))
    o_ref[...] = jnp.concatenate(outs, axis=0)


def _attention(q, kts, vs, lam_vecs, subln_w, q_len, tq, seqs_per_step, lam_init):
    n_seq = kts[0].shape[0]
    nb = seqs_per_step
    tiles = q_len // tq
    assert tiles == 1 or nb == 1
    qspec = pl.BlockSpec((nb * tq, 512), lambda b, i: (b * tiles + i, 0))
    ktspecs = [pl.BlockSpec((nb,) + kt.shape[1:], lambda b, i: (b, 0, 0, 0, 0)) for kt in kts]
    vspecs = [pl.BlockSpec((nb * kt.shape[-1] * N_HEADS_A, LANES), lambda b, i: (b, 0)) for kt in kts]
    return pl.pallas_call(
        functools.partial(_attn_kernel, n_seg=len(kts), tq=tq, lam_init=lam_init),
        grid=(n_seq // nb, tiles),
        in_specs=[qspec] + ktspecs + vspecs + [pl.BlockSpec((4, HEAD_DIM_A), lambda b, i: (0, 0)),
                                             pl.BlockSpec((1, LANES), lambda b, i: (0, 0))],
        out_specs=qspec,
        out_shape=jax.ShapeDtypeStruct(q.shape, F32),
        compiler_params=_params(2),
        name="diff_attention",
    )(q, *kts, *vs, lam_vecs, subln_w.reshape(1, LANES))


def _chunk_loop(n, body, unroll=1):
    if n <= 2:
        for c in range(n):
            body(c)
    else:
        def step(c, carry):
            body(c)
            return carry
        lax.fori_loop(0, n, step, 0, unroll=unroll)


def _ssd_kernel(*refs, seq_len, n_seq, has_h0, n_cast):
    n_in = 7 if has_h0 else 6
    n_out = 1 if has_h0 else 2
    ins, cast_in = refs[:n_in], refs[n_in:n_in + n_cast]
    outs = refs[n_in + n_cast:n_in + n_cast + n_out]
    cast_out = refs[n_in + n_cast + n_out:n_in + 2 * n_cast + n_out]
    ysc, s_sc, cd_sc, e_sc, htf, htb = refs[n_in + 2 * n_cast + n_out:]
    if has_h0:
        xact, dtv, z_ref, h0_ref, arow_ref, dsk_ref, nw_ref = ins
        (y_ref,), hfin_ref = outs, None
    else:
        xact, dtv, z_ref, arow_ref, dsk_ref, nw_ref = ins
        (y_ref, hfin_ref), h0_ref = outs, None
    _ride_along_cast(cast_in, cast_out)
    nc = seq_len // CHUNK
    n_pairs = N_HEADS_B // 2
    a_row = -jnp.exp(arow_ref[...]) * LOG2E

    row_i = lax.broadcasted_iota(jnp.int32, (CHUNK, CHUNK), 0)
    col_i = lax.broadcasted_iota(jnp.int32, (CHUNK, CHUNK), 1)
    lower = col_i <= row_i
    upper = col_i >= row_i
    tri = (jnp.where(lower, 1.0, 0.0).astype(BF16), jnp.where(upper, 1.0, 0.0).astype(BF16))
    lo_half = col_i < HEAD_DIM_B

    def split_pair(m):
        zero = jnp.zeros_like(m)
        return jnp.concatenate([jnp.where(lo_half, m, zero), jnp.where(lo_half, zero, m)], axis=0).astype(BF16)

    def decay_terms(c):
        rows = pl.ds(pl.multiple_of(c * CHUNK, CHUNK), CHUNK)
        dtc = dtv[rows, :]
        dtt = dtc.T[0:16, :]
        per_dir = []
        for d in range(2):
            col = _dot_exact_rhs(tri[d], dtc * a_row)
            rowm = col.T[0:16, :]
            far = CHUNK - 1 if d == 0 else 0
            tot = jnp.broadcast_to(rowm[:, far:far + 1], (16, CHUNK))
            dte = jnp.exp2(tot - rowm) * dtt
            per_dir.append((col, rowm - jnp.log2(dtt), dte, jnp.exp2(tot)))
        return per_dir

    def local_phase(c, per_dir=None):
        rows = pl.ds(pl.multiple_of(c * CHUNK, CHUNK), CHUNK)
        if per_dir is None:
            per_dir = decay_terms(c)
        for g in range(2):
            bg = xact[rows, 512 + g * LANES:512 + (g + 1) * LANES]
            cg = xact[rows, 768 + g * LANES:768 + (g + 1) * LANES]
            cbm = _dot_nt(cg.astype(BF16), bg.astype(BF16))
            bgt = bg.T
            for pq in range(2):
                pi = g * 2 + pq
                lanes = slice(pi * LANES, (pi + 1) * LANES)
                xrhs = split_pair(xact[rows, lanes])
                y_pair = None
                for d in range(2):
                    col, row_dt, dte, cdec = per_dir[d]
                    causal = lower if d == 0 else upper
                    m_parts, col_parts, bw_parts, cd_parts = [], [], [], []
                    for hh in (2 * pi, 2 * pi + 1):
                        hd = 8 * d + hh
                        colb = jnp.sum(jnp.where(col_i == hd, col, 0.0), axis=1, keepdims=True)
                        m_parts.append(cbm * jnp.exp2(jnp.where(causal, colb - row_dt[hd:hd + 1, :], NEG_BIG)))
                        col_parts.append(colb)
                        bw_parts.append(bgt * dte[hd:hd + 1, :])
                        cd_parts.append(cdec[hd:hd + 1, :])
                    yd = _dot(jnp.concatenate(m_parts, axis=1).astype(BF16), xrhs)
                    y_pair = yd if y_pair is None else y_pair + yd
                    s_sc[c, d, :, lanes] = _dot(jnp.concatenate(bw_parts, axis=1).astype(BF16), xrhs)
                    e_sc[c, d, :, lanes] = jnp.exp2(jnp.where(lo_half, col_parts[0], col_parts[1]))
                    cd_sc[c, d, :, lanes] = jnp.broadcast_to(
                        jnp.where(lo_half[0:1, :], cd_parts[0], cd_parts[1]), (8, LANES))
                ysc[rows, lanes] = y_pair

    if n_seq * nc <= 2:
        terms = [decay_terms(c) for c in range(n_seq * nc)]
        for c in range(n_seq * nc):
            local_phase(c, terms[c])
    else:
        _chunk_loop(n_seq * nc, local_phase, unroll=2)

    def scan_sequence(sq):
        for d, ht in ((0, htf), (1, htb)):
            for qd in range(n_pairs):
                lanes = slice(qd * LANES, (qd + 1) * LANES)
                if has_h0:
                    ht[:, lanes] = h0_ref[sq, d, lanes, :].T
                else:
                    ht[:, lanes] = jnp.zeros((D_STATE, LANES), F32)

        def scan_phase(j):
            for d, ht in ((0, htf), (1, htb)):
                c = sq * nc + (j if d == 0 else nc - 1 - j)
                rows = pl.ds(pl.multiple_of(c * CHUNK, CHUNK), CHUNK)
                for pi in range(n_pairs):
                    lanes = slice(pi * LANES, (pi + 1) * LANES)
                    g = pi // 2
                    cg = xact[rows, 768 + g * LANES:768 + (g + 1) * LANES].astype(BF16)
                    hprev = ht[:, lanes]
                    ysc[rows, lanes] += _dot(cg, hprev.astype(BF16)) * e_sc[c, d, :, lanes]
                    ht[:, lanes] = hprev * cd_sc[c, d, 0:1, lanes] + s_sc[c, d, :, lanes]

        _chunk_loop(nc, scan_phase)
        if hfin_ref is not None:
            for d, ht in ((0, htf), (1, htb)):
                for qd in range(n_pairs):
                    lanes = slice(qd * LANES, (qd + 1) * LANES)
                    hfin_ref[sq, d, lanes, :] = ht[:, lanes].T

    if n_seq == 1:
        scan_sequence(0)
    else:
        def seq_step(sq, carry):
            scan_sequence(sq)
            return carry
        lax.fori_loop(0, n_seq, seq_step, 0)

    def finish(c):
        rows = pl.ds(pl.multiple_of(c * CHUNK, CHUNK), CHUNK)
        y = ysc[rows, :] + dsk_ref[...] * xact[rows, 0:D_INNER_B]
        y = y * _silu(z_ref[rows, :])
        ms = jnp.mean(y * y, axis=1, keepdims=True)
        y_ref[rows, :] = y * lax.rsqrt(ms + 1e-5) * nw_ref[...]

    _chunk_loop(n_seq * nc, finish)


def _ssd(xbc, dt, z, h0, consts, n_batch, seq_len, seqs_per_step, ride_along=()):
    nb = seqs_per_step
    L = nb * seq_len
    nc = L // CHUNK
    row = lambda n: pl.BlockSpec((L, n), lambda b: (b, 0))
    state_spec = pl.BlockSpec((nb, 2, D_INNER_B, D_STATE), lambda b: (b, 0, 0, 0))
    has_h0 = h0 is not None
    in_specs = [row(CONV_DIM_B), row(LANES), row(D_INNER_B)]
    args = [xbc, dt, z]
    if has_h0:
        in_specs.append(state_spec)
        args.append(h0)
    for cst in consts:
        in_specs.append(pl.BlockSpec(cst.shape, lambda b: (0, 0)))
        args.append(cst)
    out_specs = [row(D_INNER_B)]
    out_shape = [jax.ShapeDtypeStruct((n_batch * seq_len, D_INNER_B), F32)]
    if not has_h0:
        out_specs.append(state_spec)
        out_shape.append(jax.ShapeDtypeStruct((n_batch, 2, D_INNER_B, D_STATE), F32))
    scratch = [pltpu.VMEM((L, D_INNER_B), F32),
               pltpu.VMEM((nc, 2, D_STATE, D_INNER_B), F32),
               pltpu.VMEM((nc, 2, 8, D_INNER_B), F32),
               pltpu.VMEM((nc, 2, CHUNK, D_INNER_B), F32),
               pltpu.VMEM((D_STATE, D_INNER_B), F32),
               pltpu.VMEM((D_STATE, D_INNER_B), F32)]
    cast_specs, cast_shapes = _ride_along_specs(ride_along, n_batch // nb)
    n_main_out = len(out_specs)
    outs = pl.pallas_call(
        functools.partial(_ssd_kernel, seq_len=seq_len, n_seq=nb, has_h0=has_h0, n_cast=len(ride_along)),
        grid=(n_batch // nb,),
        in_specs=in_specs + cast_specs,
        out_specs=out_specs + cast_specs,
        out_shape=out_shape + cast_shapes,
        scratch_shapes=scratch,
        compiler_params=_params(),
        name="bidir_ssd",
    )(*args, *ride_along)
    return outs[:n_main_out], outs[n_main_out:]


def _tail_kernel(*refs, n_in, n_ctx_tiles):
    ctx_a, lat_a = refs[:n_in], refs[n_in:2 * n_in]
    (xc_ref, xl_ref, mod_ref, wout_ref, wup_ref, wdown_ref, g1_ref, b1_ref, g2_ref, b2_ref,
     oc_ref, ol_ref) = refs[2 * n_in:]
    step = pl.program_id(0)

    tile = functools.partial(_mlp_tile, mod_ref=mod_ref, wout_ref=wout_ref, g1_ref=g1_ref, b1_ref=b1_ref,
                             wup_ref=wup_ref, wdown_ref=wdown_ref, g2_ref=g2_ref, b2_ref=b2_ref)

    @pl.when(step < n_ctx_tiles)
    def _():
        tile(ctx_a, xc_ref, oc_ref)

    @pl.when(step >= n_ctx_tiles)
    def _():
        tile(lat_a, xl_ref, ol_ref)


def _mlp_tile(a_refs, x_ref, o_ref, *, mod_ref, wout_ref, g1_ref, b1_ref, wup_ref, wdown_ref, g2_ref, b2_ref):
    m = mod_ref[0]
    n_chunks = D_FF // FF_CHUNK

    def out_proj(rows):
        d = None
        off = 0
        for a_ref in a_refs:
            n = a_ref.shape[1]
            part = _dot(a_ref[rows, :].astype(BF16), wout_ref[off:off + n, :])
            d = part if d is None else d + part
            off += n
        return d

    def head(rows, d):
        x1 = _layer_norm(ALPHA * x_ref[rows, :] + m[2:3] * d, g1_ref[...], b1_ref[...])
        return x1, (x1 * (1.0 + m[4:5]) + m[3:4]).astype(BF16)

    def up(h, c):
        return _dot(h, wup_ref[:, c * FF_CHUNK:(c + 1) * FF_CHUNK])

    def down(u, acc, c):
        u = jnp.maximum(u, 0.0)
        part = _dot((u * u).astype(BF16), wdown_ref[c * FF_CHUNK:(c + 1) * FF_CHUNK, :])
        return part if acc is None else acc + part

    def finish(rows, x1, acc):
        o_ref[rows, :] = _layer_norm(ALPHA * x1 + m[5:6] * acc, g2_ref[...], b2_ref[...])

    assert sum(TAIL_SPLIT) == x_ref.shape[0]
    k = len(TAIL_SPLIT)
    starts = [sum(TAIL_SPLIT[:t]) for t in range(k)]
    rows = [slice(s, s + n) for s, n in zip(starts, TAIL_SPLIT)]
    projected = [out_proj(r) for r in rows]
    heads = [None] * k
    heads[0] = head(rows[0], projected[0])
    u_next = up(heads[0][1], 0)
    done = None
    for t in range(k):
        if t + 1 < k:
            heads[t + 1] = head(rows[t + 1], projected[t + 1])
        acc = None
        for c in range(n_chunks):
            if c + 1 < n_chunks:
                nxt = up(heads[t][1], c + 1)
            else:
                nxt = up(heads[t + 1][1], 0) if t + 1 < k else None
            u_cur, u_next = u_next, nxt
            acc = down(u_cur, acc, c)
            if c == 0 and done is not None:
                finish(*done)
        done = (rows[t], heads[t][0], acc)
    finish(*done)


def _tail(a_ctx, a_lat, x_ctx, x_lat, mod, lat_seq_len, w_out, ln1_g, ln1_b, w_up, w_down, ln2_g, ln2_b):
    tm = TOKEN_TILE
    n_ctx = x_ctx.shape[0] // tm
    n_lat = x_lat.shape[0] // tm
    tiles_per_lat_seq = lat_seq_len // tm
    ctx_row = lambda n: pl.BlockSpec((tm, n), lambda i: (jnp.minimum(i, n_ctx - 1), 0))
    lat_row = lambda n: pl.BlockSpec((tm, n), lambda i: (jnp.maximum(i - n_ctx, 0), 0))
    lat_out = pl.BlockSpec((tm, D_MODEL), lambda i: (jnp.maximum(i - n_ctx, 0), 0))
    mod_spec = _mod_spec(lambda i: jnp.where(i < n_ctx, 0, 1 + (i - n_ctx) // tiles_per_lat_seq))
    vec = lambda v: v.reshape(1, D_MODEL)
    consts = [w_out, w_up, w_down, vec(ln1_g), vec(ln1_b), vec(ln2_g), vec(ln2_b)]
    return pl.pallas_call(
        functools.partial(_tail_kernel, n_in=len(a_ctx), n_ctx_tiles=n_ctx),
        grid=(n_ctx + n_lat,),
        in_specs=[ctx_row(a.shape[1]) for a in a_ctx] + [lat_row(a.shape[1]) for a in a_lat]
                 + [ctx_row(D_MODEL), lat_row(D_MODEL), mod_spec] + [_const_spec(v.shape) for v in consts],
        out_specs=[ctx_row(D_MODEL), lat_out],
        out_shape=[jax.ShapeDtypeStruct(x_ctx.shape, F32), jax.ShapeDtypeStruct(x_lat.shape, F32)],
        compiler_params=_params(),
        name="outproj_mlp",
    )(*a_ctx, *a_lat, x_ctx, x_lat, mod, *consts)


def _pool_kernel(x_ref, mod_ref, win_ref, pw_ref, ps_ref, o_ref, *, seq_len):
    L = seq_len
    m = mod_ref[0]
    t_i = lax.broadcasted_iota(jnp.int32, (L, L), 0)
    s_i = lax.broadcasted_iota(jnp.int32, (L, L), 1)
    t_g = lax.broadcasted_iota(jnp.int32, (L, POOL_GROUP_DIM), 0)
    bands, scales = [], []
    for w in POOL_WINDOWS:
        lo, hi = w // 2, w - w // 2
        bands.append(jnp.where((s_i >= t_i - lo) & (s_i < t_i + hi), 1.0 / w, 0.0).astype(BF16))
        scales.append(w / (jnp.minimum(t_g + hi, L) - jnp.maximum(t_g - lo, 0)).astype(F32))
    n_seq = x_ref.shape[0] // L
    seq_rows = [slice(s * L, (s + 1) * L) for s in range(n_seq)]
    hs = [(x_ref[r, :] * (1.0 + m[1:2]) + m[0:1]).astype(BF16) for r in seq_rows]
    us = [_dot(h, win_ref[...]) for h in hs]
    n_groups = len(POOL_WINDOWS)
    cols = [slice(g * POOL_GROUP_DIM, (g + 1) * POOL_GROUP_DIM) for g in range(n_groups)]

    def window_means(g):
        out = []
        for s in range(n_seq):
            ug = us[s][:, cols[g]]
            ug_hi = ug.astype(BF16)
            ug_lo = (ug - ug_hi.astype(F32)).astype(BF16)
            out.append((ug, _dot(bands[g], ug_hi) + _dot(bands[g], ug_lo)))
        return out

    def mix(g, means):
        return [_dot((wm * scales[g] - ug).astype(BF16), pw_ref[g]) * ps_ref[:, cols[g]] for ug, wm in means]

    mixed = []
    pending = window_means(0)
    for g in range(n_groups):
        nxt = window_means(g + 1) if g + 1 < n_groups else None
        mixed.append(mix(g, pending))
        pending = nxt
    for s in range(n_seq):
        o_ref[seq_rows[s], :] = jnp.concatenate([mixed[g][s] for g in range(n_groups)], axis=1)


def _pool_mixer(x, mod, row_of_tile, w_in, pool_w, pool_scale, seq_len, tm):
    t = x.shape[0]
    row = pl.BlockSpec((tm, D_MODEL), lambda i: (i, 0))
    consts = [w_in, pool_w, pool_scale.reshape(1, D_MODEL)]
    return pl.pallas_call(
        functools.partial(_pool_kernel, seq_len=seq_len),
        grid=(t // tm,),
        in_specs=[row, _mod_spec(row_of_tile)] + [_const_spec(cst.shape) for cst in consts],
        out_specs=row,
        out_shape=jax.ShapeDtypeStruct((t, D_MODEL), F32),
        compiler_params=_params(),
        name="pool_mixer",
    )(x, mod, *consts)


def _rope_tables(rows):
    r, col = np.meshgrid(np.arange(rows), np.arange(GRID_W), indexing="ij")
    r = r.reshape(-1).astype(np.float64)
    col = col.reshape(-1).astype(np.float64)
    n_freq = HEAD_DIM_A // 4
    inv = ROPE_BASE ** (-np.arange(n_freq, dtype=np.float64) / n_freq)
    ang = np.concatenate([r[:, None] * inv, col[:, None] * inv], -1)
    reps = 512 // HEAD_DIM_A
    cosf = np.tile(np.concatenate([np.cos(ang), np.cos(ang)], -1), (1, reps)).astype(np.float32)
    sinf = np.tile(np.concatenate([-np.sin(ang), np.sin(ang)], -1), (1, reps)).astype(np.float32)
    return jnp.asarray(cosf), jnp.asarray(sinf)


def kernel(x_prompt, x_sample, cache_k_l0, cache_v_l0, state_ssm_l0, c, c_ctx, w_mod_l0, b_mod_l0, w_in_l0, conv_w_l0, conv_b_l0, a_log_l0, dt_bias_l0, d_skip_l0, ssm_norm_w_l0, lam_q1_l0, lam_k1_l0, lam_q2_l0, lam_k2_l0, subln_w_l0, w_out_l0, ln1_g_l0, ln1_b_l0, w_up_l0, w_down_l0, ln2_g_l0, ln2_b_l0, w_mod_l1, b_mod_l1, w_in_l1, pool_w_l1, pool_scale_l1, w_out_l1, ln1_g_l1, ln1_b_l1, w_up_l1, w_down_l1, ln2_g_l1, ln2_b_l1):
    n_ctx, l_ctx, _ = x_prompt.shape
    n_lat, l_lat, _ = x_sample.shape
    past = cache_k_l0.shape[1]
    xc = x_prompt.reshape(n_ctx * l_ctx, D_MODEL)
    xl = x_sample.reshape(n_lat * l_lat, D_MODEL)

    cond8 = jnp.concatenate([c_ctx[None, :], c, jnp.zeros((8 - 1 - n_lat, D_MODEL), F32)], axis=0)
    ctx_row = lambda i: 0

    mod0 = _modulation(cond8, w_mod_l0, b_mod_l0)
    n_main = 4 * A_QK + A_V + D_INNER_B + CONV_DIM_B
    w_main = w_in_l0.astype(BF16)
    w_dt = jnp.pad(w_in_l0[:, n_main:], ((0, 0), (0, LANES - 2 * N_HEADS_B))).astype(BF16)
    lane_pad = lambda v: jnp.pad(v.reshape(1, -1), ((0, 0), (0, LANES - 2 * N_HEADS_B)))
    conv_consts = (conv_w_l0, conv_b_l0.reshape(1, CONV_DIM_B), lane_pad(dt_bias_l0))
    ssd_consts = [lane_pad(a_log_l0),
                  jnp.repeat(d_skip_l0, HEAD_DIM_B).reshape(1, D_INNER_B),
                  ssm_norm_w_l0.reshape(1, D_INNER_B)]
    lam_vecs = jnp.stack([lam_q1_l0, lam_k1_l0, lam_q2_l0, lam_k2_l0], axis=0)
    lam_init = 0.8 - 0.6 * math.exp(-0.3 * 0)

    (qc, kc, vc, zc, xbc_c, dt_c), (w_out0, w_up0, w_down0, w_out1, w_up1, w_down1, w_in1, pool_w) = _inproj0(
        xc, mod0, ctx_row, w_main, w_dt, *conv_consts, l_ctx, TOKEN_TILE,
        ride_along=(w_out_l0, w_up_l0, w_down_l0, w_out_l1, w_up_l1, w_down_l1, w_in_l1,
                    pool_w_l1.reshape(-1, POOL_GROUP_DIM)))
    pool_w = pool_w.reshape(pool_w_l1.shape)
    oa_c = _attention(qc, [kc], [vc], lam_vecs, subln_w_l0, l_ctx, l_ctx, CTX_SEQS_PER_STEP, lam_init)
    (y_c, new_ssm), _ = _ssd(xbc_c, dt_c, zc, None, ssd_consts, n_ctx, l_ctx, 1)

    (ql, kl, vl, zl, xbc_l, dt_l), _ = _inproj0(xl, mod0, lambda i: 1 + i, w_main, w_dt, *conv_consts, l_lat, l_lat,
                                                rope_tables=_rope_tables(l_lat // GRID_W))
    ck = jnp.transpose(cache_k_l0, (0, 2, 3, 4, 1))
    cv = cache_v_l0.reshape(n_lat * past * N_HEADS_A, 2 * HEAD_DIM_A)
    oa_l = _attention(ql, [ck, kl], [cv, vl], lam_vecs, subln_w_l0, l_lat, ATTN_Q_TILE, 1, lam_init)
    h0 = state_ssm_l0.reshape(n_lat, 2, D_INNER_B, D_STATE)
    ((y_l,), _) = _ssd(xbc_l, dt_l, zl, h0, ssd_consts, n_lat, l_lat, 1)
    tail0 = (w_out0, ln1_g_l0, ln1_b_l0, w_up0, w_down0, ln2_g_l0, ln2_b_l0)
    xc, xl = _tail([oa_c, y_c], [oa_l, y_l], xc, xl, mod0, l_lat, *tail0)

    mod1 = _modulation(cond8, w_mod_l1, b_mod_l1)
    tail1 = (w_out1, ln1_g_l1, ln1_b_l1, w_up1, w_down1, ln2_g_l1, ln2_b_l1)
    mix_c = _pool_mixer(xc, mod1, ctx_row, w_in1, pool_w, pool_scale_l1, l_ctx, TOKEN_TILE)
    mix_l = _pool_mixer(xl, mod1, lambda i: 1 + i, w_in1, pool_w, pool_scale_l1, l_lat, l_lat)
    xc, xl = _tail([mix_c], [mix_l], xc, xl, mod1, l_lat, *tail1)

    return (xc.reshape(n_ctx, l_ctx, D_MODEL), xl.reshape(n_lat, l_lat, D_MODEL),
            jnp.transpose(kc, (0, 4, 1, 2, 3)),
            vc.reshape(n_ctx, l_ctx, N_HEADS_A, 2 * HEAD_DIM_A),
            new_ssm.reshape(n_ctx, 2, N_HEADS_B, HEAD_DIM_B, D_STATE))
```

```python
import functools
import math

import jax
import jax.numpy as jnp
import numpy as np
from jax import lax
from jax.experimental import pallas as pl
from jax.experimental.pallas import tpu as pltpu

F32 = jnp.float32
BF16 = jnp.bfloat16

D_MODEL = 1024
N_MOD = 6
N_HEADS_A = 4
HEAD_DIM_A = 64
A_QK = N_HEADS_A * HEAD_DIM_A
A_V = 2 * A_QK
D_INNER_B = 512
HEAD_DIM_B = 64
N_HEADS_B = 8
D_STATE = 128
D_CONV = 5
CONV_DIM_B = 1024
CHUNK = 128
GRID_W = 64
POOL_WINDOWS = (2, 4, 8, 16)
POOL_GROUP_DIM = 256
D_FF = 4096
ROPE_BASE = 10000.0
LN_EPS = 1e-5
DEPTH = 2
ALPHA = (2 * DEPTH) ** 0.25
NEG_BIG = -1e30
LOG2E = 1.4426950408889634

LANES = 128
TOKEN_TILE = 512
ATTN_Q_TILE = 256
CTX_SEQS_PER_STEP = 4
FF_CHUNK = 1024
CONV_COL_BLOCK = 256
TAIL_SPLIT = (256, 256)
VMEM_LIMIT = 56 * 1024 * 1024


def _dot(a, b):
    return jnp.dot(a, b, preferred_element_type=F32)


def _dot_nt(a, b):
    return lax.dot_general(a, b, (((1,), (1,)), ((), ())), preferred_element_type=F32)


def _bf16_pieces(a):
    a1 = a.astype(BF16)
    r1 = a - a1.astype(F32)
    a2 = r1.astype(BF16)
    a3 = (r1 - a2.astype(F32)).astype(BF16)
    return a1, a2, a3


def _dot_exact_rhs(t01, a):
    a1, a2, a3 = _bf16_pieces(a)
    return _dot(t01, a1) + _dot(t01, a2) + _dot(t01, a3)


def _sigmoid(x):
    return 1.0 / (1.0 + jnp.exp(-x))


def _silu(x):
    return x * _sigmoid(x)


def _layer_norm(x, g, b):
    mu = jnp.mean(x, axis=-1, keepdims=True)
    xc = x - mu
    var = jnp.mean(xc * xc, axis=-1, keepdims=True)
    return xc * lax.rsqrt(var + LN_EPS) * g + b


def _const_spec(shape):
    nd = len(shape)
    return pl.BlockSpec(shape, lambda *_: (0,) * nd, pipeline_mode=pl.Buffered(1))


def _params(n_axes=1):
    return pltpu.CompilerParams(dimension_semantics=("arbitrary",) * n_axes,
                                vmem_limit_bytes=VMEM_LIMIT)


def _mod_kernel(cond_ref, w_ref, b_ref, o_ref):
    c = cond_ref[...]
    s = _silu(c).astype(BF16)
    o_ref[...] = _dot(s, w_ref[...].astype(BF16)) + b_ref[...]


def _modulation(cond8, w_mod, b_mod):
    tn = 1024
    n = w_mod.shape[1]
    out = pl.pallas_call(
        _mod_kernel,
        grid=(n // tn,),
        in_specs=[pl.BlockSpec((8, D_MODEL), lambda j: (0, 0)),
                  pl.BlockSpec((D_MODEL, tn), lambda j: (0, j)),
                  pl.BlockSpec((1, tn), lambda j: (0, j))],
        out_specs=pl.BlockSpec((8, tn), lambda j: (0, j)),
        out_shape=jax.ShapeDtypeStruct((8, n), F32),
        compiler_params=_params(),
        name="modulation",
    )(cond8, w_mod, b_mod.reshape(1, n))
    return out.reshape(8, N_MOD, D_MODEL)


def _mod_spec(row_of_tile):
    return pl.BlockSpec((1, N_MOD, D_MODEL), lambda i: (row_of_tile(i), 0, 0))


def _rope(t, cosf, sinf):
    lane = lax.broadcasted_iota(jnp.int32, t.shape, 1)
    first = (lane & (HEAD_DIM_A - 1)) < HEAD_DIM_A // 2
    width = t.shape[1]
    swapped = jnp.where(first, pltpu.roll(t, width - HEAD_DIM_A // 2, 1),
                        pltpu.roll(t, HEAD_DIM_A // 2, 1))
    return t * cosf + swapped * sinf


def _conv_silu(xbc, cw_ref, cb_ref, cols, seq_len):
    assert xbc.shape[0] == seq_len
    half = D_CONV // 2
    edge = 8
    offsets = [j - half for j in range(D_CONV) if j != half]
    shifted = {o: pltpu.roll(xbc, (-o) % seq_len, 0) for o in offsets}

    def taps(lo, hi, masked):
        acc = cb_ref[:, cols] + cw_ref[half:half + 1, cols] * xbc[lo:hi]
        pos = lo + lax.broadcasted_iota(jnp.int32, (hi - lo, xbc.shape[1]), 0)
        for o in offsets:
            s = shifted[o][lo:hi]
            if masked:
                s = jnp.where((pos >= -o) if o < 0 else (pos < seq_len - o), s, 0.0)
            acc = acc + cw_ref[o + half:o + half + 1, cols] * s
        return acc

    acc = jnp.concatenate([taps(0, edge, True), taps(edge, seq_len - edge, False),
                           taps(seq_len - edge, seq_len, True)], axis=0)
    return _silu(acc)


def _softplus(t):
    return jnp.maximum(t, 0.0) + jnp.log(1.0 + jnp.exp(-jnp.abs(t)))


def _ride_along_specs(mats, n_steps):
    specs, shapes = [], []
    for w in mats:
        assert w.shape[0] % n_steps == 0
        specs.append(pl.BlockSpec((w.shape[0] // n_steps, w.shape[1]), lambda i: (i, 0)))
        shapes.append(jax.ShapeDtypeStruct(w.shape, BF16))
    return specs, shapes


def _ride_along_cast(src_refs, dst_refs):
    for src, dst in zip(src_refs, dst_refs):
        dst[...] = src[...].astype(BF16)


def _inproj0_kernel(*refs, rope, seq_len, n_cast):
    n_in = 9 if rope else 7
    ins, cast_in = refs[:n_in], refs[n_in:n_in + n_cast]
    (q_ref, k_ref, v_ref, z_ref, xbc_ref, dt_ref) = refs[n_in + n_cast:n_in + n_cast + 6]
    cast_out = refs[n_in + n_cast + 6:]
    if rope:
        x_ref, mod_ref, w_ref, wdt_ref, cw_ref, cb_ref, dtb_ref, cos_ref, sin_ref = ins
    else:
        x_ref, mod_ref, w_ref, wdt_ref, cw_ref, cb_ref, dtb_ref = ins
    _ride_along_cast(cast_in, cast_out)
    m = mod_ref[0]
    tasks = []
    for b in range(x_ref.shape[0] // seq_len):
        rows = slice(b * seq_len, (b + 1) * seq_len)
        h = (x_ref[rows, :] * (1.0 + m[1:2]) + m[0:1]).astype(BF16)

        for c0 in range(0, CONV_DIM_B, CONV_COL_BLOCK):
            cols = slice(c0, c0 + CONV_COL_BLOCK)

            def conv_mm(h=h, c0=c0):
                return _dot(h, w_ref[:, 2048 + c0:2048 + c0 + CONV_COL_BLOCK])

            def conv_vec(raw, rows=rows, cols=cols):
                xbc_ref[rows, cols] = _conv_silu(raw, cw_ref, cb_ref, cols, seq_len)

            tasks.append((conv_mm, conv_vec))

        def qk_mm(h=h):
            return _dot(h, wdt_ref[...]), _dot(h, w_ref[:, 0:512]), _dot(h, w_ref[:, 512:1024])

        def qk_vec(res, rows=rows, b=b):
            dt_raw, q, k = res
            dt_ref[rows, :] = _softplus(dt_raw + dtb_ref[...])
            if rope:
                q = _rope(q, cos_ref[...], sin_ref[...])
                k = _rope(k, cos_ref[...], sin_ref[...])
            q_ref[rows, :] = (q * (HEAD_DIM_A ** -0.5 * LOG2E)).astype(BF16)
            for blk in range(2 * A_QK // LANES):
                t = k[:, blk * LANES:(blk + 1) * LANES].T
                which, head = blk // 2, (blk % 2) * 2
                k_ref[b, which, head] = t[0:HEAD_DIM_A]
                k_ref[b, which, head + 1] = t[HEAD_DIM_A:2 * HEAD_DIM_A]

        def vz_mm(h=h):
            return _dot(h, w_ref[:, 1024:1536]), _dot(h, w_ref[:, 1536:2048])

        def vz_vec(res, rows=rows, b=b):
            vals, z = res
            for hh in range(N_HEADS_A):
                v_ref[pl.ds(b * seq_len * N_HEADS_A + hh, seq_len, stride=N_HEADS_A), :] = (
                    vals[:, hh * LANES:(hh + 1) * LANES])
            z_ref[rows, :] = z

        tasks += [(qk_mm, qk_vec), (vz_mm, vz_vec)]

    pending = None
    for mm, vec in tasks:
        res = mm()
        if pending is not None:
            pending[0](pending[1])
        pending = (vec, res)
    pending[0](pending[1])


def _inproj0(x, mod, row_of_tile, w_main, w_dt, conv_w, conv_b, dt_bias, seq_len, tm, rope_tables=None,
             ride_along=()):
    t = x.shape[0]
    assert tm % seq_len == 0
    row = lambda n: pl.BlockSpec((tm, n), lambda i: (i, 0))
    flat = lambda n: jax.ShapeDtypeStruct((t, n), F32)
    consts = [w_main, w_dt, conv_w, conv_b, dt_bias]
    in_specs = [row(D_MODEL), _mod_spec(row_of_tile)] + [_const_spec(cst.shape) for cst in consts]
    args = [x, mod] + consts
    if rope_tables is not None:
        assert tm == seq_len
        in_specs += [_const_spec(tab.shape) for tab in rope_tables]
        args += list(rope_tables)
    kt_dims = (2, N_HEADS_A, HEAD_DIM_A, seq_len)
    k_spec = pl.BlockSpec((tm // seq_len,) + kt_dims, lambda i: (i, 0, 0, 0, 0))
    k_shape = jax.ShapeDtypeStruct((t // seq_len,) + kt_dims, F32)
    cast_specs, cast_shapes = _ride_along_specs(ride_along, t // tm)
    outs = pl.pallas_call(
        functools.partial(_inproj0_kernel, rope=rope_tables is not None, seq_len=seq_len, n_cast=len(ride_along)),
        grid=(t // tm,),
        in_specs=in_specs + cast_specs,
        out_specs=[row(512), k_spec, pl.BlockSpec((tm * N_HEADS_A, LANES), lambda i: (i, 0)),
                   row(512), row(CONV_DIM_B), row(LANES)] + cast_specs,
        out_shape=[jax.ShapeDtypeStruct((t, 512), BF16), k_shape, jax.ShapeDtypeStruct((t * N_HEADS_A, LANES), F32),
                   flat(512), flat(CONV_DIM_B), flat(LANES)] + cast_shapes,
        compiler_params=_params(),
        name="inproj0",
    )(*args, *ride_along)
    return outs[:6], outs[6:]


def _lambda_full(lam_ref, lam_init):
    lv = lam_ref[...]
    return (jnp.exp(jnp.sum(lv[0:1] * lv[1:2], axis=1, keepdims=True))
            - jnp.exp(jnp.sum(lv[2:3] * lv[3:4], axis=1, keepdims=True)) + lam_init)


def _attn_kernel(*refs, n_seg, tq, lam_init):
    q_ref = refs[0]
    kt_refs = refs[1:1 + n_seg]
    v_refs = refs[1 + n_seg:1 + 2 * n_seg]
    lam_ref, sub_ref, o_ref = refs[1 + 2 * n_seg:]
    lam = _lambda_full(lam_ref, lam_init)
    lane = lax.broadcasted_iota(jnp.int32, (1, A_QK), 1)
    masks = [(lane >= h * HEAD_DIM_A) & (lane < (h + 1) * HEAD_DIM_A) for h in range(N_HEADS_A)]
    keys = [r.shape[-1] for r in kt_refs]

    def stacked(qx):
        return jnp.concatenate([jnp.where(m, qx, 0.0) for m in masks], axis=0).astype(BF16)

    def exp_scores(qx, b, which):
        qs = stacked(qx)
        s = [_dot(qs, r[b, which].reshape(A_QK, n).astype(BF16)) for r, n in zip(kt_refs, keys)]
        mx = functools.reduce(jnp.maximum, [jnp.max(x, axis=1, keepdims=True) for x in s])
        return [jnp.exp2(x - mx).astype(BF16) for x in s]

    outs = []
    for b in range(kt_refs[0].shape[0]):
        q = q_ref[b * tq:(b + 1) * tq, :]
        e1 = exp_scores(q[:, 0:A_QK], b, 0)
        e2 = exp_scores(q[:, A_QK:2 * A_QK], b, 1)
        heads = []
        for h in range(N_HEADS_A):
            hrows = slice(h * tq, (h + 1) * tq)
            n1 = n2 = None
            for s, (v_ref, n) in enumerate(zip(v_refs, keys)):
                v_h = v_ref[pl.ds(b * n * N_HEADS_A + h, n, stride=N_HEADS_A), :]
                v_ext = jnp.concatenate([v_h.astype(BF16),
                                         jnp.ones((n, LANES), BF16)], axis=1)
                p1 = _dot(e1[s][hrows], v_ext)
                p2 = _dot(e2[s][hrows], v_ext)
                n1 = p1 if n1 is None else n1 + p1
                n2 = p2 if n2 is None else n2 + p2
            heads.append(n1[:, 0:LANES] / n1[:, LANES:] - lam * (n2[:, 0:LANES] / n2[:, LANES:]))
        o = jnp.concatenate(heads, axis=0)
        ms = jnp.mean(o * o, axis=1, keepdims=True)
        o = (o * lax.rsqrt(ms + 1e-5) * sub_ref[...]) * (1.0 - lam_init)
        outs.append(jnp.concatenate([o[h * tq:(h + 1) * tq] for h in range(N_HEADS_A)], axis=1))
    o_ref[...] = jnp.concatenate(outs, axis=0).astype(o_ref.dtype)


def _attention(q, kts, vs, lam_vecs, subln_w, q_len, tq, seqs_per_step, lam_init):
    n_seq = kts[0].shape[0]
    nb = seqs_per_step
    tiles = q_len // tq
    assert tiles == 1 or nb == 1
    qspec = pl.BlockSpec((nb * tq, 512), lambda b, i: (b * tiles + i, 0))
    ktspecs = [pl.BlockSpec((nb,) + kt.shape[1:], lambda b, i: (b, 0, 0, 0, 0)) for kt in kts]
    vspecs = [pl.BlockSpec((nb * kt.shape[-1] * N_HEADS_A, LANES), lambda b, i: (b, 0)) for kt in kts]
    return pl.pallas_call(
        functools.partial(_attn_kernel, n_seg=len(kts), tq=tq, lam_init=lam_init),
        grid=(n_seq // nb, tiles),
        in_specs=[qspec] + ktspecs + vspecs + [pl.BlockSpec((4, HEAD_DIM_A), lambda b, i: (0, 0)),
                                             pl.BlockSpec((1, LANES), lambda b, i: (0, 0))],
        out_specs=qspec,
        out_shape=jax.ShapeDtypeStruct(q.shape, BF16),
        compiler_params=_params(2),
        name="diff_attention",
    )(q, *kts, *vs, lam_vecs, subln_w.reshape(1, LANES))


def _chunk_loop(n, body, unroll=1):
    if n <= 2:
        for c in range(n):
            body(c)
    else:
        def step(c, carry):
            body(c)
            return carry
        lax.fori_loop(0, n, step, 0, unroll=unroll)


def _ssd_kernel(*refs, seq_len, n_seq, has_h0, n_cast):
    n_in = 7 if has_h0 else 6
    n_out = 1 if has_h0 else 2
    ins, cast_in = refs[:n_in], refs[n_in:n_in + n_cast]
    outs = refs[n_in + n_cast:n_in + n_cast + n_out]
    cast_out = refs[n_in + n_cast + n_out:n_in + 2 * n_cast + n_out]
    ysc, s_sc, cd_sc, e_sc, htf, htb = refs[n_in + 2 * n_cast + n_out:]
    if has_h0:
        xact, dtv, z_ref, h0_ref, arow_ref, dsk_ref, nw_ref = ins
        (y_ref,), hfin_ref = outs, None
    else:
        xact, dtv, z_ref, arow_ref, dsk_ref, nw_ref = ins
        (y_ref, hfin_ref), h0_ref = outs, None
    _ride_along_cast(cast_in, cast_out)
    nc = seq_len // CHUNK
    n_pairs = N_HEADS_B // 2
    a_row = -jnp.exp(arow_ref[...]) * LOG2E

    row_i = lax.broadcasted_iota(jnp.int32, (CHUNK, CHUNK), 0)
    col_i = lax.broadcasted_iota(jnp.int32, (CHUNK, CHUNK), 1)
    lower = col_i <= row_i
    upper = col_i >= row_i
    tri = (jnp.where(lower, 1.0, 0.0).astype(BF16), jnp.where(upper, 1.0, 0.0).astype(BF16))
    lo_half = col_i < HEAD_DIM_B

    def split_pair(m):
        zero = jnp.zeros_like(m)
        return jnp.concatenate([jnp.where(lo_half, m, zero), jnp.where(lo_half, zero, m)], axis=0).astype(BF16)

    def decay_terms(c):
        rows = pl.ds(pl.multiple_of(c * CHUNK, CHUNK), CHUNK)
        dtc = dtv[rows, :]
        dtt = dtc.T[0:16, :]
        per_dir = []
        for d in range(2):
            col = _dot_exact_rhs(tri[d], dtc * a_row)
            rowm = col.T[0:16, :]
            far = CHUNK - 1 if d == 0 else 0
            tot = jnp.broadcast_to(rowm[:, far:far + 1], (16, CHUNK))
            dte = jnp.exp2(tot - rowm) * dtt
            per_dir.append((col, rowm - jnp.log2(dtt), dte, jnp.exp2(tot)))
        return per_dir

    def local_phase(c, per_dir=None):
        rows = pl.ds(pl.multiple_of(c * CHUNK, CHUNK), CHUNK)
        if per_dir is None:
            per_dir = decay_terms(c)
        for g in range(2):
            bg = xact[rows, 512 + g * LANES:512 + (g + 1) * LANES]
            cg = xact[rows, 768 + g * LANES:768 + (g + 1) * LANES]
            cbm = _dot_nt(cg.astype(BF16), bg.astype(BF16))
            bgt = bg.T
            for pq in range(2):
                pi = g * 2 + pq
                lanes = slice(pi * LANES, (pi + 1) * LANES)
                xrhs = split_pair(xact[rows, lanes])
                y_pair = None
                for d in range(2):
                    col, row_dt, dte, cdec = per_dir[d]
                    causal = lower if d == 0 else upper
                    m_parts, col_parts, bw_parts, cd_parts = [], [], [], []
                    for hh in (2 * pi, 2 * pi + 1):
                        hd = 8 * d + hh
                        colb = jnp.sum(jnp.where(col_i == hd, col, 0.0), axis=1, keepdims=True)
                        m_parts.append(cbm * jnp.exp2(jnp.where(causal, colb - row_dt[hd:hd + 1, :], NEG_BIG)))
                        col_parts.append(colb)
                        bw_parts.append(bgt * dte[hd:hd + 1, :])
                        cd_parts.append(cdec[hd:hd + 1, :])
                    yd = _dot(jnp.concatenate(m_parts, axis=1).astype(BF16), xrhs)
                    y_pair = yd if y_pair is None else y_pair + yd
                    s_sc[c, d, :, lanes] = _dot(jnp.concatenate(bw_parts, axis=1).astype(BF16), xrhs)
                    e_sc[c, d, :, lanes] = jnp.exp2(jnp.where(lo_half, col_parts[0], col_parts[1]))
                    cd_sc[c, d, :, lanes] = jnp.broadcast_to(
                        jnp.where(lo_half[0:1, :], cd_parts[0], cd_parts[1]), (8, LANES))
                ysc[rows, lanes] = y_pair

    if n_seq * nc <= 2:
        terms = [decay_terms(c) for c in range(n_seq * nc)]
        for c in range(n_seq * nc):
            local_phase(c, terms[c])
    else:
        _chunk_loop(n_seq * nc, local_phase, unroll=2)

    def scan_sequence(sq):
        for d, ht in ((0, htf), (1, htb)):
            for qd in range(n_pairs):
                lanes = slice(qd * LANES, (qd + 1) * LANES)
                if has_h0:
                    ht[:, lanes] = h0_ref[sq, d, lanes, :].T
                else:
                    ht[:, lanes] = jnp.zeros((D_STATE, LANES), F32)

        def scan_phase(j):
            for d, ht in ((0, htf), (1, htb)):
                c = sq * nc + (j if d == 0 else nc - 1 - j)
                rows = pl.ds(pl.multiple_of(c * CHUNK, CHUNK), CHUNK)
                for pi in range(n_pairs):
                    lanes = slice(pi * LANES, (pi + 1) * LANES)
                    g = pi // 2
                    cg = xact[rows, 768 + g * LANES:768 + (g + 1) * LANES].astype(BF16)
                    hprev = ht[:, lanes]
                    ysc[rows, lanes] += _dot(cg, hprev.astype(BF16)) * e_sc[c, d, :, lanes]
                    ht[:, lanes] = hprev * cd_sc[c, d, 0:1, lanes] + s_sc[c, d, :, lanes]

        _chunk_loop(nc, scan_phase)
        if hfin_ref is not None:
            for d, ht in ((0, htf), (1, htb)):
                for qd in range(n_pairs):
                    lanes = slice(qd * LANES, (qd + 1) * LANES)
                    hfin_ref[sq, d, lanes, :] = ht[:, lanes].T

    if n_seq == 1:
        scan_sequence(0)
    else:
        def seq_step(sq, carry):
            scan_sequence(sq)
            return carry
        lax.fori_loop(0, n_seq, seq_step, 0)

    def finish(c):
        rows = pl.ds(pl.multiple_of(c * CHUNK, CHUNK), CHUNK)
        y = ysc[rows, :] + dsk_ref[...] * xact[rows, 0:D_INNER_B]
        y = y * _silu(z_ref[rows, :])
        ms = jnp.mean(y * y, axis=1, keepdims=True)
        y_ref[rows, :] = (y * lax.rsqrt(ms + 1e-5) * nw_ref[...]).astype(y_ref.dtype)

    _chunk_loop(n_seq * nc, finish)


def _ssd(xbc, dt, z, h0, consts, n_batch, seq_len, seqs_per_step, ride_along=()):
    nb = seqs_per_step
    L = nb * seq_len
    nc = L // CHUNK
    row = lambda n: pl.BlockSpec((L, n), lambda b: (b, 0))
    state_spec = pl.BlockSpec((nb, 2, D_INNER_B, D_STATE), lambda b: (b, 0, 0, 0))
    has_h0 = h0 is not None
    in_specs = [row(CONV_DIM_B), row(LANES), row(D_INNER_B)]
    args = [xbc, dt, z]
    if has_h0:
        in_specs.append(state_spec)
        args.append(h0)
    for cst in consts:
        in_specs.append(pl.BlockSpec(cst.shape, lambda b: (0, 0)))
        args.append(cst)
    out_specs = [row(D_INNER_B)]
    out_shape = [jax.ShapeDtypeStruct((n_batch * seq_len, D_INNER_B), BF16)]
    if not has_h0:
        out_specs.append(state_spec)
        out_shape.append(jax.ShapeDtypeStruct((n_batch, 2, D_INNER_B, D_STATE), F32))
    scratch = [pltpu.VMEM((L, D_INNER_B), F32),
               pltpu.VMEM((nc, 2, D_STATE, D_INNER_B), F32),
               pltpu.VMEM((nc, 2, 8, D_INNER_B), F32),
               pltpu.VMEM((nc, 2, CHUNK, D_INNER_B), F32),
               pltpu.VMEM((D_STATE, D_INNER_B), F32),
               pltpu.VMEM((D_STATE, D_INNER_B), F32)]
    cast_specs, cast_shapes = _ride_along_specs(ride_along, n_batch // nb)
    n_main_out = len(out_specs)
    outs = pl.pallas_call(
        functools.partial(_ssd_kernel, seq_len=seq_len, n_seq=nb, has_h0=has_h0, n_cast=len(ride_along)),
        grid=(n_batch // nb,),
        in_specs=in_specs + cast_specs,
        out_specs=out_specs + cast_specs,
        out_shape=out_shape + cast_shapes,
        scratch_shapes=scratch,
        compiler_params=_params(),
        name="bidir_ssd",
    )(*args, *ride_along)
    return outs[:n_main_out], outs[n_main_out:]


def _tail_kernel(*refs, n_in, n_ctx_tiles):
    ctx_a, lat_a = refs[:n_in], refs[n_in:2 * n_in]
    (xc_ref, xl_ref, mod_ref, wout_ref, wup_ref, wdown_ref, g1_ref, b1_ref, g2_ref, b2_ref,
     oc_ref, ol_ref) = refs[2 * n_in:]
    step = pl.program_id(0)

    tile = functools.partial(_mlp_tile, mod_ref=mod_ref, wout_ref=wout_ref, g1_ref=g1_ref, b1_ref=b1_ref,
                             wup_ref=wup_ref, wdown_ref=wdown_ref, g2_ref=g2_ref, b2_ref=b2_ref)

    @pl.when(step < n_ctx_tiles)
    def _():
        tile(ctx_a, xc_ref, oc_ref)

    @pl.when(step >= n_ctx_tiles)
    def _():
        tile(lat_a, xl_ref, ol_ref)


def _mlp_tile(a_refs, x_ref, o_ref, *, mod_ref, wout_ref, g1_ref, b1_ref, wup_ref, wdown_ref, g2_ref, b2_ref):
    m = mod_ref[0]
    n_chunks = D_FF // FF_CHUNK

    def out_proj(rows):
        d = None
        off = 0
        for a_ref in a_refs:
            n = a_ref.shape[1]
            part = _dot(a_ref[rows, :].astype(BF16), wout_ref[off:off + n, :])
            d = part if d is None else d + part
            off += n
        return d

    def head(rows, d):
        x1 = _layer_norm(ALPHA * x_ref[rows, :] + m[2:3] * d, g1_ref[...], b1_ref[...])
        return x1, (x1 * (1.0 + m[4:5]) + m[3:4]).astype(BF16)

    def up(h, c):
        return _dot(h, wup_ref[:, c * FF_CHUNK:(c + 1) * FF_CHUNK])

    def down(u, acc, c):
        u = jnp.maximum(u, 0.0)
        part = _dot((u * u).astype(BF16), wdown_ref[c * FF_CHUNK:(c + 1) * FF_CHUNK, :])
        return part if acc is None else acc + part

    def finish(rows, x1, acc):
        o_ref[rows, :] = _layer_norm(ALPHA * x1 + m[5:6] * acc, g2_ref[...], b2_ref[...])

    assert sum(TAIL_SPLIT) == x_ref.shape[0]
    k = len(TAIL_SPLIT)
    starts = [sum(TAIL_SPLIT[:t]) for t in range(k)]
    rows = [slice(s, s + n) for s, n in zip(starts, TAIL_SPLIT)]
    projected = [out_proj(r) for r in rows]
    heads = [None] * k
    heads[0] = head(rows[0], projected[0])
    u_next = up(heads[0][1], 0)
    done = None
    for t in range(k):
        if t + 1 < k:
            heads[t + 1] = head(rows[t + 1], projected[t + 1])
        acc = None
        for c in range(n_chunks):
            if c + 1 < n_chunks:
                nxt = up(heads[t][1], c + 1)
            else:
                nxt = up(heads[t + 1][1], 0) if t + 1 < k else None
            u_cur, u_next = u_next, nxt
            acc = down(u_cur, acc, c)
            if c == 0 and done is not None:
                finish(*done)
        done = (rows[t], heads[t][0], acc)
    finish(*done)


def _tail(a_ctx, a_lat, x_ctx, x_lat, mod, lat_seq_len, w_out, ln1_g, ln1_b, w_up, w_down, ln2_g, ln2_b):
    tm = TOKEN_TILE
    n_ctx = x_ctx.shape[0] // tm
    n_lat = x_lat.shape[0] // tm
    tiles_per_lat_seq = lat_seq_len // tm
    ctx_row = lambda n: pl.BlockSpec((tm, n), lambda i: (jnp.minimum(i, n_ctx - 1), 0))
    lat_row = lambda n: pl.BlockSpec((tm, n), lambda i: (jnp.maximum(i - n_ctx, 0), 0))
    lat_out = pl.BlockSpec((tm, D_MODEL), lambda i: (jnp.maximum(i - n_ctx, 0), 0))
    mod_spec = _mod_spec(lambda i: jnp.where(i < n_ctx, 0, 1 + (i - n_ctx) // tiles_per_lat_seq))
    vec = lambda v: v.reshape(1, D_MODEL)
    consts = [w_out, w_up, w_down, vec(ln1_g), vec(ln1_b), vec(ln2_g), vec(ln2_b)]
    return pl.pallas_call(
        functools.partial(_tail_kernel, n_in=len(a_ctx), n_ctx_tiles=n_ctx),
        grid=(n_ctx + n_lat,),
        in_specs=[ctx_row(a.shape[1]) for a in a_ctx] + [lat_row(a.shape[1]) for a in a_lat]
                 + [ctx_row(D_MODEL), lat_row(D_MODEL), mod_spec] + [_const_spec(v.shape) for v in consts],
        out_specs=[ctx_row(D_MODEL), lat_out],
        out_shape=[jax.ShapeDtypeStruct(x_ctx.shape, F32), jax.ShapeDtypeStruct(x_lat.shape, F32)],
        compiler_params=_params(),
        name="outproj_mlp",
    )(*a_ctx, *a_lat, x_ctx, x_lat, mod, *consts)


def _pool_kernel(x_ref, mod_ref, win_ref, pw_ref, ps_ref, o_ref, *, seq_len):
    L = seq_len
    m = mod_ref[0]
    t_i = lax.broadcasted_iota(jnp.int32, (L, L), 0)
    s_i = lax.broadcasted_iota(jnp.int32, (L, L), 1)
    t_g = lax.broadcasted_iota(jnp.int32, (L, POOL_GROUP_DIM), 0)
    bands, scales = [], []
    for w in POOL_WINDOWS:
        lo, hi = w // 2, w - w // 2
        bands.append(jnp.where((s_i >= t_i - lo) & (s_i < t_i + hi), 1.0 / w, 0.0).astype(BF16))
        scales.append(w / (jnp.minimum(t_g + hi, L) - jnp.maximum(t_g - lo, 0)).astype(F32))
    n_seq = x_ref.shape[0] // L
    seq_rows = [slice(s * L, (s + 1) * L) for s in range(n_seq)]
    hs = [(x_ref[r, :] * (1.0 + m[1:2]) + m[0:1]).astype(BF16) for r in seq_rows]
    us = [_dot(h, win_ref[...]) for h in hs]
    n_groups = len(POOL_WINDOWS)
    cols = [slice(g * POOL_GROUP_DIM, (g + 1) * POOL_GROUP_DIM) for g in range(n_groups)]

    def window_means(g):
        out = []
        for s in range(n_seq):
            ug = us[s][:, cols[g]]
            ug_hi = ug.astype(BF16)
            ug_lo = (ug - ug_hi.astype(F32)).astype(BF16)
            out.append((ug, _dot(bands[g], ug_hi) + _dot(bands[g], ug_lo)))
        return out

    def mix(g, means):
        return [_dot((wm * scales[g] - ug).astype(BF16), pw_ref[g]) * ps_ref[:, cols[g]] for ug, wm in means]

    mixed = []
    pending = window_means(0)
    for g in range(n_groups):
        nxt = window_means(g + 1) if g + 1 < n_groups else None
        mixed.append(mix(g, pending))
        pending = nxt
    for s in range(n_seq):
        o_ref[seq_rows[s], :] = jnp.concatenate([mixed[g][s] for g in range(n_groups)], axis=1).astype(o_ref.dtype)


def _pool_mixer(x, mod, row_of_tile, w_in, pool_w, pool_scale, seq_len, tm):
    t = x.shape[0]
    row = pl.BlockSpec((tm, D_MODEL), lambda i: (i, 0))
    consts = [w_in, pool_w, pool_scale.reshape(1, D_MODEL)]
    return pl.pallas_call(
        functools.partial(_pool_kernel, seq_len=seq_len),
        grid=(t // tm,),
        in_specs=[row, _mod_spec(row_of_tile)] + [_const_spec(cst.shape) for cst in consts],
        out_specs=row,
        out_shape=jax.ShapeDtypeStruct((t, D_MODEL), BF16),
        compiler_params=_params(),
        name="pool_mixer",
    )(x, mod, *consts)


def _rope_tables(rows):
    r, col = np.meshgrid(np.arange(rows), np.arange(GRID_W), indexing="ij")
    r = r.reshape(-1).astype(np.float64)
    col = col.reshape(-1).astype(np.float64)
    n_freq = HEAD_DIM_A // 4
    inv = ROPE_BASE ** (-np.arange(n_freq, dtype=np.float64) / n_freq)
    ang = np.concatenate([r[:, None] * inv, col[:, None] * inv], -1)
    reps = 512 // HEAD_DIM_A
    cosf = np.tile(np.concatenate([np.cos(ang), np.cos(ang)], -1), (1, reps)).astype(np.float32)
    sinf = np.tile(np.concatenate([-np.sin(ang), np.sin(ang)], -1), (1, reps)).astype(np.float32)
    return jnp.asarray(cosf), jnp.asarray(sinf)


def kernel(x_prompt, x_sample, cache_k_l0, cache_v_l0, state_ssm_l0, c, c_ctx, w_mod_l0, b_mod_l0, w_in_l0, conv_w_l0, conv_b_l0, a_log_l0, dt_bias_l0, d_skip_l0, ssm_norm_w_l0, lam_q1_l0, lam_k1_l0, lam_q2_l0, lam_k2_l0, subln_w_l0, w_out_l0, ln1_g_l0, ln1_b_l0, w_up_l0, w_down_l0, ln2_g_l0, ln2_b_l0, w_mod_l1, b_mod_l1, w_in_l1, pool_w_l1, pool_scale_l1, w_out_l1, ln1_g_l1, ln1_b_l1, w_up_l1, w_down_l1, ln2_g_l1, ln2_b_l1):
    n_ctx, l_ctx, _ = x_prompt.shape
    n_lat, l_lat, _ = x_sample.shape
    past = cache_k_l0.shape[1]
    xc = x_prompt.reshape(n_ctx * l_ctx, D_MODEL)
    xl = x_sample.reshape(n_lat * l_lat, D_MODEL)

    cond8 = jnp.concatenate([c_ctx[None, :], c, jnp.zeros((8 - 1 - n_lat, D_MODEL), F32)], axis=0)
    ctx_row = lambda i: 0

    mod0 = _modulation(cond8, w_mod_l0, b_mod_l0)
    n_main = 4 * A_QK + A_V + D_INNER_B + CONV_DIM_B
    w_main = w_in_l0.astype(BF16)
    w_dt = jnp.pad(w_in_l0[:, n_main:], ((0, 0), (0, LANES - 2 * N_HEADS_B))).astype(BF16)
    lane_pad = lambda v: jnp.pad(v.reshape(1, -1), ((0, 0), (0, LANES - 2 * N_HEADS_B)))
    conv_consts = (conv_w_l0, conv_b_l0.reshape(1, CONV_DIM_B), lane_pad(dt_bias_l0))
    ssd_consts = [lane_pad(a_log_l0),
                  jnp.repeat(d_skip_l0, HEAD_DIM_B).reshape(1, D_INNER_B),
                  ssm_norm_w_l0.reshape(1, D_INNER_B)]
    lam_vecs = jnp.stack([lam_q1_l0, lam_k1_l0, lam_q2_l0, lam_k2_l0], axis=0)
    lam_init = 0.8 - 0.6 * math.exp(-0.3 * 0)

    (qc, kc, vc, zc, xbc_c, dt_c), (w_out0, w_up0, w_down0, w_out1, w_up1, w_down1, w_in1, pool_w) = _inproj0(
        xc, mod0, ctx_row, w_main, w_dt, *conv_consts, l_ctx, TOKEN_TILE,
        ride_along=(w_out_l0, w_up_l0, w_down_l0, w_out_l1, w_up_l1, w_down_l1, w_in_l1,
                    pool_w_l1.reshape(-1, POOL_GROUP_DIM)))
    pool_w = pool_w.reshape(pool_w_l1.shape)
    oa_c = _attention(qc, [kc], [vc], lam_vecs, subln_w_l0, l_ctx, l_ctx, CTX_SEQS_PER_STEP, lam_init)
    (y_c, new_ssm), _ = _ssd(xbc_c, dt_c, zc, None, ssd_consts, n_ctx, l_ctx, 1)

    (ql, kl, vl, zl, xbc_l, dt_l), _ = _inproj0(xl, mod0, lambda i: 1 + i, w_main, w_dt, *conv_consts, l_lat, l_lat,
                                                rope_tables=_rope_tables(l_lat // GRID_W))
    ck = jnp.transpose(cache_k_l0, (0, 2, 3, 4, 1))
    cv = cache_v_l0.reshape(n_lat * past * N_HEADS_A, 2 * HEAD_DIM_A)
    oa_l = _attention(ql, [ck, kl], [cv, vl], lam_vecs, subln_w_l0, l_lat, ATTN_Q_TILE, 1, lam_init)
    h0 = state_ssm_l0.reshape(n_lat, 2, D_INNER_B, D_STATE)
    ((y_l,), _) = _ssd(xbc_l, dt_l, zl, h0, ssd_consts, n_lat, l_lat, 1)
    tail0 = (w_out0, ln1_g_l0, ln1_b_l0, w_up0, w_down0, ln2_g_l0, ln2_b_l0)
    xc, xl = _tail([oa_c, y_c], [oa_l, y_l], xc, xl, mod0, l_lat, *tail0)

    mod1 = _modulation(cond8, w_mod_l1, b_mod_l1)
    tail1 = (w_out1, ln1_g_l1, ln1_b_l1, w_up1, w_down1, ln2_g_l1, ln2_b_l1)
    mix_c = _pool_mixer(xc, mod1, ctx_row, w_in1, pool_w, pool_scale_l1, l_ctx, TOKEN_TILE)
    mix_l = _pool_mixer(xl, mod1, lambda i: 1 + i, w_in1, pool_w, pool_scale_l1, l_lat, l_lat)
    xc, xl = _tail([mix_c], [mix_l], xc, xl, mod1, l_lat, *tail1)

    return (xc.reshape(n_ctx, l_ctx, D_MODEL), xl.reshape(n_lat, l_lat, D_MODEL),
            jnp.transpose(kc, (0, 4, 1, 2, 3)),
            vc.reshape(n_ctx, l_ctx, N_HEADS_A, 2 * HEAD_DIM_A),
            new_ssm.reshape(n_ctx, 2, N_HEADS_B, HEAD_DIM_B, D_STATE))
```

```python
import functools
import math

import jax
import jax.numpy as jnp
import numpy as np
from jax import lax
from jax.experimental import pallas as pl
from jax.experimental.pallas import tpu as pltpu

F32 = jnp.float32
BF16 = jnp.bfloat16

D_MODEL = 1024
N_MOD = 6
N_HEADS_A = 4
HEAD_DIM_A = 64
A_QK = N_HEADS_A * HEAD_DIM_A
A_V = 2 * A_QK
D_INNER_B = 512
HEAD_DIM_B = 64
N_HEADS_B = 8
D_STATE = 128
D_CONV = 5
CONV_DIM_B = 1024
CHUNK = 128
GRID_W = 64
POOL_WINDOWS = (2, 4, 8, 16)
POOL_GROUP_DIM = 256
D_FF = 4096
ROPE_BASE = 10000.0
LN_EPS = 1e-5
DEPTH = 2
ALPHA = (2 * DEPTH) ** 0.25
NEG_BIG = -1e30
LOG2E = 1.4426950408889634

LANES = 128
TOKEN_TILE = 512
ATTN_Q_TILE = 256
CTX_SEQS_PER_STEP = 4
SSD_CTX_SEQS_PER_STEP = 2
SSD_UNROLL_CHUNKS = 4
FF_CHUNK = 1024
CONV_COL_BLOCK = 256
TAIL_SPLIT = (256, 256)
VMEM_LIMIT = 56 * 1024 * 1024


def _dot(a, b):
    return jnp.dot(a, b, preferred_element_type=F32)


def _dot_nt(a, b):
    return lax.dot_general(a, b, (((1,), (1,)), ((), ())), preferred_element_type=F32)


def _bf16_pieces(a):
    a1 = a.astype(BF16)
    r1 = a - a1.astype(F32)
    a2 = r1.astype(BF16)
    a3 = (r1 - a2.astype(F32)).astype(BF16)
    return a1, a2, a3


def _dot_exact_rhs(t01, a):
    a1, a2, a3 = _bf16_pieces(a)
    return _dot(t01, a1) + _dot(t01, a2) + _dot(t01, a3)


def _sigmoid(x):
    return 1.0 / (1.0 + jnp.exp(-x))


def _silu(x):
    return x * _sigmoid(x)


def _layer_norm(x, g, b):
    mu = jnp.mean(x, axis=-1, keepdims=True)
    xc = x - mu
    var = jnp.mean(xc * xc, axis=-1, keepdims=True)
    return xc * lax.rsqrt(var + LN_EPS) * g + b


def _const_spec(shape):
    nd = len(shape)
    return pl.BlockSpec(shape, lambda *_: (0,) * nd, pipeline_mode=pl.Buffered(1))


def _params(n_axes=1):
    return pltpu.CompilerParams(dimension_semantics=("arbitrary",) * n_axes,
                                vmem_limit_bytes=VMEM_LIMIT)


def _mod_kernel(cond_ref, w_ref, b_ref, o_ref):
    c = cond_ref[...]
    s = _silu(c).astype(BF16)
    o_ref[...] = _dot(s, w_ref[...].astype(BF16)) + b_ref[...]


def _modulation(cond8, w_mod, b_mod):
    tn = 1024
    n = w_mod.shape[1]
    out = pl.pallas_call(
        _mod_kernel,
        grid=(n // tn,),
        in_specs=[pl.BlockSpec((8, D_MODEL), lambda j: (0, 0)),
                  pl.BlockSpec((D_MODEL, tn), lambda j: (0, j)),
                  pl.BlockSpec((1, tn), lambda j: (0, j))],
        out_specs=pl.BlockSpec((8, tn), lambda j: (0, j)),
        out_shape=jax.ShapeDtypeStruct((8, n), F32),
        compiler_params=_params(),
        name="modulation",
    )(cond8, w_mod, b_mod.reshape(1, n))
    return out.reshape(8, N_MOD, D_MODEL)


def _mod_spec(row_of_tile):
    return pl.BlockSpec((1, N_MOD, D_MODEL), lambda i: (row_of_tile(i), 0, 0))


def _rope(t, cosf, sinf):
    lane = lax.broadcasted_iota(jnp.int32, t.shape, 1)
    first = (lane & (HEAD_DIM_A - 1)) < HEAD_DIM_A // 2
    width = t.shape[1]
    swapped = jnp.where(first, pltpu.roll(t, width - HEAD_DIM_A // 2, 1),
                        pltpu.roll(t, HEAD_DIM_A // 2, 1))
    return t * cosf + swapped * sinf


def _conv_silu(xbc, cw_ref, cb_ref, cols, seq_len):
    assert xbc.shape[0] == seq_len
    half = D_CONV // 2
    edge = 8
    offsets = [j - half for j in range(D_CONV) if j != half]
    shifted = {o: pltpu.roll(xbc, (-o) % seq_len, 0) for o in offsets}

    def taps(lo, hi, masked):
        acc = cb_ref[:, cols] + cw_ref[half:half + 1, cols] * xbc[lo:hi]
        pos = lo + lax.broadcasted_iota(jnp.int32, (hi - lo, xbc.shape[1]), 0)
        for o in offsets:
            s = shifted[o][lo:hi]
            if masked:
                s = jnp.where((pos >= -o) if o < 0 else (pos < seq_len - o), s, 0.0)
            acc = acc + cw_ref[o + half:o + half + 1, cols] * s
        return acc

    acc = jnp.concatenate([taps(0, edge, True), taps(edge, seq_len - edge, False),
                           taps(seq_len - edge, seq_len, True)], axis=0)
    return _silu(acc)


def _softplus(t):
    return jnp.maximum(t, 0.0) + jnp.log(1.0 + jnp.exp(-jnp.abs(t)))


def _ride_along_specs(mats, n_steps):
    specs, shapes = [], []
    for w in mats:
        assert w.shape[0] % n_steps == 0
        specs.append(pl.BlockSpec((w.shape[0] // n_steps, w.shape[1]), lambda i: (i, 0)))
        shapes.append(jax.ShapeDtypeStruct(w.shape, BF16))
    return specs, shapes


def _ride_along_cast(src_refs, dst_refs):
    for src, dst in zip(src_refs, dst_refs):
        dst[...] = src[...].astype(BF16)


def _inproj0_kernel(*refs, rope, seq_len, n_cast):
    n_in = 9 if rope else 7
    ins, cast_in = refs[:n_in], refs[n_in:n_in + n_cast]
    (q_ref, k_ref, v_ref, z_ref, xbc_ref, dt_ref) = refs[n_in + n_cast:n_in + n_cast + 6]
    cast_out = refs[n_in + n_cast + 6:]
    if rope:
        x_ref, mod_ref, w_ref, wdt_ref, cw_ref, cb_ref, dtb_ref, cos_ref, sin_ref = ins
    else:
        x_ref, mod_ref, w_ref, wdt_ref, cw_ref, cb_ref, dtb_ref = ins
    _ride_along_cast(cast_in, cast_out)
    m = mod_ref[0]
    tasks = []
    for b in range(x_ref.shape[0] // seq_len):
        rows = slice(b * seq_len, (b + 1) * seq_len)
        h = (x_ref[rows, :] * (1.0 + m[1:2]) + m[0:1]).astype(BF16)

        for c0 in range(0, CONV_DIM_B, CONV_COL_BLOCK):
            cols = slice(c0, c0 + CONV_COL_BLOCK)

            def conv_mm(h=h, c0=c0):
                return _dot(h, w_ref[:, 2048 + c0:2048 + c0 + CONV_COL_BLOCK])

            def conv_vec(raw, rows=rows, cols=cols):
                xbc_ref[rows, cols] = _conv_silu(raw, cw_ref, cb_ref, cols, seq_len)

            tasks.append((conv_mm, conv_vec))

        def qk_mm(h=h):
            return _dot(h, wdt_ref[...]), _dot(h, w_ref[:, 0:512]), _dot(h, w_ref[:, 512:1024])

        def qk_vec(res, rows=rows, b=b):
            dt_raw, q, k = res
            dt_ref[rows, :] = _softplus(dt_raw + dtb_ref[...])
            if rope:
                q = _rope(q, cos_ref[...], sin_ref[...])
                k = _rope(k, cos_ref[...], sin_ref[...])
            q_ref[rows, :] = (q * (HEAD_DIM_A ** -0.5 * LOG2E)).astype(BF16)
            for blk in range(2 * A_QK // LANES):
                t = k[:, blk * LANES:(blk + 1) * LANES].T
                which, head = blk // 2, (blk % 2) * 2
                k_ref[b, which, head] = t[0:HEAD_DIM_A]
                k_ref[b, which, head + 1] = t[HEAD_DIM_A:2 * HEAD_DIM_A]

        def vz_mm(h=h):
            return _dot(h, w_ref[:, 1024:1536]), _dot(h, w_ref[:, 1536:2048])

        def vz_vec(res, rows=rows, b=b):
            vals, z = res
            for hh in range(N_HEADS_A):
                v_ref[pl.ds(b * seq_len * N_HEADS_A + hh, seq_len, stride=N_HEADS_A), :] = (
                    vals[:, hh * LANES:(hh + 1) * LANES])
            z_ref[rows, :] = z

        tasks += [(qk_mm, qk_vec), (vz_mm, vz_vec)]

    pending = None
    for mm, vec in tasks:
        res = mm()
        if pending is not None:
            pending[0](pending[1])
        pending = (vec, res)
    pending[0](pending[1])


def _inproj0(x, mod, row_of_tile, w_main, w_dt, conv_w, conv_b, dt_bias, seq_len, tm, rope_tables=None,
             ride_along=()):
    t = x.shape[0]
    assert tm % seq_len == 0
    row = lambda n: pl.BlockSpec((tm, n), lambda i: (i, 0))
    flat = lambda n: jax.ShapeDtypeStruct((t, n), F32)
    consts = [w_main, w_dt, conv_w, conv_b, dt_bias]
    in_specs = [row(D_MODEL), _mod_spec(row_of_tile)] + [_const_spec(cst.shape) for cst in consts]
    args = [x, mod] + consts
    if rope_tables is not None:
        assert tm == seq_len
        in_specs += [_const_spec(tab.shape) for tab in rope_tables]
        args += list(rope_tables)
    kt_dims = (2, N_HEADS_A, HEAD_DIM_A, seq_len)
    k_spec = pl.BlockSpec((tm // seq_len,) + kt_dims, lambda i: (i, 0, 0, 0, 0))
    k_shape = jax.ShapeDtypeStruct((t // seq_len,) + kt_dims, F32)
    cast_specs, cast_shapes = _ride_along_specs(ride_along, t // tm)
    outs = pl.pallas_call(
        functools.partial(_inproj0_kernel, rope=rope_tables is not None, seq_len=seq_len, n_cast=len(ride_along)),
        grid=(t // tm,),
        in_specs=in_specs + cast_specs,
        out_specs=[row(512), k_spec, pl.BlockSpec((tm * N_HEADS_A, LANES), lambda i: (i, 0)),
                   row(512), row(CONV_DIM_B), row(LANES)] + cast_specs,
        out_shape=[jax.ShapeDtypeStruct((t, 512), BF16), k_shape, jax.ShapeDtypeStruct((t * N_HEADS_A, LANES), F32),
                   flat(512), flat(CONV_DIM_B), flat(LANES)] + cast_shapes,
        compiler_params=_params(),
        name="inproj0",
    )(*args, *ride_along)
    return outs[:6], outs[6:]


def _lambda_full(lam_ref, lam_init):
    lv = lam_ref[...]
    return (jnp.exp(jnp.sum(lv[0:1] * lv[1:2], axis=1, keepdims=True))
            - jnp.exp(jnp.sum(lv[2:3] * lv[3:4], axis=1, keepdims=True)) + lam_init)


def _attn_kernel(*refs, n_seg, tq, lam_init):
    q_ref = refs[0]
    kt_refs = refs[1:1 + n_seg]
    v_refs = refs[1 + n_seg:1 + 2 * n_seg]
    lam_ref, sub_ref, o_ref = refs[1 + 2 * n_seg:]
    lam = _lambda_full(lam_ref, lam_init)
    lane = lax.broadcasted_iota(jnp.int32, (1, A_QK), 1)
    masks = [(lane >= h * HEAD_DIM_A) & (lane < (h + 1) * HEAD_DIM_A) for h in range(N_HEADS_A)]
    keys = [r.shape[-1] for r in kt_refs]

    def stacked(qx):
        return jnp.concatenate([jnp.where(m, qx, 0.0) for m in masks], axis=0).astype(BF16)

    def exp_scores(qx, b, which):
        qs = stacked(qx)
        s = [_dot(qs, r[b, which].reshape(A_QK, n).astype(BF16)) for r, n in zip(kt_refs, keys)]
        mx = functools.reduce(jnp.maximum, [jnp.max(x, axis=1, keepdims=True) for x in s])
        return [jnp.exp2(x - mx).astype(BF16) for x in s]

    outs = []
    for b in range(kt_refs[0].shape[0]):
        q = q_ref[b * tq:(b + 1) * tq, :]
        e1 = exp_scores(q[:, 0:A_QK], b, 0)
        e2 = exp_scores(q[:, A_QK:2 * A_QK], b, 1)
        heads = []
        for h in range(N_HEADS_A):
            hrows = slice(h * tq, (h + 1) * tq)
            n1 = n2 = None
            for s, (v_ref, n) in enumerate(zip(v_refs, keys)):
                v_h = v_ref[pl.ds(b * n * N_HEADS_A + h, n, stride=N_HEADS_A), :]
                v_ext = jnp.concatenate([v_h.astype(BF16),
                                         jnp.ones((n, LANES), BF16)], axis=1)
                p1 = _dot(e1[s][hrows], v_ext)
                p2 = _dot(e2[s][hrows], v_ext)
                n1 = p1 if n1 is None else n1 + p1
                n2 = p2 if n2 is None else n2 + p2
            heads.append(n1[:, 0:LANES] / n1[:, LANES:] - lam * (n2[:, 0:LANES] / n2[:, LANES:]))
        o = jnp.concatenate(heads, axis=0)
        ms = jnp.mean(o * o, axis=1, keepdims=True)
        o = (o * lax.rsqrt(ms + 1e-5) * sub_ref[...]) * (1.0 - lam_init)
        outs.append(jnp.concatenate([o[h * tq:(h + 1) * tq] for h in range(N_HEADS_A)], axis=1))
    o_ref[...] = jnp.concatenate(outs, axis=0).astype(o_ref.dtype)


def _attention(q, kts, vs, lam_vecs, subln_w, q_len, tq, seqs_per_step, lam_init):
    n_seq = kts[0].shape[0]
    nb = seqs_per_step
    tiles = q_len // tq
    assert tiles == 1 or nb == 1
    qspec = pl.BlockSpec((nb * tq, 512), lambda b, i: (b * tiles + i, 0))
    ktspecs = [pl.BlockSpec((nb,) + kt.shape[1:], lambda b, i: (b, 0, 0, 0, 0)) for kt in kts]
    vspecs = [pl.BlockSpec((nb * kt.shape[-1] * N_HEADS_A, LANES), lambda b, i: (b, 0)) for kt in kts]
    return pl.pallas_call(
        functools.partial(_attn_kernel, n_seg=len(kts), tq=tq, lam_init=lam_init),
        grid=(n_seq // nb, tiles),
        in_specs=[qspec] + ktspecs + vspecs + [pl.BlockSpec((4, HEAD_DIM_A), lambda b, i: (0, 0)),
                                             pl.BlockSpec((1, LANES), lambda b, i: (0, 0))],
        out_specs=qspec,
        out_shape=jax.ShapeDtypeStruct((q.shape[0], A_V + D_INNER_B), BF16),
        compiler_params=_params(2),
        name="diff_attention",
    )(q, *kts, *vs, lam_vecs, subln_w.reshape(1, LANES))


def _chunk_loop(n, body, unroll=1):
    if n <= 2:
        for c in range(n):
            body(c)
    else:
        def step(c, carry):
            body(c)
            return carry
        lax.fori_loop(0, n, step, 0, unroll=unroll)


def _ssd_kernel(*refs, seq_len, n_seq, has_h0, n_cast):
    refs = refs[1:]
    n_in = 7 if has_h0 else 6
    n_out = 1 if has_h0 else 2
    ins, cast_in = refs[:n_in], refs[n_in:n_in + n_cast]
    outs = refs[n_in + n_cast:n_in + n_cast + n_out]
    cast_out = refs[n_in + n_cast + n_out:n_in + 2 * n_cast + n_out]
    ysc, s_sc, cd_sc, e_sc, htf, htb = refs[n_in + 2 * n_cast + n_out:]
    if has_h0:
        xact, dtv, z_ref, h0_ref, arow_ref, dsk_ref, nw_ref = ins
        (y_ref,), hfin_ref = outs, None
    else:
        xact, dtv, z_ref, arow_ref, dsk_ref, nw_ref = ins
        (y_ref, hfin_ref), h0_ref = outs, None
    _ride_along_cast(cast_in, cast_out)
    nc = seq_len // CHUNK
    n_pairs = N_HEADS_B // 2
    a_row = -jnp.exp(arow_ref[...]) * LOG2E

    row_i = lax.broadcasted_iota(jnp.int32, (CHUNK, CHUNK), 0)
    col_i = lax.broadcasted_iota(jnp.int32, (CHUNK, CHUNK), 1)
    lower = col_i <= row_i
    upper = col_i >= row_i
    tri = (jnp.where(lower, 1.0, 0.0).astype(BF16), jnp.where(upper, 1.0, 0.0).astype(BF16))
    lo_half = col_i < HEAD_DIM_B

    def split_pair(m):
        zero = jnp.zeros_like(m)
        return jnp.concatenate([jnp.where(lo_half, m, zero), jnp.where(lo_half, zero, m)], axis=0).astype(BF16)

    def decay_terms(c):
        rows = pl.ds(pl.multiple_of(c * CHUNK, CHUNK), CHUNK)
        dtc = dtv[rows, :]
        dtt = dtc.T[0:16, :]
        per_dir = []
        for d in range(2):
            col = _dot_exact_rhs(tri[d], dtc * a_row)
            rowm = col.T[0:16, :]
            far = CHUNK - 1 if d == 0 else 0
            tot = jnp.broadcast_to(rowm[:, far:far + 1], (16, CHUNK))
            dte = jnp.exp2(tot - rowm) * dtt
            per_dir.append((col, rowm - jnp.log2(dtt), dte, jnp.exp2(tot)))
        return per_dir

    def local_phase(c, per_dir=None):
        rows = pl.ds(pl.multiple_of(c * CHUNK, CHUNK), CHUNK)
        if per_dir is None:
            per_dir = decay_terms(c)
        for g in range(2):
            bg = xact[rows, 512 + g * LANES:512 + (g + 1) * LANES]
            cg = xact[rows, 768 + g * LANES:768 + (g + 1) * LANES]
            cbm = _dot_nt(cg.astype(BF16), bg.astype(BF16))
            bgt = bg.T
            for pq in range(2):
                pi = g * 2 + pq
                lanes = slice(pi * LANES, (pi + 1) * LANES)
                xrhs = split_pair(xact[rows, lanes])
                y_pair = None
                for d in range(2):
                    col, row_dt, dte, cdec = per_dir[d]
                    causal = lower if d == 0 else upper
                    m_parts, col_parts, bw_parts, cd_parts = [], [], [], []
                    for hh in (2 * pi, 2 * pi + 1):
                        hd = 8 * d + hh
                        colb = jnp.sum(jnp.where(col_i == hd, col, 0.0), axis=1, keepdims=True)
                        m_parts.append(cbm * jnp.exp2(jnp.where(causal, colb - row_dt[hd:hd + 1, :], NEG_BIG)))
                        col_parts.append(colb)
                        bw_parts.append(bgt * dte[hd:hd + 1, :])
                        cd_parts.append(cdec[hd:hd + 1, :])
                    yd = _dot(jnp.concatenate(m_parts, axis=1).astype(BF16), xrhs)
                    y_pair = yd if y_pair is None else y_pair + yd
                    s_sc[c, d, :, lanes] = _dot(jnp.concatenate(bw_parts, axis=1).astype(BF16), xrhs)
                    e_sc[c, d, :, lanes] = jnp.exp2(jnp.where(lo_half, col_parts[0], col_parts[1]))
                    cd_sc[c, d, :, lanes] = jnp.broadcast_to(
                        jnp.where(lo_half[0:1, :], cd_parts[0], cd_parts[1]), (8, LANES))
                ysc[rows, lanes] = y_pair

    if n_seq * nc <= SSD_UNROLL_CHUNKS:
        terms = [decay_terms(c) for c in range(n_seq * nc)]
        for c in range(n_seq * nc):
            local_phase(c, terms[c])
    else:
        _chunk_loop(n_seq * nc, local_phase, unroll=2)

    def scan_sequence(sq):
        for d, ht in ((0, htf), (1, htb)):
            for qd in range(n_pairs):
                lanes = slice(qd * LANES, (qd + 1) * LANES)
                if has_h0:
                    ht[:, lanes] = h0_ref[sq, d, lanes, :].T
                else:
                    ht[:, lanes] = jnp.zeros((D_STATE, LANES), F32)

        def scan_phase(j):
            for d, ht in ((0, htf), (1, htb)):
                c = sq * nc + (j if d == 0 else nc - 1 - j)
                rows = pl.ds(pl.multiple_of(c * CHUNK, CHUNK), CHUNK)
                for pi in range(n_pairs):
                    lanes = slice(pi * LANES, (pi + 1) * LANES)
                    g = pi // 2
                    cg = xact[rows, 768 + g * LANES:768 + (g + 1) * LANES].astype(BF16)
                    hprev = ht[:, lanes]
                    ysc[rows, lanes] += _dot(cg, hprev.astype(BF16)) * e_sc[c, d, :, lanes]
                    ht[:, lanes] = hprev * cd_sc[c, d, 0:1, lanes] + s_sc[c, d, :, lanes]

        _chunk_loop(nc, scan_phase)
        if hfin_ref is not None:
            for d, ht in ((0, htf), (1, htb)):
                for qd in range(n_pairs):
                    lanes = slice(qd * LANES, (qd + 1) * LANES)
                    hfin_ref[sq, d, lanes, :] = ht[:, lanes].T

    if n_seq == 1:
        scan_sequence(0)
    else:
        def seq_step(sq, carry):
            scan_sequence(sq)
            return carry
        lax.fori_loop(0, n_seq, seq_step, 0)

    def finish(c):
        rows = pl.ds(pl.multiple_of(c * CHUNK, CHUNK), CHUNK)
        y = ysc[rows, :] + dsk_ref[...] * xact[rows, 0:D_INNER_B]
        y = y * _silu(z_ref[rows, :])
        ms = jnp.mean(y * y, axis=1, keepdims=True)
        y_ref[rows, :] = (y * lax.rsqrt(ms + 1e-5) * nw_ref[...]).astype(y_ref.dtype)

    _chunk_loop(n_seq * nc, finish)


def _ssd(mixer_out, xbc, dt, z, h0, consts, n_batch, seq_len, seqs_per_step, ride_along=()):
    nb = seqs_per_step
    L = nb * seq_len
    nc = L // CHUNK
    row = lambda n: pl.BlockSpec((L, n), lambda b: (b, 0))
    state_spec = pl.BlockSpec((nb, 2, D_INNER_B, D_STATE), lambda b: (b, 0, 0, 0))
    has_h0 = h0 is not None
    in_specs = [pl.BlockSpec(memory_space=pl.ANY), row(CONV_DIM_B), row(LANES), row(D_INNER_B)]
    args = [mixer_out, xbc, dt, z]
    if has_h0:
        in_specs.append(state_spec)
        args.append(h0)
    for cst in consts:
        in_specs.append(pl.BlockSpec(cst.shape, lambda b: (0, 0)))
        args.append(cst)
    out_specs = [pl.BlockSpec((L, D_INNER_B), lambda b: (b, A_V // D_INNER_B))]
    out_shape = [jax.ShapeDtypeStruct(mixer_out.shape, mixer_out.dtype)]
    if not has_h0:
        out_specs.append(state_spec)
        out_shape.append(jax.ShapeDtypeStruct((n_batch, 2, D_INNER_B, D_STATE), F32))
    scratch = [pltpu.VMEM((L, D_INNER_B), F32),
               pltpu.VMEM((nc, 2, D_STATE, D_INNER_B), F32),
               pltpu.VMEM((nc, 2, 8, D_INNER_B), F32),
               pltpu.VMEM((nc, 2, CHUNK, D_INNER_B), F32),
               pltpu.VMEM((D_STATE, D_INNER_B), F32),
               pltpu.VMEM((D_STATE, D_INNER_B), F32)]
    cast_specs, cast_shapes = _ride_along_specs(ride_along, n_batch // nb)
    n_main_out = len(out_specs)
    outs = pl.pallas_call(
        functools.partial(_ssd_kernel, seq_len=seq_len, n_seq=nb, has_h0=has_h0, n_cast=len(ride_along)),
        grid=(n_batch // nb,),
        in_specs=in_specs + cast_specs,
        out_specs=out_specs + cast_specs,
        out_shape=out_shape + cast_shapes,
        scratch_shapes=scratch,
        input_output_aliases={0: 0},
        compiler_params=_params(),
        name="bidir_ssd",
    )(*args, *ride_along)
    return outs[:n_main_out], outs[n_main_out:]


def _tail_kernel(*refs, n_in, n_ctx_tiles):
    ctx_a, lat_a = refs[:n_in], refs[n_in:2 * n_in]
    (xc_ref, xl_ref, mod_ref, wout_ref, wup_ref, wdown_ref, g1_ref, b1_ref, g2_ref, b2_ref,
     oc_ref, ol_ref) = refs[2 * n_in:]
    step = pl.program_id(0)

    tile = functools.partial(_mlp_tile, mod_ref=mod_ref, wout_ref=wout_ref, g1_ref=g1_ref, b1_ref=b1_ref,
                             wup_ref=wup_ref, wdown_ref=wdown_ref, g2_ref=g2_ref, b2_ref=b2_ref)

    @pl.when(step < n_ctx_tiles)
    def _():
        tile(ctx_a, xc_ref, oc_ref)

    @pl.when(step >= n_ctx_tiles)
    def _():
        tile(lat_a, xl_ref, ol_ref)


def _mlp_tile(a_refs, x_ref, o_ref, *, mod_ref, wout_ref, g1_ref, b1_ref, wup_ref, wdown_ref, g2_ref, b2_ref):
    m = mod_ref[0]
    n_chunks = D_FF // FF_CHUNK

    def out_proj(rows):
        d = None
        off = 0
        for a_ref in a_refs:
            n = a_ref.shape[1]
            part = _dot(a_ref[rows, :].astype(BF16), wout_ref[off:off + n, :])
            d = part if d is None else d + part
            off += n
        return d

    def head(rows, d):
        x1 = _layer_norm(ALPHA * x_ref[rows, :] + m[2:3] * d, g1_ref[...], b1_ref[...])
        return x1, (x1 * (1.0 + m[4:5]) + m[3:4]).astype(BF16)

    def up(h, c):
        return _dot(h, wup_ref[:, c * FF_CHUNK:(c + 1) * FF_CHUNK])

    def down(u, acc, c):
        u = jnp.maximum(u, 0.0)
        part = _dot((u * u).astype(BF16), wdown_ref[c * FF_CHUNK:(c + 1) * FF_CHUNK, :])
        return part if acc is None else acc + part

    def finish(rows, x1, acc):
        o_ref[rows, :] = _layer_norm(ALPHA * x1 + m[5:6] * acc, g2_ref[...], b2_ref[...])

    assert sum(TAIL_SPLIT) == x_ref.shape[0]
    k = len(TAIL_SPLIT)
    starts = [sum(TAIL_SPLIT[:t]) for t in range(k)]
    rows = [slice(s, s + n) for s, n in zip(starts, TAIL_SPLIT)]
    projected = [out_proj(r) for r in rows]
    heads = [None] * k
    heads[0] = head(rows[0], projected[0])
    u_next = up(heads[0][1], 0)
    done = None
    for t in range(k):
        if t + 1 < k:
            heads[t + 1] = head(rows[t + 1], projected[t + 1])
        acc = None
        for c in range(n_chunks):
            if c + 1 < n_chunks:
                nxt = up(heads[t][1], c + 1)
            else:
                nxt = up(heads[t + 1][1], 0) if t + 1 < k else None
            u_cur, u_next = u_next, nxt
            acc = down(u_cur, acc, c)
            if c == 0 and done is not None:
                finish(*done)
        done = (rows[t], heads[t][0], acc)
    finish(*done)


def _tail(a_ctx, a_lat, x_ctx, x_lat, mod, lat_seq_len, w_out, ln1_g, ln1_b, w_up, w_down, ln2_g, ln2_b):
    tm = TOKEN_TILE
    n_ctx = x_ctx.shape[0] // tm
    n_lat = x_lat.shape[0] // tm
    tiles_per_lat_seq = lat_seq_len // tm
    ctx_row = lambda n: pl.BlockSpec((tm, n), lambda i: (jnp.minimum(i, n_ctx - 1), 0))
    lat_row = lambda n: pl.BlockSpec((tm, n), lambda i: (jnp.maximum(i - n_ctx, 0), 0))
    lat_out = pl.BlockSpec((tm, D_MODEL), lambda i: (jnp.maximum(i - n_ctx, 0), 0))
    mod_spec = _mod_spec(lambda i: jnp.where(i < n_ctx, 0, 1 + (i - n_ctx) // tiles_per_lat_seq))
    vec = lambda v: v.reshape(1, D_MODEL)
    consts = [w_out, w_up, w_down, vec(ln1_g), vec(ln1_b), vec(ln2_g), vec(ln2_b)]
    return pl.pallas_call(
        functools.partial(_tail_kernel, n_in=len(a_ctx), n_ctx_tiles=n_ctx),
        grid=(n_ctx + n_lat,),
        in_specs=[ctx_row(a.shape[1]) for a in a_ctx] + [lat_row(a.shape[1]) for a in a_lat]
                 + [ctx_row(D_MODEL), lat_row(D_MODEL), mod_spec] + [_const_spec(v.shape) for v in consts],
        out_specs=[ctx_row(D_MODEL), lat_out],
        out_shape=[jax.ShapeDtypeStruct(x_ctx.shape, F32), jax.ShapeDtypeStruct(x_lat.shape, F32)],
        compiler_params=_params(),
        name="outproj_mlp",
    )(*a_ctx, *a_lat, x_ctx, x_lat, mod, *consts)


def _pool_kernel(x_ref, mod_ref, win_ref, pw_ref, ps_ref, o_ref, *, seq_len):
    L = seq_len
    m = mod_ref[0]
    t_i = lax.broadcasted_iota(jnp.int32, (L, L), 0)
    s_i = lax.broadcasted_iota(jnp.int32, (L, L), 1)
    t_g = lax.broadcasted_iota(jnp.int32, (L, POOL_GROUP_DIM), 0)
    bands, scales = [], []
    for w in POOL_WINDOWS:
        lo, hi = w // 2, w - w // 2
        bands.append(jnp.where((s_i >= t_i - lo) & (s_i < t_i + hi), 1.0 / w, 0.0).astype(BF16))
        scales.append(w / (jnp.minimum(t_g + hi, L) - jnp.maximum(t_g - lo, 0)).astype(F32))
    n_seq = x_ref.shape[0] // L
    seq_rows = [slice(s * L, (s + 1) * L) for s in range(n_seq)]
    hs = [(x_ref[r, :] * (1.0 + m[1:2]) + m[0:1]).astype(BF16) for r in seq_rows]
    us = [_dot(h, win_ref[...]) for h in hs]
    n_groups = len(POOL_WINDOWS)
    cols = [slice(g * POOL_GROUP_DIM, (g + 1) * POOL_GROUP_DIM) for g in range(n_groups)]

    def window_means(g):
        out = []
        for s in range(n_seq):
            ug = us[s][:, cols[g]]
            ug_hi = ug.astype(BF16)
            ug_lo = (ug - ug_hi.astype(F32)).astype(BF16)
            out.append((ug, _dot(bands[g], ug_hi) + _dot(bands[g], ug_lo)))
        return out

    def mix(g, means):
        return [_dot((wm * scales[g] - ug).astype(BF16), pw_ref[g]) * ps_ref[:, cols[g]] for ug, wm in means]

    mixed = []
    pending = window_means(0)
    for g in range(n_groups):
        nxt = window_means(g + 1) if g + 1 < n_groups else None
        mixed.append(mix(g, pending))
        pending = nxt
    for s in range(n_seq):
        o_ref[seq_rows[s], :] = jnp.concatenate([mixed[g][s] for g in range(n_groups)], axis=1).astype(o_ref.dtype)


def _pool_mixer(x, mod, row_of_tile, w_in, pool_w, pool_scale, seq_len, tm):
    t = x.shape[0]
    row = pl.BlockSpec((tm, D_MODEL), lambda i: (i, 0))
    consts = [w_in, pool_w, pool_scale.reshape(1, D_MODEL)]
    return pl.pallas_call(
        functools.partial(_pool_kernel, seq_len=seq_len),
        grid=(t // tm,),
        in_specs=[row, _mod_spec(row_of_tile)] + [_const_spec(cst.shape) for cst in consts],
        out_specs=row,
        out_shape=jax.ShapeDtypeStruct((t, D_MODEL), BF16),
        compiler_params=_params(),
        name="pool_mixer",
    )(x, mod, *consts)


def _rope_tables(rows):
    r, col = np.meshgrid(np.arange(rows), np.arange(GRID_W), indexing="ij")
    r = r.reshape(-1).astype(np.float64)
    col = col.reshape(-1).astype(np.float64)
    n_freq = HEAD_DIM_A // 4
    inv = ROPE_BASE ** (-np.arange(n_freq, dtype=np.float64) / n_freq)
    ang = np.concatenate([r[:, None] * inv, col[:, None] * inv], -1)
    reps = 512 // HEAD_DIM_A
    cosf = np.tile(np.concatenate([np.cos(ang), np.cos(ang)], -1), (1, reps)).astype(np.float32)
    sinf = np.tile(np.concatenate([-np.sin(ang), np.sin(ang)], -1), (1, reps)).astype(np.float32)
    return jnp.asarray(cosf), jnp.asarray(sinf)


def kernel(x_prompt, x_sample, cache_k_l0, cache_v_l0, state_ssm_l0, c, c_ctx, w_mod_l0, b_mod_l0, w_in_l0, conv_w_l0, conv_b_l0, a_log_l0, dt_bias_l0, d_skip_l0, ssm_norm_w_l0, lam_q1_l0, lam_k1_l0, lam_q2_l0, lam_k2_l0, subln_w_l0, w_out_l0, ln1_g_l0, ln1_b_l0, w_up_l0, w_down_l0, ln2_g_l0, ln2_b_l0, w_mod_l1, b_mod_l1, w_in_l1, pool_w_l1, pool_scale_l1, w_out_l1, ln1_g_l1, ln1_b_l1, w_up_l1, w_down_l1, ln2_g_l1, ln2_b_l1):
    n_ctx, l_ctx, _ = x_prompt.shape
    n_lat, l_lat, _ = x_sample.shape
    past = cache_k_l0.shape[1]
    xc = x_prompt.reshape(n_ctx * l_ctx, D_MODEL)
    xl = x_sample.reshape(n_lat * l_lat, D_MODEL)

    cond8 = jnp.concatenate([c_ctx[None, :], c, jnp.zeros((8 - 1 - n_lat, D_MODEL), F32)], axis=0)
    ctx_row = lambda i: 0

    mod0 = _modulation(cond8, w_mod_l0, b_mod_l0)
    n_main = 4 * A_QK + A_V + D_INNER_B + CONV_DIM_B
    w_main = w_in_l0.astype(BF16)
    w_dt = jnp.pad(w_in_l0[:, n_main:], ((0, 0), (0, LANES - 2 * N_HEADS_B))).astype(BF16)
    lane_pad = lambda v: jnp.pad(v.reshape(1, -1), ((0, 0), (0, LANES - 2 * N_HEADS_B)))
    conv_consts = (conv_w_l0, conv_b_l0.reshape(1, CONV_DIM_B), lane_pad(dt_bias_l0))
    ssd_consts = [lane_pad(a_log_l0),
                  jnp.repeat(d_skip_l0, HEAD_DIM_B).reshape(1, D_INNER_B),
                  ssm_norm_w_l0.reshape(1, D_INNER_B)]
    lam_vecs = jnp.stack([lam_q1_l0, lam_k1_l0, lam_q2_l0, lam_k2_l0], axis=0)
    lam_init = 0.8 - 0.6 * math.exp(-0.3 * 0)

    (qc, kc, vc, zc, xbc_c, dt_c), (w_out0, w_up0, w_down0, w_out1, w_up1, w_down1, w_in1, pool_w) = _inproj0(
        xc, mod0, ctx_row, w_main, w_dt, *conv_consts, l_ctx, TOKEN_TILE,
        ride_along=(w_out_l0, w_up_l0, w_down_l0, w_out_l1, w_up_l1, w_down_l1, w_in_l1,
                    pool_w_l1.reshape(-1, POOL_GROUP_DIM)))
    pool_w = pool_w.reshape(pool_w_l1.shape)
    oa_c = _attention(qc, [kc], [vc], lam_vecs, subln_w_l0, l_ctx, l_ctx, CTX_SEQS_PER_STEP, lam_init)
    (mix0_c, new_ssm), _ = _ssd(oa_c, xbc_c, dt_c, zc, None, ssd_consts, n_ctx, l_ctx, SSD_CTX_SEQS_PER_STEP)

    (ql, kl, vl, zl, xbc_l, dt_l), _ = _inproj0(xl, mod0, lambda i: 1 + i, w_main, w_dt, *conv_consts, l_lat, l_lat,
                                                rope_tables=_rope_tables(l_lat // GRID_W))
    ck = jnp.transpose(cache_k_l0, (0, 2, 3, 4, 1))
    cv = cache_v_l0.reshape(n_lat * past * N_HEADS_A, 2 * HEAD_DIM_A)
    oa_l = _attention(ql, [ck, kl], [cv, vl], lam_vecs, subln_w_l0, l_lat, ATTN_Q_TILE, 1, lam_init)
    h0 = state_ssm_l0.reshape(n_lat, 2, D_INNER_B, D_STATE)
    ((mix0_l,), _) = _ssd(oa_l, xbc_l, dt_l, zl, h0, ssd_consts, n_lat, l_lat, 1)
    tail0 = (w_out0, ln1_g_l0, ln1_b_l0, w_up0, w_down0, ln2_g_l0, ln2_b_l0)
    xc, xl = _tail([mix0_c], [mix0_l], xc, xl, mod0, l_lat, *tail0)

    mod1 = _modulation(cond8, w_mod_l1, b_mod_l1)
    tail1 = (w_out1, ln1_g_l1, ln1_b_l1, w_up1, w_down1, ln2_g_l1, ln2_b_l1)
    mix_c = _pool_mixer(xc, mod1, ctx_row, w_in1, pool_w, pool_scale_l1, l_ctx, TOKEN_TILE)
    mix_l = _pool_mixer(xl, mod1, lambda i: 1 + i, w_in1, pool_w, pool_scale_l1, l_lat, l_lat)
    xc, xl = _tail([mix_c], [mix_l], xc, xl, mod1, l_lat, *tail1)

    return (xc.reshape(n_ctx, l_ctx, D_MODEL), xl.reshape(n_lat, l_lat, D_MODEL),
            jnp.transpose(kc, (0, 4, 1, 2, 3)),
            vc.reshape(n_ctx, l_ctx, N_HEADS_A, 2 * HEAD_DIM_A),
            new_ssm.reshape(n_ctx, 2, N_HEADS_B, HEAD_DIM_B, D_STATE))
```

```python
import functools
import math

import jax
import jax.numpy as jnp
import numpy as np
from jax import lax
from jax.experimental import pallas as pl
from jax.experimental.pallas import tpu as pltpu

F32 = jnp.float32
BF16 = jnp.bfloat16

D_MODEL = 1024
N_MOD = 6
N_HEADS_A = 4
HEAD_DIM_A = 64
A_QK = N_HEADS_A * HEAD_DIM_A
A_V = 2 * A_QK
D_INNER_B = 512
HEAD_DIM_B = 64
N_HEADS_B = 8
D_STATE = 128
D_CONV = 5
CONV_DIM_B = 1024
CHUNK = 128
GRID_W = 64
POOL_WINDOWS = (2, 4, 8, 16)
POOL_GROUP_DIM = 256
D_FF = 4096
ROPE_BASE = 10000.0
LN_EPS = 1e-5
DEPTH = 2
ALPHA = (2 * DEPTH) ** 0.25
NEG_BIG = -1e30
LOG2E = 1.4426950408889634

LANES = 128
TOKEN_TILE = 512
ATTN_Q_TILE = 256
CTX_SEQS_PER_STEP = 4
SSD_CTX_SEQS_PER_STEP = 1
SSD_UNROLL_CHUNKS = 4
FF_CHUNK = 1024
CONV_COL_BLOCK = 256
TAIL_SPLIT = (256, 256)
VMEM_LIMIT = 56 * 1024 * 1024


def _dot(a, b):
    return jnp.dot(a, b, preferred_element_type=F32)


def _dot_nt(a, b):
    return lax.dot_general(a, b, (((1,), (1,)), ((), ())), preferred_element_type=F32)


def _bf16_pieces(a):
    a1 = a.astype(BF16)
    r1 = a - a1.astype(F32)
    a2 = r1.astype(BF16)
    a3 = (r1 - a2.astype(F32)).astype(BF16)
    return a1, a2, a3


def _dot_exact_rhs(t01, a):
    a1, a2, a3 = _bf16_pieces(a)
    return _dot(t01, a1) + _dot(t01, a2) + _dot(t01, a3)


def _sigmoid(x):
    return 1.0 / (1.0 + jnp.exp(-x))


def _silu(x):
    return x * _sigmoid(x)


def _layer_norm(x, g, b):
    mu = jnp.mean(x, axis=-1, keepdims=True)
    xc = x - mu
    var = jnp.mean(xc * xc, axis=-1, keepdims=True)
    return xc * lax.rsqrt(var + LN_EPS) * g + b


def _const_spec(shape):
    nd = len(shape)
    return pl.BlockSpec(shape, lambda *_: (0,) * nd, pipeline_mode=pl.Buffered(1))


def _params(n_axes=1):
    return pltpu.CompilerParams(dimension_semantics=("arbitrary",) * n_axes,
                                vmem_limit_bytes=VMEM_LIMIT)


def _mod_kernel(cond_ref, w0_ref, b0_ref, w1_ref, b1_ref, o0_ref, o1_ref, *, n_first):
    s = _silu(cond_ref[...]).astype(BF16)
    j = pl.program_id(0)

    @pl.when(j < n_first)
    def _():
        o0_ref[...] = _dot(s, w0_ref[...].astype(BF16)) + b0_ref[...]

    @pl.when(j >= n_first)
    def _():
        o1_ref[...] = _dot(s, w1_ref[...].astype(BF16)) + b1_ref[...]


def _modulation(cond8, w_mod0, b_mod0, w_mod1, b_mod1):
    tn = 1024
    n = w_mod0.shape[1]
    steps = n // tn
    first = lambda j: (0, jnp.minimum(j, steps - 1))
    second = lambda j: (0, jnp.maximum(j - steps, 0))
    outs = pl.pallas_call(
        functools.partial(_mod_kernel, n_first=steps),
        grid=(2 * steps,),
        in_specs=[pl.BlockSpec((8, D_MODEL), lambda j: (0, 0)),
                  pl.BlockSpec((D_MODEL, tn), first), pl.BlockSpec((1, tn), first),
                  pl.BlockSpec((D_MODEL, tn), second), pl.BlockSpec((1, tn), second)],
        out_specs=[pl.BlockSpec((8, tn), first), pl.BlockSpec((8, tn), second)],
        out_shape=[jax.ShapeDtypeStruct((8, n), F32)] * 2,
        compiler_params=_params(),
        name="modulation",
    )(cond8, w_mod0, b_mod0.reshape(1, n), w_mod1, b_mod1.reshape(1, n))
    return [o.reshape(8, N_MOD, D_MODEL) for o in outs]


def _mod_spec(row_of_tile):
    return pl.BlockSpec((1, N_MOD, D_MODEL), lambda i: (row_of_tile(i), 0, 0))


def _rope(t, cosf, sinf):
    lane = lax.broadcasted_iota(jnp.int32, t.shape, 1)
    first = (lane & (HEAD_DIM_A - 1)) < HEAD_DIM_A // 2
    width = t.shape[1]
    swapped = jnp.where(first, pltpu.roll(t, width - HEAD_DIM_A // 2, 1),
                        pltpu.roll(t, HEAD_DIM_A // 2, 1))
    return t * cosf + swapped * sinf


def _conv_silu(xbc, cw_ref, cb_ref, cols, seq_len):
    assert xbc.shape[0] == seq_len
    half = D_CONV // 2
    edge = 8
    offsets = [j - half for j in range(D_CONV) if j != half]
    shifted = {o: pltpu.roll(xbc, (-o) % seq_len, 0) for o in offsets}

    def taps(lo, hi, masked):
        acc = cb_ref[:, cols] + cw_ref[half:half + 1, cols] * xbc[lo:hi]
        pos = lo + lax.broadcasted_iota(jnp.int32, (hi - lo, xbc.shape[1]), 0)
        for o in offsets:
            s = shifted[o][lo:hi]
            if masked:
                s = jnp.where((pos >= -o) if o < 0 else (pos < seq_len - o), s, 0.0)
            acc = acc + cw_ref[o + half:o + half + 1, cols] * s
        return acc

    acc = jnp.concatenate([taps(0, edge, True), taps(edge, seq_len - edge, False),
                           taps(seq_len - edge, seq_len, True)], axis=0)
    return _silu(acc)


def _softplus(t):
    return jnp.maximum(t, 0.0) + jnp.log(1.0 + jnp.exp(-jnp.abs(t)))


def _ride_along_specs(mats, n_steps):
    specs, shapes = [], []
    for w in mats:
        assert w.shape[0] % n_steps == 0
        specs.append(pl.BlockSpec((w.shape[0] // n_steps, w.shape[1]), lambda i: (i, 0)))
        shapes.append(jax.ShapeDtypeStruct(w.shape, BF16))
    return specs, shapes


def _ride_along_cast(src_refs, dst_refs):
    for src, dst in zip(src_refs, dst_refs):
        dst[...] = src[...].astype(BF16)


def _inproj0_kernel(*refs, rope, seq_len, n_cast):
    n_in = 9 if rope else 7
    ins, cast_in = refs[:n_in], refs[n_in:n_in + n_cast]
    (q_ref, k_ref, v_ref, z_ref, xbc_ref, dt_ref) = refs[n_in + n_cast:n_in + n_cast + 6]
    cast_out = refs[n_in + n_cast + 6:]
    if rope:
        x_ref, mod_ref, w_ref, wdt_ref, cw_ref, cb_ref, dtb_ref, cos_ref, sin_ref = ins
    else:
        x_ref, mod_ref, w_ref, wdt_ref, cw_ref, cb_ref, dtb_ref = ins
    _ride_along_cast(cast_in, cast_out)
    m = mod_ref[0]
    tasks = []
    for b in range(x_ref.shape[0] // seq_len):
        rows = slice(b * seq_len, (b + 1) * seq_len)
        h = (x_ref[rows, :] * (1.0 + m[1:2]) + m[0:1]).astype(BF16)

        for c0 in range(0, CONV_DIM_B, CONV_COL_BLOCK):
            cols = slice(c0, c0 + CONV_COL_BLOCK)

            def conv_mm(h=h, c0=c0):
                return _dot(h, w_ref[:, 2048 + c0:2048 + c0 + CONV_COL_BLOCK])

            def conv_vec(raw, rows=rows, cols=cols):
                xbc_ref[rows, cols] = _conv_silu(raw, cw_ref, cb_ref, cols, seq_len)

            tasks.append((conv_mm, conv_vec))

        def qk_mm(h=h):
            return _dot(h, wdt_ref[...]), _dot(h, w_ref[:, 0:512]), _dot(h, w_ref[:, 512:1024])

        def qk_vec(res, rows=rows, b=b):
            dt_raw, q, k = res
            dt_ref[rows, :] = _softplus(dt_raw + dtb_ref[...])
            if rope:
                q = _rope(q, cos_ref[...], sin_ref[...])
                k = _rope(k, cos_ref[...], sin_ref[...])
            q_ref[rows, :] = (q * (HEAD_DIM_A ** -0.5 * LOG2E)).astype(BF16)
            for blk in range(2 * A_QK // LANES):
                t = k[:, blk * LANES:(blk + 1) * LANES].T
                which, head = blk // 2, (blk % 2) * 2
                k_ref[b, which, head] = t[0:HEAD_DIM_A]
                k_ref[b, which, head + 1] = t[HEAD_DIM_A:2 * HEAD_DIM_A]

        def vz_mm(h=h):
            return _dot(h, w_ref[:, 1024:1536]), _dot(h, w_ref[:, 1536:2048])

        def vz_vec(res, rows=rows, b=b):
            vals, z = res
            for hh in range(N_HEADS_A):
                v_ref[pl.ds(b * seq_len * N_HEADS_A + hh, seq_len, stride=N_HEADS_A), :] = (
                    vals[:, hh * LANES:(hh + 1) * LANES])
            z_ref[rows, :] = z

        tasks += [(qk_mm, qk_vec), (vz_mm, vz_vec)]

    pending = None
    for mm, vec in tasks:
        res = mm()
        if pending is not None:
            pending[0](pending[1])
        pending = (vec, res)
    pending[0](pending[1])


def _inproj0(x, mod, row_of_tile, w_main, w_dt, conv_w, conv_b, dt_bias, seq_len, tm, rope_tables=None,
             ride_along=()):
    t = x.shape[0]
    assert tm % seq_len == 0
    row = lambda n: pl.BlockSpec((tm, n), lambda i: (i, 0))
    flat = lambda n: jax.ShapeDtypeStruct((t, n), F32)
    consts = [w_main, w_dt, conv_w, conv_b, dt_bias]
    in_specs = [row(D_MODEL), _mod_spec(row_of_tile)] + [_const_spec(cst.shape) for cst in consts]
    args = [x, mod] + consts
    if rope_tables is not None:
        assert tm == seq_len
        in_specs += [_const_spec(tab.shape) for tab in rope_tables]
        args += list(rope_tables)
    kt_dims = (2, N_HEADS_A, HEAD_DIM_A, seq_len)
    k_spec = pl.BlockSpec((tm // seq_len,) + kt_dims, lambda i: (i, 0, 0, 0, 0))
    k_shape = jax.ShapeDtypeStruct((t // seq_len,) + kt_dims, F32)
    cast_specs, cast_shapes = _ride_along_specs(ride_along, t // tm)
    outs = pl.pallas_call(
        functools.partial(_inproj0_kernel, rope=rope_tables is not None, seq_len=seq_len, n_cast=len(ride_along)),
        grid=(t // tm,),
        in_specs=in_specs + cast_specs,
        out_specs=[row(512), k_spec, pl.BlockSpec((tm * N_HEADS_A, LANES), lambda i: (i, 0)),
                   row(512), row(CONV_DIM_B), row(LANES)] + cast_specs,
        out_shape=[jax.ShapeDtypeStruct((t, 512), BF16), k_shape, jax.ShapeDtypeStruct((t * N_HEADS_A, LANES), F32),
                   flat(512), flat(CONV_DIM_B), flat(LANES)] + cast_shapes,
        compiler_params=_params(),
        name="inproj0",
    )(*args, *ride_along)
    return outs[:6], outs[6:]


def _lambda_full(lam_ref, lam_init):
    lv = lam_ref[...]
    return (jnp.exp(jnp.sum(lv[0:1] * lv[1:2], axis=1, keepdims=True))
            - jnp.exp(jnp.sum(lv[2:3] * lv[3:4], axis=1, keepdims=True)) + lam_init)


def _attn_kernel(*refs, n_seg, tq, lam_init):
    q_ref = refs[0]
    kt_refs = refs[1:1 + n_seg]
    v_refs = refs[1 + n_seg:1 + 2 * n_seg]
    lam_ref, sub_ref, o_ref = refs[1 + 2 * n_seg:]
    lam = _lambda_full(lam_ref, lam_init)
    lane = lax.broadcasted_iota(jnp.int32, (1, A_QK), 1)
    masks = [(lane >= h * HEAD_DIM_A) & (lane < (h + 1) * HEAD_DIM_A) for h in range(N_HEADS_A)]
    keys = [r.shape[-1] for r in kt_refs]

    def stacked(qx):
        return jnp.concatenate([jnp.where(m, qx, 0.0) for m in masks], axis=0).astype(BF16)

    def exp_scores(qx, b, which):
        qs = stacked(qx)
        s = [_dot(qs, r[b, which].reshape(A_QK, n).astype(BF16)) for r, n in zip(kt_refs, keys)]
        mx = functools.reduce(jnp.maximum, [jnp.max(x, axis=1, keepdims=True) for x in s])
        return [jnp.exp2(x - mx).astype(BF16) for x in s]

    outs = []
    for b in range(kt_refs[0].shape[0]):
        q = q_ref[b * tq:(b + 1) * tq, :]
        e1 = exp_scores(q[:, 0:A_QK], b, 0)
        e2 = exp_scores(q[:, A_QK:2 * A_QK], b, 1)
        heads = []
        for h in range(N_HEADS_A):
            hrows = slice(h * tq, (h + 1) * tq)
            n1 = n2 = None
            for s, (v_ref, n) in enumerate(zip(v_refs, keys)):
                v_h = v_ref[pl.ds(b * n * N_HEADS_A + h, n, stride=N_HEADS_A), :]
                v_ext = jnp.concatenate([v_h.astype(BF16),
                                         jnp.ones((n, LANES), BF16)], axis=1)
                p1 = _dot(e1[s][hrows], v_ext)
                p2 = _dot(e2[s][hrows], v_ext)
                n1 = p1 if n1 is None else n1 + p1
                n2 = p2 if n2 is None else n2 + p2
            heads.append(n1[:, 0:LANES] / n1[:, LANES:] - lam * (n2[:, 0:LANES] / n2[:, LANES:]))
        o = jnp.concatenate(heads, axis=0)
        ms = jnp.mean(o * o, axis=1, keepdims=True)
        o = (o * lax.rsqrt(ms + 1e-5) * sub_ref[...]) * (1.0 - lam_init)
        outs.append(jnp.concatenate([o[h * tq:(h + 1) * tq] for h in range(N_HEADS_A)], axis=1))
    o_ref[...] = jnp.concatenate(outs, axis=0).astype(o_ref.dtype)


def _attention(q, kts, vs, lam_vecs, subln_w, q_len, tq, seqs_per_step, lam_init):
    n_seq = kts[0].shape[0]
    nb = seqs_per_step
    tiles = q_len // tq
    assert tiles == 1 or nb == 1
    qspec = pl.BlockSpec((nb * tq, 512), lambda b, i: (b * tiles + i, 0))
    ktspecs = [pl.BlockSpec((nb,) + kt.shape[1:], lambda b, i: (b, 0, 0, 0, 0)) for kt in kts]
    vspecs = [pl.BlockSpec((nb * kt.shape[-1] * N_HEADS_A, LANES), lambda b, i: (b, 0)) for kt in kts]
    return pl.pallas_call(
        functools.partial(_attn_kernel, n_seg=len(kts), tq=tq, lam_init=lam_init),
        grid=(n_seq // nb, tiles),
        in_specs=[qspec] + ktspecs + vspecs + [pl.BlockSpec((4, HEAD_DIM_A), lambda b, i: (0, 0)),
                                             pl.BlockSpec((1, LANES), lambda b, i: (0, 0))],
        out_specs=qspec,
        out_shape=jax.ShapeDtypeStruct((q.shape[0], A_V + D_INNER_B), BF16),
        compiler_params=_params(2),
        name="diff_attention",
    )(q, *kts, *vs, lam_vecs, subln_w.reshape(1, LANES))


def _chunk_loop(n, body, unroll=1):
    if n <= 2:
        for c in range(n):
            body(c)
    else:
        def step(c, carry):
            body(c)
            return carry
        lax.fori_loop(0, n, step, 0, unroll=unroll)


def _ssd_kernel(*refs, seq_len, n_seq, has_h0, n_cast):
    refs = refs[1:]
    n_in = 7 if has_h0 else 6
    n_out = 1 if has_h0 else 2
    ins, cast_in = refs[:n_in], refs[n_in:n_in + n_cast]
    outs = refs[n_in + n_cast:n_in + n_cast + n_out]
    cast_out = refs[n_in + n_cast + n_out:n_in + 2 * n_cast + n_out]
    ysc, s_sc, cd_sc, e_sc, htf, htb = refs[n_in + 2 * n_cast + n_out:]
    if has_h0:
        xact, dtv, z_ref, h0_ref, arow_ref, dsk_ref, nw_ref = ins
        (y_ref,), hfin_ref = outs, None
    else:
        xact, dtv, z_ref, arow_ref, dsk_ref, nw_ref = ins
        (y_ref, hfin_ref), h0_ref = outs, None
    _ride_along_cast(cast_in, cast_out)
    nc = seq_len // CHUNK
    n_pairs = N_HEADS_B // 2
    a_row = -jnp.exp(arow_ref[...]) * LOG2E

    row_i = lax.broadcasted_iota(jnp.int32, (CHUNK, CHUNK), 0)
    col_i = lax.broadcasted_iota(jnp.int32, (CHUNK, CHUNK), 1)
    lower = col_i <= row_i
    upper = col_i >= row_i
    tri = (jnp.where(lower, 1.0, 0.0).astype(BF16), jnp.where(upper, 1.0, 0.0).astype(BF16))
    lo_half = col_i < HEAD_DIM_B

    def split_pair(m):
        zero = jnp.zeros_like(m)
        return jnp.concatenate([jnp.where(lo_half, m, zero), jnp.where(lo_half, zero, m)], axis=0).astype(BF16)

    def decay_terms(c):
        rows = pl.ds(pl.multiple_of(c * CHUNK, CHUNK), CHUNK)
        dtc = dtv[rows, :]
        dtt = dtc.T[0:16, :]
        per_dir = []
        for d in range(2):
            col = _dot_exact_rhs(tri[d], dtc * a_row)
            rowm = col.T[0:16, :]
            far = CHUNK - 1 if d == 0 else 0
            tot = jnp.broadcast_to(rowm[:, far:far + 1], (16, CHUNK))
            dte = jnp.exp2(tot - rowm) * dtt
            per_dir.append((col, rowm - jnp.log2(dtt), dte, jnp.exp2(tot)))
        return per_dir

    def local_phase(c, per_dir=None):
        rows = pl.ds(pl.multiple_of(c * CHUNK, CHUNK), CHUNK)
        if per_dir is None:
            per_dir = decay_terms(c)
        for g in range(2):
            bg = xact[rows, 512 + g * LANES:512 + (g + 1) * LANES]
            cg = xact[rows, 768 + g * LANES:768 + (g + 1) * LANES]
            cbm = _dot_nt(cg.astype(BF16), bg.astype(BF16))
            bgt = bg.T
            for pq in range(2):
                pi = g * 2 + pq
                lanes = slice(pi * LANES, (pi + 1) * LANES)
                xrhs = split_pair(xact[rows, lanes])
                y_pair = None
                for d in range(2):
                    col, row_dt, dte, cdec = per_dir[d]
                    causal = lower if d == 0 else upper
                    m_parts, col_parts, bw_parts, cd_parts = [], [], [], []
                    for hh in (2 * pi, 2 * pi + 1):
                        hd = 8 * d + hh
                        colb = jnp.sum(jnp.where(col_i == hd, col, 0.0), axis=1, keepdims=True)
                        m_parts.append(cbm * jnp.exp2(jnp.where(causal, colb - row_dt[hd:hd + 1, :], NEG_BIG)))
                        col_parts.append(colb)
                        bw_parts.append(bgt * dte[hd:hd + 1, :])
                        cd_parts.append(cdec[hd:hd + 1, :])
                    yd = _dot(jnp.concatenate(m_parts, axis=1).astype(BF16), xrhs)
                    y_pair = yd if y_pair is None else y_pair + yd
                    s_sc[c, d, :, lanes] = _dot(jnp.concatenate(bw_parts, axis=1).astype(BF16), xrhs)
                    e_sc[c, d, :, lanes] = jnp.exp2(jnp.where(lo_half, col_parts[0], col_parts[1]))
                    cd_sc[c, d, :, lanes] = jnp.broadcast_to(
                        jnp.where(lo_half[0:1, :], cd_parts[0], cd_parts[1]), (8, LANES))
                ysc[rows, lanes] = y_pair

    if n_seq * nc <= SSD_UNROLL_CHUNKS:
        terms = [decay_terms(c) for c in range(n_seq * nc)]
        for c in range(n_seq * nc):
            local_phase(c, terms[c])
    else:
        _chunk_loop(n_seq * nc, local_phase, unroll=2)

    def scan_sequence(sq):
        for d, ht in ((0, htf), (1, htb)):
            for qd in range(n_pairs):
                lanes = slice(qd * LANES, (qd + 1) * LANES)
                if has_h0:
                    ht[:, lanes] = h0_ref[sq, d, lanes, :].T
                else:
                    ht[:, lanes] = jnp.zeros((D_STATE, LANES), F32)

        def scan_phase(j):
            for d, ht in ((0, htf), (1, htb)):
                c = sq * nc + (j if d == 0 else nc - 1 - j)
                rows = pl.ds(pl.multiple_of(c * CHUNK, CHUNK), CHUNK)
                for pi in range(n_pairs):
                    lanes = slice(pi * LANES, (pi + 1) * LANES)
                    g = pi // 2
                    cg = xact[rows, 768 + g * LANES:768 + (g + 1) * LANES].astype(BF16)
                    hprev = ht[:, lanes]
                    ysc[rows, lanes] += _dot(cg, hprev.astype(BF16)) * e_sc[c, d, :, lanes]
                    ht[:, lanes] = hprev * cd_sc[c, d, 0:1, lanes] + s_sc[c, d, :, lanes]

        _chunk_loop(nc, scan_phase)
        if hfin_ref is not None:
            for d, ht in ((0, htf), (1, htb)):
                for qd in range(n_pairs):
                    lanes = slice(qd * LANES, (qd + 1) * LANES)
                    hfin_ref[sq, d, lanes, :] = ht[:, lanes].T

    if n_seq == 1:
        scan_sequence(0)
    else:
        def seq_step(sq, carry):
            scan_sequence(sq)
            return carry
        lax.fori_loop(0, n_seq, seq_step, 0)

    def finish(c):
        rows = pl.ds(pl.multiple_of(c * CHUNK, CHUNK), CHUNK)
        y = ysc[rows, :] + dsk_ref[...] * xact[rows, 0:D_INNER_B]
        y = y * _silu(z_ref[rows, :])
        ms = jnp.mean(y * y, axis=1, keepdims=True)
        y_ref[rows, :] = (y * lax.rsqrt(ms + 1e-5) * nw_ref[...]).astype(y_ref.dtype)

    _chunk_loop(n_seq * nc, finish)


def _ssd(mixer_out, xbc, dt, z, h0, consts, n_batch, seq_len, seqs_per_step, ride_along=()):
    nb = seqs_per_step
    L = nb * seq_len
    nc = L // CHUNK
    row = lambda n: pl.BlockSpec((L, n), lambda b: (b, 0))
    state_spec = pl.BlockSpec((nb, 2, D_INNER_B, D_STATE), lambda b: (b, 0, 0, 0))
    has_h0 = h0 is not None
    in_specs = [pl.BlockSpec(memory_space=pl.ANY), row(CONV_DIM_B), row(LANES), row(D_INNER_B)]
    args = [mixer_out, xbc, dt, z]
    if has_h0:
        in_specs.append(state_spec)
        args.append(h0)
    for cst in consts:
        in_specs.append(pl.BlockSpec(cst.shape, lambda b: (0, 0)))
        args.append(cst)
    out_specs = [pl.BlockSpec((L, D_INNER_B), lambda b: (b, A_V // D_INNER_B))]
    out_shape = [jax.ShapeDtypeStruct(mixer_out.shape, mixer_out.dtype)]
    if not has_h0:
        out_specs.append(state_spec)
        out_shape.append(jax.ShapeDtypeStruct((n_batch, 2, D_INNER_B, D_STATE), F32))
    scratch = [pltpu.VMEM((L, D_INNER_B), F32),
               pltpu.VMEM((nc, 2, D_STATE, D_INNER_B), F32),
               pltpu.VMEM((nc, 2, 8, D_INNER_B), F32),
               pltpu.VMEM((nc, 2, CHUNK, D_INNER_B), F32),
               pltpu.VMEM((D_STATE, D_INNER_B), F32),
               pltpu.VMEM((D_STATE, D_INNER_B), F32)]
    cast_specs, cast_shapes = _ride_along_specs(ride_along, n_batch // nb)
    n_main_out = len(out_specs)
    outs = pl.pallas_call(
        functools.partial(_ssd_kernel, seq_len=seq_len, n_seq=nb, has_h0=has_h0, n_cast=len(ride_along)),
        grid=(n_batch // nb,),
        in_specs=in_specs + cast_specs,
        out_specs=out_specs + cast_specs,
        out_shape=out_shape + cast_shapes,
        scratch_shapes=scratch,
        input_output_aliases={0: 0},
        compiler_params=_params(),
        name="bidir_ssd",
    )(*args, *ride_along)
    return outs[:n_main_out], outs[n_main_out:]


def _tail_kernel(*refs, n_in, n_ctx_tiles):
    ctx_a, lat_a = refs[:n_in], refs[n_in:2 * n_in]
    (xc_ref, xl_ref, mod_ref, wout_ref, wup_ref, wdown_ref, g1_ref, b1_ref, g2_ref, b2_ref,
     oc_ref, ol_ref) = refs[2 * n_in:]
    step = pl.program_id(0)

    tile = functools.partial(_mlp_tile, mod_ref=mod_ref, wout_ref=wout_ref, g1_ref=g1_ref, b1_ref=b1_ref,
                             wup_ref=wup_ref, wdown_ref=wdown_ref, g2_ref=g2_ref, b2_ref=b2_ref)

    @pl.when(step < n_ctx_tiles)
    def _():
        tile(ctx_a, xc_ref, oc_ref)

    @pl.when(step >= n_ctx_tiles)
    def _():
        tile(lat_a, xl_ref, ol_ref)


def _mlp_tile(a_refs, x_ref, o_ref, *, mod_ref, wout_ref, g1_ref, b1_ref, wup_ref, wdown_ref, g2_ref, b2_ref):
    m = mod_ref[0]
    n_chunks = D_FF // FF_CHUNK

    def out_proj(rows):
        d = None
        off = 0
        for a_ref in a_refs:
            n = a_ref.shape[1]
            part = _dot(a_ref[rows, :].astype(BF16), wout_ref[off:off + n, :])
            d = part if d is None else d + part
            off += n
        return d

    def head(rows, d):
        x1 = _layer_norm(ALPHA * x_ref[rows, :] + m[2:3] * d, g1_ref[...], b1_ref[...])
        return x1, (x1 * (1.0 + m[4:5]) + m[3:4]).astype(BF16)

    def up(h, c):
        return _dot(h, wup_ref[:, c * FF_CHUNK:(c + 1) * FF_CHUNK])

    def down(u, acc, c):
        u = jnp.maximum(u, 0.0)
        part = _dot((u * u).astype(BF16), wdown_ref[c * FF_CHUNK:(c + 1) * FF_CHUNK, :])
        return part if acc is None else acc + part

    def finish(rows, x1, acc):
        o_ref[rows, :] = _layer_norm(ALPHA * x1 + m[5:6] * acc, g2_ref[...], b2_ref[...])

    assert sum(TAIL_SPLIT) == x_ref.shape[0]
    k = len(TAIL_SPLIT)
    starts = [sum(TAIL_SPLIT[:t]) for t in range(k)]
    rows = [slice(s, s + n) for s, n in zip(starts, TAIL_SPLIT)]
    projected = [out_proj(r) for r in rows]
    heads = [None] * k
    heads[0] = head(rows[0], projected[0])
    u_next = up(heads[0][1], 0)
    done = None
    for t in range(k):
        if t + 1 < k:
            heads[t + 1] = head(rows[t + 1], projected[t + 1])
        acc = None
        for c in range(n_chunks):
            if c + 1 < n_chunks:
                nxt = up(heads[t][1], c + 1)
            else:
                nxt = up(heads[t + 1][1], 0) if t + 1 < k else None
            u_cur, u_next = u_next, nxt
            acc = down(u_cur, acc, c)
            if c == 0 and done is not None:
                finish(*done)
        done = (rows[t], heads[t][0], acc)
    finish(*done)


def _tail(a_ctx, a_lat, x_ctx, x_lat, mod, lat_seq_len, w_out, ln1_g, ln1_b, w_up, w_down, ln2_g, ln2_b):
    tm = TOKEN_TILE
    n_ctx = x_ctx.shape[0] // tm
    n_lat = x_lat.shape[0] // tm
    tiles_per_lat_seq = lat_seq_len // tm
    ctx_row = lambda n: pl.BlockSpec((tm, n), lambda i: (jnp.minimum(i, n_ctx - 1), 0))
    lat_row = lambda n: pl.BlockSpec((tm, n), lambda i: (jnp.maximum(i - n_ctx, 0), 0))
    lat_out = pl.BlockSpec((tm, D_MODEL), lambda i: (jnp.maximum(i - n_ctx, 0), 0))
    mod_spec = _mod_spec(lambda i: jnp.where(i < n_ctx, 0, 1 + (i - n_ctx) // tiles_per_lat_seq))
    vec = lambda v: v.reshape(1, D_MODEL)
    consts = [w_out, w_up, w_down, vec(ln1_g), vec(ln1_b), vec(ln2_g), vec(ln2_b)]
    return pl.pallas_call(
        functools.partial(_tail_kernel, n_in=len(a_ctx), n_ctx_tiles=n_ctx),
        grid=(n_ctx + n_lat,),
        in_specs=[ctx_row(a.shape[1]) for a in a_ctx] + [lat_row(a.shape[1]) for a in a_lat]
                 + [ctx_row(D_MODEL), lat_row(D_MODEL), mod_spec] + [_const_spec(v.shape) for v in consts],
        out_specs=[ctx_row(D_MODEL), lat_out],
        out_shape=[jax.ShapeDtypeStruct(x_ctx.shape, F32), jax.ShapeDtypeStruct(x_lat.shape, F32)],
        compiler_params=_params(),
        name="outproj_mlp",
    )(*a_ctx, *a_lat, x_ctx, x_lat, mod, *consts)


def _pool_kernel(x_ref, mod_ref, win_ref, pw_ref, ps_ref, o_ref, *, seq_len):
    L = seq_len
    m = mod_ref[0]
    t_i = lax.broadcasted_iota(jnp.int32, (L, L), 0)
    s_i = lax.broadcasted_iota(jnp.int32, (L, L), 1)
    t_g = lax.broadcasted_iota(jnp.int32, (L, POOL_GROUP_DIM), 0)
    bands, scales = [], []
    for w in POOL_WINDOWS:
        lo, hi = w // 2, w - w // 2
        bands.append(jnp.where((s_i >= t_i - lo) & (s_i < t_i + hi), 1.0 / w, 0.0).astype(BF16))
        scales.append(w / (jnp.minimum(t_g + hi, L) - jnp.maximum(t_g - lo, 0)).astype(F32))
    n_seq = x_ref.shape[0] // L
    seq_rows = [slice(s * L, (s + 1) * L) for s in range(n_seq)]
    hs = [(x_ref[r, :] * (1.0 + m[1:2]) + m[0:1]).astype(BF16) for r in seq_rows]
    us = [_dot(h, win_ref[...]) for h in hs]
    n_groups = len(POOL_WINDOWS)
    cols = [slice(g * POOL_GROUP_DIM, (g + 1) * POOL_GROUP_DIM) for g in range(n_groups)]

    def window_means(g):
        out = []
        for s in range(n_seq):
            ug = us[s][:, cols[g]]
            ug_hi = ug.astype(BF16)
            ug_lo = (ug - ug_hi.astype(F32)).astype(BF16)
            out.append((ug, _dot(bands[g], ug_hi) + _dot(bands[g], ug_lo)))
        return out

    def mix(g, means):
        return [_dot((wm * scales[g] - ug).astype(BF16), pw_ref[g]) * ps_ref[:, cols[g]] for ug, wm in means]

    mixed = []
    pending = window_means(0)
    for g in range(n_groups):
        nxt = window_means(g + 1) if g + 1 < n_groups else None
        mixed.append(mix(g, pending))
        pending = nxt
    for s in range(n_seq):
        o_ref[seq_rows[s], :] = jnp.concatenate([mixed[g][s] for g in range(n_groups)], axis=1).astype(o_ref.dtype)


def _pool_mixer(x, mod, row_of_tile, w_in, pool_w, pool_scale, seq_len, tm):
    t = x.shape[0]
    row = pl.BlockSpec((tm, D_MODEL), lambda i: (i, 0))
    consts = [w_in, pool_w, pool_scale.reshape(1, D_MODEL)]
    return pl.pallas_call(
        functools.partial(_pool_kernel, seq_len=seq_len),
        grid=(t // tm,),
        in_specs=[row, _mod_spec(row_of_tile)] + [_const_spec(cst.shape) for cst in consts],
        out_specs=row,
        out_shape=jax.ShapeDtypeStruct((t, D_MODEL), BF16),
        compiler_params=_params(),
        name="pool_mixer",
    )(x, mod, *consts)


def _rope_tables(rows):
    r, col = np.meshgrid(np.arange(rows), np.arange(GRID_W), indexing="ij")
    r = r.reshape(-1).astype(np.float64)
    col = col.reshape(-1).astype(np.float64)
    n_freq = HEAD_DIM_A // 4
    inv = ROPE_BASE ** (-np.arange(n_freq, dtype=np.float64) / n_freq)
    ang = np.concatenate([r[:, None] * inv, col[:, None] * inv], -1)
    reps = 512 // HEAD_DIM_A
    cosf = np.tile(np.concatenate([np.cos(ang), np.cos(ang)], -1), (1, reps)).astype(np.float32)
    sinf = np.tile(np.concatenate([-np.sin(ang), np.sin(ang)], -1), (1, reps)).astype(np.float32)
    return jnp.asarray(cosf), jnp.asarray(sinf)


def kernel(x_prompt, x_sample, cache_k_l0, cache_v_l0, state_ssm_l0, c, c_ctx, w_mod_l0, b_mod_l0, w_in_l0, conv_w_l0, conv_b_l0, a_log_l0, dt_bias_l0, d_skip_l0, ssm_norm_w_l0, lam_q1_l0, lam_k1_l0, lam_q2_l0, lam_k2_l0, subln_w_l0, w_out_l0, ln1_g_l0, ln1_b_l0, w_up_l0, w_down_l0, ln2_g_l0, ln2_b_l0, w_mod_l1, b_mod_l1, w_in_l1, pool_w_l1, pool_scale_l1, w_out_l1, ln1_g_l1, ln1_b_l1, w_up_l1, w_down_l1, ln2_g_l1, ln2_b_l1):
    n_ctx, l_ctx, _ = x_prompt.shape
    n_lat, l_lat, _ = x_sample.shape
    past = cache_k_l0.shape[1]
    xc = x_prompt.reshape(n_ctx * l_ctx, D_MODEL)
    xl = x_sample.reshape(n_lat * l_lat, D_MODEL)

    cond8 = jnp.concatenate([c_ctx[None, :], c, jnp.zeros((8 - 1 - n_lat, D_MODEL), F32)], axis=0)
    ctx_row = lambda i: 0

    mod0, mod1 = _modulation(cond8, w_mod_l0, b_mod_l0, w_mod_l1, b_mod_l1)
    n_main = 4 * A_QK + A_V + D_INNER_B + CONV_DIM_B
    w_main = w_in_l0.astype(BF16)
    w_dt = jnp.pad(w_in_l0[:, n_main:], ((0, 0), (0, LANES - 2 * N_HEADS_B))).astype(BF16)
    lane_pad = lambda v: jnp.pad(v.reshape(1, -1), ((0, 0), (0, LANES - 2 * N_HEADS_B)))
    conv_consts = (conv_w_l0, conv_b_l0.reshape(1, CONV_DIM_B), lane_pad(dt_bias_l0))
    ssd_consts = [lane_pad(a_log_l0),
                  jnp.repeat(d_skip_l0, HEAD_DIM_B).reshape(1, D_INNER_B),
                  ssm_norm_w_l0.reshape(1, D_INNER_B)]
    lam_vecs = jnp.stack([lam_q1_l0, lam_k1_l0, lam_q2_l0, lam_k2_l0], axis=0)
    lam_init = 0.8 - 0.6 * math.exp(-0.3 * 0)

    (qc, kc, vc, zc, xbc_c, dt_c), (w_out0, w_up0, w_down0, w_out1, w_up1, w_down1, w_in1, pool_w) = _inproj0(
        xc, mod0, ctx_row, w_main, w_dt, *conv_consts, l_ctx, TOKEN_TILE,
        ride_along=(w_out_l0, w_up_l0, w_down_l0, w_out_l1, w_up_l1, w_down_l1, w_in_l1,
                    pool_w_l1.reshape(-1, POOL_GROUP_DIM)))
    pool_w = pool_w.reshape(pool_w_l1.shape)
    oa_c = _attention(qc, [kc], [vc], lam_vecs, subln_w_l0, l_ctx, l_ctx, CTX_SEQS_PER_STEP, lam_init)
    (mix0_c, new_ssm), _ = _ssd(oa_c, xbc_c, dt_c, zc, None, ssd_consts, n_ctx, l_ctx, SSD_CTX_SEQS_PER_STEP)

    (ql, kl, vl, zl, xbc_l, dt_l), _ = _inproj0(xl, mod0, lambda i: 1 + i, w_main, w_dt, *conv_consts, l_lat, l_lat,
                                                rope_tables=_rope_tables(l_lat // GRID_W))
    ck = jnp.transpose(cache_k_l0, (0, 2, 3, 4, 1))
    cv = cache_v_l0.reshape(n_lat * past * N_HEADS_A, 2 * HEAD_DIM_A)
    oa_l = _attention(ql, [ck, kl], [cv, vl], lam_vecs, subln_w_l0, l_lat, ATTN_Q_TILE, 1, lam_init)
    h0 = state_ssm_l0.reshape(n_lat, 2, D_INNER_B, D_STATE)
    ((mix0_l,), _) = _ssd(oa_l, xbc_l, dt_l, zl, h0, ssd_consts, n_lat, l_lat, 1)
    tail0 = (w_out0, ln1_g_l0, ln1_b_l0, w_up0, w_down0, ln2_g_l0, ln2_b_l0)
    xc, xl = _tail([mix0_c], [mix0_l], xc, xl, mod0, l_lat, *tail0)

    tail1 = (w_out1, ln1_g_l1, ln1_b_l1, w_up1, w_down1, ln2_g_l1, ln2_b_l1)
    mix_c = _pool_mixer(xc, mod1, ctx_row, w_in1, pool_w, pool_scale_l1, l_ctx, TOKEN_TILE)
    mix_l = _pool_mixer(xl, mod1, lambda i: 1 + i, w_in1, pool_w, pool_scale_l1, l_lat, l_lat)
    xc, xl = _tail([mix_c], [mix_l], xc, xl, mod1, l_lat, *tail1)

    return (xc.reshape(n_ctx, l_ctx, D_MODEL), xl.reshape(n_lat, l_lat, D_MODEL),
            jnp.transpose(kc, (0, 4, 1, 2, 3)),
            vc.reshape(n_ctx, l_ctx, N_HEADS_A, 2 * HEAD_DIM_A),
            new_ssm.reshape(n_ctx, 2, N_HEADS_B, HEAD_DIM_B, D_STATE))
```

```python
import functools
import math

import jax
import jax.numpy as jnp
import numpy as np
from jax import lax
from jax.experimental import pallas as pl
from jax.experimental.pallas import tpu as pltpu

F32 = jnp.float32
BF16 = jnp.bfloat16

D_MODEL = 1024
N_MOD = 6
N_HEADS_A = 4
HEAD_DIM_A = 64
A_QK = N_HEADS_A * HEAD_DIM_A
A_V = 2 * A_QK
D_INNER_B = 512
HEAD_DIM_B = 64
N_HEADS_B = 8
D_STATE = 128
D_CONV = 5
CONV_DIM_B = 1024
CHUNK = 128
GRID_W = 64
POOL_WINDOWS = (2, 4, 8, 16)
POOL_GROUP_DIM = 256
D_FF = 4096
ROPE_BASE = 10000.0
LN_EPS = 1e-5
DEPTH = 2
ALPHA = (2 * DEPTH) ** 0.25
NEG_BIG = -1e30
LOG2E = 1.4426950408889634

LANES = 128
TOKEN_TILE = 512
ATTN_Q_TILE = 256
CTX_SEQS_PER_STEP = 4
SSD_CTX_SEQS_PER_STEP = 1
SSD_UNROLL_CHUNKS = 4
FF_CHUNK = 1024
CONV_COL_BLOCK = 256
TAIL_SPLIT = (256, 256)
VMEM_LIMIT = 56 * 1024 * 1024


def _dot(a, b):
    return jnp.dot(a, b, preferred_element_type=F32)


def _dot_nt(a, b):
    return lax.dot_general(a, b, (((1,), (1,)), ((), ())), preferred_element_type=F32)


def _bf16_pieces(a):
    a1 = a.astype(BF16)
    r1 = a - a1.astype(F32)
    a2 = r1.astype(BF16)
    a3 = (r1 - a2.astype(F32)).astype(BF16)
    return a1, a2, a3


def _dot_exact_rhs(t01, a):
    a1, a2, a3 = _bf16_pieces(a)
    return _dot(t01, a1) + _dot(t01, a2) + _dot(t01, a3)


def _sigmoid(x):
    return 1.0 / (1.0 + jnp.exp(-x))


def _silu(x):
    return x * _sigmoid(x)


def _layer_norm(x, g, b):
    mu = jnp.mean(x, axis=-1, keepdims=True)
    xc = x - mu
    var = jnp.mean(xc * xc, axis=-1, keepdims=True)
    return xc * lax.rsqrt(var + LN_EPS) * g + b


def _const_spec(shape):
    nd = len(shape)
    return pl.BlockSpec(shape, lambda *_: (0,) * nd, pipeline_mode=pl.Buffered(1))


def _params(n_axes=1):
    return pltpu.CompilerParams(dimension_semantics=("arbitrary",) * n_axes,
                                vmem_limit_bytes=VMEM_LIMIT)


def _mod_kernel(cond_ref, w0_ref, b0_ref, w1_ref, b1_ref, o0_ref, o1_ref, *, n_first):
    s = _silu(cond_ref[...]).astype(BF16)
    j = pl.program_id(0)

    @pl.when(j < n_first)
    def _():
        o0_ref[...] = _dot(s, w0_ref[...].astype(BF16)) + b0_ref[...]

    @pl.when(j >= n_first)
    def _():
        o1_ref[...] = _dot(s, w1_ref[...].astype(BF16)) + b1_ref[...]


def _modulation(cond8, w_mod0, b_mod0, w_mod1, b_mod1):
    tn = 1024
    n = w_mod0.shape[1]
    steps = n // tn
    first = lambda j: (0, jnp.minimum(j, steps - 1))
    second = lambda j: (0, jnp.maximum(j - steps, 0))
    outs = pl.pallas_call(
        functools.partial(_mod_kernel, n_first=steps),
        grid=(2 * steps,),
        in_specs=[pl.BlockSpec((8, D_MODEL), lambda j: (0, 0)),
                  pl.BlockSpec((D_MODEL, tn), first), pl.BlockSpec((1, tn), first),
                  pl.BlockSpec((D_MODEL, tn), second), pl.BlockSpec((1, tn), second)],
        out_specs=[pl.BlockSpec((8, tn), first), pl.BlockSpec((8, tn), second)],
        out_shape=[jax.ShapeDtypeStruct((8, n), F32)] * 2,
        compiler_params=_params(),
        name="modulation",
    )(cond8, w_mod0, b_mod0.reshape(1, n), w_mod1, b_mod1.reshape(1, n))
    return [o.reshape(8, N_MOD, D_MODEL) for o in outs]


def _mod_spec(row_of_tile):
    return pl.BlockSpec((1, N_MOD, D_MODEL), lambda i: (row_of_tile(i), 0, 0))


def _rope(t, cosf, sinf):
    lane = lax.broadcasted_iota(jnp.int32, t.shape, 1)
    first = (lane & (HEAD_DIM_A - 1)) < HEAD_DIM_A // 2
    width = t.shape[1]
    swapped = jnp.where(first, pltpu.roll(t, width - HEAD_DIM_A // 2, 1),
                        pltpu.roll(t, HEAD_DIM_A // 2, 1))
    return t * cosf + swapped * sinf


def _conv_silu(xbc, cw_ref, cb_ref, cols, seq_len):
    assert xbc.shape[0] == seq_len
    half = D_CONV // 2
    edge = 8
    offsets = [j - half for j in range(D_CONV) if j != half]
    shifted = {o: pltpu.roll(xbc, (-o) % seq_len, 0) for o in offsets}

    def taps(lo, hi, masked):
        acc = cb_ref[:, cols] + cw_ref[half:half + 1, cols] * xbc[lo:hi]
        pos = lo + lax.broadcasted_iota(jnp.int32, (hi - lo, xbc.shape[1]), 0)
        for o in offsets:
            s = shifted[o][lo:hi]
            if masked:
                s = jnp.where((pos >= -o) if o < 0 else (pos < seq_len - o), s, 0.0)
            acc = acc + cw_ref[o + half:o + half + 1, cols] * s
        return acc

    acc = jnp.concatenate([taps(0, edge, True), taps(edge, seq_len - edge, False),
                           taps(seq_len - edge, seq_len, True)], axis=0)
    return _silu(acc)


def _softplus(t):
    return jnp.maximum(t, 0.0) + jnp.log(1.0 + jnp.exp(-jnp.abs(t)))


def _ride_along_specs(mats, n_steps):
    specs, shapes = [], []
    for w in mats:
        assert w.shape[0] % n_steps == 0
        specs.append(pl.BlockSpec((w.shape[0] // n_steps, w.shape[1]), lambda i: (i, 0)))
        shapes.append(jax.ShapeDtypeStruct(w.shape, BF16))
    return specs, shapes


def _ride_along_cast(src_refs, dst_refs):
    for src, dst in zip(src_refs, dst_refs):
        dst[...] = src[...].astype(BF16)


def _inproj0_kernel(*refs, rope, seq_len, n_cast):
    n_in = 9 if rope else 7
    ins, cast_in = refs[:n_in], refs[n_in:n_in + n_cast]
    (q_ref, k_ref, v_ref, z_ref, xbc_ref, dt_ref) = refs[n_in + n_cast:n_in + n_cast + 6]
    cast_out = refs[n_in + n_cast + 6:]
    if rope:
        x_ref, mod_ref, w_ref, wdt_ref, cw_ref, cb_ref, dtb_ref, cos_ref, sin_ref = ins
    else:
        x_ref, mod_ref, w_ref, wdt_ref, cw_ref, cb_ref, dtb_ref = ins
    _ride_along_cast(cast_in, cast_out)
    m = mod_ref[0]
    tasks = []
    for b in range(x_ref.shape[0] // seq_len):
        rows = slice(b * seq_len, (b + 1) * seq_len)
        h = (x_ref[rows, :] * (1.0 + m[1:2]) + m[0:1]).astype(BF16)

        for c0 in range(0, CONV_DIM_B, CONV_COL_BLOCK):
            cols = slice(c0, c0 + CONV_COL_BLOCK)

            def conv_mm(h=h, c0=c0):
                return _dot(h, w_ref[:, 2048 + c0:2048 + c0 + CONV_COL_BLOCK])

            def conv_vec(raw, rows=rows, cols=cols):
                xbc_ref[rows, cols] = _conv_silu(raw, cw_ref, cb_ref, cols, seq_len)

            tasks.append((conv_mm, conv_vec))

        def qk_mm(h=h):
            return _dot(h, wdt_ref[...]), _dot(h, w_ref[:, 0:512]), _dot(h, w_ref[:, 512:1024])

        def qk_vec(res, rows=rows, b=b):
            dt_raw, q, k = res
            dt_ref[rows, :] = _softplus(dt_raw + dtb_ref[...])
            if rope:
                q = _rope(q, cos_ref[...], sin_ref[...])
                k = _rope(k, cos_ref[...], sin_ref[...])
            q_ref[rows, :] = (q * (HEAD_DIM_A ** -0.5 * LOG2E)).astype(BF16)
            for blk in range(2 * A_QK // LANES):
                t = k[:, blk * LANES:(blk + 1) * LANES].T
                which, head = blk // 2, (blk % 2) * 2
                k_ref[b, which, head] = t[0:HEAD_DIM_A]
                k_ref[b, which, head + 1] = t[HEAD_DIM_A:2 * HEAD_DIM_A]

        def vz_mm(h=h):
            return _dot(h, w_ref[:, 1024:1536]), _dot(h, w_ref[:, 1536:2048])

        def vz_vec(res, rows=rows, b=b):
            vals, z = res
            for hh in range(N_HEADS_A):
                v_ref[pl.ds(b * seq_len * N_HEADS_A + hh, seq_len, stride=N_HEADS_A), :] = (
                    vals[:, hh * LANES:(hh + 1) * LANES])
            z_ref[rows, :] = z

        tasks += [(qk_mm, qk_vec), (vz_mm, vz_vec)]

    pending = None
    for mm, vec in tasks:
        res = mm()
        if pending is not None:
            pending[0](pending[1])
        pending = (vec, res)
    pending[0](pending[1])


def _inproj0(x, mod, row_of_tile, w_main, w_dt, conv_w, conv_b, dt_bias, seq_len, tm, rope_tables=None,
             ride_along=()):
    t = x.shape[0]
    assert tm % seq_len == 0
    row = lambda n: pl.BlockSpec((tm, n), lambda i: (i, 0))
    flat = lambda n: jax.ShapeDtypeStruct((t, n), F32)
    consts = [w_main, w_dt, conv_w, conv_b, dt_bias]
    in_specs = [row(D_MODEL), _mod_spec(row_of_tile)] + [_const_spec(cst.shape) for cst in consts]
    args = [x, mod] + consts
    if rope_tables is not None:
        assert tm == seq_len
        in_specs += [_const_spec(tab.shape) for tab in rope_tables]
        args += list(rope_tables)
    kt_dims = (2, N_HEADS_A, HEAD_DIM_A, seq_len)
    k_spec = pl.BlockSpec((tm // seq_len,) + kt_dims, lambda i: (i, 0, 0, 0, 0))
    k_shape = jax.ShapeDtypeStruct((t // seq_len,) + kt_dims, F32)
    cast_specs, cast_shapes = _ride_along_specs(ride_along, t // tm)
    outs = pl.pallas_call(
        functools.partial(_inproj0_kernel, rope=rope_tables is not None, seq_len=seq_len, n_cast=len(ride_along)),
        grid=(t // tm,),
        in_specs=in_specs + cast_specs,
        out_specs=[row(512), k_spec, pl.BlockSpec((tm * N_HEADS_A, LANES), lambda i: (i, 0)),
                   row(512), row(CONV_DIM_B), row(LANES)] + cast_specs,
        out_shape=[jax.ShapeDtypeStruct((t, 512), BF16), k_shape, jax.ShapeDtypeStruct((t * N_HEADS_A, LANES), F32),
                   flat(512), flat(CONV_DIM_B), flat(LANES)] + cast_shapes,
        compiler_params=_params(),
        name="inproj0",
    )(*args, *ride_along)
    return outs[:6], outs[6:]


def _lambda_full(lam_ref, lam_init):
    lv = lam_ref[...]
    return (jnp.exp(jnp.sum(lv[0:1] * lv[1:2], axis=1, keepdims=True))
            - jnp.exp(jnp.sum(lv[2:3] * lv[3:4], axis=1, keepdims=True)) + lam_init)


def _attn_kernel(*refs, n_seg, tq, lam_init):
    q_ref = refs[0]
    kt_refs = refs[1:1 + n_seg]
    v_refs = refs[1 + n_seg:1 + 2 * n_seg]
    lam_ref, sub_ref, o_ref = refs[1 + 2 * n_seg:]
    lam = _lambda_full(lam_ref, lam_init)
    lane = lax.broadcasted_iota(jnp.int32, (1, A_QK), 1)
    masks = [(lane >= h * HEAD_DIM_A) & (lane < (h + 1) * HEAD_DIM_A) for h in range(N_HEADS_A)]
    keys = [r.shape[-1] for r in kt_refs]

    def stacked(qx):
        return jnp.concatenate([jnp.where(m, qx, 0.0) for m in masks], axis=0).astype(BF16)

    def exp_scores(qx, b, which):
        qs = stacked(qx)
        s = [_dot(qs, r[b, which].reshape(A_QK, n).astype(BF16)) for r, n in zip(kt_refs, keys)]
        mx = functools.reduce(jnp.maximum, [jnp.max(x, axis=1, keepdims=True) for x in s])
        return [jnp.exp2(x - mx).astype(BF16) for x in s]

    outs = []
    for b in range(kt_refs[0].shape[0]):
        q = q_ref[b * tq:(b + 1) * tq, :]
        e1 = exp_scores(q[:, 0:A_QK], b, 0)
        e2 = exp_scores(q[:, A_QK:2 * A_QK], b, 1)
        heads = []
        for h in range(N_HEADS_A):
            hrows = slice(h * tq, (h + 1) * tq)
            n1 = n2 = None
            for s, (v_ref, n) in enumerate(zip(v_refs, keys)):
                v_h = v_ref[pl.ds(b * n * N_HEADS_A + h, n, stride=N_HEADS_A), :]
                v_ext = jnp.concatenate([v_h.astype(BF16),
                                         jnp.ones((n, LANES), BF16)], axis=1)
                p1 = _dot(e1[s][hrows], v_ext)
                p2 = _dot(e2[s][hrows], v_ext)
                n1 = p1 if n1 is None else n1 + p1
                n2 = p2 if n2 is None else n2 + p2
            heads.append(n1[:, 0:LANES] / n1[:, LANES:] - lam * (n2[:, 0:LANES] / n2[:, LANES:]))
        o = jnp.concatenate(heads, axis=0)
        ms = jnp.mean(o * o, axis=1, keepdims=True)
        o = (o * lax.rsqrt(ms + 1e-5) * sub_ref[...]) * (1.0 - lam_init)
        outs.append(jnp.concatenate([o[h * tq:(h + 1) * tq] for h in range(N_HEADS_A)], axis=1))
    o_ref[...] = jnp.concatenate(outs, axis=0).astype(o_ref.dtype)


def _attention(q, kts, vs, lam_vecs, subln_w, q_len, tq, seqs_per_step, lam_init):
    n_seq = kts[0].shape[0]
    nb = seqs_per_step
    tiles = q_len // tq
    assert tiles == 1 or nb == 1
    qspec = pl.BlockSpec((nb * tq, 512), lambda b, i: (b * tiles + i, 0))
    ktspecs = [pl.BlockSpec((nb,) + kt.shape[1:], lambda b, i: (b, 0, 0, 0, 0)) for kt in kts]
    vspecs = [pl.BlockSpec((nb * kt.shape[-1] * N_HEADS_A, LANES), lambda b, i: (b, 0)) for kt in kts]
    return pl.pallas_call(
        functools.partial(_attn_kernel, n_seg=len(kts), tq=tq, lam_init=lam_init),
        grid=(n_seq // nb, tiles),
        in_specs=[qspec] + ktspecs + vspecs + [pl.BlockSpec((4, HEAD_DIM_A), lambda b, i: (0, 0)),
                                             pl.BlockSpec((1, LANES), lambda b, i: (0, 0))],
        out_specs=qspec,
        out_shape=jax.ShapeDtypeStruct(q.shape, BF16),
        compiler_params=_params(2),
        name="diff_attention",
    )(q, *kts, *vs, lam_vecs, subln_w.reshape(1, LANES))


def _chunk_loop(n, body, unroll=1):
    if n <= 2:
        for c in range(n):
            body(c)
    else:
        def step(c, carry):
            body(c)
            return carry
        lax.fori_loop(0, n, step, 0, unroll=unroll)


def _ssd_kernel(*refs, seq_len, n_seq, has_h0, n_cast):
    n_in = 7 if has_h0 else 6
    n_out = 1 if has_h0 else 2
    ins, cast_in = refs[:n_in], refs[n_in:n_in + n_cast]
    outs = refs[n_in + n_cast:n_in + n_cast + n_out]
    cast_out = refs[n_in + n_cast + n_out:n_in + 2 * n_cast + n_out]
    ysc, s_sc, cd_sc, e_sc, htf, htb = refs[n_in + 2 * n_cast + n_out:]
    if has_h0:
        xact, dtv, z_ref, h0_ref, arow_ref, dsk_ref, nw_ref = ins
        (y_ref,), hfin_ref = outs, None
    else:
        xact, dtv, z_ref, arow_ref, dsk_ref, nw_ref = ins
        (y_ref, hfin_ref), h0_ref = outs, None
    _ride_along_cast(cast_in, cast_out)
    nc = seq_len // CHUNK
    n_pairs = N_HEADS_B // 2
    a_row = -jnp.exp(arow_ref[...]) * LOG2E

    row_i = lax.broadcasted_iota(jnp.int32, (CHUNK, CHUNK), 0)
    col_i = lax.broadcasted_iota(jnp.int32, (CHUNK, CHUNK), 1)
    lower = col_i <= row_i
    upper = col_i >= row_i
    tri = (jnp.where(lower, 1.0, 0.0).astype(BF16), jnp.where(upper, 1.0, 0.0).astype(BF16))
    lo_half = col_i < HEAD_DIM_B

    def split_pair(m):
        zero = jnp.zeros_like(m)
        return jnp.concatenate([jnp.where(lo_half, m, zero), jnp.where(lo_half, zero, m)], axis=0).astype(BF16)

    def decay_terms(c):
        rows = pl.ds(pl.multiple_of(c * CHUNK, CHUNK), CHUNK)
        dtc = dtv[rows, :]
        dtt = dtc.T[0:16, :]
        per_dir = []
        for d in range(2):
            col = _dot_exact_rhs(tri[d], dtc * a_row)
            rowm = col.T[0:16, :]
            far = CHUNK - 1 if d == 0 else 0
            tot = jnp.broadcast_to(rowm[:, far:far + 1], (16, CHUNK))
            dte = jnp.exp2(tot - rowm) * dtt
            per_dir.append((col, rowm - jnp.log2(dtt), dte, jnp.exp2(tot)))
        return per_dir

    def local_phase(c, per_dir=None):
        rows = pl.ds(pl.multiple_of(c * CHUNK, CHUNK), CHUNK)
        if per_dir is None:
            per_dir = decay_terms(c)
        for g in range(2):
            bg = xact[rows, 512 + g * LANES:512 + (g + 1) * LANES]
            cg = xact[rows, 768 + g * LANES:768 + (g + 1) * LANES]
            cbm = _dot_nt(cg.astype(BF16), bg.astype(BF16))
            bgt = bg.T
            for pq in range(2):
                pi = g * 2 + pq
                lanes = slice(pi * LANES, (pi + 1) * LANES)
                xrhs = split_pair(xact[rows, lanes])
                y_pair = None
                for d in range(2):
                    col, row_dt, dte, cdec = per_dir[d]
                    causal = lower if d == 0 else upper
                    m_parts, col_parts, bw_parts, cd_parts = [], [], [], []
                    for hh in (2 * pi, 2 * pi + 1):
                        hd = 8 * d + hh
                        colb = jnp.sum(jnp.where(col_i == hd, col, 0.0), axis=1, keepdims=True)
                        m_parts.append(cbm * jnp.exp2(jnp.where(causal, colb - row_dt[hd:hd + 1, :], NEG_BIG)))
                        col_parts.append(colb)
                        bw_parts.append(bgt * dte[hd:hd + 1, :])
                        cd_parts.append(cdec[hd:hd + 1, :])
                    yd = _dot(jnp.concatenate(m_parts, axis=1).astype(BF16), xrhs)
                    y_pair = yd if y_pair is None else y_pair + yd
                    s_sc[c, d, :, lanes] = _dot(jnp.concatenate(bw_parts, axis=1).astype(BF16), xrhs)
                    e_sc[c, d, :, lanes] = jnp.exp2(jnp.where(lo_half, col_parts[0], col_parts[1]))
                    cd_sc[c, d, :, lanes] = jnp.broadcast_to(
                        jnp.where(lo_half[0:1, :], cd_parts[0], cd_parts[1]), (8, LANES))
                ysc[rows, lanes] = y_pair

    if n_seq * nc <= SSD_UNROLL_CHUNKS:
        terms = [decay_terms(c) for c in range(n_seq * nc)]
        for c in range(n_seq * nc):
            local_phase(c, terms[c])
    else:
        _chunk_loop(n_seq * nc, local_phase, unroll=2)

    def scan_sequence(sq):
        for d, ht in ((0, htf), (1, htb)):
            for qd in range(n_pairs):
                lanes = slice(qd * LANES, (qd + 1) * LANES)
                if has_h0:
                    ht[:, lanes] = h0_ref[sq, d, lanes, :].T
                else:
                    ht[:, lanes] = jnp.zeros((D_STATE, LANES), F32)

        def scan_phase(j):
            for d, ht in ((0, htf), (1, htb)):
                c = sq * nc + (j if d == 0 else nc - 1 - j)
                rows = pl.ds(pl.multiple_of(c * CHUNK, CHUNK), CHUNK)
                for pi in range(n_pairs):
                    lanes = slice(pi * LANES, (pi + 1) * LANES)
                    g = pi // 2
                    cg = xact[rows, 768 + g * LANES:768 + (g + 1) * LANES].astype(BF16)
                    hprev = ht[:, lanes]
                    ysc[rows, lanes] += _dot(cg, hprev.astype(BF16)) * e_sc[c, d, :, lanes]
                    ht[:, lanes] = hprev * cd_sc[c, d, 0:1, lanes] + s_sc[c, d, :, lanes]

        _chunk_loop(nc, scan_phase)
        if hfin_ref is not None:
            for d, ht in ((0, htf), (1, htb)):
                for qd in range(n_pairs):
                    lanes = slice(qd * LANES, (qd + 1) * LANES)
                    hfin_ref[sq, d, lanes, :] = ht[:, lanes].T

    if n_seq == 1:
        scan_sequence(0)
    else:
        def seq_step(sq, carry):
            scan_sequence(sq)
            return carry
        lax.fori_loop(0, n_seq, seq_step, 0)

    def finish(c):
        rows = pl.ds(pl.multiple_of(c * CHUNK, CHUNK), CHUNK)
        y = ysc[rows, :] + dsk_ref[...] * xact[rows, 0:D_INNER_B]
        y = y * _silu(z_ref[rows, :])
        ms = jnp.mean(y * y, axis=1, keepdims=True)
        y_ref[rows, :] = (y * lax.rsqrt(ms + 1e-5) * nw_ref[...]).astype(y_ref.dtype)

    _chunk_loop(n_seq * nc, finish)


def _ssd(xbc, dt, z, h0, consts, n_batch, seq_len, seqs_per_step, ride_along=()):
    nb = seqs_per_step
    L = nb * seq_len
    nc = L // CHUNK
    row = lambda n: pl.BlockSpec((L, n), lambda b: (b, 0))
    state_spec = pl.BlockSpec((nb, 2, D_INNER_B, D_STATE), lambda b: (b, 0, 0, 0))
    has_h0 = h0 is not None
    in_specs = [row(CONV_DIM_B), row(LANES), row(D_INNER_B)]
    args = [xbc, dt, z]
    if has_h0:
        in_specs.append(state_spec)
        args.append(h0)
    for cst in consts:
        in_specs.append(pl.BlockSpec(cst.shape, lambda b: (0, 0)))
        args.append(cst)
    out_specs = [row(D_INNER_B)]
    out_shape = [jax.ShapeDtypeStruct((n_batch * seq_len, D_INNER_B), BF16)]
    if not has_h0:
        out_specs.append(state_spec)
        out_shape.append(jax.ShapeDtypeStruct((n_batch, 2, D_INNER_B, D_STATE), F32))
    scratch = [pltpu.VMEM((L, D_INNER_B), F32),
               pltpu.VMEM((nc, 2, D_STATE, D_INNER_B), F32),
               pltpu.VMEM((nc, 2, 8, D_INNER_B), F32),
               pltpu.VMEM((nc, 2, CHUNK, D_INNER_B), F32),
               pltpu.VMEM((D_STATE, D_INNER_B), F32),
               pltpu.VMEM((D_STATE, D_INNER_B), F32)]
    cast_specs, cast_shapes = _ride_along_specs(ride_along, n_batch // nb)
    n_main_out = len(out_specs)
    outs = pl.pallas_call(
        functools.partial(_ssd_kernel, seq_len=seq_len, n_seq=nb, has_h0=has_h0, n_cast=len(ride_along)),
        grid=(n_batch // nb,),
        in_specs=in_specs + cast_specs,
        out_specs=out_specs + cast_specs,
        out_shape=out_shape + cast_shapes,
        scratch_shapes=scratch,
        compiler_params=_params(),
        name="bidir_ssd",
    )(*args, *ride_along)
    return outs[:n_main_out], outs[n_main_out:]


def _tail_kernel(*refs, n_in, n_ctx_tiles):
    ctx_a, lat_a = refs[:n_in], refs[n_in:2 * n_in]
    (xc_ref, xl_ref, mod_ref, wout_ref, wup_ref, wdown_ref, g1_ref, b1_ref, g2_ref, b2_ref,
     oc_ref, ol_ref) = refs[2 * n_in:]
    step = pl.program_id(0)

    tile = functools.partial(_mlp_tile, mod_ref=mod_ref, wout_ref=wout_ref, g1_ref=g1_ref, b1_ref=b1_ref,
                             wup_ref=wup_ref, wdown_ref=wdown_ref, g2_ref=g2_ref, b2_ref=b2_ref)

    @pl.when(step < n_ctx_tiles)
    def _():
        tile(ctx_a, xc_ref, oc_ref)

    @pl.when(step >= n_ctx_tiles)
    def _():
        tile(lat_a, xl_ref, ol_ref)


def _mlp_tile(a_refs, x_ref, o_ref, *, mod_ref, wout_ref, g1_ref, b1_ref, wup_ref, wdown_ref, g2_ref, b2_ref):
    m = mod_ref[0]
    n_chunks = D_FF // FF_CHUNK

    def out_proj(rows):
        d = None
        off = 0
        for a_ref in a_refs:
            n = a_ref.shape[1]
            part = _dot(a_ref[rows, :].astype(BF16), wout_ref[off:off + n, :])
            d = part if d is None else d + part
            off += n
        return d

    def head(rows, d):
        x1 = _layer_norm(ALPHA * x_ref[rows, :] + m[2:3] * d, g1_ref[...], b1_ref[...])
        return x1, (x1 * (1.0 + m[4:5]) + m[3:4]).astype(BF16)

    def up(h, c):
        return _dot(h, wup_ref[:, c * FF_CHUNK:(c + 1) * FF_CHUNK])

    def down(u, acc, c):
        u = jnp.maximum(u, 0.0)
        part = _dot((u * u).astype(BF16), wdown_ref[c * FF_CHUNK:(c + 1) * FF_CHUNK, :])
        return part if acc is None else acc + part

    def finish(rows, x1, acc):
        o_ref[rows, :] = _layer_norm(ALPHA * x1 + m[5:6] * acc, g2_ref[...], b2_ref[...])

    assert sum(TAIL_SPLIT) == x_ref.shape[0]
    k = len(TAIL_SPLIT)
    starts = [sum(TAIL_SPLIT[:t]) for t in range(k)]
    rows = [slice(s, s + n) for s, n in zip(starts, TAIL_SPLIT)]
    projected = [out_proj(r) for r in rows]
    heads = [None] * k
    heads[0] = head(rows[0], projected[0])
    u_next = up(heads[0][1], 0)
    done = None
    for t in range(k):
        if t + 1 < k:
            heads[t + 1] = head(rows[t + 1], projected[t + 1])
        acc = None
        for c in range(n_chunks):
            if c + 1 < n_chunks:
                nxt = up(heads[t][1], c + 1)
            else:
                nxt = up(heads[t + 1][1], 0) if t + 1 < k else None
            u_cur, u_next = u_next, nxt
            acc = down(u_cur, acc, c)
            if c == 0 and done is not None:
                finish(*done)
        done = (rows[t], heads[t][0], acc)
    finish(*done)


def _tail(a_ctx, a_lat, x_ctx, x_lat, mod, lat_seq_len, w_out, ln1_g, ln1_b, w_up, w_down, ln2_g, ln2_b):
    tm = TOKEN_TILE
    n_ctx = x_ctx.shape[0] // tm
    n_lat = x_lat.shape[0] // tm
    tiles_per_lat_seq = lat_seq_len // tm
    ctx_row = lambda n: pl.BlockSpec((tm, n), lambda i: (jnp.minimum(i, n_ctx - 1), 0))
    lat_row = lambda n: pl.BlockSpec((tm, n), lambda i: (jnp.maximum(i - n_ctx, 0), 0))
    lat_out = pl.BlockSpec((tm, D_MODEL), lambda i: (jnp.maximum(i - n_ctx, 0), 0))
    mod_spec = _mod_spec(lambda i: jnp.where(i < n_ctx, 0, 1 + (i - n_ctx) // tiles_per_lat_seq))
    vec = lambda v: v.reshape(1, D_MODEL)
    consts = [w_out, w_up, w_down, vec(ln1_g), vec(ln1_b), vec(ln2_g), vec(ln2_b)]
    return pl.pallas_call(
        functools.partial(_tail_kernel, n_in=len(a_ctx), n_ctx_tiles=n_ctx),
        grid=(n_ctx + n_lat,),
        in_specs=[ctx_row(a.shape[1]) for a in a_ctx] + [lat_row(a.shape[1]) for a in a_lat]
                 + [ctx_row(D_MODEL), lat_row(D_MODEL), mod_spec] + [_const_spec(v.shape) for v in consts],
        out_specs=[ctx_row(D_MODEL), lat_out],
        out_shape=[jax.ShapeDtypeStruct(x_ctx.shape, F32), jax.ShapeDtypeStruct(x_lat.shape, F32)],
        compiler_params=_params(),
        name="outproj_mlp",
    )(*a_ctx, *a_lat, x_ctx, x_lat, mod, *consts)


def _pool_kernel(x_ref, mod_ref, win_ref, pw_ref, ps_ref, o_ref, *, seq_len):
    L = seq_len
    m = mod_ref[0]
    t_i = lax.broadcasted_iota(jnp.int32, (L, L), 0)
    s_i = lax.broadcasted_iota(jnp.int32, (L, L), 1)
    t_g = lax.broadcasted_iota(jnp.int32, (L, POOL_GROUP_DIM), 0)
    bands, scales = [], []
    for w in POOL_WINDOWS:
        lo, hi = w // 2, w - w // 2
        bands.append(jnp.where((s_i >= t_i - lo) & (s_i < t_i + hi), 1.0 / w, 0.0).astype(BF16))
        scales.append(w / (jnp.minimum(t_g + hi, L) - jnp.maximum(t_g - lo, 0)).astype(F32))
    n_seq = x_ref.shape[0] // L
    seq_rows = [slice(s * L, (s + 1) * L) for s in range(n_seq)]
    hs = [(x_ref[r, :] * (1.0 + m[1:2]) + m[0:1]).astype(BF16) for r in seq_rows]
    us = [_dot(h, win_ref[...]) for h in hs]
    n_groups = len(POOL_WINDOWS)
    cols = [slice(g * POOL_GROUP_DIM, (g + 1) * POOL_GROUP_DIM) for g in range(n_groups)]

    def window_means(g):
        out = []
        for s in range(n_seq):
            ug = us[s][:, cols[g]]
            ug_hi = ug.astype(BF16)
            ug_lo = (ug - ug_hi.astype(F32)).astype(BF16)
            out.append((ug, _dot(bands[g], ug_hi) + _dot(bands[g], ug_lo)))
        return out

    def mix(g, means):
        return [_dot((wm * scales[g] - ug).astype(BF16), pw_ref[g]) * ps_ref[:, cols[g]] for ug, wm in means]

    mixed = []
    pending = window_means(0)
    for g in range(n_groups):
        nxt = window_means(g + 1) if g + 1 < n_groups else None
        mixed.append(mix(g, pending))
        pending = nxt
    for s in range(n_seq):
        o_ref[seq_rows[s], :] = jnp.concatenate([mixed[g][s] for g in range(n_groups)], axis=1).astype(o_ref.dtype)


def _pool_mixer(x, mod, row_of_tile, w_in, pool_w, pool_scale, seq_len, tm):
    t = x.shape[0]
    row = pl.BlockSpec((tm, D_MODEL), lambda i: (i, 0))
    consts = [w_in, pool_w, pool_scale.reshape(1, D_MODEL)]
    return pl.pallas_call(
        functools.partial(_pool_kernel, seq_len=seq_len),
        grid=(t // tm,),
        in_specs=[row, _mod_spec(row_of_tile)] + [_const_spec(cst.shape) for cst in consts],
        out_specs=row,
        out_shape=jax.ShapeDtypeStruct((t, D_MODEL), BF16),
        compiler_params=_params(),
        name="pool_mixer",
    )(x, mod, *consts)


def _rope_tables(rows):
    r, col = np.meshgrid(np.arange(rows), np.arange(GRID_W), indexing="ij")
    r = r.reshape(-1).astype(np.float64)
    col = col.reshape(-1).astype(np.float64)
    n_freq = HEAD_DIM_A // 4
    inv = ROPE_BASE ** (-np.arange(n_freq, dtype=np.float64) / n_freq)
    ang = np.concatenate([r[:, None] * inv, col[:, None] * inv], -1)
    reps = 512 // HEAD_DIM_A
    cosf = np.tile(np.concatenate([np.cos(ang), np.cos(ang)], -1), (1, reps)).astype(np.float32)
    sinf = np.tile(np.concatenate([-np.sin(ang), np.sin(ang)], -1), (1, reps)).astype(np.float32)
    return jnp.asarray(cosf), jnp.asarray(sinf)


def kernel(x_prompt, x_sample, cache_k_l0, cache_v_l0, state_ssm_l0, c, c_ctx, w_mod_l0, b_mod_l0, w_in_l0, conv_w_l0, conv_b_l0, a_log_l0, dt_bias_l0, d_skip_l0, ssm_norm_w_l0, lam_q1_l0, lam_k1_l0, lam_q2_l0, lam_k2_l0, subln_w_l0, w_out_l0, ln1_g_l0, ln1_b_l0, w_up_l0, w_down_l0, ln2_g_l0, ln2_b_l0, w_mod_l1, b_mod_l1, w_in_l1, pool_w_l1, pool_scale_l1, w_out_l1, ln1_g_l1, ln1_b_l1, w_up_l1, w_down_l1, ln2_g_l1, ln2_b_l1):
    n_ctx, l_ctx, _ = x_prompt.shape
    n_lat, l_lat, _ = x_sample.shape
    past = cache_k_l0.shape[1]
    xc = x_prompt.reshape(n_ctx * l_ctx, D_MODEL)
    xl = x_sample.reshape(n_lat * l_lat, D_MODEL)

    cond8 = jnp.concatenate([c_ctx[None, :], c, jnp.zeros((8 - 1 - n_lat, D_MODEL), F32)], axis=0)
    ctx_row = lambda i: 0

    mod0, mod1 = _modulation(cond8, w_mod_l0, b_mod_l0, w_mod_l1, b_mod_l1)
    n_main = 4 * A_QK + A_V + D_INNER_B + CONV_DIM_B
    w_main = w_in_l0.astype(BF16)
    w_dt = jnp.pad(w_in_l0[:, n_main:], ((0, 0), (0, LANES - 2 * N_HEADS_B))).astype(BF16)
    lane_pad = lambda v: jnp.pad(v.reshape(1, -1), ((0, 0), (0, LANES - 2 * N_HEADS_B)))
    conv_consts = (conv_w_l0, conv_b_l0.reshape(1, CONV_DIM_B), lane_pad(dt_bias_l0))
    ssd_consts = [lane_pad(a_log_l0),
                  jnp.repeat(d_skip_l0, HEAD_DIM_B).reshape(1, D_INNER_B),
                  ssm_norm_w_l0.reshape(1, D_INNER_B)]
    lam_vecs = jnp.stack([lam_q1_l0, lam_k1_l0, lam_q2_l0, lam_k2_l0], axis=0)
    lam_init = 0.8 - 0.6 * math.exp(-0.3 * 0)

    (qc, kc, vc, zc, xbc_c, dt_c), (w_out0, w_up0, w_down0, w_out1, w_up1, w_down1, w_in1, pool_w) = _inproj0(
        xc, mod0, ctx_row, w_main, w_dt, *conv_consts, l_ctx, TOKEN_TILE,
        ride_along=(w_out_l0, w_up_l0, w_down_l0, w_out_l1, w_up_l1, w_down_l1, w_in_l1,
                    pool_w_l1.reshape(-1, POOL_GROUP_DIM)))
    pool_w = pool_w.reshape(pool_w_l1.shape)
    oa_c = _attention(qc, [kc], [vc], lam_vecs, subln_w_l0, l_ctx, l_ctx, CTX_SEQS_PER_STEP, lam_init)
    (y_c, new_ssm), _ = _ssd(xbc_c, dt_c, zc, None, ssd_consts, n_ctx, l_ctx, SSD_CTX_SEQS_PER_STEP)

    (ql, kl, vl, zl, xbc_l, dt_l), _ = _inproj0(xl, mod0, lambda i: 1 + i, w_main, w_dt, *conv_consts, l_lat, l_lat,
                                                rope_tables=_rope_tables(l_lat // GRID_W))
    ck = jnp.transpose(cache_k_l0, (0, 2, 3, 4, 1))
    cv = cache_v_l0.reshape(n_lat * past * N_HEADS_A, 2 * HEAD_DIM_A)
    oa_l = _attention(ql, [ck, kl], [cv, vl], lam_vecs, subln_w_l0, l_lat, ATTN_Q_TILE, 1, lam_init)
    h0 = state_ssm_l0.reshape(n_lat, 2, D_INNER_B, D_STATE)
    ((y_l,), _) = _ssd(xbc_l, dt_l, zl, h0, ssd_consts, n_lat, l_lat, 1)
    tail0 = (w_out0, ln1_g_l0, ln1_b_l0, w_up0, w_down0, ln2_g_l0, ln2_b_l0)
    xc, xl = _tail([oa_c, y_c], [oa_l, y_l], xc, xl, mod0, l_lat, *tail0)

    tail1 = (w_out1, ln1_g_l1, ln1_b_l1, w_up1, w_down1, ln2_g_l1, ln2_b_l1)
    mix_c = _pool_mixer(xc, mod1, ctx_row, w_in1, pool_w, pool_scale_l1, l_ctx, TOKEN_TILE)
    mix_l = _pool_mixer(xl, mod1, lambda i: 1 + i, w_in1, pool_w, pool_scale_l1, l_lat, l_lat)
    xc, xl = _tail([mix_c], [mix_l], xc, xl, mod1, l_lat, *tail1)

    return (xc.reshape(n_ctx, l_ctx, D_MODEL), xl.reshape(n_lat, l_lat, D_MODEL),
            jnp.transpose(kc, (0, 4, 1, 2, 3)),
            vc.reshape(n_ctx, l_ctx, N_HEADS_A, 2 * HEAD_DIM_A),
            new_ssm.reshape(n_ctx, 2, N_HEADS_B, HEAD_DIM_B, D_STATE))
```

```python
import functools
import math

import jax
import jax.numpy as jnp
import numpy as np
from jax import lax
from jax.experimental import pallas as pl
from jax.experimental.pallas import tpu as pltpu

F32 = jnp.float32
BF16 = jnp.bfloat16

D_MODEL = 1024
N_MOD = 6
N_HEADS_A = 4
HEAD_DIM_A = 64
A_QK = N_HEADS_A * HEAD_DIM_A
A_V = 2 * A_QK
D_INNER_B = 512
HEAD_DIM_B = 64
N_HEADS_B = 8
D_STATE = 128
D_CONV = 5
CONV_DIM_B = 1024
CHUNK = 128
GRID_W = 64
POOL_WINDOWS = (2, 4, 8, 16)
POOL_GROUP_DIM = 256
D_FF = 4096
ROPE_BASE = 10000.0
LN_EPS = 1e-5
DEPTH = 2
ALPHA = (2 * DEPTH) ** 0.25
NEG_BIG = -1e30
LOG2E = 1.4426950408889634

LANES = 128
TOKEN_TILE = 512
POOL_CTX_TILE = 1024
ATTN_Q_TILE = 256
CTX_SEQS_PER_STEP = 4
SSD_CTX_SEQS_PER_STEP = 1
SSD_UNROLL_CHUNKS = 4
FF_CHUNK = 1024
CONV_COL_BLOCK = 256
TAIL_SPLIT = (256, 256)
VMEM_LIMIT = 56 * 1024 * 1024


def _dot(a, b):
    return jnp.dot(a, b, preferred_element_type=F32)


def _dot_nt(a, b):
    return lax.dot_general(a, b, (((1,), (1,)), ((), ())), preferred_element_type=F32)


def _bf16_pieces(a):
    a1 = a.astype(BF16)
    r1 = a - a1.astype(F32)
    a2 = r1.astype(BF16)
    a3 = (r1 - a2.astype(F32)).astype(BF16)
    return a1, a2, a3


def _dot_exact_rhs(t01, a):
    a1, a2, a3 = _bf16_pieces(a)
    return _dot(t01, a1) + _dot(t01, a2) + _dot(t01, a3)


def _sigmoid(x):
    return 1.0 / (1.0 + jnp.exp(-x))


def _silu(x):
    return x * _sigmoid(x)


def _layer_norm(x, g, b):
    mu = jnp.mean(x, axis=-1, keepdims=True)
    xc = x - mu
    var = jnp.mean(xc * xc, axis=-1, keepdims=True)
    return xc * lax.rsqrt(var + LN_EPS) * g + b


def _const_spec(shape):
    nd = len(shape)
    return pl.BlockSpec(shape, lambda *_: (0,) * nd, pipeline_mode=pl.Buffered(1))


def _params(n_axes=1):
    return pltpu.CompilerParams(dimension_semantics=("arbitrary",) * n_axes,
                                vmem_limit_bytes=VMEM_LIMIT)


def _mod_kernel(cond_ref, w0_ref, b0_ref, w1_ref, b1_ref, o0_ref, o1_ref, *, n_first):
    s = _silu(cond_ref[...]).astype(BF16)
    j = pl.program_id(0)

    @pl.when(j < n_first)
    def _():
        o0_ref[...] = _dot(s, w0_ref[...].astype(BF16)) + b0_ref[...]

    @pl.when(j >= n_first)
    def _():
        o1_ref[...] = _dot(s, w1_ref[...].astype(BF16)) + b1_ref[...]


def _modulation(cond8, w_mod0, b_mod0, w_mod1, b_mod1):
    tn = 1024
    n = w_mod0.shape[1]
    steps = n // tn
    first = lambda j: (0, jnp.minimum(j, steps - 1))
    second = lambda j: (0, jnp.maximum(j - steps, 0))
    outs = pl.pallas_call(
        functools.partial(_mod_kernel, n_first=steps),
        grid=(2 * steps,),
        in_specs=[pl.BlockSpec((8, D_MODEL), lambda j: (0, 0)),
                  pl.BlockSpec((D_MODEL, tn), first), pl.BlockSpec((1, tn), first),
                  pl.BlockSpec((D_MODEL, tn), second), pl.BlockSpec((1, tn), second)],
        out_specs=[pl.BlockSpec((8, tn), first), pl.BlockSpec((8, tn), second)],
        out_shape=[jax.ShapeDtypeStruct((8, n), F32)] * 2,
        compiler_params=_params(),
        name="modulation",
    )(cond8, w_mod0, b_mod0.reshape(1, n), w_mod1, b_mod1.reshape(1, n))
    return [o.reshape(8, N_MOD, D_MODEL) for o in outs]


def _mod_spec(row_of_tile):
    return pl.BlockSpec((1, N_MOD, D_MODEL), lambda i: (row_of_tile(i), 0, 0))


def _rope(t, cosf, sinf):
    lane = lax.broadcasted_iota(jnp.int32, t.shape, 1)
    first = (lane & (HEAD_DIM_A - 1)) < HEAD_DIM_A // 2
    width = t.shape[1]
    swapped = jnp.where(first, pltpu.roll(t, width - HEAD_DIM_A // 2, 1),
                        pltpu.roll(t, HEAD_DIM_A // 2, 1))
    return t * cosf + swapped * sinf


def _conv_silu(xbc, cw_ref, cb_ref, cols, seq_len):
    assert xbc.shape[0] == seq_len
    half = D_CONV // 2
    edge = 8
    offsets = [j - half for j in range(D_CONV) if j != half]
    shifted = {o: pltpu.roll(xbc, (-o) % seq_len, 0) for o in offsets}

    def taps(lo, hi, masked):
        acc = cb_ref[:, cols] + cw_ref[half:half + 1, cols] * xbc[lo:hi]
        pos = lo + lax.broadcasted_iota(jnp.int32, (hi - lo, xbc.shape[1]), 0)
        for o in offsets:
            s = shifted[o][lo:hi]
            if masked:
                s = jnp.where((pos >= -o) if o < 0 else (pos < seq_len - o), s, 0.0)
            acc = acc + cw_ref[o + half:o + half + 1, cols] * s
        return acc

    acc = jnp.concatenate([taps(0, edge, True), taps(edge, seq_len - edge, False),
                           taps(seq_len - edge, seq_len, True)], axis=0)
    return _silu(acc)


def _softplus(t):
    return jnp.maximum(t, 0.0) + jnp.log(1.0 + jnp.exp(-jnp.abs(t)))


def _ride_along_specs(mats, n_steps):
    specs, shapes = [], []
    for w in mats:
        assert w.shape[0] % n_steps == 0
        specs.append(pl.BlockSpec((w.shape[0] // n_steps, w.shape[1]), lambda i: (i, 0)))
        shapes.append(jax.ShapeDtypeStruct(w.shape, BF16))
    return specs, shapes


def _ride_along_cast(src_refs, dst_refs):
    for src, dst in zip(src_refs, dst_refs):
        dst[...] = src[...].astype(BF16)


def _inproj0_kernel(*refs, rope, seq_len, n_cast):
    n_in = 9 if rope else 7
    ins, cast_in = refs[:n_in], refs[n_in:n_in + n_cast]
    (q_ref, k_ref, v_ref, z_ref, xbc_ref, dt_ref) = refs[n_in + n_cast:n_in + n_cast + 6]
    cast_out = refs[n_in + n_cast + 6:]
    if rope:
        x_ref, mod_ref, w_ref, wdt_ref, cw_ref, cb_ref, dtb_ref, cos_ref, sin_ref = ins
    else:
        x_ref, mod_ref, w_ref, wdt_ref, cw_ref, cb_ref, dtb_ref = ins
    _ride_along_cast(cast_in, cast_out)
    m = mod_ref[0]
    tasks = []
    for b in range(x_ref.shape[0] // seq_len):
        rows = slice(b * seq_len, (b + 1) * seq_len)
        h = (x_ref[rows, :] * (1.0 + m[1:2]) + m[0:1]).astype(BF16)

        for c0 in range(0, CONV_DIM_B, CONV_COL_BLOCK):
            cols = slice(c0, c0 + CONV_COL_BLOCK)

            def conv_mm(h=h, c0=c0):
                return _dot(h, w_ref[:, 2048 + c0:2048 + c0 + CONV_COL_BLOCK])

            def conv_vec(raw, rows=rows, cols=cols):
                xbc_ref[rows, cols] = _conv_silu(raw, cw_ref, cb_ref, cols, seq_len)

            tasks.append((conv_mm, conv_vec))

        def qk_mm(h=h):
            return _dot(h, wdt_ref[...]), _dot(h, w_ref[:, 0:512]), _dot(h, w_ref[:, 512:1024])

        def qk_vec(res, rows=rows, b=b):
            dt_raw, q, k = res
            dt_ref[rows, :] = _softplus(dt_raw + dtb_ref[...])
            if rope:
                q = _rope(q, cos_ref[...], sin_ref[...])
                k = _rope(k, cos_ref[...], sin_ref[...])
            q_ref[rows, :] = (q * (HEAD_DIM_A ** -0.5 * LOG2E)).astype(BF16)
            for blk in range(2 * A_QK // LANES):
                t = k[:, blk * LANES:(blk + 1) * LANES].T
                which, head = blk // 2, (blk % 2) * 2
                k_ref[b, which, head] = t[0:HEAD_DIM_A]
                k_ref[b, which, head + 1] = t[HEAD_DIM_A:2 * HEAD_DIM_A]

        def vz_mm(h=h):
            return _dot(h, w_ref[:, 1024:1536]), _dot(h, w_ref[:, 1536:2048])

        def vz_vec(res, rows=rows, b=b):
            vals, z = res
            for hh in range(N_HEADS_A):
                v_ref[pl.ds(b * seq_len * N_HEADS_A + hh, seq_len, stride=N_HEADS_A), :] = (
                    vals[:, hh * LANES:(hh + 1) * LANES])
            z_ref[rows, :] = z

        tasks += [(qk_mm, qk_vec), (vz_mm, vz_vec)]

    pending = None
    for mm, vec in tasks:
        res = mm()
        if pending is not None:
            pending[0](pending[1])
        pending = (vec, res)
    pending[0](pending[1])


def _inproj0(x, mod, row_of_tile, w_main, w_dt, conv_w, conv_b, dt_bias, seq_len, tm, rope_tables=None,
             ride_along=()):
    t = x.shape[0]
    assert tm % seq_len == 0
    row = lambda n: pl.BlockSpec((tm, n), lambda i: (i, 0))
    flat = lambda n: jax.ShapeDtypeStruct((t, n), F32)
    consts = [w_main, w_dt, conv_w, conv_b, dt_bias]
    in_specs = [row(D_MODEL), _mod_spec(row_of_tile)] + [_const_spec(cst.shape) for cst in consts]
    args = [x, mod] + consts
    if rope_tables is not None:
        assert tm == seq_len
        in_specs += [_const_spec(tab.shape) for tab in rope_tables]
        args += list(rope_tables)
    kt_dims = (2, N_HEADS_A, HEAD_DIM_A, seq_len)
    k_spec = pl.BlockSpec((tm // seq_len,) + kt_dims, lambda i: (i, 0, 0, 0, 0))
    k_shape = jax.ShapeDtypeStruct((t // seq_len,) + kt_dims, F32)
    cast_specs, cast_shapes = _ride_along_specs(ride_along, t // tm)
    outs = pl.pallas_call(
        functools.partial(_inproj0_kernel, rope=rope_tables is not None, seq_len=seq_len, n_cast=len(ride_along)),
        grid=(t // tm,),
        in_specs=in_specs + cast_specs,
        out_specs=[row(512), k_spec, pl.BlockSpec((tm * N_HEADS_A, LANES), lambda i: (i, 0)),
                   row(512), row(CONV_DIM_B), row(LANES)] + cast_specs,
        out_shape=[jax.ShapeDtypeStruct((t, 512), BF16), k_shape, jax.ShapeDtypeStruct((t * N_HEADS_A, LANES), F32),
                   flat(512), flat(CONV_DIM_B), flat(LANES)] + cast_shapes,
        compiler_params=_params(),
        name="inproj0",
    )(*args, *ride_along)
    return outs[:6], outs[6:]


def _lambda_full(lam_ref, lam_init):
    lv = lam_ref[...]
    return (jnp.exp(jnp.sum(lv[0:1] * lv[1:2], axis=1, keepdims=True))
            - jnp.exp(jnp.sum(lv[2:3] * lv[3:4], axis=1, keepdims=True)) + lam_init)


def _attn_kernel(*refs, n_seg, tq, lam_init):
    q_ref = refs[0]
    kt_refs = refs[1:1 + n_seg]
    v_refs = refs[1 + n_seg:1 + 2 * n_seg]
    lam_ref, sub_ref, o_ref = refs[1 + 2 * n_seg:]
    lam = _lambda_full(lam_ref, lam_init)
    lane = lax.broadcasted_iota(jnp.int32, (1, A_QK), 1)
    masks = [(lane >= h * HEAD_DIM_A) & (lane < (h + 1) * HEAD_DIM_A) for h in range(N_HEADS_A)]
    keys = [r.shape[-1] for r in kt_refs]

    def stacked(qx):
        return jnp.concatenate([jnp.where(m, qx, 0.0) for m in masks], axis=0).astype(BF16)

    def exp_scores(qx, b, which):
        qs = stacked(qx)
        s = [_dot(qs, r[b, which].reshape(A_QK, n).astype(BF16)) for r, n in zip(kt_refs, keys)]
        mx = functools.reduce(jnp.maximum, [jnp.max(x, axis=1, keepdims=True) for x in s])
        return [jnp.exp2(x - mx).astype(BF16) for x in s]

    outs = []
    for b in range(kt_refs[0].shape[0]):
        q = q_ref[b * tq:(b + 1) * tq, :]
        e1 = exp_scores(q[:, 0:A_QK], b, 0)
        e2 = exp_scores(q[:, A_QK:2 * A_QK], b, 1)
        heads = []
        for h in range(N_HEADS_A):
            hrows = slice(h * tq, (h + 1) * tq)
            n1 = n2 = None
            for s, (v_ref, n) in enumerate(zip(v_refs, keys)):
                v_h = v_ref[pl.ds(b * n * N_HEADS_A + h, n, stride=N_HEADS_A), :]
                v_ext = jnp.concatenate([v_h.astype(BF16),
                                         jnp.ones((n, LANES), BF16)], axis=1)
                p1 = _dot(e1[s][hrows], v_ext)
                p2 = _dot(e2[s][hrows], v_ext)
                n1 = p1 if n1 is None else n1 + p1
                n2 = p2 if n2 is None else n2 + p2
            heads.append(n1[:, 0:LANES] / n1[:, LANES:] - lam * (n2[:, 0:LANES] / n2[:, LANES:]))
        o = jnp.concatenate(heads, axis=0)
        ms = jnp.mean(o * o, axis=1, keepdims=True)
        o = (o * lax.rsqrt(ms + 1e-5) * sub_ref[...]) * (1.0 - lam_init)
        outs.append(jnp.concatenate([o[h * tq:(h + 1) * tq] for h in range(N_HEADS_A)], axis=1))
    o_ref[...] = jnp.concatenate(outs, axis=0).astype(o_ref.dtype)


def _attention(q, kts, vs, lam_vecs, subln_w, q_len, tq, seqs_per_step, lam_init):
    n_seq = kts[0].shape[0]
    nb = seqs_per_step
    tiles = q_len // tq
    assert tiles == 1 or nb == 1
    qspec = pl.BlockSpec((nb * tq, 512), lambda b, i: (b * tiles + i, 0))
    ktspecs = [pl.BlockSpec((nb,) + kt.shape[1:], lambda b, i: (b, 0, 0, 0, 0)) for kt in kts]
    vspecs = [pl.BlockSpec((nb * kt.shape[-1] * N_HEADS_A, LANES), lambda b, i: (b, 0)) for kt in kts]
    return pl.pallas_call(
        functools.partial(_attn_kernel, n_seg=len(kts), tq=tq, lam_init=lam_init),
        grid=(n_seq // nb, tiles),
        in_specs=[qspec] + ktspecs + vspecs + [pl.BlockSpec((4, HEAD_DIM_A), lambda b, i: (0, 0)),
                                             pl.BlockSpec((1, LANES), lambda b, i: (0, 0))],
        out_specs=qspec,
        out_shape=jax.ShapeDtypeStruct(q.shape, BF16),
        compiler_params=_params(2),
        name="diff_attention",
    )(q, *kts, *vs, lam_vecs, subln_w.reshape(1, LANES))


def _chunk_loop(n, body, unroll=1):
    if n <= 2:
        for c in range(n):
            body(c)
    else:
        def step(c, carry):
            body(c)
            return carry
        lax.fori_loop(0, n, step, 0, unroll=unroll)


def _ssd_kernel(*refs, seq_len, n_seq, has_h0, n_cast):
    n_in = 7 if has_h0 else 6
    n_out = 1 if has_h0 else 2
    ins, cast_in = refs[:n_in], refs[n_in:n_in + n_cast]
    outs = refs[n_in + n_cast:n_in + n_cast + n_out]
    cast_out = refs[n_in + n_cast + n_out:n_in + 2 * n_cast + n_out]
    ysc, s_sc, cd_sc, e_sc, htf, htb = refs[n_in + 2 * n_cast + n_out:]
    if has_h0:
        xact, dtv, z_ref, h0_ref, arow_ref, dsk_ref, nw_ref = ins
        (y_ref,), hfin_ref = outs, None
    else:
        xact, dtv, z_ref, arow_ref, dsk_ref, nw_ref = ins
        (y_ref, hfin_ref), h0_ref = outs, None
    _ride_along_cast(cast_in, cast_out)
    nc = seq_len // CHUNK
    n_pairs = N_HEADS_B // 2
    a_row = -jnp.exp(arow_ref[...]) * LOG2E

    row_i = lax.broadcasted_iota(jnp.int32, (CHUNK, CHUNK), 0)
    col_i = lax.broadcasted_iota(jnp.int32, (CHUNK, CHUNK), 1)
    lower = col_i <= row_i
    upper = col_i >= row_i
    tri = (jnp.where(lower, 1.0, 0.0).astype(BF16), jnp.where(upper, 1.0, 0.0).astype(BF16))
    lo_half = col_i < HEAD_DIM_B

    def split_pair(m):
        zero = jnp.zeros_like(m)
        return jnp.concatenate([jnp.where(lo_half, m, zero), jnp.where(lo_half, zero, m)], axis=0).astype(BF16)

    def decay_terms(c):
        rows = pl.ds(pl.multiple_of(c * CHUNK, CHUNK), CHUNK)
        dtc = dtv[rows, :]
        dtt = dtc.T[0:16, :]
        per_dir = []
        for d in range(2):
            col = _dot_exact_rhs(tri[d], dtc * a_row)
            rowm = col.T[0:16, :]
            far = CHUNK - 1 if d == 0 else 0
            tot = jnp.broadcast_to(rowm[:, far:far + 1], (16, CHUNK))
            dte = jnp.exp2(tot - rowm) * dtt
            per_dir.append((col, rowm - jnp.log2(dtt), dte, jnp.exp2(tot)))
        return per_dir

    def local_phase(c, per_dir=None):
        rows = pl.ds(pl.multiple_of(c * CHUNK, CHUNK), CHUNK)
        if per_dir is None:
            per_dir = decay_terms(c)
        for g in range(2):
            bg = xact[rows, 512 + g * LANES:512 + (g + 1) * LANES]
            cg = xact[rows, 768 + g * LANES:768 + (g + 1) * LANES]
            cbm = _dot_nt(cg.astype(BF16), bg.astype(BF16))
            bgt = bg.T
            for pq in range(2):
                pi = g * 2 + pq
                lanes = slice(pi * LANES, (pi + 1) * LANES)
                xrhs = split_pair(xact[rows, lanes])
                y_pair = None
                for d in range(2):
                    col, row_dt, dte, cdec = per_dir[d]
                    causal = lower if d == 0 else upper
                    m_parts, col_parts, bw_parts, cd_parts = [], [], [], []
                    for hh in (2 * pi, 2 * pi + 1):
                        hd = 8 * d + hh
                        colb = jnp.sum(jnp.where(col_i == hd, col, 0.0), axis=1, keepdims=True)
                        m_parts.append(cbm * jnp.exp2(jnp.where(causal, colb - row_dt[hd:hd + 1, :], NEG_BIG)))
                        col_parts.append(colb)
                        bw_parts.append(bgt * dte[hd:hd + 1, :])
                        cd_parts.append(cdec[hd:hd + 1, :])
                    yd = _dot(jnp.concatenate(m_parts, axis=1).astype(BF16), xrhs)
                    y_pair = yd if y_pair is None else y_pair + yd
                    s_sc[c, d, :, lanes] = _dot(jnp.concatenate(bw_parts, axis=1).astype(BF16), xrhs)
                    e_sc[c, d, :, lanes] = jnp.exp2(jnp.where(lo_half, col_parts[0], col_parts[1]))
                    cd_sc[c, d, :, lanes] = jnp.broadcast_to(
                        jnp.where(lo_half[0:1, :], cd_parts[0], cd_parts[1]), (8, LANES))
                ysc[rows, lanes] = y_pair

    if n_seq * nc <= SSD_UNROLL_CHUNKS:
        terms = [decay_terms(c) for c in range(n_seq * nc)]
        for c in range(n_seq * nc):
            local_phase(c, terms[c])
    else:
        _chunk_loop(n_seq * nc, local_phase, unroll=2)

    def scan_sequence(sq):
        for d, ht in ((0, htf), (1, htb)):
            for qd in range(n_pairs):
                lanes = slice(qd * LANES, (qd + 1) * LANES)
                if has_h0:
                    ht[:, lanes] = h0_ref[sq, d, lanes, :].T
                else:
                    ht[:, lanes] = jnp.zeros((D_STATE, LANES), F32)

        def scan_phase(j):
            for d, ht in ((0, htf), (1, htb)):
                c = sq * nc + (j if d == 0 else nc - 1 - j)
                rows = pl.ds(pl.multiple_of(c * CHUNK, CHUNK), CHUNK)
                for pi in range(n_pairs):
                    lanes = slice(pi * LANES, (pi + 1) * LANES)
                    g = pi // 2
                    cg = xact[rows, 768 + g * LANES:768 + (g + 1) * LANES].astype(BF16)
                    hprev = ht[:, lanes]
                    ysc[rows, lanes] += _dot(cg, hprev.astype(BF16)) * e_sc[c, d, :, lanes]
                    ht[:, lanes] = hprev * cd_sc[c, d, 0:1, lanes] + s_sc[c, d, :, lanes]

        _chunk_loop(nc, scan_phase)
        if hfin_ref is not None:
            for d, ht in ((0, htf), (1, htb)):
                for qd in range(n_pairs):
                    lanes = slice(qd * LANES, (qd + 1) * LANES)
                    hfin_ref[sq, d, lanes, :] = ht[:, lanes].T

    if n_seq == 1:
        scan_sequence(0)
    else:
        def seq_step(sq, carry):
            scan_sequence(sq)
            return carry
        lax.fori_loop(0, n_seq, seq_step, 0)

    def finish(c):
        rows = pl.ds(pl.multiple_of(c * CHUNK, CHUNK), CHUNK)
        y = ysc[rows, :] + dsk_ref[...] * xact[rows, 0:D_INNER_B]
        y = y * _silu(z_ref[rows, :])
        ms = jnp.mean(y * y, axis=1, keepdims=True)
        y_ref[rows, :] = (y * lax.rsqrt(ms + 1e-5) * nw_ref[...]).astype(y_ref.dtype)

    _chunk_loop(n_seq * nc, finish)


def _ssd(xbc, dt, z, h0, consts, n_batch, seq_len, seqs_per_step, ride_along=()):
    nb = seqs_per_step
    L = nb * seq_len
    nc = L // CHUNK
    row = lambda n: pl.BlockSpec((L, n), lambda b: (b, 0))
    state_spec = pl.BlockSpec((nb, 2, D_INNER_B, D_STATE), lambda b: (b, 0, 0, 0))
    has_h0 = h0 is not None
    in_specs = [row(CONV_DIM_B), row(LANES), row(D_INNER_B)]
    args = [xbc, dt, z]
    if has_h0:
        in_specs.append(state_spec)
        args.append(h0)
    for cst in consts:
        in_specs.append(pl.BlockSpec(cst.shape, lambda b: (0, 0)))
        args.append(cst)
    out_specs = [row(D_INNER_B)]
    out_shape = [jax.ShapeDtypeStruct((n_batch * seq_len, D_INNER_B), BF16)]
    if not has_h0:
        out_specs.append(state_spec)
        out_shape.append(jax.ShapeDtypeStruct((n_batch, 2, D_INNER_B, D_STATE), F32))
    scratch = [pltpu.VMEM((L, D_INNER_B), F32),
               pltpu.VMEM((nc, 2, D_STATE, D_INNER_B), F32),
               pltpu.VMEM((nc, 2, 8, D_INNER_B), F32),
               pltpu.VMEM((nc, 2, CHUNK, D_INNER_B), F32),
               pltpu.VMEM((D_STATE, D_INNER_B), F32),
               pltpu.VMEM((D_STATE, D_INNER_B), F32)]
    cast_specs, cast_shapes = _ride_along_specs(ride_along, n_batch // nb)
    n_main_out = len(out_specs)
    outs = pl.pallas_call(
        functools.partial(_ssd_kernel, seq_len=seq_len, n_seq=nb, has_h0=has_h0, n_cast=len(ride_along)),
        grid=(n_batch // nb,),
        in_specs=in_specs + cast_specs,
        out_specs=out_specs + cast_specs,
        out_shape=out_shape + cast_shapes,
        scratch_shapes=scratch,
        compiler_params=_params(),
        name="bidir_ssd",
    )(*args, *ride_along)
    return outs[:n_main_out], outs[n_main_out:]


def _tail_kernel(*refs, n_in, n_ctx_tiles):
    ctx_a, lat_a = refs[:n_in], refs[n_in:2 * n_in]
    (xc_ref, xl_ref, mod_ref, wout_ref, wup_ref, wdown_ref, g1_ref, b1_ref, g2_ref, b2_ref,
     oc_ref, ol_ref) = refs[2 * n_in:]
    step = pl.program_id(0)

    tile = functools.partial(_mlp_tile, mod_ref=mod_ref, wout_ref=wout_ref, g1_ref=g1_ref, b1_ref=b1_ref,
                             wup_ref=wup_ref, wdown_ref=wdown_ref, g2_ref=g2_ref, b2_ref=b2_ref)

    @pl.when(step < n_ctx_tiles)
    def _():
        tile(ctx_a, xc_ref, oc_ref)

    @pl.when(step >= n_ctx_tiles)
    def _():
        tile(lat_a, xl_ref, ol_ref)


def _mlp_tile(a_refs, x_ref, o_ref, *, mod_ref, wout_ref, g1_ref, b1_ref, wup_ref, wdown_ref, g2_ref, b2_ref):
    m = mod_ref[0]
    n_chunks = D_FF // FF_CHUNK

    def out_proj(rows):
        d = None
        off = 0
        for a_ref in a_refs:
            n = a_ref.shape[1]
            part = _dot(a_ref[rows, :].astype(BF16), wout_ref[off:off + n, :])
            d = part if d is None else d + part
            off += n
        return d

    def head(rows, d):
        x1 = _layer_norm(ALPHA * x_ref[rows, :] + m[2:3] * d, g1_ref[...], b1_ref[...])
        return x1, (x1 * (1.0 + m[4:5]) + m[3:4]).astype(BF16)

    def up(h, c):
        return _dot(h, wup_ref[:, c * FF_CHUNK:(c + 1) * FF_CHUNK])

    def down(u, acc, c):
        u = jnp.maximum(u, 0.0)
        part = _dot((u * u).astype(BF16), wdown_ref[c * FF_CHUNK:(c + 1) * FF_CHUNK, :])
        return part if acc is None else acc + part

    def finish(rows, x1, acc):
        o_ref[rows, :] = _layer_norm(ALPHA * x1 + m[5:6] * acc, g2_ref[...], b2_ref[...])

    assert sum(TAIL_SPLIT) == x_ref.shape[0]
    k = len(TAIL_SPLIT)
    starts = [sum(TAIL_SPLIT[:t]) for t in range(k)]
    rows = [slice(s, s + n) for s, n in zip(starts, TAIL_SPLIT)]
    projected = [out_proj(r) for r in rows]
    heads = [None] * k
    heads[0] = head(rows[0], projected[0])
    u_next = up(heads[0][1], 0)
    done = None
    for t in range(k):
        if t + 1 < k:
            heads[t + 1] = head(rows[t + 1], projected[t + 1])
        acc = None
        for c in range(n_chunks):
            if c + 1 < n_chunks:
                nxt = up(heads[t][1], c + 1)
            else:
                nxt = up(heads[t + 1][1], 0) if t + 1 < k else None
            u_cur, u_next = u_next, nxt
            acc = down(u_cur, acc, c)
            if c == 0 and done is not None:
                finish(*done)
        done = (rows[t], heads[t][0], acc)
    finish(*done)


def _tail(a_ctx, a_lat, x_ctx, x_lat, mod, lat_seq_len, w_out, ln1_g, ln1_b, w_up, w_down, ln2_g, ln2_b):
    tm = TOKEN_TILE
    n_ctx = x_ctx.shape[0] // tm
    n_lat = x_lat.shape[0] // tm
    tiles_per_lat_seq = lat_seq_len // tm
    ctx_row = lambda n: pl.BlockSpec((tm, n), lambda i: (jnp.minimum(i, n_ctx - 1), 0))
    lat_row = lambda n: pl.BlockSpec((tm, n), lambda i: (jnp.maximum(i - n_ctx, 0), 0))
    lat_out = pl.BlockSpec((tm, D_MODEL), lambda i: (jnp.maximum(i - n_ctx, 0), 0))
    mod_spec = _mod_spec(lambda i: jnp.where(i < n_ctx, 0, 1 + (i - n_ctx) // tiles_per_lat_seq))
    vec = lambda v: v.reshape(1, D_MODEL)
    consts = [w_out, w_up, w_down, vec(ln1_g), vec(ln1_b), vec(ln2_g), vec(ln2_b)]
    return pl.pallas_call(
        functools.partial(_tail_kernel, n_in=len(a_ctx), n_ctx_tiles=n_ctx),
        grid=(n_ctx + n_lat,),
        in_specs=[ctx_row(a.shape[1]) for a in a_ctx] + [lat_row(a.shape[1]) for a in a_lat]
                 + [ctx_row(D_MODEL), lat_row(D_MODEL), mod_spec] + [_const_spec(v.shape) for v in consts],
        out_specs=[ctx_row(D_MODEL), lat_out],
        out_shape=[jax.ShapeDtypeStruct(x_ctx.shape, F32), jax.ShapeDtypeStruct(x_lat.shape, F32)],
        compiler_params=_params(),
        name="outproj_mlp",
    )(*a_ctx, *a_lat, x_ctx, x_lat, mod, *consts)


def _pool_kernel(x_ref, mod_ref, win_ref, pw_ref, ps_ref, o_ref, *, seq_len):
    L = seq_len
    m = mod_ref[0]
    t_i = lax.broadcasted_iota(jnp.int32, (L, L), 0)
    s_i = lax.broadcasted_iota(jnp.int32, (L, L), 1)
    t_g = lax.broadcasted_iota(jnp.int32, (L, POOL_GROUP_DIM), 0)
    bands, scales = [], []
    for w in POOL_WINDOWS:
        lo, hi = w // 2, w - w // 2
        bands.append(jnp.where((s_i >= t_i - lo) & (s_i < t_i + hi), 1.0 / w, 0.0).astype(BF16))
        scales.append(w / (jnp.minimum(t_g + hi, L) - jnp.maximum(t_g - lo, 0)).astype(F32))
    n_seq = x_ref.shape[0] // L
    seq_rows = [slice(s * L, (s + 1) * L) for s in range(n_seq)]
    hs = [(x_ref[r, :] * (1.0 + m[1:2]) + m[0:1]).astype(BF16) for r in seq_rows]
    us = [_dot(h, win_ref[...]) for h in hs]
    n_groups = len(POOL_WINDOWS)
    cols = [slice(g * POOL_GROUP_DIM, (g + 1) * POOL_GROUP_DIM) for g in range(n_groups)]

    def window_means(g):
        out = []
        for s in range(n_seq):
            ug = us[s][:, cols[g]]
            ug_hi = ug.astype(BF16)
            ug_lo = (ug - ug_hi.astype(F32)).astype(BF16)
            out.append((ug, _dot(bands[g], ug_hi) + _dot(bands[g], ug_lo)))
        return out

    def mix(g, means):
        return [_dot((wm * scales[g] - ug).astype(BF16), pw_ref[g]) * ps_ref[:, cols[g]] for ug, wm in means]

    mixed = []
    pending = window_means(0)
    for g in range(n_groups):
        nxt = window_means(g + 1) if g + 1 < n_groups else None
        mixed.append(mix(g, pending))
        pending = nxt
    for s in range(n_seq):
        o_ref[seq_rows[s], :] = jnp.concatenate([mixed[g][s] for g in range(n_groups)], axis=1).astype(o_ref.dtype)


def _pool_mixer(x, mod, row_of_tile, w_in, pool_w, pool_scale, seq_len, tm):
    t = x.shape[0]
    row = pl.BlockSpec((tm, D_MODEL), lambda i: (i, 0))
    consts = [w_in, pool_w, pool_scale.reshape(1, D_MODEL)]
    return pl.pallas_call(
        functools.partial(_pool_kernel, seq_len=seq_len),
        grid=(t // tm,),
        in_specs=[row, _mod_spec(row_of_tile)] + [_const_spec(cst.shape) for cst in consts],
        out_specs=row,
        out_shape=jax.ShapeDtypeStruct((t, D_MODEL), BF16),
        compiler_params=_params(),
        name="pool_mixer",
    )(x, mod, *consts)


def _rope_tables(rows):
    r, col = np.meshgrid(np.arange(rows), np.arange(GRID_W), indexing="ij")
    r = r.reshape(-1).astype(np.float64)
    col = col.reshape(-1).astype(np.float64)
    n_freq = HEAD_DIM_A // 4
    inv = ROPE_BASE ** (-np.arange(n_freq, dtype=np.float64) / n_freq)
    ang = np.concatenate([r[:, None] * inv, col[:, None] * inv], -1)
    reps = 512 // HEAD_DIM_A
    cosf = np.tile(np.concatenate([np.cos(ang), np.cos(ang)], -1), (1, reps)).astype(np.float32)
    sinf = np.tile(np.concatenate([-np.sin(ang), np.sin(ang)], -1), (1, reps)).astype(np.float32)
    return jnp.asarray(cosf), jnp.asarray(sinf)


def kernel(x_prompt, x_sample, cache_k_l0, cache_v_l0, state_ssm_l0, c, c_ctx, w_mod_l0, b_mod_l0, w_in_l0, conv_w_l0, conv_b_l0, a_log_l0, dt_bias_l0, d_skip_l0, ssm_norm_w_l0, lam_q1_l0, lam_k1_l0, lam_q2_l0, lam_k2_l0, subln_w_l0, w_out_l0, ln1_g_l0, ln1_b_l0, w_up_l0, w_down_l0, ln2_g_l0, ln2_b_l0, w_mod_l1, b_mod_l1, w_in_l1, pool_w_l1, pool_scale_l1, w_out_l1, ln1_g_l1, ln1_b_l1, w_up_l1, w_down_l1, ln2_g_l1, ln2_b_l1):
    n_ctx, l_ctx, _ = x_prompt.shape
    n_lat, l_lat, _ = x_sample.shape
    past = cache_k_l0.shape[1]
    xc = x_prompt.reshape(n_ctx * l_ctx, D_MODEL)
    xl = x_sample.reshape(n_lat * l_lat, D_MODEL)

    cond8 = jnp.concatenate([c_ctx[None, :], c, jnp.zeros((8 - 1 - n_lat, D_MODEL), F32)], axis=0)
    ctx_row = lambda i: 0

    mod0, mod1 = _modulation(cond8, w_mod_l0, b_mod_l0, w_mod_l1, b_mod_l1)
    n_main = 4 * A_QK + A_V + D_INNER_B + CONV_DIM_B
    w_main = w_in_l0.astype(BF16)
    w_dt = jnp.pad(w_in_l0[:, n_main:], ((0, 0), (0, LANES - 2 * N_HEADS_B))).astype(BF16)
    lane_pad = lambda v: jnp.pad(v.reshape(1, -1), ((0, 0), (0, LANES - 2 * N_HEADS_B)))
    conv_consts = (conv_w_l0, conv_b_l0.reshape(1, CONV_DIM_B), lane_pad(dt_bias_l0))
    ssd_consts = [lane_pad(a_log_l0),
                  jnp.repeat(d_skip_l0, HEAD_DIM_B).reshape(1, D_INNER_B),
                  ssm_norm_w_l0.reshape(1, D_INNER_B)]
    lam_vecs = jnp.stack([lam_q1_l0, lam_k1_l0, lam_q2_l0, lam_k2_l0], axis=0)
    lam_init = 0.8 - 0.6 * math.exp(-0.3 * 0)

    (qc, kc, vc, zc, xbc_c, dt_c), (w_out0, w_up0, w_down0, w_out1, w_up1, w_down1, w_in1, pool_w) = _inproj0(
        xc, mod0, ctx_row, w_main, w_dt, *conv_consts, l_ctx, TOKEN_TILE,
        ride_along=(w_out_l0, w_up_l0, w_down_l0, w_out_l1, w_up_l1, w_down_l1, w_in_l1,
                    pool_w_l1.reshape(-1, POOL_GROUP_DIM)))
    pool_w = pool_w.reshape(pool_w_l1.shape)
    oa_c = _attention(qc, [kc], [vc], lam_vecs, subln_w_l0, l_ctx, l_ctx, CTX_SEQS_PER_STEP, lam_init)
    (y_c, new_ssm), _ = _ssd(xbc_c, dt_c, zc, None, ssd_consts, n_ctx, l_ctx, SSD_CTX_SEQS_PER_STEP)

    (ql, kl, vl, zl, xbc_l, dt_l), _ = _inproj0(xl, mod0, lambda i: 1 + i, w_main, w_dt, *conv_consts, l_lat, l_lat,
                                                rope_tables=_rope_tables(l_lat // GRID_W))
    ck = jnp.transpose(cache_k_l0, (0, 2, 3, 4, 1))
    cv = cache_v_l0.reshape(n_lat * past * N_HEADS_A, 2 * HEAD_DIM_A)
    oa_l = _attention(ql, [ck, kl], [cv, vl], lam_vecs, subln_w_l0, l_lat, ATTN_Q_TILE, 1, lam_init)
    h0 = state_ssm_l0.reshape(n_lat, 2, D_INNER_B, D_STATE)
    ((y_l,), _) = _ssd(xbc_l, dt_l, zl, h0, ssd_consts, n_lat, l_lat, 1)
    tail0 = (w_out0, ln1_g_l0, ln1_b_l0, w_up0, w_down0, ln2_g_l0, ln2_b_l0)
    xc, xl = _tail([oa_c, y_c], [oa_l, y_l], xc, xl, mod0, l_lat, *tail0)

    tail1 = (w_out1, ln1_g_l1, ln1_b_l1, w_up1, w_down1, ln2_g_l1, ln2_b_l1)
    mix_c = _pool_mixer(xc, mod1, ctx_row, w_in1, pool_w, pool_scale_l1, l_ctx, POOL_CTX_TILE)
    mix_l = _pool_mixer(xl, mod1, lambda i: 1 + i, w_in1, pool_w, pool_scale_l1, l_lat, l_lat)
    xc, xl = _tail([mix_c], [mix_l], xc, xl, mod1, l_lat, *tail1)

    return (xc.reshape(n_ctx, l_ctx, D_MODEL), xl.reshape(n_lat, l_lat, D_MODEL),
            jnp.transpose(kc, (0, 4, 1, 2, 3)),
            vc.reshape(n_ctx, l_ctx, N_HEADS_A, 2 * HEAD_DIM_A),
            new_ssm.reshape(n_ctx, 2, N_HEADS_B, HEAD_DIM_B, D_STATE))
```

```python
import functools
import math

import jax
import jax.numpy as jnp
import numpy as np
from jax import lax
from jax.experimental import pallas as pl
from jax.experimental.pallas import tpu as pltpu

F32 = jnp.float32
BF16 = jnp.bfloat16

D_MODEL = 1024
N_MOD = 6
N_HEADS_A = 4
HEAD_DIM_A = 64
A_QK = N_HEADS_A * HEAD_DIM_A
A_V = 2 * A_QK
D_INNER_B = 512
HEAD_DIM_B = 64
N_HEADS_B = 8
D_STATE = 128
D_CONV = 5
CONV_DIM_B = 1024
CHUNK = 128
GRID_W = 64
POOL_WINDOWS = (2, 4, 8, 16)
POOL_GROUP_DIM = 256
D_FF = 4096
ROPE_BASE = 10000.0
LN_EPS = 1e-5
DEPTH = 2
ALPHA = (2 * DEPTH) ** 0.25
NEG_BIG = -1e30
LOG2E = 1.4426950408889634

LANES = 128
TOKEN_TILE = 512
ATTN_Q_TILE = 256
CTX_SEQS_PER_STEP = 8
SSD_CTX_SEQS_PER_STEP = 1
SSD_UNROLL_CHUNKS = 4
FF_CHUNK = 1024
CONV_COL_BLOCK = 256
TAIL_SPLIT = (256, 256)
VMEM_LIMIT = 56 * 1024 * 1024


def _dot(a, b):
    return jnp.dot(a, b, preferred_element_type=F32)


def _dot_nt(a, b):
    return lax.dot_general(a, b, (((1,), (1,)), ((), ())), preferred_element_type=F32)


def _bf16_pieces(a):
    a1 = a.astype(BF16)
    r1 = a - a1.astype(F32)
    a2 = r1.astype(BF16)
    a3 = (r1 - a2.astype(F32)).astype(BF16)
    return a1, a2, a3


def _dot_exact_rhs(t01, a):
    a1, a2, a3 = _bf16_pieces(a)
    return _dot(t01, a1) + _dot(t01, a2) + _dot(t01, a3)


def _sigmoid(x):
    return 1.0 / (1.0 + jnp.exp(-x))


def _silu(x):
    return x * _sigmoid(x)


def _layer_norm(x, g, b):
    mu = jnp.mean(x, axis=-1, keepdims=True)
    xc = x - mu
    var = jnp.mean(xc * xc, axis=-1, keepdims=True)
    return xc * lax.rsqrt(var + LN_EPS) * g + b


def _const_spec(shape):
    nd = len(shape)
    return pl.BlockSpec(shape, lambda *_: (0,) * nd, pipeline_mode=pl.Buffered(1))


def _params(n_axes=1):
    return pltpu.CompilerParams(dimension_semantics=("arbitrary",) * n_axes,
                                vmem_limit_bytes=VMEM_LIMIT)


def _mod_kernel(cond_ref, w0_ref, b0_ref, w1_ref, b1_ref, o0_ref, o1_ref, *, n_first):
    s = _silu(cond_ref[...]).astype(BF16)
    j = pl.program_id(0)

    @pl.when(j < n_first)
    def _():
        o0_ref[...] = _dot(s, w0_ref[...].astype(BF16)) + b0_ref[...]

    @pl.when(j >= n_first)
    def _():
        o1_ref[...] = _dot(s, w1_ref[...].astype(BF16)) + b1_ref[...]


def _modulation(cond8, w_mod0, b_mod0, w_mod1, b_mod1):
    tn = 1024
    n = w_mod0.shape[1]
    steps = n // tn
    first = lambda j: (0, jnp.minimum(j, steps - 1))
    second = lambda j: (0, jnp.maximum(j - steps, 0))
    outs = pl.pallas_call(
        functools.partial(_mod_kernel, n_first=steps),
        grid=(2 * steps,),
        in_specs=[pl.BlockSpec((8, D_MODEL), lambda j: (0, 0)),
                  pl.BlockSpec((D_MODEL, tn), first), pl.BlockSpec((1, tn), first),
                  pl.BlockSpec((D_MODEL, tn), second), pl.BlockSpec((1, tn), second)],
        out_specs=[pl.BlockSpec((8, tn), first), pl.BlockSpec((8, tn), second)],
        out_shape=[jax.ShapeDtypeStruct((8, n), F32)] * 2,
        compiler_params=_params(),
        name="modulation",
    )(cond8, w_mod0, b_mod0.reshape(1, n), w_mod1, b_mod1.reshape(1, n))
    return [o.reshape(8, N_MOD, D_MODEL) for o in outs]


def _mod_spec(row_of_tile):
    return pl.BlockSpec((1, N_MOD, D_MODEL), lambda i: (row_of_tile(i), 0, 0))


def _rope(t, cosf, sinf):
    lane = lax.broadcasted_iota(jnp.int32, t.shape, 1)
    first = (lane & (HEAD_DIM_A - 1)) < HEAD_DIM_A // 2
    width = t.shape[1]
    swapped = jnp.where(first, pltpu.roll(t, width - HEAD_DIM_A // 2, 1),
                        pltpu.roll(t, HEAD_DIM_A // 2, 1))
    return t * cosf + swapped * sinf


def _conv_silu(xbc, cw_ref, cb_ref, cols, seq_len):
    assert xbc.shape[0] == seq_len
    half = D_CONV // 2
    edge = 8
    offsets = [j - half for j in range(D_CONV) if j != half]
    shifted = {o: pltpu.roll(xbc, (-o) % seq_len, 0) for o in offsets}

    def taps(lo, hi, masked):
        acc = cb_ref[:, cols] + cw_ref[half:half + 1, cols] * xbc[lo:hi]
        pos = lo + lax.broadcasted_iota(jnp.int32, (hi - lo, xbc.shape[1]), 0)
        for o in offsets:
            s = shifted[o][lo:hi]
            if masked:
                s = jnp.where((pos >= -o) if o < 0 else (pos < seq_len - o), s, 0.0)
            acc = acc + cw_ref[o + half:o + half + 1, cols] * s
        return acc

    acc = jnp.concatenate([taps(0, edge, True), taps(edge, seq_len - edge, False),
                           taps(seq_len - edge, seq_len, True)], axis=0)
    return _silu(acc)


def _softplus(t):
    return jnp.maximum(t, 0.0) + jnp.log(1.0 + jnp.exp(-jnp.abs(t)))


def _ride_along_specs(mats, n_steps):
    specs, shapes = [], []
    for w in mats:
        assert w.shape[0] % n_steps == 0
        specs.append(pl.BlockSpec((w.shape[0] // n_steps, w.shape[1]), lambda i: (i, 0)))
        shapes.append(jax.ShapeDtypeStruct(w.shape, BF16))
    return specs, shapes


def _ride_along_cast(src_refs, dst_refs):
    for src, dst in zip(src_refs, dst_refs):
        dst[...] = src[...].astype(BF16)


def _inproj0_kernel(*refs, rope, seq_len, n_cast):
    n_in = 9 if rope else 7
    ins, cast_in = refs[:n_in], refs[n_in:n_in + n_cast]
    (q_ref, k_ref, v_ref, z_ref, xbc_ref, dt_ref) = refs[n_in + n_cast:n_in + n_cast + 6]
    cast_out = refs[n_in + n_cast + 6:]
    if rope:
        x_ref, mod_ref, w_ref, wdt_ref, cw_ref, cb_ref, dtb_ref, cos_ref, sin_ref = ins
    else:
        x_ref, mod_ref, w_ref, wdt_ref, cw_ref, cb_ref, dtb_ref = ins
    _ride_along_cast(cast_in, cast_out)
    m = mod_ref[0]
    tasks = []
    for b in range(x_ref.shape[0] // seq_len):
        rows = slice(b * seq_len, (b + 1) * seq_len)
        h = (x_ref[rows, :] * (1.0 + m[1:2]) + m[0:1]).astype(BF16)

        for c0 in range(0, CONV_DIM_B, CONV_COL_BLOCK):
            cols = slice(c0, c0 + CONV_COL_BLOCK)

            def conv_mm(h=h, c0=c0):
                return _dot(h, w_ref[:, 2048 + c0:2048 + c0 + CONV_COL_BLOCK])

            def conv_vec(raw, rows=rows, cols=cols):
                xbc_ref[rows, cols] = _conv_silu(raw, cw_ref, cb_ref, cols, seq_len)

            tasks.append((conv_mm, conv_vec))

        def qk_mm(h=h):
            return _dot(h, wdt_ref[...]), _dot(h, w_ref[:, 0:512]), _dot(h, w_ref[:, 512:1024])

        def qk_vec(res, rows=rows, b=b):
            dt_raw, q, k = res
            dt_ref[rows, :] = _softplus(dt_raw + dtb_ref[...])
            if rope:
                q = _rope(q, cos_ref[...], sin_ref[...])
                k = _rope(k, cos_ref[...], sin_ref[...])
            q_ref[rows, :] = (q * (HEAD_DIM_A ** -0.5 * LOG2E)).astype(BF16)
            for blk in range(2 * A_QK // LANES):
                t = k[:, blk * LANES:(blk + 1) * LANES].T
                which, head = blk // 2, (blk % 2) * 2
                k_ref[b, which, head] = t[0:HEAD_DIM_A]
                k_ref[b, which, head + 1] = t[HEAD_DIM_A:2 * HEAD_DIM_A]

        def vz_mm(h=h):
            return _dot(h, w_ref[:, 1024:1536]), _dot(h, w_ref[:, 1536:2048])

        def vz_vec(res, rows=rows, b=b):
            vals, z = res
            for hh in range(N_HEADS_A):
                v_ref[pl.ds(b * seq_len * N_HEADS_A + hh, seq_len, stride=N_HEADS_A), :] = (
                    vals[:, hh * LANES:(hh + 1) * LANES])
            z_ref[rows, :] = z

        tasks += [(qk_mm, qk_vec), (vz_mm, vz_vec)]

    pending = None
    for mm, vec in tasks:
        res = mm()
        if pending is not None:
            pending[0](pending[1])
        pending = (vec, res)
    pending[0](pending[1])


def _inproj0(x, mod, row_of_tile, w_main, w_dt, conv_w, conv_b, dt_bias, seq_len, tm, rope_tables=None,
             ride_along=()):
    t = x.shape[0]
    assert tm % seq_len == 0
    row = lambda n: pl.BlockSpec((tm, n), lambda i: (i, 0))
    flat = lambda n: jax.ShapeDtypeStruct((t, n), F32)
    consts = [w_main, w_dt, conv_w, conv_b, dt_bias]
    in_specs = [row(D_MODEL), _mod_spec(row_of_tile)] + [_const_spec(cst.shape) for cst in consts]
    args = [x, mod] + consts
    if rope_tables is not None:
        assert tm == seq_len
        in_specs += [_const_spec(tab.shape) for tab in rope_tables]
        args += list(rope_tables)
    kt_dims = (2, N_HEADS_A, HEAD_DIM_A, seq_len)
    k_spec = pl.BlockSpec((tm // seq_len,) + kt_dims, lambda i: (i, 0, 0, 0, 0))
    k_shape = jax.ShapeDtypeStruct((t // seq_len,) + kt_dims, F32)
    cast_specs, cast_shapes = _ride_along_specs(ride_along, t // tm)
    outs = pl.pallas_call(
        functools.partial(_inproj0_kernel, rope=rope_tables is not None, seq_len=seq_len, n_cast=len(ride_along)),
        grid=(t // tm,),
        in_specs=in_specs + cast_specs,
        out_specs=[row(512), k_spec, pl.BlockSpec((tm * N_HEADS_A, LANES), lambda i: (i, 0)),
                   row(512), row(CONV_DIM_B), row(LANES)] + cast_specs,
        out_shape=[jax.ShapeDtypeStruct((t, 512), BF16), k_shape, jax.ShapeDtypeStruct((t * N_HEADS_A, LANES), F32),
                   flat(512), flat(CONV_DIM_B), flat(LANES)] + cast_shapes,
        compiler_params=_params(),
        name="inproj0",
    )(*args, *ride_along)
    return outs[:6], outs[6:]


def _lambda_full(lam_ref, lam_init):
    lv = lam_ref[...]
    return (jnp.exp(jnp.sum(lv[0:1] * lv[1:2], axis=1, keepdims=True))
            - jnp.exp(jnp.sum(lv[2:3] * lv[3:4], axis=1, keepdims=True)) + lam_init)


def _attn_kernel(*refs, n_seg, tq, lam_init):
    q_ref = refs[0]
    kt_refs = refs[1:1 + n_seg]
    v_refs = refs[1 + n_seg:1 + 2 * n_seg]
    lam_ref, sub_ref, o_ref = refs[1 + 2 * n_seg:]
    lam = _lambda_full(lam_ref, lam_init)
    lane = lax.broadcasted_iota(jnp.int32, (1, A_QK), 1)
    masks = [(lane >= h * HEAD_DIM_A) & (lane < (h + 1) * HEAD_DIM_A) for h in range(N_HEADS_A)]
    keys = [r.shape[-1] for r in kt_refs]

    def stacked(qx):
        return jnp.concatenate([jnp.where(m, qx, 0.0) for m in masks], axis=0).astype(BF16)

    def exp_scores(qx, b, which):
        qs = stacked(qx)
        s = [_dot(qs, r[b, which].reshape(A_QK, n).astype(BF16)) for r, n in zip(kt_refs, keys)]
        mx = functools.reduce(jnp.maximum, [jnp.max(x, axis=1, keepdims=True) for x in s])
        return [jnp.exp2(x - mx).astype(BF16) for x in s]

    outs = []
    for b in range(kt_refs[0].shape[0]):
        q = q_ref[b * tq:(b + 1) * tq, :]
        e1 = exp_scores(q[:, 0:A_QK], b, 0)
        e2 = exp_scores(q[:, A_QK:2 * A_QK], b, 1)
        heads = []
        for h in range(N_HEADS_A):
            hrows = slice(h * tq, (h + 1) * tq)
            n1 = n2 = None
            for s, (v_ref, n) in enumerate(zip(v_refs, keys)):
                v_h = v_ref[pl.ds(b * n * N_HEADS_A + h, n, stride=N_HEADS_A), :]
                v_ext = jnp.concatenate([v_h.astype(BF16),
                                         jnp.ones((n, LANES), BF16)], axis=1)
                p1 = _dot(e1[s][hrows], v_ext)
                p2 = _dot(e2[s][hrows], v_ext)
                n1 = p1 if n1 is None else n1 + p1
                n2 = p2 if n2 is None else n2 + p2
            heads.append(n1[:, 0:LANES] / n1[:, LANES:] - lam * (n2[:, 0:LANES] / n2[:, LANES:]))
        o = jnp.concatenate(heads, axis=0)
        ms = jnp.mean(o * o, axis=1, keepdims=True)
        o = (o * lax.rsqrt(ms + 1e-5) * sub_ref[...]) * (1.0 - lam_init)
        outs.append(jnp.concatenate([o[h * tq:(h + 1) * tq] for h in range(N_HEADS_A)], axis=1))
    o_ref[...] = jnp.concatenate(outs, axis=0).astype(o_ref.dtype)


def _attention(q, kts, vs, lam_vecs, subln_w, q_len, tq, seqs_per_step, lam_init):
    n_seq = kts[0].shape[0]
    nb = seqs_per_step
    tiles = q_len // tq
    assert tiles == 1 or nb == 1
    qspec = pl.BlockSpec((nb * tq, 512), lambda b, i: (b * tiles + i, 0))
    ktspecs = [pl.BlockSpec((nb,) + kt.shape[1:], lambda b, i: (b, 0, 0, 0, 0)) for kt in kts]
    vspecs = [pl.BlockSpec((nb * kt.shape[-1] * N_HEADS_A, LANES), lambda b, i: (b, 0)) for kt in kts]
    return pl.pallas_call(
        functools.partial(_attn_kernel, n_seg=len(kts), tq=tq, lam_init=lam_init),
        grid=(n_seq // nb, tiles),
        in_specs=[qspec] + ktspecs + vspecs + [pl.BlockSpec((4, HEAD_DIM_A), lambda b, i: (0, 0)),
                                             pl.BlockSpec((1, LANES), lambda b, i: (0, 0))],
        out_specs=qspec,
        out_shape=jax.ShapeDtypeStruct(q.shape, BF16),
        compiler_params=_params(2),
        name="diff_attention",
    )(q, *kts, *vs, lam_vecs, subln_w.reshape(1, LANES))


def _chunk_loop(n, body, unroll=1):
    if n <= 2:
        for c in range(n):
            body(c)
    else:
        def step(c, carry):
            body(c)
            return carry
        lax.fori_loop(0, n, step, 0, unroll=unroll)


def _ssd_kernel(*refs, seq_len, n_seq, has_h0, n_cast):
    n_in = 7 if has_h0 else 6
    n_out = 1 if has_h0 else 2
    ins, cast_in = refs[:n_in], refs[n_in:n_in + n_cast]
    outs = refs[n_in + n_cast:n_in + n_cast + n_out]
    cast_out = refs[n_in + n_cast + n_out:n_in + 2 * n_cast + n_out]
    ysc, s_sc, cd_sc, e_sc, htf, htb = refs[n_in + 2 * n_cast + n_out:]
    if has_h0:
        xact, dtv, z_ref, h0_ref, arow_ref, dsk_ref, nw_ref = ins
        (y_ref,), hfin_ref = outs, None
    else:
        xact, dtv, z_ref, arow_ref, dsk_ref, nw_ref = ins
        (y_ref, hfin_ref), h0_ref = outs, None
    _ride_along_cast(cast_in, cast_out)
    nc = seq_len // CHUNK
    n_pairs = N_HEADS_B // 2
    a_row = -jnp.exp(arow_ref[...]) * LOG2E

    row_i = lax.broadcasted_iota(jnp.int32, (CHUNK, CHUNK), 0)
    col_i = lax.broadcasted_iota(jnp.int32, (CHUNK, CHUNK), 1)
    lower = col_i <= row_i
    upper = col_i >= row_i
    tri = (jnp.where(lower, 1.0, 0.0).astype(BF16), jnp.where(upper, 1.0, 0.0).astype(BF16))
    lo_half = col_i < HEAD_DIM_B

    def split_pair(m):
        zero = jnp.zeros_like(m)
        return jnp.concatenate([jnp.where(lo_half, m, zero), jnp.where(lo_half, zero, m)], axis=0).astype(BF16)

    def decay_terms(c):
        rows = pl.ds(pl.multiple_of(c * CHUNK, CHUNK), CHUNK)
        dtc = dtv[rows, :]
        dtt = dtc.T[0:16, :]
        per_dir = []
        for d in range(2):
            col = _dot_exact_rhs(tri[d], dtc * a_row)
            rowm = col.T[0:16, :]
            far = CHUNK - 1 if d == 0 else 0
            tot = jnp.broadcast_to(rowm[:, far:far + 1], (16, CHUNK))
            dte = jnp.exp2(tot - rowm) * dtt
            per_dir.append((col, rowm - jnp.log2(dtt), dte, jnp.exp2(tot)))
        return per_dir

    def local_phase(c, per_dir=None):
        rows = pl.ds(pl.multiple_of(c * CHUNK, CHUNK), CHUNK)
        if per_dir is None:
            per_dir = decay_terms(c)
        for g in range(2):
            bg = xact[rows, 512 + g * LANES:512 + (g + 1) * LANES]
            cg = xact[rows, 768 + g * LANES:768 + (g + 1) * LANES]
            cbm = _dot_nt(cg.astype(BF16), bg.astype(BF16))
            bgt = bg.T
            for pq in range(2):
                pi = g * 2 + pq
                lanes = slice(pi * LANES, (pi + 1) * LANES)
                xrhs = split_pair(xact[rows, lanes])
                y_pair = None
                for d in range(2):
                    col, row_dt, dte, cdec = per_dir[d]
                    causal = lower if d == 0 else upper
                    m_parts, col_parts, bw_parts, cd_parts = [], [], [], []
                    for hh in (2 * pi, 2 * pi + 1):
                        hd = 8 * d + hh
                        colb = jnp.sum(jnp.where(col_i == hd, col, 0.0), axis=1, keepdims=True)
                        m_parts.append(cbm * jnp.exp2(jnp.where(causal, colb - row_dt[hd:hd + 1, :], NEG_BIG)))
                        col_parts.append(colb)
                        bw_parts.append(bgt * dte[hd:hd + 1, :])
                        cd_parts.append(cdec[hd:hd + 1, :])
                    yd = _dot(jnp.concatenate(m_parts, axis=1).astype(BF16), xrhs)
                    y_pair = yd if y_pair is None else y_pair + yd
                    s_sc[c, d, :, lanes] = _dot(jnp.concatenate(bw_parts, axis=1).astype(BF16), xrhs)
                    e_sc[c, d, :, lanes] = jnp.exp2(jnp.where(lo_half, col_parts[0], col_parts[1]))
                    cd_sc[c, d, :, lanes] = jnp.broadcast_to(
                        jnp.where(lo_half[0:1, :], cd_parts[0], cd_parts[1]), (8, LANES))
                ysc[rows, lanes] = y_pair

    if n_seq * nc <= SSD_UNROLL_CHUNKS:
        terms = [decay_terms(c) for c in range(n_seq * nc)]
        for c in range(n_seq * nc):
            local_phase(c, terms[c])
    else:
        _chunk_loop(n_seq * nc, local_phase, unroll=2)

    def scan_sequence(sq):
        for d, ht in ((0, htf), (1, htb)):
            for qd in range(n_pairs):
                lanes = slice(qd * LANES, (qd + 1) * LANES)
                if has_h0:
                    ht[:, lanes] = h0_ref[sq, d, lanes, :].T
                else:
                    ht[:, lanes] = jnp.zeros((D_STATE, LANES), F32)

        def scan_phase(j):
            for d, ht in ((0, htf), (1, htb)):
                c = sq * nc + (j if d == 0 else nc - 1 - j)
                rows = pl.ds(pl.multiple_of(c * CHUNK, CHUNK), CHUNK)
                for pi in range(n_pairs):
                    lanes = slice(pi * LANES, (pi + 1) * LANES)
                    g = pi // 2
                    cg = xact[rows, 768 + g * LANES:768 + (g + 1) * LANES].astype(BF16)
                    hprev = ht[:, lanes]
                    ysc[rows, lanes] += _dot(cg, hprev.astype(BF16)) * e_sc[c, d, :, lanes]
                    ht[:, lanes] = hprev * cd_sc[c, d, 0:1, lanes] + s_sc[c, d, :, lanes]

        _chunk_loop(nc, scan_phase)
        if hfin_ref is not None:
            for d, ht in ((0, htf), (1, htb)):
                for qd in range(n_pairs):
                    lanes = slice(qd * LANES, (qd + 1) * LANES)
                    hfin_ref[sq, d, lanes, :] = ht[:, lanes].T

    if n_seq == 1:
        scan_sequence(0)
    else:
        def seq_step(sq, carry):
            scan_sequence(sq)
            return carry
        lax.fori_loop(0, n_seq, seq_step, 0)

    def finish(c):
        rows = pl.ds(pl.multiple_of(c * CHUNK, CHUNK), CHUNK)
        y = ysc[rows, :] + dsk_ref[...] * xact[rows, 0:D_INNER_B]
        y = y * _silu(z_ref[rows, :])
        ms = jnp.mean(y * y, axis=1, keepdims=True)
        y_ref[rows, :] = (y * lax.rsqrt(ms + 1e-5) * nw_ref[...]).astype(y_ref.dtype)

    _chunk_loop(n_seq * nc, finish)


def _ssd(xbc, dt, z, h0, consts, n_batch, seq_len, seqs_per_step, ride_along=()):
    nb = seqs_per_step
    L = nb * seq_len
    nc = L // CHUNK
    row = lambda n: pl.BlockSpec((L, n), lambda b: (b, 0))
    state_spec = pl.BlockSpec((nb, 2, D_INNER_B, D_STATE), lambda b: (b, 0, 0, 0))
    has_h0 = h0 is not None
    in_specs = [row(CONV_DIM_B), row(LANES), row(D_INNER_B)]
    args = [xbc, dt, z]
    if has_h0:
        in_specs.append(state_spec)
        args.append(h0)
    for cst in consts:
        in_specs.append(pl.BlockSpec(cst.shape, lambda b: (0, 0)))
        args.append(cst)
    out_specs = [row(D_INNER_B)]
    out_shape = [jax.ShapeDtypeStruct((n_batch * seq_len, D_INNER_B), BF16)]
    if not has_h0:
        out_specs.append(state_spec)
        out_shape.append(jax.ShapeDtypeStruct((n_batch, 2, D_INNER_B, D_STATE), F32))
    scratch = [pltpu.VMEM((L, D_INNER_B), F32),
               pltpu.VMEM((nc, 2, D_STATE, D_INNER_B), F32),
               pltpu.VMEM((nc, 2, 8, D_INNER_B), F32),
               pltpu.VMEM((nc, 2, CHUNK, D_INNER_B), F32),
               pltpu.VMEM((D_STATE, D_INNER_B), F32),
               pltpu.VMEM((D_STATE, D_INNER_B), F32)]
    cast_specs, cast_shapes = _ride_along_specs(ride_along, n_batch // nb)
    n_main_out = len(out_specs)
    outs = pl.pallas_call(
        functools.partial(_ssd_kernel, seq_len=seq_len, n_seq=nb, has_h0=has_h0, n_cast=len(ride_along)),
        grid=(n_batch // nb,),
        in_specs=in_specs + cast_specs,
        out_specs=out_specs + cast_specs,
        out_shape=out_shape + cast_shapes,
        scratch_shapes=scratch,
        compiler_params=_params(),
        name="bidir_ssd",
    )(*args, *ride_along)
    return outs[:n_main_out], outs[n_main_out:]


def _tail_kernel(*refs, n_in, n_ctx_tiles):
    ctx_a, lat_a = refs[:n_in], refs[n_in:2 * n_in]
    (xc_ref, xl_ref, mod_ref, wout_ref, wup_ref, wdown_ref, g1_ref, b1_ref, g2_ref, b2_ref,
     oc_ref, ol_ref) = refs[2 * n_in:]
    step = pl.program_id(0)

    tile = functools.partial(_mlp_tile, mod_ref=mod_ref, wout_ref=wout_ref, g1_ref=g1_ref, b1_ref=b1_ref,
                             wup_ref=wup_ref, wdown_ref=wdown_ref, g2_ref=g2_ref, b2_ref=b2_ref)

    @pl.when(step < n_ctx_tiles)
    def _():
        tile(ctx_a, xc_ref, oc_ref)

    @pl.when(step >= n_ctx_tiles)
    def _():
        tile(lat_a, xl_ref, ol_ref)


def _mlp_tile(a_refs, x_ref, o_ref, *, mod_ref, wout_ref, g1_ref, b1_ref, wup_ref, wdown_ref, g2_ref, b2_ref):
    m = mod_ref[0]
    n_chunks = D_FF // FF_CHUNK

    def out_proj(rows):
        d = None
        off = 0
        for a_ref in a_refs:
            n = a_ref.shape[1]
            part = _dot(a_ref[rows, :].astype(BF16), wout_ref[off:off + n, :])
            d = part if d is None else d + part
            off += n
        return d

    def head(rows, d):
        x1 = _layer_norm(ALPHA * x_ref[rows, :] + m[2:3] * d, g1_ref[...], b1_ref[...])
        return x1, (x1 * (1.0 + m[4:5]) + m[3:4]).astype(BF16)

    def up(h, c):
        return _dot(h, wup_ref[:, c * FF_CHUNK:(c + 1) * FF_CHUNK])

    def down(u, acc, c):
        u = jnp.maximum(u, 0.0)
        part = _dot((u * u).astype(BF16), wdown_ref[c * FF_CHUNK:(c + 1) * FF_CHUNK, :])
        return part if acc is None else acc + part

    def finish(rows, x1, acc):
        o_ref[rows, :] = _layer_norm(ALPHA * x1 + m[5:6] * acc, g2_ref[...], b2_ref[...])

    assert sum(TAIL_SPLIT) == x_ref.shape[0]
    k = len(TAIL_SPLIT)
    starts = [sum(TAIL_SPLIT[:t]) for t in range(k)]
    rows = [slice(s, s + n) for s, n in zip(starts, TAIL_SPLIT)]
    projected = [out_proj(r) for r in rows]
    heads = [None] * k
    heads[0] = head(rows[0], projected[0])
    u_next = up(heads[0][1], 0)
    done = None
    for t in range(k):
        if t + 1 < k:
            heads[t + 1] = head(rows[t + 1], projected[t + 1])
        acc = None
        for c in range(n_chunks):
            if c + 1 < n_chunks:
                nxt = up(heads[t][1], c + 1)
            else:
                nxt = up(heads[t + 1][1], 0) if t + 1 < k else None
            u_cur, u_next = u_next, nxt
            acc = down(u_cur, acc, c)
            if c == 0 and done is not None:
                finish(*done)
        done = (rows[t], heads[t][0], acc)
    finish(*done)


def _tail(a_ctx, a_lat, x_ctx, x_lat, mod, lat_seq_len, w_out, ln1_g, ln1_b, w_up, w_down, ln2_g, ln2_b):
    tm = TOKEN_TILE
    n_ctx = x_ctx.shape[0] // tm
    n_lat = x_lat.shape[0] // tm
    tiles_per_lat_seq = lat_seq_len // tm
    ctx_row = lambda n: pl.BlockSpec((tm, n), lambda i: (jnp.minimum(i, n_ctx - 1), 0))
    lat_row = lambda n: pl.BlockSpec((tm, n), lambda i: (jnp.maximum(i - n_ctx, 0), 0))
    lat_out = pl.BlockSpec((tm, D_MODEL), lambda i: (jnp.maximum(i - n_ctx, 0), 0))
    mod_spec = _mod_spec(lambda i: jnp.where(i < n_ctx, 0, 1 + (i - n_ctx) // tiles_per_lat_seq))
    vec = lambda v: v.reshape(1, D_MODEL)
    consts = [w_out, w_up, w_down, vec(ln1_g), vec(ln1_b), vec(ln2_g), vec(ln2_b)]
    return pl.pallas_call(
        functools.partial(_tail_kernel, n_in=len(a_ctx), n_ctx_tiles=n_ctx),
        grid=(n_ctx + n_lat,),
        in_specs=[ctx_row(a.shape[1]) for a in a_ctx] + [lat_row(a.shape[1]) for a in a_lat]
                 + [ctx_row(D_MODEL), lat_row(D_MODEL), mod_spec] + [_const_spec(v.shape) for v in consts],
        out_specs=[ctx_row(D_MODEL), lat_out],
        out_shape=[jax.ShapeDtypeStruct(x_ctx.shape, F32), jax.ShapeDtypeStruct(x_lat.shape, F32)],
        compiler_params=_params(),
        name="outproj_mlp",
    )(*a_ctx, *a_lat, x_ctx, x_lat, mod, *consts)


def _pool_kernel(x_ref, mod_ref, win_ref, pw_ref, ps_ref, o_ref, *, seq_len):
    L = seq_len
    m = mod_ref[0]
    t_i = lax.broadcasted_iota(jnp.int32, (L, L), 0)
    s_i = lax.broadcasted_iota(jnp.int32, (L, L), 1)
    t_g = lax.broadcasted_iota(jnp.int32, (L, POOL_GROUP_DIM), 0)
    bands, scales = [], []
    for w in POOL_WINDOWS:
        lo, hi = w // 2, w - w // 2
        bands.append(jnp.where((s_i >= t_i - lo) & (s_i < t_i + hi), 1.0 / w, 0.0).astype(BF16))
        scales.append(w / (jnp.minimum(t_g + hi, L) - jnp.maximum(t_g - lo, 0)).astype(F32))
    n_seq = x_ref.shape[0] // L
    seq_rows = [slice(s * L, (s + 1) * L) for s in range(n_seq)]
    hs = [(x_ref[r, :] * (1.0 + m[1:2]) + m[0:1]).astype(BF16) for r in seq_rows]
    us = [_dot(h, win_ref[...]) for h in hs]
    n_groups = len(POOL_WINDOWS)
    cols = [slice(g * POOL_GROUP_DIM, (g + 1) * POOL_GROUP_DIM) for g in range(n_groups)]

    def window_means(g):
        out = []
        for s in range(n_seq):
            ug = us[s][:, cols[g]]
            ug_hi = ug.astype(BF16)
            ug_lo = (ug - ug_hi.astype(F32)).astype(BF16)
            out.append((ug, _dot(bands[g], ug_hi) + _dot(bands[g], ug_lo)))
        return out

    def mix(g, means):
        return [_dot((wm * scales[g] - ug).astype(BF16), pw_ref[g]) * ps_ref[:, cols[g]] for ug, wm in means]

    mixed = []
    pending = window_means(0)
    for g in range(n_groups):
        nxt = window_means(g + 1) if g + 1 < n_groups else None
        mixed.append(mix(g, pending))
        pending = nxt
    for s in range(n_seq):
        o_ref[seq_rows[s], :] = jnp.concatenate([mixed[g][s] for g in range(n_groups)], axis=1).astype(o_ref.dtype)


def _pool_mixer(x, mod, row_of_tile, w_in, pool_w, pool_scale, seq_len, tm):
    t = x.shape[0]
    row = pl.BlockSpec((tm, D_MODEL), lambda i: (i, 0))
    consts = [w_in, pool_w, pool_scale.reshape(1, D_MODEL)]
    return pl.pallas_call(
        functools.partial(_pool_kernel, seq_len=seq_len),
        grid=(t // tm,),
        in_specs=[row, _mod_spec(row_of_tile)] + [_const_spec(cst.shape) for cst in consts],
        out_specs=row,
        out_shape=jax.ShapeDtypeStruct((t, D_MODEL), BF16),
        compiler_params=_params(),
        name="pool_mixer",
    )(x, mod, *consts)


def _rope_tables(rows):
    r, col = np.meshgrid(np.arange(rows), np.arange(GRID_W), indexing="ij")
    r = r.reshape(-1).astype(np.float64)
    col = col.reshape(-1).astype(np.float64)
    n_freq = HEAD_DIM_A // 4
    inv = ROPE_BASE ** (-np.arange(n_freq, dtype=np.float64) / n_freq)
    ang = np.concatenate([r[:, None] * inv, col[:, None] * inv], -1)
    reps = 512 // HEAD_DIM_A
    cosf = np.tile(np.concatenate([np.cos(ang), np.cos(ang)], -1), (1, reps)).astype(np.float32)
    sinf = np.tile(np.concatenate([-np.sin(ang), np.sin(ang)], -1), (1, reps)).astype(np.float32)
    return jnp.asarray(cosf), jnp.asarray(sinf)


def kernel(x_prompt, x_sample, cache_k_l0, cache_v_l0, state_ssm_l0, c, c_ctx, w_mod_l0, b_mod_l0, w_in_l0, conv_w_l0, conv_b_l0, a_log_l0, dt_bias_l0, d_skip_l0, ssm_norm_w_l0, lam_q1_l0, lam_k1_l0, lam_q2_l0, lam_k2_l0, subln_w_l0, w_out_l0, ln1_g_l0, ln1_b_l0, w_up_l0, w_down_l0, ln2_g_l0, ln2_b_l0, w_mod_l1, b_mod_l1, w_in_l1, pool_w_l1, pool_scale_l1, w_out_l1, ln1_g_l1, ln1_b_l1, w_up_l1, w_down_l1, ln2_g_l1, ln2_b_l1):
    n_ctx, l_ctx, _ = x_prompt.shape
    n_lat, l_lat, _ = x_sample.shape
    past = cache_k_l0.shape[1]
    xc = x_prompt.reshape(n_ctx * l_ctx, D_MODEL)
    xl = x_sample.reshape(n_lat * l_lat, D_MODEL)

    cond8 = jnp.concatenate([c_ctx[None, :], c, jnp.zeros((8 - 1 - n_lat, D_MODEL), F32)], axis=0)
    ctx_row = lambda i: 0

    mod0, mod1 = _modulation(cond8, w_mod_l0, b_mod_l0, w_mod_l1, b_mod_l1)
    n_main = 4 * A_QK + A_V + D_INNER_B + CONV_DIM_B
    w_main = w_in_l0.astype(BF16)
    w_dt = jnp.pad(w_in_l0[:, n_main:], ((0, 0), (0, LANES - 2 * N_HEADS_B))).astype(BF16)
    lane_pad = lambda v: jnp.pad(v.reshape(1, -1), ((0, 0), (0, LANES - 2 * N_HEADS_B)))
    conv_consts = (conv_w_l0, conv_b_l0.reshape(1, CONV_DIM_B), lane_pad(dt_bias_l0))
    ssd_consts = [lane_pad(a_log_l0),
                  jnp.repeat(d_skip_l0, HEAD_DIM_B).reshape(1, D_INNER_B),
                  ssm_norm_w_l0.reshape(1, D_INNER_B)]
    lam_vecs = jnp.stack([lam_q1_l0, lam_k1_l0, lam_q2_l0, lam_k2_l0], axis=0)
    lam_init = 0.8 - 0.6 * math.exp(-0.3 * 0)

    (qc, kc, vc, zc, xbc_c, dt_c), (w_out0, w_up0, w_down0, w_out1, w_up1, w_down1, w_in1, pool_w) = _inproj0(
        xc, mod0, ctx_row, w_main, w_dt, *conv_consts, l_ctx, TOKEN_TILE,
        ride_along=(w_out_l0, w_up_l0, w_down_l0, w_out_l1, w_up_l1, w_down_l1, w_in_l1,
                    pool_w_l1.reshape(-1, POOL_GROUP_DIM)))
    pool_w = pool_w.reshape(pool_w_l1.shape)
    oa_c = _attention(qc, [kc], [vc], lam_vecs, subln_w_l0, l_ctx, l_ctx, CTX_SEQS_PER_STEP, lam_init)
    (y_c, new_ssm), _ = _ssd(xbc_c, dt_c, zc, None, ssd_consts, n_ctx, l_ctx, SSD_CTX_SEQS_PER_STEP)

    (ql, kl, vl, zl, xbc_l, dt_l), _ = _inproj0(xl, mod0, lambda i: 1 + i, w_main, w_dt, *conv_consts, l_lat, l_lat,
                                                rope_tables=_rope_tables(l_lat // GRID_W))
    ck = jnp.transpose(cache_k_l0, (0, 2, 3, 4, 1))
    cv = cache_v_l0.reshape(n_lat * past * N_HEADS_A, 2 * HEAD_DIM_A)
    oa_l = _attention(ql, [ck, kl], [cv, vl], lam_vecs, subln_w_l0, l_lat, ATTN_Q_TILE, 1, lam_init)
    h0 = state_ssm_l0.reshape(n_lat, 2, D_INNER_B, D_STATE)
    ((y_l,), _) = _ssd(xbc_l, dt_l, zl, h0, ssd_consts, n_lat, l_lat, 1)
    tail0 = (w_out0, ln1_g_l0, ln1_b_l0, w_up0, w_down0, ln2_g_l0, ln2_b_l0)
    xc, xl = _tail([oa_c, y_c], [oa_l, y_l], xc, xl, mod0, l_lat, *tail0)

    tail1 = (w_out1, ln1_g_l1, ln1_b_l1, w_up1, w_down1, ln2_g_l1, ln2_b_l1)
    mix_c = _pool_mixer(xc, mod1, ctx_row, w_in1, pool_w, pool_scale_l1, l_ctx, TOKEN_TILE)
    mix_l = _pool_mixer(xl, mod1, lambda i: 1 + i, w_in1, pool_w, pool_scale_l1, l_lat, l_lat)
    xc, xl = _tail([mix_c], [mix_l], xc, xl, mod1, l_lat, *tail1)

    return (xc.reshape(n_ctx, l_ctx, D_MODEL), xl.reshape(n_lat, l_lat, D_MODEL),
            jnp.transpose(kc, (0, 4, 1, 2, 3)),
            vc.reshape(n_ctx, l_ctx, N_HEADS_A, 2 * HEAD_DIM_A),
            new_ssm.reshape(n_ctx, 2, N_HEADS_B, HEAD_DIM_B, D_STATE))
```

```python
import functools
import math

import jax
import jax.numpy as jnp
import numpy as np
from jax import lax
from jax.experimental import pallas as pl
from jax.experimental.pallas import tpu as pltpu

F32 = jnp.float32
BF16 = jnp.bfloat16

D_MODEL = 1024
N_MOD = 6
N_HEADS_A = 4
HEAD_DIM_A = 64
A_QK = N_HEADS_A * HEAD_DIM_A
A_V = 2 * A_QK
D_INNER_B = 512
HEAD_DIM_B = 64
N_HEADS_B = 8
D_STATE = 128
D_CONV = 5
CONV_DIM_B = 1024
CHUNK = 128
GRID_W = 64
POOL_WINDOWS = (2, 4, 8, 16)
POOL_GROUP_DIM = 256
D_FF = 4096
ROPE_BASE = 10000.0
LN_EPS = 1e-5
DEPTH = 2
ALPHA = (2 * DEPTH) ** 0.25
NEG_BIG = -1e30
LOG2E = 1.4426950408889634

LANES = 128
TOKEN_TILE = 512
POOL_CTX_TILE = 1024
ATTN_Q_TILE = 256
CTX_SEQS_PER_STEP = 8
SSD_CTX_SEQS_PER_STEP = 1
SSD_UNROLL_CHUNKS = 4
FF_CHUNK = 1024
CONV_COL_BLOCK = 256
TAIL_SPLIT = (256, 256)
VMEM_LIMIT = 56 * 1024 * 1024


def _dot(a, b):
    return jnp.dot(a, b, preferred_element_type=F32)


def _dot_nt(a, b):
    return lax.dot_general(a, b, (((1,), (1,)), ((), ())), preferred_element_type=F32)


def _bf16_pieces(a):
    a1 = a.astype(BF16)
    r1 = a - a1.astype(F32)
    a2 = r1.astype(BF16)
    a3 = (r1 - a2.astype(F32)).astype(BF16)
    return a1, a2, a3


def _dot_exact_rhs(t01, a):
    a1, a2, a3 = _bf16_pieces(a)
    return _dot(t01, a1) + _dot(t01, a2) + _dot(t01, a3)


def _sigmoid(x):
    return 1.0 / (1.0 + jnp.exp(-x))


def _silu(x):
    return x * _sigmoid(x)


def _layer_norm(x, g, b):
    mu = jnp.mean(x, axis=-1, keepdims=True)
    xc = x - mu
    var = jnp.mean(xc * xc, axis=-1, keepdims=True)
    return xc * lax.rsqrt(var + LN_EPS) * g + b


def _const_spec(shape):
    nd = len(shape)
    return pl.BlockSpec(shape, lambda *_: (0,) * nd, pipeline_mode=pl.Buffered(1))


def _params(n_axes=1):
    return pltpu.CompilerParams(dimension_semantics=("arbitrary",) * n_axes,
                                vmem_limit_bytes=VMEM_LIMIT)


def _mod_kernel(cond_ref, w0_ref, b0_ref, w1_ref, b1_ref, o0_ref, o1_ref, *, n_first):
    s = _silu(cond_ref[...]).astype(BF16)
    j = pl.program_id(0)

    @pl.when(j < n_first)
    def _():
        o0_ref[...] = _dot(s, w0_ref[...].astype(BF16)) + b0_ref[...]

    @pl.when(j >= n_first)
    def _():
        o1_ref[...] = _dot(s, w1_ref[...].astype(BF16)) + b1_ref[...]


def _modulation(cond8, w_mod0, b_mod0, w_mod1, b_mod1):
    tn = 1024
    n = w_mod0.shape[1]
    steps = n // tn
    first = lambda j: (0, jnp.minimum(j, steps - 1))
    second = lambda j: (0, jnp.maximum(j - steps, 0))
    outs = pl.pallas_call(
        functools.partial(_mod_kernel, n_first=steps),
        grid=(2 * steps,),
        in_specs=[pl.BlockSpec((8, D_MODEL), lambda j: (0, 0)),
                  pl.BlockSpec((D_MODEL, tn), first), pl.BlockSpec((1, tn), first),
                  pl.BlockSpec((D_MODEL, tn), second), pl.BlockSpec((1, tn), second)],
        out_specs=[pl.BlockSpec((8, tn), first), pl.BlockSpec((8, tn), second)],
        out_shape=[jax.ShapeDtypeStruct((8, n), F32)] * 2,
        compiler_params=_params(),
        name="modulation",
    )(cond8, w_mod0, b_mod0.reshape(1, n), w_mod1, b_mod1.reshape(1, n))
    return [o.reshape(8, N_MOD, D_MODEL) for o in outs]


def _mod_spec(row_of_tile):
    return pl.BlockSpec((1, N_MOD, D_MODEL), lambda i: (row_of_tile(i), 0, 0))


def _rope(t, cosf, sinf):
    lane = lax.broadcasted_iota(jnp.int32, t.shape, 1)
    first = (lane & (HEAD_DIM_A - 1)) < HEAD_DIM_A // 2
    width = t.shape[1]
    swapped = jnp.where(first, pltpu.roll(t, width - HEAD_DIM_A // 2, 1),
                        pltpu.roll(t, HEAD_DIM_A // 2, 1))
    return t * cosf + swapped * sinf


def _conv_silu(xbc, cw_ref, cb_ref, cols, seq_len):
    assert xbc.shape[0] == seq_len
    half = D_CONV // 2
    edge = 8
    offsets = [j - half for j in range(D_CONV) if j != half]
    shifted = {o: pltpu.roll(xbc, (-o) % seq_len, 0) for o in offsets}

    def taps(lo, hi, masked):
        acc = cb_ref[:, cols] + cw_ref[half:half + 1, cols] * xbc[lo:hi]
        pos = lo + lax.broadcasted_iota(jnp.int32, (hi - lo, xbc.shape[1]), 0)
        for o in offsets:
            s = shifted[o][lo:hi]
            if masked:
                s = jnp.where((pos >= -o) if o < 0 else (pos < seq_len - o), s, 0.0)
            acc = acc + cw_ref[o + half:o + half + 1, cols] * s
        return acc

    acc = jnp.concatenate([taps(0, edge, True), taps(edge, seq_len - edge, False),
                           taps(seq_len - edge, seq_len, True)], axis=0)
    return _silu(acc)


def _softplus(t):
    return jnp.maximum(t, 0.0) + jnp.log(1.0 + jnp.exp(-jnp.abs(t)))


def _ride_along_specs(mats, n_steps):
    specs, shapes = [], []
    for w in mats:
        assert w.shape[0] % n_steps == 0
        specs.append(pl.BlockSpec((w.shape[0] // n_steps, w.shape[1]), lambda i: (i, 0)))
        shapes.append(jax.ShapeDtypeStruct(w.shape, BF16))
    return specs, shapes


def _ride_along_cast(src_refs, dst_refs):
    for src, dst in zip(src_refs, dst_refs):
        dst[...] = src[...].astype(BF16)


def _inproj0_kernel(*refs, rope, seq_len, n_cast):
    n_in = 9 if rope else 7
    ins, cast_in = refs[:n_in], refs[n_in:n_in + n_cast]
    (q_ref, k_ref, v_ref, z_ref, xbc_ref, dt_ref) = refs[n_in + n_cast:n_in + n_cast + 6]
    cast_out = refs[n_in + n_cast + 6:]
    if rope:
        x_ref, mod_ref, w_ref, wdt_ref, cw_ref, cb_ref, dtb_ref, cos_ref, sin_ref = ins
    else:
        x_ref, mod_ref, w_ref, wdt_ref, cw_ref, cb_ref, dtb_ref = ins
    _ride_along_cast(cast_in, cast_out)
    m = mod_ref[0]
    tasks = []
    for b in range(x_ref.shape[0] // seq_len):
        rows = slice(b * seq_len, (b + 1) * seq_len)
        h = (x_ref[rows, :] * (1.0 + m[1:2]) + m[0:1]).astype(BF16)

        for c0 in range(0, CONV_DIM_B, CONV_COL_BLOCK):
            cols = slice(c0, c0 + CONV_COL_BLOCK)

            def conv_mm(h=h, c0=c0):
                return _dot(h, w_ref[:, 2048 + c0:2048 + c0 + CONV_COL_BLOCK])

            def conv_vec(raw, rows=rows, cols=cols):
                xbc_ref[rows, cols] = _conv_silu(raw, cw_ref, cb_ref, cols, seq_len)

            tasks.append((conv_mm, conv_vec))

        def qk_mm(h=h):
            return _dot(h, wdt_ref[...]), _dot(h, w_ref[:, 0:512]), _dot(h, w_ref[:, 512:1024])

        def qk_vec(res, rows=rows, b=b):
            dt_raw, q, k = res
            dt_ref[rows, :] = _softplus(dt_raw + dtb_ref[...])
            if rope:
                q = _rope(q, cos_ref[...], sin_ref[...])
                k = _rope(k, cos_ref[...], sin_ref[...])
            q_ref[rows, :] = (q * (HEAD_DIM_A ** -0.5 * LOG2E)).astype(BF16)
            for blk in range(2 * A_QK // LANES):
                t = k[:, blk * LANES:(blk + 1) * LANES].T
                which, head = blk // 2, (blk % 2) * 2
                k_ref[b, which, head] = t[0:HEAD_DIM_A]
                k_ref[b, which, head + 1] = t[HEAD_DIM_A:2 * HEAD_DIM_A]

        def vz_mm(h=h):
            return _dot(h, w_ref[:, 1024:1536]), _dot(h, w_ref[:, 1536:2048])

        def vz_vec(res, rows=rows, b=b):
            vals, z = res
            for hh in range(N_HEADS_A):
                v_ref[pl.ds(b * seq_len * N_HEADS_A + hh, seq_len, stride=N_HEADS_A), :] = (
                    vals[:, hh * LANES:(hh + 1) * LANES])
            z_ref[rows, :] = z

        tasks += [(qk_mm, qk_vec), (vz_mm, vz_vec)]

    pending = None
    for mm, vec in tasks:
        res = mm()
        if pending is not None:
            pending[0](pending[1])
        pending = (vec, res)
    pending[0](pending[1])


def _inproj0(x, mod, row_of_tile, w_main, w_dt, conv_w, conv_b, dt_bias, seq_len, tm, rope_tables=None,
             ride_along=()):
    t = x.shape[0]
    assert tm % seq_len == 0
    row = lambda n: pl.BlockSpec((tm, n), lambda i: (i, 0))
    flat = lambda n: jax.ShapeDtypeStruct((t, n), F32)
    consts = [w_main, w_dt, conv_w, conv_b, dt_bias]
    in_specs = [row(D_MODEL), _mod_spec(row_of_tile)] + [_const_spec(cst.shape) for cst in consts]
    args = [x, mod] + consts
    if rope_tables is not None:
        assert tm == seq_len
        in_specs += [_const_spec(tab.shape) for tab in rope_tables]
        args += list(rope_tables)
    kt_dims = (2, N_HEADS_A, HEAD_DIM_A, seq_len)
    k_spec = pl.BlockSpec((tm // seq_len,) + kt_dims, lambda i: (i, 0, 0, 0, 0))
    k_shape = jax.ShapeDtypeStruct((t // seq_len,) + kt_dims, F32)
    cast_specs, cast_shapes = _ride_along_specs(ride_along, t // tm)
    outs = pl.pallas_call(
        functools.partial(_inproj0_kernel, rope=rope_tables is not None, seq_len=seq_len, n_cast=len(ride_along)),
        grid=(t // tm,),
        in_specs=in_specs + cast_specs,
        out_specs=[row(512), k_spec, pl.BlockSpec((tm * N_HEADS_A, LANES), lambda i: (i, 0)),
                   row(512), row(CONV_DIM_B), row(LANES)] + cast_specs,
        out_shape=[jax.ShapeDtypeStruct((t, 512), BF16), k_shape, jax.ShapeDtypeStruct((t * N_HEADS_A, LANES), F32),
                   flat(512), flat(CONV_DIM_B), flat(LANES)] + cast_shapes,
        compiler_params=_params(),
        name="inproj0",
    )(*args, *ride_along)
    return outs[:6], outs[6:]


def _lambda_full(lam_ref, lam_init):
    lv = lam_ref[...]
    return (jnp.exp(jnp.sum(lv[0:1] * lv[1:2], axis=1, keepdims=True))
            - jnp.exp(jnp.sum(lv[2:3] * lv[3:4], axis=1, keepdims=True)) + lam_init)


def _attn_kernel(*refs, n_seg, tq, lam_init):
    q_ref = refs[0]
    kt_refs = refs[1:1 + n_seg]
    v_refs = refs[1 + n_seg:1 + 2 * n_seg]
    lam_ref, sub_ref, o_ref = refs[1 + 2 * n_seg:]
    lam = _lambda_full(lam_ref, lam_init)
    lane = lax.broadcasted_iota(jnp.int32, (1, A_QK), 1)
    masks = [(lane >= h * HEAD_DIM_A) & (lane < (h + 1) * HEAD_DIM_A) for h in range(N_HEADS_A)]
    keys = [r.shape[-1] for r in kt_refs]

    def stacked(qx):
        return jnp.concatenate([jnp.where(m, qx, 0.0) for m in masks], axis=0).astype(BF16)

    def exp_scores(qx, b, which):
        qs = stacked(qx)
        s = [_dot(qs, r[b, which].reshape(A_QK, n).astype(BF16)) for r, n in zip(kt_refs, keys)]
        mx = functools.reduce(jnp.maximum, [jnp.max(x, axis=1, keepdims=True) for x in s])
        return [jnp.exp2(x - mx).astype(BF16) for x in s]

    outs = []
    for b in range(kt_refs[0].shape[0]):
        q = q_ref[b * tq:(b + 1) * tq, :]
        e1 = exp_scores(q[:, 0:A_QK], b, 0)
        e2 = exp_scores(q[:, A_QK:2 * A_QK], b, 1)
        heads = []
        for h in range(N_HEADS_A):
            hrows = slice(h * tq, (h + 1) * tq)
            n1 = n2 = None
            for s, (v_ref, n) in enumerate(zip(v_refs, keys)):
                v_h = v_ref[pl.ds(b * n * N_HEADS_A + h, n, stride=N_HEADS_A), :]
                v_ext = jnp.concatenate([v_h.astype(BF16),
                                         jnp.ones((n, LANES), BF16)], axis=1)
                p1 = _dot(e1[s][hrows], v_ext)
                p2 = _dot(e2[s][hrows], v_ext)
                n1 = p1 if n1 is None else n1 + p1
                n2 = p2 if n2 is None else n2 + p2
            heads.append(n1[:, 0:LANES] / n1[:, LANES:] - lam * (n2[:, 0:LANES] / n2[:, LANES:]))
        o = jnp.concatenate(heads, axis=0)
        ms = jnp.mean(o * o, axis=1, keepdims=True)
        o = (o * lax.rsqrt(ms + 1e-5) * sub_ref[...]) * (1.0 - lam_init)
        outs.append(jnp.concatenate([o[h * tq:(h + 1) * tq] for h in range(N_HEADS_A)], axis=1))
    o_ref[...] = jnp.concatenate(outs, axis=0).astype(o_ref.dtype)


def _attention(q, kts, vs, lam_vecs, subln_w, q_len, tq, seqs_per_step, lam_init):
    n_seq = kts[0].shape[0]
    nb = seqs_per_step
    tiles = q_len // tq
    assert tiles == 1 or nb == 1
    qspec = pl.BlockSpec((nb * tq, 512), lambda b, i: (b * tiles + i, 0))
    ktspecs = [pl.BlockSpec((nb,) + kt.shape[1:], lambda b, i: (b, 0, 0, 0, 0)) for kt in kts]
    vspecs = [pl.BlockSpec((nb * kt.shape[-1] * N_HEADS_A, LANES), lambda b, i: (b, 0)) for kt in kts]
    return pl.pallas_call(
        functools.partial(_attn_kernel, n_seg=len(kts), tq=tq, lam_init=lam_init),
        grid=(n_seq // nb, tiles),
        in_specs=[qspec] + ktspecs + vspecs + [pl.BlockSpec((4, HEAD_DIM_A), lambda b, i: (0, 0)),
                                             pl.BlockSpec((1, LANES), lambda b, i: (0, 0))],
        out_specs=qspec,
        out_shape=jax.ShapeDtypeStruct(q.shape, BF16),
        compiler_params=_params(2),
        name="diff_attention",
    )(q, *kts, *vs, lam_vecs, subln_w.reshape(1, LANES))


def _chunk_loop(n, body, unroll=1):
    if n <= 2:
        for c in range(n):
            body(c)
    else:
        def step(c, carry):
            body(c)
            return carry
        lax.fori_loop(0, n, step, 0, unroll=unroll)


def _ssd_kernel(*refs, seq_len, n_seq, has_h0, n_cast):
    n_in = 7 if has_h0 else 6
    n_out = 1 if has_h0 else 2
    ins, cast_in = refs[:n_in], refs[n_in:n_in + n_cast]
    outs = refs[n_in + n_cast:n_in + n_cast + n_out]
    cast_out = refs[n_in + n_cast + n_out:n_in + 2 * n_cast + n_out]
    ysc, s_sc, cd_sc, e_sc, htf, htb = refs[n_in + 2 * n_cast + n_out:]
    if has_h0:
        xact, dtv, z_ref, h0_ref, arow_ref, dsk_ref, nw_ref = ins
        (y_ref,), hfin_ref = outs, None
    else:
        xact, dtv, z_ref, arow_ref, dsk_ref, nw_ref = ins
        (y_ref, hfin_ref), h0_ref = outs, None
    _ride_along_cast(cast_in, cast_out)
    nc = seq_len // CHUNK
    n_pairs = N_HEADS_B // 2
    a_row = -jnp.exp(arow_ref[...]) * LOG2E

    row_i = lax.broadcasted_iota(jnp.int32, (CHUNK, CHUNK), 0)
    col_i = lax.broadcasted_iota(jnp.int32, (CHUNK, CHUNK), 1)
    lower = col_i <= row_i
    upper = col_i >= row_i
    tri = (jnp.where(lower, 1.0, 0.0).astype(BF16), jnp.where(upper, 1.0, 0.0).astype(BF16))
    lo_half = col_i < HEAD_DIM_B

    def split_pair(m):
        zero = jnp.zeros_like(m)
        return jnp.concatenate([jnp.where(lo_half, m, zero), jnp.where(lo_half, zero, m)], axis=0).astype(BF16)

    def decay_terms(c):
        rows = pl.ds(pl.multiple_of(c * CHUNK, CHUNK), CHUNK)
        dtc = dtv[rows, :]
        dtt = dtc.T[0:16, :]
        per_dir = []
        for d in range(2):
            col = _dot_exact_rhs(tri[d], dtc * a_row)
            rowm = col.T[0:16, :]
            far = CHUNK - 1 if d == 0 else 0
            tot = jnp.broadcast_to(rowm[:, far:far + 1], (16, CHUNK))
            dte = jnp.exp2(tot - rowm) * dtt
            per_dir.append((col, rowm - jnp.log2(dtt), dte, jnp.exp2(tot)))
        return per_dir

    def local_phase(c, per_dir=None):
        rows = pl.ds(pl.multiple_of(c * CHUNK, CHUNK), CHUNK)
        if per_dir is None:
            per_dir = decay_terms(c)
        for g in range(2):
            bg = xact[rows, 512 + g * LANES:512 + (g + 1) * LANES]
            cg = xact[rows, 768 + g * LANES:768 + (g + 1) * LANES]
            cbm = _dot_nt(cg.astype(BF16), bg.astype(BF16))
            bgt = bg.T
            for pq in range(2):
                pi = g * 2 + pq
                lanes = slice(pi * LANES, (pi + 1) * LANES)
                xrhs = split_pair(xact[rows, lanes])
                y_pair = None
                for d in range(2):
                    col, row_dt, dte, cdec = per_dir[d]
                    causal = lower if d == 0 else upper
                    m_parts, col_parts, bw_parts, cd_parts = [], [], [], []
                    for hh in (2 * pi, 2 * pi + 1):
                        hd = 8 * d + hh
                        colb = jnp.sum(jnp.where(col_i == hd, col, 0.0), axis=1, keepdims=True)
                        m_parts.append(cbm * jnp.exp2(jnp.where(causal, colb - row_dt[hd:hd + 1, :], NEG_BIG)))
                        col_parts.append(colb)
                        bw_parts.append(bgt * dte[hd:hd + 1, :])
                        cd_parts.append(cdec[hd:hd + 1, :])
                    yd = _dot(jnp.concatenate(m_parts, axis=1).astype(BF16), xrhs)
                    y_pair = yd if y_pair is None else y_pair + yd
                    s_sc[c, d, :, lanes] = _dot(jnp.concatenate(bw_parts, axis=1).astype(BF16), xrhs)
                    e_sc[c, d, :, lanes] = jnp.exp2(jnp.where(lo_half, col_parts[0], col_parts[1]))
                    cd_sc[c, d, :, lanes] = jnp.broadcast_to(
                        jnp.where(lo_half[0:1, :], cd_parts[0], cd_parts[1]), (8, LANES))
                ysc[rows, lanes] = y_pair

    if n_seq * nc <= SSD_UNROLL_CHUNKS:
        terms = [decay_terms(c) for c in range(n_seq * nc)]
        for c in range(n_seq * nc):
            local_phase(c, terms[c])
    else:
        _chunk_loop(n_seq * nc, local_phase, unroll=2)

    def scan_sequence(sq):
        for d, ht in ((0, htf), (1, htb)):
            for qd in range(n_pairs):
                lanes = slice(qd * LANES, (qd + 1) * LANES)
                if has_h0:
                    ht[:, lanes] = h0_ref[sq, d, lanes, :].T
                else:
                    ht[:, lanes] = jnp.zeros((D_STATE, LANES), F32)

        def scan_phase(j):
            for d, ht in ((0, htf), (1, htb)):
                c = sq * nc + (j if d == 0 else nc - 1 - j)
                rows = pl.ds(pl.multiple_of(c * CHUNK, CHUNK), CHUNK)
                for pi in range(n_pairs):
                    lanes = slice(pi * LANES, (pi + 1) * LANES)
                    g = pi // 2
                    cg = xact[rows, 768 + g * LANES:768 + (g + 1) * LANES].astype(BF16)
                    hprev = ht[:, lanes]
                    ysc[rows, lanes] += _dot(cg, hprev.astype(BF16)) * e_sc[c, d, :, lanes]
                    ht[:, lanes] = hprev * cd_sc[c, d, 0:1, lanes] + s_sc[c, d, :, lanes]

        _chunk_loop(nc, scan_phase)
        if hfin_ref is not None:
            for d, ht in ((0, htf), (1, htb)):
                for qd in range(n_pairs):
                    lanes = slice(qd * LANES, (qd + 1) * LANES)
                    hfin_ref[sq, d, lanes, :] = ht[:, lanes].T

    if n_seq == 1:
        scan_sequence(0)
    else:
        def seq_step(sq, carry):
            scan_sequence(sq)
            return carry
        lax.fori_loop(0, n_seq, seq_step, 0)

    def finish(c):
        rows = pl.ds(pl.multiple_of(c * CHUNK, CHUNK), CHUNK)
        y = ysc[rows, :] + dsk_ref[...] * xact[rows, 0:D_INNER_B]
        y = y * _silu(z_ref[rows, :])
        ms = jnp.mean(y * y, axis=1, keepdims=True)
        y_ref[rows, :] = (y * lax.rsqrt(ms + 1e-5) * nw_ref[...]).astype(y_ref.dtype)

    _chunk_loop(n_seq * nc, finish)


def _ssd(xbc, dt, z, h0, consts, n_batch, seq_len, seqs_per_step, ride_along=()):
    nb = seqs_per_step
    L = nb * seq_len
    nc = L // CHUNK
    row = lambda n: pl.BlockSpec((L, n), lambda b: (b, 0))
    state_spec = pl.BlockSpec((nb, 2, D_INNER_B, D_STATE), lambda b: (b, 0, 0, 0))
    has_h0 = h0 is not None
    in_specs = [row(CONV_DIM_B), row(LANES), row(D_INNER_B)]
    args = [xbc, dt, z]
    if has_h0:
        in_specs.append(state_spec)
        args.append(h0)
    for cst in consts:
        in_specs.append(pl.BlockSpec(cst.shape, lambda b: (0, 0)))
        args.append(cst)
    out_specs = [row(D_INNER_B)]
    out_shape = [jax.ShapeDtypeStruct((n_batch * seq_len, D_INNER_B), BF16)]
    if not has_h0:
        out_specs.append(state_spec)
        out_shape.append(jax.ShapeDtypeStruct((n_batch, 2, D_INNER_B, D_STATE), F32))
    scratch = [pltpu.VMEM((L, D_INNER_B), F32),
               pltpu.VMEM((nc, 2, D_STATE, D_INNER_B), F32),
               pltpu.VMEM((nc, 2, 8, D_INNER_B), F32),
               pltpu.VMEM((nc, 2, CHUNK, D_INNER_B), F32),
               pltpu.VMEM((D_STATE, D_INNER_B), F32),
               pltpu.VMEM((D_STATE, D_INNER_B), F32)]
    cast_specs, cast_shapes = _ride_along_specs(ride_along, n_batch // nb)
    n_main_out = len(out_specs)
    outs = pl.pallas_call(
        functools.partial(_ssd_kernel, seq_len=seq_len, n_seq=nb, has_h0=has_h0, n_cast=len(ride_along)),
        grid=(n_batch // nb,),
        in_specs=in_specs + cast_specs,
        out_specs=out_specs + cast_specs,
        out_shape=out_shape + cast_shapes,
        scratch_shapes=scratch,
        compiler_params=_params(),
        name="bidir_ssd",
    )(*args, *ride_along)
    return outs[:n_main_out], outs[n_main_out:]


def _tail_kernel(*refs, n_in, n_ctx_tiles):
    ctx_a, lat_a = refs[:n_in], refs[n_in:2 * n_in]
    (xc_ref, xl_ref, mod_ref, wout_ref, wup_ref, wdown_ref, g1_ref, b1_ref, g2_ref, b2_ref,
     oc_ref, ol_ref) = refs[2 * n_in:]
    step = pl.program_id(0)

    tile = functools.partial(_mlp_tile, mod_ref=mod_ref, wout_ref=wout_ref, g1_ref=g1_ref, b1_ref=b1_ref,
                             wup_ref=wup_ref, wdown_ref=wdown_ref, g2_ref=g2_ref, b2_ref=b2_ref)

    @pl.when(step < n_ctx_tiles)
    def _():
        tile(ctx_a, xc_ref, oc_ref)

    @pl.when(step >= n_ctx_tiles)
    def _():
        tile(lat_a, xl_ref, ol_ref)


def _mlp_tile(a_refs, x_ref, o_ref, *, mod_ref, wout_ref, g1_ref, b1_ref, wup_ref, wdown_ref, g2_ref, b2_ref):
    m = mod_ref[0]
    n_chunks = D_FF // FF_CHUNK

    def out_proj(rows):
        d = None
        off = 0
        for a_ref in a_refs:
            n = a_ref.shape[1]
            part = _dot(a_ref[rows, :].astype(BF16), wout_ref[off:off + n, :])
            d = part if d is None else d + part
            off += n
        return d

    def head(rows, d):
        x1 = _layer_norm(ALPHA * x_ref[rows, :] + m[2:3] * d, g1_ref[...], b1_ref[...])
        return x1, (x1 * (1.0 + m[4:5]) + m[3:4]).astype(BF16)

    def up(h, c):
        return _dot(h, wup_ref[:, c * FF_CHUNK:(c + 1) * FF_CHUNK])

    def down(u, acc, c):
        u = jnp.maximum(u, 0.0)
        part = _dot((u * u).astype(BF16), wdown_ref[c * FF_CHUNK:(c + 1) * FF_CHUNK, :])
        return part if acc is None else acc + part

    def finish(rows, x1, acc):
        o_ref[rows, :] = _layer_norm(ALPHA * x1 + m[5:6] * acc, g2_ref[...], b2_ref[...])

    assert sum(TAIL_SPLIT) == x_ref.shape[0]
    k = len(TAIL_SPLIT)
    starts = [sum(TAIL_SPLIT[:t]) for t in range(k)]
    rows = [slice(s, s + n) for s, n in zip(starts, TAIL_SPLIT)]
    projected = [out_proj(r) for r in rows]
    heads = [None] * k
    heads[0] = head(rows[0], projected[0])
    u_next = up(heads[0][1], 0)
    done = None
    for t in range(k):
        if t + 1 < k:
            heads[t + 1] = head(rows[t + 1], projected[t + 1])
        acc = None
        for c in range(n_chunks):
            if c + 1 < n_chunks:
                nxt = up(heads[t][1], c + 1)
            else:
                nxt = up(heads[t + 1][1], 0) if t + 1 < k else None
            u_cur, u_next = u_next, nxt
            acc = down(u_cur, acc, c)
            if c == 0 and done is not None:
                finish(*done)
        done = (rows[t], heads[t][0], acc)
    finish(*done)


def _tail(a_ctx, a_lat, x_ctx, x_lat, mod, lat_seq_len, w_out, ln1_g, ln1_b, w_up, w_down, ln2_g, ln2_b):
    tm = TOKEN_TILE
    n_ctx = x_ctx.shape[0] // tm
    n_lat = x_lat.shape[0] // tm
    tiles_per_lat_seq = lat_seq_len // tm
    ctx_row = lambda n: pl.BlockSpec((tm, n), lambda i: (jnp.minimum(i, n_ctx - 1), 0))
    lat_row = lambda n: pl.BlockSpec((tm, n), lambda i: (jnp.maximum(i - n_ctx, 0), 0))
    lat_out = pl.BlockSpec((tm, D_MODEL), lambda i: (jnp.maximum(i - n_ctx, 0), 0))
    mod_spec = _mod_spec(lambda i: jnp.where(i < n_ctx, 0, 1 + (i - n_ctx) // tiles_per_lat_seq))
    vec = lambda v: v.reshape(1, D_MODEL)
    consts = [w_out, w_up, w_down, vec(ln1_g), vec(ln1_b), vec(ln2_g), vec(ln2_b)]
    return pl.pallas_call(
        functools.partial(_tail_kernel, n_in=len(a_ctx), n_ctx_tiles=n_ctx),
        grid=(n_ctx + n_lat,),
        in_specs=[ctx_row(a.shape[1]) for a in a_ctx] + [lat_row(a.shape[1]) for a in a_lat]
                 + [ctx_row(D_MODEL), lat_row(D_MODEL), mod_spec] + [_const_spec(v.shape) for v in consts],
        out_specs=[ctx_row(D_MODEL), lat_out],
        out_shape=[jax.ShapeDtypeStruct(x_ctx.shape, F32), jax.ShapeDtypeStruct(x_lat.shape, F32)],
        compiler_params=_params(),
        name="outproj_mlp",
    )(*a_ctx, *a_lat, x_ctx, x_lat, mod, *consts)


def _pool_kernel(x_ref, mod_ref, win_ref, pw_ref, ps_ref, o_ref, *, seq_len):
    L = seq_len
    m = mod_ref[0]
    t_i = lax.broadcasted_iota(jnp.int32, (L, L), 0)
    s_i = lax.broadcasted_iota(jnp.int32, (L, L), 1)
    t_g = lax.broadcasted_iota(jnp.int32, (L, POOL_GROUP_DIM), 0)
    bands, scales = [], []
    for w in POOL_WINDOWS:
        lo, hi = w // 2, w - w // 2
        bands.append(jnp.where((s_i >= t_i - lo) & (s_i < t_i + hi), 1.0 / w, 0.0).astype(BF16))
        scales.append(w / (jnp.minimum(t_g + hi, L) - jnp.maximum(t_g - lo, 0)).astype(F32))
    n_seq = x_ref.shape[0] // L
    seq_rows = [slice(s * L, (s + 1) * L) for s in range(n_seq)]
    hs = [(x_ref[r, :] * (1.0 + m[1:2]) + m[0:1]).astype(BF16) for r in seq_rows]
    us = [_dot(h, win_ref[...]) for h in hs]
    n_groups = len(POOL_WINDOWS)
    cols = [slice(g * POOL_GROUP_DIM, (g + 1) * POOL_GROUP_DIM) for g in range(n_groups)]

    def window_means(g):
        out = []
        for s in range(n_seq):
            ug = us[s][:, cols[g]]
            ug_hi = ug.astype(BF16)
            ug_lo = (ug - ug_hi.astype(F32)).astype(BF16)
            out.append((ug, _dot(bands[g], ug_hi) + _dot(bands[g], ug_lo)))
        return out

    def mix(g, means):
        return [_dot((wm * scales[g] - ug).astype(BF16), pw_ref[g]) * ps_ref[:, cols[g]] for ug, wm in means]

    mixed = []
    pending = window_means(0)
    for g in range(n_groups):
        nxt = window_means(g + 1) if g + 1 < n_groups else None
        mixed.append(mix(g, pending))
        pending = nxt
    for s in range(n_seq):
        o_ref[seq_rows[s], :] = jnp.concatenate([mixed[g][s] for g in range(n_groups)], axis=1).astype(o_ref.dtype)


def _pool_mixer(x, mod, row_of_tile, w_in, pool_w, pool_scale, seq_len, tm):
    t = x.shape[0]
    row = pl.BlockSpec((tm, D_MODEL), lambda i: (i, 0))
    consts = [w_in, pool_w, pool_scale.reshape(1, D_MODEL)]
    return pl.pallas_call(
        functools.partial(_pool_kernel, seq_len=seq_len),
        grid=(t // tm,),
        in_specs=[row, _mod_spec(row_of_tile)] + [_const_spec(cst.shape) for cst in consts],
        out_specs=row,
        out_shape=jax.ShapeDtypeStruct((t, D_MODEL), BF16),
        compiler_params=_params(),
        name="pool_mixer",
    )(x, mod, *consts)


def _rope_tables(rows):
    r, col = np.meshgrid(np.arange(rows), np.arange(GRID_W), indexing="ij")
    r = r.reshape(-1).astype(np.float64)
    col = col.reshape(-1).astype(np.float64)
    n_freq = HEAD_DIM_A // 4
    inv = ROPE_BASE ** (-np.arange(n_freq, dtype=np.float64) / n_freq)
    ang = np.concatenate([r[:, None] * inv, col[:, None] * inv], -1)
    reps = 512 // HEAD_DIM_A
    cosf = np.tile(np.concatenate([np.cos(ang), np.cos(ang)], -1), (1, reps)).astype(np.float32)
    sinf = np.tile(np.concatenate([-np.sin(ang), np.sin(ang)], -1), (1, reps)).astype(np.float32)
    return jnp.asarray(cosf), jnp.asarray(sinf)


def kernel(x_prompt, x_sample, cache_k_l0, cache_v_l0, state_ssm_l0, c, c_ctx, w_mod_l0, b_mod_l0, w_in_l0, conv_w_l0, conv_b_l0, a_log_l0, dt_bias_l0, d_skip_l0, ssm_norm_w_l0, lam_q1_l0, lam_k1_l0, lam_q2_l0, lam_k2_l0, subln_w_l0, w_out_l0, ln1_g_l0, ln1_b_l0, w_up_l0, w_down_l0, ln2_g_l0, ln2_b_l0, w_mod_l1, b_mod_l1, w_in_l1, pool_w_l1, pool_scale_l1, w_out_l1, ln1_g_l1, ln1_b_l1, w_up_l1, w_down_l1, ln2_g_l1, ln2_b_l1):
    n_ctx, l_ctx, _ = x_prompt.shape
    n_lat, l_lat, _ = x_sample.shape
    past = cache_k_l0.shape[1]
    xc = x_prompt.reshape(n_ctx * l_ctx, D_MODEL)
    xl = x_sample.reshape(n_lat * l_lat, D_MODEL)

    cond8 = jnp.concatenate([c_ctx[None, :], c, jnp.zeros((8 - 1 - n_lat, D_MODEL), F32)], axis=0)
    ctx_row = lambda i: 0

    mod0, mod1 = _modulation(cond8, w_mod_l0, b_mod_l0, w_mod_l1, b_mod_l1)
    n_main = 4 * A_QK + A_V + D_INNER_B + CONV_DIM_B
    w_main = w_in_l0.astype(BF16)
    w_dt = jnp.pad(w_in_l0[:, n_main:], ((0, 0), (0, LANES - 2 * N_HEADS_B))).astype(BF16)
    lane_pad = lambda v: jnp.pad(v.reshape(1, -1), ((0, 0), (0, LANES - 2 * N_HEADS_B)))
    conv_consts = (conv_w_l0, conv_b_l0.reshape(1, CONV_DIM_B), lane_pad(dt_bias_l0))
    ssd_consts = [lane_pad(a_log_l0),
                  jnp.repeat(d_skip_l0, HEAD_DIM_B).reshape(1, D_INNER_B),
                  ssm_norm_w_l0.reshape(1, D_INNER_B)]
    lam_vecs = jnp.stack([lam_q1_l0, lam_k1_l0, lam_q2_l0, lam_k2_l0], axis=0)
    lam_init = 0.8 - 0.6 * math.exp(-0.3 * 0)

    (qc, kc, vc, zc, xbc_c, dt_c), (w_out0, w_up0, w_down0, w_out1, w_up1, w_down1, w_in1, pool_w) = _inproj0(
        xc, mod0, ctx_row, w_main, w_dt, *conv_consts, l_ctx, TOKEN_TILE,
        ride_along=(w_out_l0, w_up_l0, w_down_l0, w_out_l1, w_up_l1, w_down_l1, w_in_l1,
                    pool_w_l1.reshape(-1, POOL_GROUP_DIM)))
    pool_w = pool_w.reshape(pool_w_l1.shape)
    oa_c = _attention(qc, [kc], [vc], lam_vecs, subln_w_l0, l_ctx, l_ctx, CTX_SEQS_PER_STEP, lam_init)
    (y_c, new_ssm), _ = _ssd(xbc_c, dt_c, zc, None, ssd_consts, n_ctx, l_ctx, SSD_CTX_SEQS_PER_STEP)

    (ql, kl, vl, zl, xbc_l, dt_l), _ = _inproj0(xl, mod0, lambda i: 1 + i, w_main, w_dt, *conv_consts, l_lat, l_lat,
                                                rope_tables=_rope_tables(l_lat // GRID_W))
    ck = jnp.transpose(cache_k_l0, (0, 2, 3, 4, 1))
    cv = cache_v_l0.reshape(n_lat * past * N_HEADS_A, 2 * HEAD_DIM_A)
    oa_l = _attention(ql, [ck, kl], [cv, vl], lam_vecs, subln_w_l0, l_lat, ATTN_Q_TILE, 1, lam_init)
    h0 = state_ssm_l0.reshape(n_lat, 2, D_INNER_B, D_STATE)
    ((y_l,), _) = _ssd(xbc_l, dt_l, zl, h0, ssd_consts, n_lat, l_lat, 1)
    tail0 = (w_out0, ln1_g_l0, ln1_b_l0, w_up0, w_down0, ln2_g_l0, ln2_b_l0)
    xc, xl = _tail([oa_c, y_c], [oa_l, y_l], xc, xl, mod0, l_lat, *tail0)

    tail1 = (w_out1, ln1_g_l1, ln1_b_l1, w_up1, w_down1, ln2_g_l1, ln2_b_l1)
    mix_c = _pool_mixer(xc, mod1, ctx_row, w_in1, pool_w, pool_scale_l1, l_ctx, POOL_CTX_TILE)
    mix_l = _pool_mixer(xl, mod1, lambda i: 1 + i, w_in1, pool_w, pool_scale_l1, l_lat, l_lat)
    xc, xl = _tail([mix_c], [mix_l], xc, xl, mod1, l_lat, *tail1)

    return (xc.reshape(n_ctx, l_ctx, D_MODEL), xl.reshape(n_lat, l_lat, D_MODEL),
            jnp.transpose(kc, (0, 4, 1, 2, 3)),
            vc.reshape(n_ctx, l_ctx, N_HEADS_A, 2 * HEAD_DIM_A),
            new_ssm.reshape(n_ctx, 2, N_HEADS_B, HEAD_DIM_B, D_STATE))
```
